```python
import jax, jax.numpy as jnp
from jax import lax
import numpy as np

D_MODEL = 1024
BATCH = 2
SEQ = 16384
DEPTH = 1

CHUNK = 64
EPS = 1e-6

GLA_HEADS = 4
GLA_DK = 64
GLA_DV = 128
GLA_GATE_RANK = 16
GLA_GATE_TAU = 16.0
GLA_GATE_BIAS_INIT = 2.0

ATT_HEADS = 8
ATT_DH = 64
N_BACK_CHUNKS = 8
BAND_CHUNKS = N_BACK_CHUNKS + 1
REL_CLIP = 256
MASK_VALUE = -1e30

N_EXPERTS = 32
TOP_K = 4
D_FF = D_MODEL
SWIGLU_ALPHA = 1.702
SWIGLU_LIMIT = 7.0
MOE_BLOCK = 256

GLA_QK_W = GLA_HEADS * GLA_DK
GLA_V_W = GLA_HEADS * GLA_DV
ATT_W = ATT_HEADS * ATT_DH
MIX_W = GLA_V_W + ATT_W
IN_SIZES = (GLA_QK_W, GLA_QK_W, GLA_V_W, GLA_GATE_RANK, GLA_V_W, ATT_W, ATT_W, ATT_W)
IN_W = sum(IN_SIZES)
IN_SPLITS = [int(s) for s in np.cumsum(IN_SIZES)[:-1]]

kernel_name = "hybrid_gla_bandattn_moe_block"


def rms_norm(x, g):
    xf = x.astype(jnp.float32)
    y = xf * lax.rsqrt(jnp.mean(xf * xf, axis=-1, keepdims=True) + EPS)
    return (y * g.astype(jnp.float32)).astype(x.dtype)


def gla_mixer(q, k, v, gate_lr, g_out, gate_up, gate_bias, norm_g):
    B, S = q.shape[0], q.shape[1]
    NC = S // CHUNK
    f32 = jnp.float32
    qc = q.astype(f32).reshape(B, NC, CHUNK, GLA_HEADS, GLA_DK) * (GLA_DK ** -0.5)
    kc = k.astype(f32).reshape(B, NC, CHUNK, GLA_HEADS, GLA_DK)
    vc = v.astype(f32).reshape(B, NC, CHUNK, GLA_HEADS, GLA_DV)
    log_a = jax.nn.log_sigmoid((gate_lr @ gate_up + gate_bias).astype(f32)) / GLA_GATE_TAU
    log_a = log_a.reshape(B, NC, CHUNK, GLA_HEADS, GLA_DK)
    cum = jnp.cumsum(log_a, axis=2)
    tot = cum[:, :, -1]
    k_dec = kc * jnp.exp(tot[:, :, None] - cum)
    u = jnp.einsum('bnchk,bnchv->bnhkv', k_dec, vc)

    def step(state, inp):
        decay, u_n = inp
        state = decay[..., None] * state + u_n
        return state, state

    s0 = jnp.zeros((B, GLA_HEADS, GLA_DK, GLA_DV), f32)
    _, states = lax.scan(step, s0, (jnp.moveaxis(jnp.exp(tot), 1, 0), jnp.moveaxis(u, 1, 0)))
    states = jnp.moveaxis(states, 0, 1)
    o = jnp.einsum('bnchk,bnhkv->bnchv', qc, states)
    o = rms_norm(o, norm_g).reshape(B, S, GLA_V_W)
    o = o * jax.nn.silu(g_out.astype(f32))
    return o.astype(q.dtype)


def band_chunk_attention(q, k, v, q_norm_g, k_norm_g, rel_bias):
    B, S = q.shape[0], q.shape[1]
    NC = S // CHUNK
    qn = rms_norm(q.reshape(B, S, ATT_HEADS, ATT_DH), q_norm_g).reshape(B, NC, CHUNK, ATT_HEADS, ATT_DH)
    kn = rms_norm(k.reshape(B, S, ATT_HEADS, ATT_DH), k_norm_g).reshape(B, NC, CHUNK, ATT_HEADS, ATT_DH)
    vc = v.reshape(B, NC, CHUNK, ATT_HEADS, ATT_DH)

    def band(t):
        tp = jnp.pad(t, ((0, 0), (N_BACK_CHUNKS, 0), (0, 0), (0, 0), (0, 0)))
        tb = jnp.stack([tp[:, i:i + NC] for i in range(BAND_CHUNKS)], axis=2)
        return tb.reshape(B, NC, BAND_CHUNKS * CHUNK, ATT_HEADS, ATT_DH)

    kb, vb = band(kn), band(vc)
    scores = jnp.einsum('bnqhd,bnkhd->bnhqk', qn, kb).astype(jnp.float32) * (ATT_DH ** -0.5)
    rel = (N_BACK_CHUNKS * CHUNK + np.arange(CHUNK)[:, None]) - np.arange(BAND_CHUNKS * CHUNK)[None, :]
    idx = np.clip(rel, -REL_CLIP, REL_CLIP) + REL_CLIP
    bias = rel_bias[:, idx].astype(jnp.float32)
    key_chunk = np.arange(NC)[:, None] - N_BACK_CHUNKS + np.arange(BAND_CHUNKS)[None, :]
    valid = np.repeat(key_chunk >= 0, CHUNK, axis=1)
    scores = jnp.where(valid[None, :, None, None, :], scores + bias[None, None], MASK_VALUE)
    p = jax.nn.softmax(scores, axis=-1).astype(vb.dtype)
    o = jnp.einsum('bnhqk,bnkhd->bnqhd', p, vb)
    return o.reshape(B, S, ATT_W)


def moe_ffn(h, router_w, router_b, w_in, b_in, w_out, b_out):
    N, D = h.shape
    logits = (h @ router_w + router_b).astype(jnp.float32)
    top_val, top_idx = lax.top_k(logits, TOP_K)
    gates = jax.nn.softmax(top_val, axis=-1).astype(h.dtype)

    A = N * TOP_K
    n_blocks = -(-A // MOE_BLOCK) + N_EXPERTS
    P = n_blocks * MOE_BLOCK
    flat_e = top_idx.reshape(-1).astype(jnp.int32)
    flat_tok = jnp.repeat(jnp.arange(N, dtype=jnp.int32), TOP_K)
    flat_g = gates.reshape(-1)
    order = jnp.argsort(flat_e, stable=True)
    e_sorted, tok_sorted, g_sorted = flat_e[order], flat_tok[order], flat_g[order]
    counts = jnp.bincount(flat_e, length=N_EXPERTS).astype(jnp.int32)
    start = jnp.cumsum(counts) - counts
    padded = ((counts + MOE_BLOCK - 1) // MOE_BLOCK) * MOE_BLOCK
    pend = jnp.cumsum(padded)
    pstart = pend - padded
    dest = pstart[e_sorted] + (jnp.arange(A, dtype=jnp.int32) - start[e_sorted])
    buf_tok = jnp.full((P,), N, jnp.int32).at[dest].set(tok_sorted)
    buf_g = jnp.zeros((P,), h.dtype).at[dest].set(g_sorted)
    block_expert = jnp.minimum(
        jnp.searchsorted(pend, jnp.arange(n_blocks, dtype=jnp.int32) * MOE_BLOCK, side='right'),
        N_EXPERTS - 1).astype(jnp.int32)
    h_pad = jnp.concatenate([h, jnp.zeros((1, D), h.dtype)], axis=0)
    xb = h_pad[buf_tok].reshape(n_blocks, MOE_BLOCK, D)

    def expert_block(args):
        xblk, e = args
        hc = xblk @ w_in[e] + b_in[e]
        glu, lin = hc[:, :D_FF], hc[:, D_FF:]
        glu = jnp.minimum(glu, SWIGLU_LIMIT)
        lin = jnp.clip(lin, -SWIGLU_LIMIT, SWIGLU_LIMIT)
        act = glu * jax.nn.sigmoid(SWIGLU_ALPHA * glu) * (lin + 1.0)
        return act @ w_out[e] + b_out[e]

    y = lax.map(expert_block, (xb, block_expert)).reshape(P, D)
    out = jnp.zeros((N + 1, D), h.dtype).at[buf_tok].add(y * buf_g[:, None])
    return out[:N]


def hybrid_layer(x, norm1_g, w_in, gla_gate_up, gla_gate_bias, gla_norm_g, q_norm_g, k_norm_g,
                 rel_bias, w_out, norm2_g, router_w, router_b, moe_w_in, moe_b_in, moe_w_out, moe_b_out):
    B, S, D = x.shape
    h = rms_norm(x, norm1_g)
    proj = h @ w_in
    g_q, g_k, g_v, g_lr, g_og, a_q, a_k, a_v = jnp.split(proj, IN_SPLITS, axis=-1)
    o_gla = gla_mixer(g_q, g_k, g_v, g_lr, g_og, gla_gate_up, gla_gate_bias, gla_norm_g)
    o_att = band_chunk_attention(a_q, a_k, a_v, q_norm_g, k_norm_g, rel_bias)
    x = x + jnp.concatenate([o_gla, o_att], axis=-1) @ w_out
    h2 = rms_norm(x, norm2_g).reshape(B * S, D)
    y = moe_ffn(h2, router_w, router_b, moe_w_in, moe_b_in, moe_w_out, moe_b_out)
    return x + y.reshape(B, S, D)


def setup_inputs(seed: int = 0) -> dict:
    key = jax.random.key(seed)
    ks = jax.random.split(key, 17)
    f32 = jnp.float32
    L, D = DEPTH, D_MODEL
    nrm = lambda k, shape, s: jax.random.normal(k, shape, f32) * s
    return {
        "x": nrm(ks[0], (BATCH, SEQ, D), 1.0),
        "norm1_g": 1.0 + nrm(ks[1], (L, D), 0.02),
        "w_in": nrm(ks[2], (L, D, IN_W), D ** -0.5),
        "gla_gate_up": nrm(ks[3], (L, GLA_GATE_RANK, GLA_QK_W), GLA_GATE_RANK ** -0.5),
        "gla_gate_bias": GLA_GATE_BIAS_INIT + nrm(ks[4], (L, GLA_QK_W), 0.1),
        "gla_norm_g": 1.0 + nrm(ks[5], (L, GLA_DV), 0.02),
        "q_norm_g": 1.0 + nrm(ks[6], (L, ATT_DH), 0.02),
        "k_norm_g": 1.0 + nrm(ks[7], (L, ATT_DH), 0.02),
        "rel_bias": nrm(ks[8], (L, ATT_HEADS, 2 * REL_CLIP + 1), 0.3),
        "w_out": nrm(ks[9], (L, MIX_W, D), MIX_W ** -0.5),
        "norm2_g": 1.0 + nrm(ks[10], (L, D), 0.02),
        "router_w": nrm(ks[11], (L, D, N_EXPERTS), D ** -0.5),
        "router_b": nrm(ks[12], (L, N_EXPERTS), 0.01),
        "moe_w_in": nrm(ks[13], (L, N_EXPERTS, D, 2 * D_FF), D ** -0.5),
        "moe_b_in": nrm(ks[14], (L, N_EXPERTS, 2 * D_FF), 0.01),
        "moe_w_out": nrm(ks[15], (L, N_EXPERTS, D_FF, D), D_FF ** -0.5),
        "moe_b_out": nrm(ks[16], (L, N_EXPERTS, D), 0.01),
    }


def reference(x, norm1_g, w_in, gla_gate_up, gla_gate_bias, gla_norm_g, q_norm_g, k_norm_g,
              rel_bias, w_out, norm2_g, router_w, router_b, moe_w_in, moe_b_in, moe_w_out, moe_b_out):
    for l in range(DEPTH):
        x = hybrid_layer(x, norm1_g[l], w_in[l], gla_gate_up[l], gla_gate_bias[l], gla_norm_g[l],
                         q_norm_g[l], k_norm_g[l], rel_bias[l], w_out[l], norm2_g[l], router_w[l],
                         router_b[l], moe_w_in[l], moe_b_in[l], moe_w_out[l], moe_b_out[l])
    return x
```

```python
import functools

import numpy as np
import jax
import jax.numpy as jnp
from jax import lax
from jax.experimental import pallas as pl
from jax.experimental.pallas import tpu as pltpu

CHUNK = 64
EPS = 1e-6
GLA_HEADS = 4
GLA_DK = 64
GLA_DV = 128
GLA_GATE_RANK = 16
GLA_GATE_TAU = 16.0
ATT_HEADS = 8
ATT_DH = 64
N_BACK_CHUNKS = 8
BAND_CHUNKS = N_BACK_CHUNKS + 1
REL_CLIP = 256
MASK_VALUE = -1e30
N_EXPERTS = 32
TOP_K = 4
SWIGLU_ALPHA = 1.702
SWIGLU_LIMIT = 7.0
MOE_BLOCK = 256

LANE = 128
GLA_QK_W = GLA_HEADS * GLA_DK
GLA_V_W = GLA_HEADS * GLA_DV
ATT_W = ATT_HEADS * ATT_DH
IN_SIZES = (GLA_QK_W, GLA_QK_W, GLA_V_W, GLA_GATE_RANK, GLA_V_W, ATT_W, ATT_W, ATT_W)
SEQ_TILE = N_BACK_CHUNKS * CHUNK
ROW_TILE = 512
COMBINE_TILE = 256
VMEM_LIMIT = 48 * 1024 * 1024

f32 = jnp.float32
bf16 = jnp.bfloat16


def _rms(x, g):
    return x * lax.rsqrt(jnp.mean(x * x, axis=-1, keepdims=True) + EPS) * g


def _dot(a, b):
    return jnp.dot(a, b, preferred_element_type=f32)


def _dot_nt(a, b):
    return lax.dot_general(a, b, (((1,), (1,)), ((), ())), preferred_element_type=f32)


def _dot_tn(a, b):
    return lax.dot_general(a, b, (((0,), (0,)), ((), ())), preferred_element_type=f32)


def _split_bf16(x):
    hi = x.astype(bf16)
    lo = (x - hi.astype(f32)).astype(bf16)
    return hi, lo


def _inproj_kernel(x_ref, g_ref, w_ref, *out_refs):
    h = _rms(x_ref[...], g_ref[...]).astype(bf16)
    off = 0
    for o_ref in out_refs:
        w = o_ref.shape[-1]
        o_ref[...] = _dot(h, w_ref[:, off:off + w]).astype(o_ref.dtype)
        off += w


def _inproj(x2, g, w, widths):
    n, d = x2.shape
    return pl.pallas_call(
        _inproj_kernel,
        grid=(n // ROW_TILE,),
        in_specs=[
            pl.BlockSpec((ROW_TILE, d), lambda i: (i, 0)),
            pl.BlockSpec((1, d), lambda i: (0, 0)),
            pl.BlockSpec(w.shape, lambda i: (0, 0)),
        ],
        out_specs=[pl.BlockSpec((ROW_TILE, wd), lambda i: (i, 0)) for wd in widths],
        out_shape=[jax.ShapeDtypeStruct((n, wd), bf16) for wd in widths],
        compiler_params=pltpu.CompilerParams(
            dimension_semantics=("arbitrary",), vmem_limit_bytes=VMEM_LIMIT),
        name="inproj",
    )(x2, g, w)


def _gla_kernel(q_ref, k_ref, v_ref, lr_ref, og_ref, gup_ref, gb_ref, ng_ref, o_ref, st_ref):
    t = q_ref.shape[0]

    @pl.when(pl.program_id(1) == 0)
    def _():
        st_ref[...] = jnp.zeros_like(st_ref)

    z = _dot(lr_ref[...], gup_ref[...]) + gb_ref[...]
    log_a = (jnp.minimum(z, 0.0) - jnp.log1p(jnp.exp(-jnp.abs(z)))) * (1.0 / GLA_GATE_TAU)
    row = lax.broadcasted_iota(jnp.int32, (t, t), 0)
    col = lax.broadcasted_iota(jnp.int32, (t, t), 1)
    tri = jnp.where((col <= row) & ((col // CHUNK) == (row // CHUNK)), 1.0, 0.0).astype(bf16)
    la_hi, la_lo = _split_bf16(log_a)
    cum_all = _dot(tri, la_hi) + _dot(tri, la_lo)
    lane = lax.broadcasted_iota(jnp.int32, (1, LANE), 1)
    half_mask = (lane < GLA_DK, lane >= GLA_DK)

    for c in range(t // CHUNK):
        rows = slice(c * CHUNK, (c + 1) * CHUNK)
        cum = cum_all[rows]
        tot = cum[CHUNK - 1:CHUNK]
        kdec = k_ref[rows, :].astype(f32) * jnp.exp(tot - cum)
        dec = jnp.exp(tot)
        for h in range(GLA_HEADS):
            p, half = divmod(h, 2)
            pair = slice(p * LANE, (p + 1) * LANE)
            head = slice(h * GLA_DV, (h + 1) * GLA_DV)
            kd = jnp.where(half_mask[half], kdec[:, pair], 0.0).astype(bf16)
            u_t = _dot_tn(v_ref[rows, head], kd)
            st = st_ref[h] * dec[:, pair] + u_t
            st_ref[h] = st
            o = _dot_nt(q_ref[rows, pair], st.astype(bf16)) * (GLA_DK ** -0.5)
            o = _rms(o, ng_ref[...])
            g = og_ref[rows, head].astype(f32)
            o_ref[rows, head] = (o * (g / (1.0 + jnp.exp(-g)))).astype(o_ref.dtype)


def _gla(gq, gk, gv, glr, gog, gup, gb, ng, batch, seq):
    nt = seq // SEQ_TILE
    tile = lambda w: pl.BlockSpec((SEQ_TILE, w), lambda b, i: (b * nt + i, 0))
    full = lambda a: pl.BlockSpec(a.shape, lambda b, i: (0,) * a.ndim)
    return pl.pallas_call(
        _gla_kernel,
        grid=(batch, nt),
        in_specs=[tile(GLA_QK_W), tile(GLA_QK_W), tile(GLA_V_W), tile(LANE), tile(GLA_V_W),
                  full(gup), full(gb), full(ng)],
        out_specs=tile(GLA_V_W),
        out_shape=jax.ShapeDtypeStruct((batch * seq, GLA_V_W), bf16),
        scratch_shapes=[pltpu.VMEM((GLA_HEADS, GLA_DV, LANE), f32)],
        compiler_params=pltpu.CompilerParams(
            dimension_semantics=("arbitrary", "arbitrary"), vmem_limit_bytes=VMEM_LIMIT),
        name="gla",
    )(gq, gk, gv, glr, gog, gup, gb, ng)


def _head_norm(x, g):
    lo = lax.broadcasted_iota(jnp.int32, (1, LANE), 1) < ATT_DH
    sq = x * x
    s0 = jnp.sum(jnp.where(lo, sq, 0.0), axis=-1, keepdims=True)
    s1 = jnp.sum(jnp.where(lo, 0.0, sq), axis=-1, keepdims=True)
    r = jnp.where(lo, lax.rsqrt(s0 * (1.0 / ATT_DH) + EPS), lax.rsqrt(s1 * (1.0 / ATT_DH) + EPS))
    return x * r * g


def _att_kernel(q_ref, k_ref, v_ref, bias_ref, qg_ref, kg_ref, o_ref, qn_ref, kb_ref, vb_ref):
    t = q_ref.shape[0]
    band = BAND_CHUNKS * CHUNK
    first = pl.program_id(1) == 0

    @pl.when(first)
    def _():
        kb_ref[0:t, :] = jnp.zeros((t, ATT_W), bf16)
        vb_ref[0:t, :] = jnp.zeros((t, ATT_W), bf16)

    @pl.when(jnp.logical_not(first))
    def _():
        kb_ref[0:t, :] = kb_ref[t:2 * t, :]
        vb_ref[0:t, :] = vb_ref[t:2 * t, :]

    for p in range(ATT_W // LANE):
        pair = slice(p * LANE, (p + 1) * LANE)
        qn_ref[:, pair] = _head_norm(q_ref[:, pair].astype(f32), qg_ref[...]).astype(bf16)
        kb_ref[t:2 * t, pair] = _head_norm(k_ref[:, pair].astype(f32), kg_ref[...]).astype(bf16)
    vb_ref[t:2 * t, :] = v_ref[...]

    lane = lax.broadcasted_iota(jnp.int32, (1, LANE), 1)
    lo = lane < ATT_DH
    colk = lax.broadcasted_iota(jnp.int32, (1, band), 1)

    def chunk_body(c, carry):
        r0 = pl.multiple_of(c * CHUNK, CHUNK)
        first_valid = jnp.where(first, t - c * CHUNK, 0)
        valid = colk >= first_valid
        for p in range(ATT_W // LANE):
            pair = slice(p * LANE, (p + 1) * LANE)
            q2 = qn_ref[pl.ds(r0, CHUNK), pair]
            k2 = kb_ref[pl.ds(r0, band), pair]
            v2 = vb_ref[pl.ds(r0, band), pair]
            outs = []
            for half in range(2):
                qm = jnp.where(lo if half == 0 else jnp.logical_not(lo), q2, jnp.zeros_like(q2))
                s = _dot_nt(qm, k2) * (ATT_DH ** -0.5) + bias_ref[2 * p + half]
                s = jnp.where(valid, s, MASK_VALUE)
                e = jnp.exp(s - jnp.max(s, axis=-1, keepdims=True))
                l = jnp.sum(e, axis=-1, keepdims=True)
                outs.append(_dot(e.astype(bf16), v2) / l)
            o_ref[pl.ds(r0, CHUNK), pair] = jnp.where(lo, outs[0], outs[1]).astype(o_ref.dtype)
        return carry

    lax.fori_loop(0, t // CHUNK, chunk_body, 0)


def _attention(aq, ak, av, bias, qg, kg, batch, seq):
    nt = seq // SEQ_TILE
    tile = pl.BlockSpec((SEQ_TILE, ATT_W), lambda b, i: (b * nt + i, 0))
    full = lambda a: pl.BlockSpec(a.shape, lambda b, i: (0,) * a.ndim)
    return pl.pallas_call(
        _att_kernel,
        grid=(batch, nt),
        in_specs=[tile, tile, tile, full(bias), full(qg), full(kg)],
        out_specs=tile,
        out_shape=jax.ShapeDtypeStruct((batch * seq, ATT_W), bf16),
        scratch_shapes=[pltpu.VMEM((SEQ_TILE, ATT_W), bf16),
                        pltpu.VMEM((2 * SEQ_TILE, ATT_W), bf16),
                        pltpu.VMEM((2 * SEQ_TILE, ATT_W), bf16)],
        compiler_params=pltpu.CompilerParams(
            dimension_semantics=("arbitrary", "arbitrary"), vmem_limit_bytes=VMEM_LIMIT),
        name="attention",
    )(aq, ak, av, bias, qg, kg)


def _mid_kernel(og_ref, oa_ref, x_ref, wo_ref, g2_ref, rwh_ref, rwl_ref, rb_ref,
                xm_ref, hp_ref, meta_ref, gate_ref, cnt_ref, carry_ref):
    t, d = x_ref.shape
    hw = d // 2

    @pl.when(pl.program_id(0) == 0)
    def _():
        carry_ref[...] = jnp.zeros_like(carry_ref)

    xm = x_ref[...] + _dot(og_ref[...], wo_ref[0:GLA_V_W, :]) + _dot(oa_ref[...], wo_ref[GLA_V_W:, :])
    xm_ref[...] = xm
    h2 = _rms(xm, g2_ref[...])
    h_hi, h_lo = _split_bf16(h2)
    a = pltpu.bitcast(h_hi[:, :hw].astype(f32), jnp.uint32)
    b = pltpu.bitcast(h_hi[:, hw:].astype(f32), jnp.uint32)
    hp_ref[...] = (a >> 16) | (b & jnp.uint32(0xFFFF0000))

    logits = (_dot(h_hi, rwh_ref[...]) + _dot(h_lo, rwh_ref[...]) + _dot(h_hi, rwl_ref[...])
              + rb_ref[...])
    lane = lax.broadcasted_iota(jnp.int32, (t, LANE), 1)
    lane_f = lane.astype(f32)
    l = jnp.where(lane < N_EXPERTS, logits, -jnp.inf)
    vals, idxs = [], []
    for _ in range(TOP_K):
        m = jnp.max(l, axis=-1, keepdims=True)
        ik = jnp.min(jnp.where(l == m, lane_f, float(LANE)), axis=-1, keepdims=True)
        vals.append(m)
        idxs.append(ik)
        l = jnp.where(lane_f == ik, -jnp.inf, l)
    es = [jnp.exp(v - vals[0]) for v in vals]
    den = es[0] + es[1] + es[2] + es[3]

    onehots = [lane_f == ik for ik in idxs]
    sel = jnp.zeros((t, LANE), f32)
    for oh in onehots:
        sel = sel + jnp.where(oh, 1.0, 0.0)
    row = lax.broadcasted_iota(jnp.int32, (t, t), 0)
    col = lax.broadcasted_iota(jnp.int32, (t, t), 1)
    tri = jnp.where(col < row, 1.0, 0.0).astype(bf16)
    prefix = _dot(tri, sel.astype(bf16)) + carry_ref[...]
    new_carry = carry_ref[...] + jnp.sum(sel, axis=0, keepdims=True)
    carry_ref[...] = new_carry
    cnt_ref[...] = new_carry

    meta = jnp.zeros((t, LANE), f32)
    gates = jnp.zeros((t, LANE), f32)
    for k in range(TOP_K):
        rank_k = jnp.sum(jnp.where(onehots[k], prefix, 0.0), axis=-1, keepdims=True)
        meta = jnp.where(lane == k, idxs[k], meta)
        meta = jnp.where(lane == TOP_K + k, rank_k, meta)
        gates = jnp.where(lane == k, es[k] / den, gates)
    meta_ref[...] = meta.astype(jnp.int32)
    gate_ref[...] = gates


def _mid(o_gla, o_att, x2, wo, g2, rwh, rwl, rb):
    n, d = x2.shape
    tile = lambda w: pl.BlockSpec((ROW_TILE, w), lambda i: (i, 0))
    full = lambda a: pl.BlockSpec(a.shape, lambda i: (0,) * a.ndim)
    return pl.pallas_call(
        _mid_kernel,
        grid=(n // ROW_TILE,),
        in_specs=[tile(GLA_V_W), tile(ATT_W), tile(d), full(wo), full(g2), full(rwh), full(rwl),
                  full(rb)],
        out_specs=[tile(d), tile(d // 2), tile(LANE), tile(LANE),
                   pl.BlockSpec((1, LANE), lambda i: (0, 0))],
        out_shape=[jax.ShapeDtypeStruct((n, d), f32),
                   jax.ShapeDtypeStruct((n, d // 2), jnp.uint32),
                   jax.ShapeDtypeStruct((n, LANE), jnp.int32),
                   jax.ShapeDtypeStruct((n, LANE), f32),
                   jax.ShapeDtypeStruct((1, LANE), f32)],
        scratch_shapes=[pltpu.VMEM((1, LANE), f32)],
        compiler_params=pltpu.CompilerParams(
            dimension_semantics=("arbitrary",), vmem_limit_bytes=VMEM_LIMIT),
        name="mid",
    )(o_gla, o_att, x2, wo, g2, rwh, rwl, rb)


def _dispatch_kernel(dest_ref, h_ref, xb_ref, sem):
    t = h_ref.shape[0]

    def row_copy(r, j):
        d = dest_ref[0, 0, r * TOP_K + j]
        return pltpu.make_async_copy(h_ref.at[pl.ds(r, 1)], xb_ref.at[pl.ds(d, 1)], sem)

    def issue(r, c):
        for j in range(TOP_K):
            row_copy(r, j).start()
        return c

    def drain(r, c):
        for j in range(TOP_K):
            row_copy(r, j).wait()
        return c

    lax.fori_loop(0, t, issue, 0)
    lax.fori_loop(0, t, drain, 0)


def _dispatch(dest3, hp, n_rows):
    n, hw = hp.shape
    return pl.pallas_call(
        _dispatch_kernel,
        grid=(n // ROW_TILE,),
        in_specs=[pl.BlockSpec((1, 1, ROW_TILE * TOP_K), lambda i: (i, 0, 0),
                               memory_space=pltpu.SMEM),
                  pl.BlockSpec((ROW_TILE, hw), lambda i: (i, 0))],
        out_specs=pl.BlockSpec(memory_space=pl.ANY),
        out_shape=jax.ShapeDtypeStruct((n_rows, hw), jnp.uint32),
        scratch_shapes=[pltpu.SemaphoreType.DMA],
        compiler_params=pltpu.CompilerParams(
            dimension_semantics=("arbitrary",), vmem_limit_bytes=VMEM_LIMIT),
        name="dispatch",
    )(dest3, hp)


def _expert_kernel(be_ref, nv_ref, nu_ref, x_ref, wi_ref, bi_ref, wo_ref, bo_ref, y_ref,
                   wib_ref, wob_ref):
    b = pl.program_id(0)
    blk, hw = x_ref.shape
    dff = wo_ref.shape[1]

    @pl.when(b < nu_ref[0])
    def _():
        changed = jnp.logical_or(b == 0, be_ref[b] != be_ref[jnp.maximum(b - 1, 0)])

        @pl.when(changed)
        def _():
            wib_ref[...] = wi_ref[0].astype(bf16)
            wob_ref[...] = wo_ref[0].astype(bf16)

        rows = lax.broadcasted_iota(jnp.int32, (blk, hw), 0)
        w = jnp.where(rows < nv_ref[b], x_ref[...], jnp.uint32(0))
        xa = pltpu.bitcast(w << 16, f32).astype(bf16)
        xb = pltpu.bitcast(w & jnp.uint32(0xFFFF0000), f32).astype(bf16)
        hc = _dot(xa, wib_ref[0:hw, :]) + _dot(xb, wib_ref[hw:, :]) + bi_ref[0]
        glu = jnp.minimum(hc[:, :dff], SWIGLU_LIMIT)
        lin = jnp.clip(hc[:, dff:], -SWIGLU_LIMIT, SWIGLU_LIMIT)
        act = glu * (1.0 / (1.0 + jnp.exp(-SWIGLU_ALPHA * glu))) * (lin + 1.0)
        y_ref[...] = _dot(act.astype(bf16), wob_ref[...]) + bo_ref[0]


def _experts(block_expert, block_valid, n_used, xb, w_in, b_in, w_out, b_out):
    n_rows, hw = xb.shape
    e, d, dff2 = w_in.shape
    dff = w_out.shape[1]
    n_blocks = n_rows // MOE_BLOCK
    grid_spec = pltpu.PrefetchScalarGridSpec(
        num_scalar_prefetch=3,
        grid=(n_blocks,),
        in_specs=[
            pl.BlockSpec((MOE_BLOCK, hw), lambda b, be, nv, nu: (b, 0)),
            pl.BlockSpec((1, d, dff2), lambda b, be, nv, nu: (be[b], 0, 0)),
            pl.BlockSpec((1, 1, dff2), lambda b, be, nv, nu: (be[b], 0, 0)),
            pl.BlockSpec((1, dff, d), lambda b, be, nv, nu: (be[b], 0, 0)),
            pl.BlockSpec((1, 1, d), lambda b, be, nv, nu: (be[b], 0, 0)),
        ],
        out_specs=pl.BlockSpec((MOE_BLOCK, d), lambda b, be, nv, nu: (b, 0)),
        scratch_shapes=[pltpu.VMEM((d, dff2), bf16), pltpu.VMEM((dff, d), bf16)],
    )
    return pl.pallas_call(
        _expert_kernel,
        grid_spec=grid_spec,
        out_shape=jax.ShapeDtypeStruct((n_rows, d), f32),
        compiler_params=pltpu.CompilerParams(
            dimension_semantics=("arbitrary",), vmem_limit_bytes=VMEM_LIMIT),
        name="experts",
    )(block_expert, block_valid, n_used, xb, w_in, b_in.reshape(e, 1, dff2), w_out,
      b_out.reshape(e, 1, d))


def _combine_kernel(dest_ref, gate_ref, xm_ref, y_ref, o_ref, yb_ref, sem):
    t = xm_ref.shape[0]

    def row_copy(r, j):
        d = dest_ref[0, 0, r * TOP_K + j]
        return pltpu.make_async_copy(y_ref.at[pl.ds(d, 1)], yb_ref.at[j, pl.ds(r, 1)], sem)

    def issue(r, c):
        for j in range(TOP_K):
            row_copy(r, j).start()
        return c

    def drain(r, c):
        for j in range(TOP_K):
            row_copy(r, j).wait()
        return c

    lax.fori_loop(0, t, issue, 0)
    lax.fori_loop(0, t, drain, 0)
    acc = xm_ref[...]
    for j in range(TOP_K):
        acc = acc + gate_ref[:, j:j + 1] * yb_ref[j]
    o_ref[...] = acc


def _combine(dest3, gates, xm, y):
    n, d = xm.shape
    t = COMBINE_TILE
    return pl.pallas_call(
        _combine_kernel,
        grid=(n // t,),
        in_specs=[pl.BlockSpec((1, 1, t * TOP_K), lambda i: (i, 0, 0), memory_space=pltpu.SMEM),
                  pl.BlockSpec((t, LANE), lambda i: (i, 0)),
                  pl.BlockSpec((t, d), lambda i: (i, 0)),
                  pl.BlockSpec(memory_space=pl.ANY)],
        out_specs=pl.BlockSpec((t, d), lambda i: (i, 0)),
        out_shape=jax.ShapeDtypeStruct((n, d), f32),
        scratch_shapes=[pltpu.VMEM((TOP_K, t, d), f32), pltpu.SemaphoreType.DMA],
        compiler_params=pltpu.CompilerParams(
            dimension_semantics=("arbitrary",), vmem_limit_bytes=VMEM_LIMIT),
        name="combine",
    )(dest3, gates, xm, y)


def _rel_bias_table(rel_bias):
    rel = (N_BACK_CHUNKS * CHUNK + np.arange(CHUNK)[:, None]) - np.arange(BAND_CHUNKS * CHUNK)[None, :]
    idx = np.clip(rel, -REL_CLIP, REL_CLIP) + REL_CLIP
    return rel_bias[:, idx].astype(f32)


def _routing_tables(counts, meta, n):
    n_blocks = -(-(n * TOP_K) // MOE_BLOCK) + N_EXPERTS
    padded = ((counts + MOE_BLOCK - 1) // MOE_BLOCK) * MOE_BLOCK
    pend = jnp.cumsum(padded)
    pstart = pend - padded
    idx = meta[:, :TOP_K]
    rank = meta[:, TOP_K:2 * TOP_K]
    dest = pstart[idx] + rank
    blk0 = jnp.arange(n_blocks, dtype=jnp.int32) * MOE_BLOCK
    n_used = (pend[-1] // MOE_BLOCK).astype(jnp.int32)
    be = jnp.minimum(jnp.searchsorted(pend, blk0, side='right'), N_EXPERTS - 1).astype(jnp.int32)
    last = be[jnp.maximum(n_used - 1, 0)]
    be = jnp.where(jnp.arange(n_blocks) < n_used, be, last)
    valid = jnp.clip(counts[be] - (blk0 - pstart[be]), 0, MOE_BLOCK).astype(jnp.int32)
    return dest.astype(jnp.int32), be, valid, n_used.reshape(1), n_blocks


def _layer(x, norm1_g, w_in, gate_up, gate_bias, gla_norm_g, q_norm_g, k_norm_g, rel_bias, w_out,
           norm2_g, router_w, router_b, moe_w_in, moe_b_in, moe_w_out, moe_b_out):
    batch, seq, d = x.shape
    n = batch * seq
    x2 = x.reshape(n, d)

    pieces = jnp.split(w_in, np.cumsum(IN_SIZES)[:-1].tolist(), axis=-1)
    pieces[3] = jnp.pad(pieces[3], ((0, 0), (0, LANE - GLA_GATE_RANK)))
    widths = [p.shape[-1] for p in pieces]
    w_all = jnp.concatenate(pieces, axis=-1).astype(bf16)
    gq, gk, gv, glr, gog, aq, ak, av = _inproj(x2, norm1_g.reshape(1, d), w_all, widths)

    gup = jnp.pad(gate_up, ((0, LANE - GLA_GATE_RANK), (0, 0))).astype(bf16)
    o_gla = _gla(gq, gk, gv, glr, gog, gup, gate_bias.reshape(1, -1), gla_norm_g.reshape(1, -1),
                 batch, seq)
    tile2 = lambda g: jnp.tile(g.reshape(1, -1), (1, LANE // ATT_DH))
    o_att = _attention(aq, ak, av, _rel_bias_table(rel_bias), tile2(q_norm_g), tile2(k_norm_g),
                       batch, seq)

    rw = jnp.pad(router_w, ((0, 0), (0, LANE - N_EXPERTS)))
    rw_hi = rw.astype(bf16)
    rw_lo = (rw - rw_hi.astype(f32)).astype(bf16)
    rb = jnp.pad(router_b, (0, LANE - N_EXPERTS)).reshape(1, LANE)
    xm, hp, meta, gates, cnt = _mid(o_gla, o_att, x2, w_out.astype(bf16), norm2_g.reshape(1, d),
                                    rw_hi, rw_lo, rb)

    counts = cnt[0, :N_EXPERTS].astype(jnp.int32)
    dest, be, valid, n_used, n_blocks = _routing_tables(counts, meta, n)
    xb = _dispatch(dest.reshape(n // ROW_TILE, 1, ROW_TILE * TOP_K), hp, n_blocks * MOE_BLOCK)
    y = _experts(be, valid, n_used, xb, moe_w_in, moe_b_in, moe_w_out, moe_b_out)
    out = _combine(dest.reshape(n // COMBINE_TILE, 1, COMBINE_TILE * TOP_K), gates, xm, y)
    return out.reshape(batch, seq, d)


def kernel(x, norm1_g, w_in, gla_gate_up, gla_gate_bias, gla_norm_g, q_norm_g, k_norm_g, rel_bias, w_out, norm2_g, router_w, router_b, moe_w_in, moe_b_in, moe_w_out, moe_b_out):
    for l in range(norm1_g.shape[0]):
        x = _layer(x, norm1_g[l], w_in[l], gla_gate_up[l], gla_gate_bias[l], gla_norm_g[l],
                   q_norm_g[l], k_norm_g[l], rel_bias[l], w_out[l], norm2_g[l], router_w[l],
                   router_b[l], moe_w_in[l], moe_b_in[l], moe_w_out[l], moe_b_out[l])
    return x
```

```python
import functools

import numpy as np
import jax
import jax.numpy as jnp
from jax import lax
from jax.experimental import pallas as pl
from jax.experimental.pallas import tpu as pltpu

CHUNK = 64
EPS = 1e-6
GLA_HEADS = 4
GLA_DK = 64
GLA_DV = 128
GLA_GATE_RANK = 16
GLA_GATE_TAU = 16.0
ATT_HEADS = 8
ATT_DH = 64
N_BACK_CHUNKS = 8
BAND_CHUNKS = N_BACK_CHUNKS + 1
REL_CLIP = 256
MASK_VALUE = -1e30
N_EXPERTS = 32
TOP_K = 4
SWIGLU_ALPHA = 1.702
SWIGLU_LIMIT = 7.0
MOE_BLOCK = 256

LANE = 128
GLA_QK_W = GLA_HEADS * GLA_DK
GLA_V_W = GLA_HEADS * GLA_DV
ATT_W = ATT_HEADS * ATT_DH
IN_SIZES = (GLA_QK_W, GLA_QK_W, GLA_V_W, GLA_GATE_RANK, GLA_V_W, ATT_W, ATT_W, ATT_W)
SEQ_TILE = N_BACK_CHUNKS * CHUNK
ROW_TILE = 512
COMBINE_TILE = 256
VMEM_LIMIT = 48 * 1024 * 1024

f32 = jnp.float32
bf16 = jnp.bfloat16


def _rms(x, g):
    return x * lax.rsqrt(jnp.mean(x * x, axis=-1, keepdims=True) + EPS) * g


def _dot(a, b):
    return jnp.dot(a, b, preferred_element_type=f32)


def _dot_nt(a, b):
    return lax.dot_general(a, b, (((1,), (1,)), ((), ())), preferred_element_type=f32)


def _dot_tn(a, b):
    return lax.dot_general(a, b, (((0,), (0,)), ((), ())), preferred_element_type=f32)


def _split_bf16(x):
    hi = x.astype(bf16)
    lo = (x - hi.astype(f32)).astype(bf16)
    return hi, lo


def _inproj_kernel(x_ref, g_ref, w_ref, *out_refs):
    h = _rms(x_ref[...], g_ref[...]).astype(bf16)
    off = 0
    for o_ref in out_refs:
        w = o_ref.shape[-1]
        o_ref[...] = _dot(h, w_ref[:, off:off + w]).astype(o_ref.dtype)
        off += w


def _inproj(x2, g, w, widths):
    n, d = x2.shape
    return pl.pallas_call(
        _inproj_kernel,
        grid=(n // ROW_TILE,),
        in_specs=[
            pl.BlockSpec((ROW_TILE, d), lambda i: (i, 0)),
            pl.BlockSpec((1, d), lambda i: (0, 0)),
            pl.BlockSpec(w.shape, lambda i: (0, 0)),
        ],
        out_specs=[pl.BlockSpec((ROW_TILE, wd), lambda i: (i, 0)) for wd in widths],
        out_shape=[jax.ShapeDtypeStruct((n, wd), bf16) for wd in widths],
        compiler_params=pltpu.CompilerParams(
            dimension_semantics=("arbitrary",), vmem_limit_bytes=VMEM_LIMIT),
        name="inproj",
    )(x2, g, w)


def _gla_kernel(q_ref, k_ref, v_ref, lr_ref, og_ref, gup_ref, gb_ref, ng_ref, o_ref, st_ref):
    t = q_ref.shape[0]

    @pl.when(pl.program_id(1) == 0)
    def _():
        st_ref[...] = jnp.zeros_like(st_ref)

    z = _dot(lr_ref[...], gup_ref[...]) + gb_ref[...]
    log_a = (jnp.minimum(z, 0.0) - jnp.log1p(jnp.exp(-jnp.abs(z)))) * (1.0 / GLA_GATE_TAU)
    row = lax.broadcasted_iota(jnp.int32, (t, t), 0)
    col = lax.broadcasted_iota(jnp.int32, (t, t), 1)
    tri = jnp.where((col <= row) & ((col // CHUNK) == (row // CHUNK)), 1.0, 0.0).astype(bf16)
    la_hi, la_lo = _split_bf16(log_a)
    cum_all = _dot(tri, la_hi) + _dot(tri, la_lo)
    lane = lax.broadcasted_iota(jnp.int32, (1, LANE), 1)
    half_mask = (lane < GLA_DK, lane >= GLA_DK)

    for c in range(t // CHUNK):
        rows = slice(c * CHUNK, (c + 1) * CHUNK)
        cum = cum_all[rows]
        tot = cum[CHUNK - 1:CHUNK]
        kdec = k_ref[rows, :].astype(f32) * jnp.exp(tot - cum)
        dec = jnp.exp(tot)
        for h in range(GLA_HEADS):
            p, half = divmod(h, 2)
            pair = slice(p * LANE, (p + 1) * LANE)
            head = slice(h * GLA_DV, (h + 1) * GLA_DV)
            kd = jnp.where(half_mask[half], kdec[:, pair], 0.0).astype(bf16)
            u_t = _dot_tn(v_ref[rows, head], kd)
            st = st_ref[h] * dec[:, pair] + u_t
            st_ref[h] = st
            o = _dot_nt(q_ref[rows, pair], st.astype(bf16)) * (GLA_DK ** -0.5)
            o = _rms(o, ng_ref[...])
            g = og_ref[rows, head].astype(f32)
            o_ref[rows, head] = (o * (g / (1.0 + jnp.exp(-g)))).astype(o_ref.dtype)


def _gla(gq, gk, gv, glr, gog, gup, gb, ng, batch, seq):
    nt = seq // SEQ_TILE
    tile = lambda w: pl.BlockSpec((SEQ_TILE, w), lambda b, i: (b * nt + i, 0))
    full = lambda a: pl.BlockSpec(a.shape, lambda b, i: (0,) * a.ndim)
    return pl.pallas_call(
        _gla_kernel,
        grid=(batch, nt),
        in_specs=[tile(GLA_QK_W), tile(GLA_QK_W), tile(GLA_V_W), tile(LANE), tile(GLA_V_W),
                  full(gup), full(gb), full(ng)],
        out_specs=tile(GLA_V_W),
        out_shape=jax.ShapeDtypeStruct((batch * seq, GLA_V_W), bf16),
        scratch_shapes=[pltpu.VMEM((GLA_HEADS, GLA_DV, LANE), f32)],
        compiler_params=pltpu.CompilerParams(
            dimension_semantics=("arbitrary", "arbitrary"), vmem_limit_bytes=VMEM_LIMIT),
        name="gla",
    )(gq, gk, gv, glr, gog, gup, gb, ng)


def _head_norm(x, g, ones_bd):
    sq_hi, sq_lo = _split_bf16(x * x)
    ssq = _dot(sq_hi, ones_bd) + _dot(sq_lo, ones_bd)
    return x * lax.rsqrt(ssq * (1.0 / ATT_DH) + EPS) * g


def _att_kernel(q_ref, k_ref, v_ref, bias_ref, qg_ref, kg_ref, o_ref, qs_ref, kb_ref, vb_ref,
                sn_ref):
    t = q_ref.shape[0]
    n_pairs = ATT_W // LANE
    band = BAND_CHUNKS * CHUNK
    first = pl.program_id(1) == 0

    @pl.when(first)
    def _():
        kb_ref[0:t, :] = jnp.zeros((t, ATT_W), bf16)
        vb_ref[0:t, :] = jnp.zeros((t, 2 * ATT_W), bf16)

    @pl.when(jnp.logical_not(first))
    def _():
        kb_ref[0:t, :] = kb_ref[t:2 * t, :]
        vb_ref[0:t, :] = vb_ref[t:2 * t, :]

    lane = lax.broadcasted_iota(jnp.int32, (1, LANE), 1)
    lo = lane < ATT_DH
    r_i = lax.broadcasted_iota(jnp.int32, (LANE, LANE), 0)
    c_i = lax.broadcasted_iota(jnp.int32, (LANE, LANE), 1)
    ones_bd = jnp.where((r_i < ATT_DH) == (c_i < ATT_DH), 1.0, 0.0).astype(bf16)

    for p in range(n_pairs):
        pair = slice(p * LANE, (p + 1) * LANE)
        kb_ref[t:2 * t, pair] = _head_norm(k_ref[:, pair].astype(f32), kg_ref[...], ones_bd).astype(bf16)
        qn = _head_norm(q_ref[:, pair].astype(f32), qg_ref[...], ones_bd) * (ATT_DH ** -0.5)
        q_lo = jnp.where(lo, qn, 0.0).astype(bf16)
        q_hi = jnp.where(lo, 0.0, qn).astype(bf16)
        for c in range(t // CHUNK):
            rows = slice(c * CHUNK, (c + 1) * CHUNK)
            qs_ref[c * n_pairs + p, 0:CHUNK, :] = q_lo[rows]
            qs_ref[c * n_pairs + p, CHUNK:2 * CHUNK, :] = q_hi[rows]
        vb_ref[t:2 * t, 2 * p * LANE:(2 * p + 1) * LANE] = v_ref[:, pair]
        vb_ref[t:2 * t, (2 * p + 1) * LANE:(2 * p + 2) * LANE] = jnp.ones((t, LANE), bf16)

    colk = lax.broadcasted_iota(jnp.int32, (1, band), 1)

    def chunk_loop(masked):
        n_chunks = t // CHUNK

        def scores(c, p):
            k2 = kb_ref[pl.ds(pl.multiple_of(c * CHUNK, CHUNK), band), p * LANE:(p + 1) * LANE]
            return _dot_nt(qs_ref[c * n_pairs + p], k2)

        def chunk_body(c, carry):
            r0 = pl.multiple_of(c * CHUNK, CHUNK)
            s_next = sn_ref[...]
            for p in range(n_pairs):
                s = s_next + bias_ref[p]
                if p + 1 < n_pairs:
                    s_next = scores(c, p + 1)
                else:
                    sn_ref[...] = scores(jnp.minimum(c + 1, n_chunks - 1), 0)
                if masked:
                    s = jnp.where(colk >= t - c * CHUNK, s, MASK_VALUE)
                e = jnp.exp(s - jnp.max(s, axis=-1, keepdims=True))
                v2 = vb_ref[pl.ds(r0, band), 2 * p * LANE:(2 * p + 2) * LANE]
                pvl = _dot(e.astype(bf16), v2)
                pv = pvl[:, 0:LANE] / pvl[:, LANE:2 * LANE]
                o_ref[pl.ds(r0, CHUNK), p * LANE:(p + 1) * LANE] = jnp.where(
                    lo, pv[0:CHUNK], pv[CHUNK:2 * CHUNK]).astype(o_ref.dtype)
            return carry

        sn_ref[...] = scores(jnp.int32(0), 0)
        lax.fori_loop(0, n_chunks, chunk_body, 0)

    @pl.when(first)
    def _():
        chunk_loop(True)

    @pl.when(jnp.logical_not(first))
    def _():
        chunk_loop(False)


def _attention(aq, ak, av, bias, qg, kg, batch, seq):
    nt = seq // SEQ_TILE
    n_pairs = ATT_W // LANE
    tile = pl.BlockSpec((SEQ_TILE, ATT_W), lambda b, i: (b * nt + i, 0))
    full = lambda a: pl.BlockSpec(a.shape, lambda b, i: (0,) * a.ndim)
    bias2 = bias.reshape(n_pairs, 2 * CHUNK, BAND_CHUNKS * CHUNK)
    return pl.pallas_call(
        _att_kernel,
        grid=(batch, nt),
        in_specs=[tile, tile, tile, full(bias2), full(qg), full(kg)],
        out_specs=tile,
        out_shape=jax.ShapeDtypeStruct((batch * seq, ATT_W), bf16),
        scratch_shapes=[pltpu.VMEM((SEQ_TILE // CHUNK * n_pairs, 2 * CHUNK, LANE), bf16),
                        pltpu.VMEM((2 * SEQ_TILE, ATT_W), bf16),
                        pltpu.VMEM((2 * SEQ_TILE, 2 * ATT_W), bf16),
                        pltpu.VMEM((2 * CHUNK, BAND_CHUNKS * CHUNK), f32)],
        compiler_params=pltpu.CompilerParams(
            dimension_semantics=("arbitrary", "arbitrary"), vmem_limit_bytes=VMEM_LIMIT),
        name="attention",
    )(aq, ak, av, bias2, qg, kg)


def _mid_kernel(og_ref, oa_ref, x_ref, wo_ref, g2_ref, rwh_ref, rwl_ref, rb_ref,
                xm_ref, hp_ref, meta_ref, gate_ref, cnt_ref, carry_ref):
    t, d = x_ref.shape
    hw = d // 2

    @pl.when(pl.program_id(0) == 0)
    def _():
        carry_ref[...] = jnp.zeros_like(carry_ref)

    xm = x_ref[...] + _dot(og_ref[...], wo_ref[0:GLA_V_W, :]) + _dot(oa_ref[...], wo_ref[GLA_V_W:, :])
    xm_ref[...] = xm
    h2 = _rms(xm, g2_ref[...])
    h_hi, h_lo = _split_bf16(h2)
    a = pltpu.bitcast(h_hi[:, :hw].astype(f32), jnp.uint32)
    b = pltpu.bitcast(h_hi[:, hw:].astype(f32), jnp.uint32)
    hp_ref[...] = (a >> 16) | (b & jnp.uint32(0xFFFF0000))

    logits = (_dot(h_hi, rwh_ref[...]) + _dot(h_lo, rwh_ref[...]) + _dot(h_hi, rwl_ref[...])
              + rb_ref[...])
    lane = lax.broadcasted_iota(jnp.int32, (t, LANE), 1)
    lane_f = lane.astype(f32)
    l = jnp.where(lane < N_EXPERTS, logits, -jnp.inf)
    vals, idxs = [], []
    for _ in range(TOP_K):
        m = jnp.max(l, axis=-1, keepdims=True)
        ik = jnp.min(jnp.where(l == m, lane_f, float(LANE)), axis=-1, keepdims=True)
        vals.append(m)
        idxs.append(ik)
        l = jnp.where(lane_f == ik, -jnp.inf, l)
    es = [jnp.exp(v - vals[0]) for v in vals]
    den = es[0] + es[1] + es[2] + es[3]

    onehots = [lane_f == ik for ik in idxs]
    sel = jnp.zeros((t, LANE), f32)
    for oh in onehots:
        sel = sel + jnp.where(oh, 1.0, 0.0)
    row = lax.broadcasted_iota(jnp.int32, (t, t), 0)
    col = lax.broadcasted_iota(jnp.int32, (t, t), 1)
    tri = jnp.where(col < row, 1.0, 0.0).astype(bf16)
    prefix = _dot(tri, sel.astype(bf16)) + carry_ref[...]
    new_carry = carry_ref[...] + jnp.sum(sel, axis=0, keepdims=True)
    carry_ref[...] = new_carry
    cnt_ref[...] = new_carry

    meta = jnp.zeros((t, LANE), f32)
    gates = jnp.zeros((t, LANE), f32)
    for k in range(TOP_K):
        rank_k = jnp.sum(jnp.where(onehots[k], prefix, 0.0), axis=-1, keepdims=True)
        meta = jnp.where(lane == k, idxs[k], meta)
        meta = jnp.where(lane == TOP_K + k, rank_k, meta)
        gates = jnp.where(lane == k, es[k] / den, gates)
    meta_ref[...] = meta.astype(jnp.int32)
    gate_ref[...] = gates


def _mid(o_gla, o_att, x2, wo, g2, rwh, rwl, rb):
    n, d = x2.shape
    tile = lambda w: pl.BlockSpec((ROW_TILE, w), lambda i: (i, 0))
    full = lambda a: pl.BlockSpec(a.shape, lambda i: (0,) * a.ndim)
    return pl.pallas_call(
        _mid_kernel,
        grid=(n // ROW_TILE,),
        in_specs=[tile(GLA_V_W), tile(ATT_W), tile(d), full(wo), full(g2), full(rwh), full(rwl),
                  full(rb)],
        out_specs=[tile(d), tile(d // 2), tile(LANE), tile(LANE),
                   pl.BlockSpec((1, LANE), lambda i: (0, 0))],
        out_shape=[jax.ShapeDtypeStruct((n, d), f32),
                   jax.ShapeDtypeStruct((n, d // 2), jnp.uint32),
                   jax.ShapeDtypeStruct((n, LANE), jnp.int32),
                   jax.ShapeDtypeStruct((n, LANE), f32),
                   jax.ShapeDtypeStruct((1, LANE), f32)],
        scratch_shapes=[pltpu.VMEM((1, LANE), f32)],
        compiler_params=pltpu.CompilerParams(
            dimension_semantics=("arbitrary",), vmem_limit_bytes=VMEM_LIMIT),
        name="mid",
    )(o_gla, o_att, x2, wo, g2, rwh, rwl, rb)


def _dispatch_kernel(dest_ref, h_ref, xb_ref, sem):
    t = h_ref.shape[0]

    def row_copy(r, j):
        d = dest_ref[0, 0, r * TOP_K + j]
        return pltpu.make_async_copy(h_ref.at[pl.ds(r, 1)], xb_ref.at[pl.ds(d, 1)], sem)

    def issue(r, c):
        for j in range(TOP_K):
            row_copy(r, j).start()
        return c

    def drain(r, c):
        for j in range(TOP_K):
            row_copy(r, j).wait()
        return c

    lax.fori_loop(0, t, issue, 0)
    lax.fori_loop(0, t, drain, 0)


def _dispatch(dest3, hp, n_rows):
    n, hw = hp.shape
    return pl.pallas_call(
        _dispatch_kernel,
        grid=(n // ROW_TILE,),
        in_specs=[pl.BlockSpec((1, 1, ROW_TILE * TOP_K), lambda i: (i, 0, 0),
                               memory_space=pltpu.SMEM),
                  pl.BlockSpec((ROW_TILE, hw), lambda i: (i, 0))],
        out_specs=pl.BlockSpec(memory_space=pl.ANY),
        out_shape=jax.ShapeDtypeStruct((n_rows, hw), jnp.uint32),
        scratch_shapes=[pltpu.SemaphoreType.DMA],
        compiler_params=pltpu.CompilerParams(
            dimension_semantics=("arbitrary",), vmem_limit_bytes=VMEM_LIMIT),
        name="dispatch",
    )(dest3, hp)


def _expert_kernel(be_ref, nv_ref, nu_ref, x_ref, wi_ref, bi_ref, wo_ref, bo_ref, y_ref,
                   wib_ref, wob_ref):
    b = pl.program_id(0)
    blk, hw = x_ref.shape
    dff = wo_ref.shape[1]

    @pl.when(b < nu_ref[0])
    def _():
        changed = jnp.logical_or(b == 0, be_ref[b] != be_ref[jnp.maximum(b - 1, 0)])

        @pl.when(changed)
        def _():
            wib_ref[...] = wi_ref[0].astype(bf16)
            wob_ref[...] = wo_ref[0].astype(bf16)

        rows = lax.broadcasted_iota(jnp.int32, (blk, hw), 0)
        w = jnp.where(rows < nv_ref[b], x_ref[...], jnp.uint32(0))
        xa = pltpu.bitcast(w << 16, f32).astype(bf16)
        xb = pltpu.bitcast(w & jnp.uint32(0xFFFF0000), f32).astype(bf16)
        hc = _dot(xa, wib_ref[0:hw, :]) + _dot(xb, wib_ref[hw:, :]) + bi_ref[0]
        glu = jnp.minimum(hc[:, :dff], SWIGLU_LIMIT)
        lin = jnp.clip(hc[:, dff:], -SWIGLU_LIMIT, SWIGLU_LIMIT)
        act = glu * (1.0 / (1.0 + jnp.exp(-SWIGLU_ALPHA * glu))) * (lin + 1.0)
        y_ref[...] = _dot(act.astype(bf16), wob_ref[...]) + bo_ref[0]


def _experts(block_expert, block_valid, n_used, xb, w_in, b_in, w_out, b_out):
    n_rows, hw = xb.shape
    e, d, dff2 = w_in.shape
    dff = w_out.shape[1]
    n_blocks = n_rows // MOE_BLOCK
    grid_spec = pltpu.PrefetchScalarGridSpec(
        num_scalar_prefetch=3,
        grid=(n_blocks,),
        in_specs=[
            pl.BlockSpec((MOE_BLOCK, hw), lambda b, be, nv, nu: (b, 0)),
            pl.BlockSpec((1, d, dff2), lambda b, be, nv, nu: (be[b], 0, 0)),
            pl.BlockSpec((1, 1, dff2), lambda b, be, nv, nu: (be[b], 0, 0)),
            pl.BlockSpec((1, dff, d), lambda b, be, nv, nu: (be[b], 0, 0)),
            pl.BlockSpec((1, 1, d), lambda b, be, nv, nu: (be[b], 0, 0)),
        ],
        out_specs=pl.BlockSpec((MOE_BLOCK, d), lambda b, be, nv, nu: (b, 0)),
        scratch_shapes=[pltpu.VMEM((d, dff2), bf16), pltpu.VMEM((dff, d), bf16)],
    )
    return pl.pallas_call(
        _expert_kernel,
        grid_spec=grid_spec,
        out_shape=jax.ShapeDtypeStruct((n_rows, d), f32),
        compiler_params=pltpu.CompilerParams(
            dimension_semantics=("arbitrary",), vmem_limit_bytes=VMEM_LIMIT),
        name="experts",
    )(block_expert, block_valid, n_used, xb, w_in, b_in.reshape(e, 1, dff2), w_out,
      b_out.reshape(e, 1, d))


def _combine_kernel(dest_ref, gate_ref, xm_ref, y_ref, o_ref, yb_ref, sem):
    t = xm_ref.shape[0]

    def row_copy(r, j):
        d = dest_ref[0, 0, r * TOP_K + j]
        return pltpu.make_async_copy(y_ref.at[pl.ds(d, 1)], yb_ref.at[j, pl.ds(r, 1)], sem)

    def issue(r, c):
        for j in range(TOP_K):
            row_copy(r, j).start()
        return c

    def drain(r, c):
        for j in range(TOP_K):
            row_copy(r, j).wait()
        return c

    lax.fori_loop(0, t, issue, 0)
    lax.fori_loop(0, t, drain, 0)
    acc = xm_ref[...]
    for j in range(TOP_K):
        acc = acc + gate_ref[:, j:j + 1] * yb_ref[j]
    o_ref[...] = acc


def _combine(dest3, gates, xm, y):
    n, d = xm.shape
    t = COMBINE_TILE
    return pl.pallas_call(
        _combine_kernel,
        grid=(n // t,),
        in_specs=[pl.BlockSpec((1, 1, t * TOP_K), lambda i: (i, 0, 0), memory_space=pltpu.SMEM),
                  pl.BlockSpec((t, LANE), lambda i: (i, 0)),
                  pl.BlockSpec((t, d), lambda i: (i, 0)),
                  pl.BlockSpec(memory_space=pl.ANY)],
        out_specs=pl.BlockSpec((t, d), lambda i: (i, 0)),
        out_shape=jax.ShapeDtypeStruct((n, d), f32),
        scratch_shapes=[pltpu.VMEM((TOP_K, t, d), f32), pltpu.SemaphoreType.DMA],
        compiler_params=pltpu.CompilerParams(
            dimension_semantics=("arbitrary",), vmem_limit_bytes=VMEM_LIMIT),
        name="combine",
    )(dest3, gates, xm, y)


def _rel_bias_table(rel_bias):
    rel = (N_BACK_CHUNKS * CHUNK + np.arange(CHUNK)[:, None]) - np.arange(BAND_CHUNKS * CHUNK)[None, :]
    idx = np.clip(rel, -REL_CLIP, REL_CLIP) + REL_CLIP
    return rel_bias[:, idx].astype(f32)


def _routing_tables(counts, meta, n):
    n_blocks = -(-(n * TOP_K) // MOE_BLOCK) + N_EXPERTS
    padded = ((counts + MOE_BLOCK - 1) // MOE_BLOCK) * MOE_BLOCK
    pend = jnp.cumsum(padded)
    pstart = pend - padded
    idx = meta[:, :TOP_K]
    rank = meta[:, TOP_K:2 * TOP_K]
    experts = jnp.arange(N_EXPERTS, dtype=jnp.int32)
    dest = rank + jnp.sum(jnp.where(idx[:, :, None] == experts, pstart, 0), axis=-1)
    blk0 = jnp.arange(n_blocks, dtype=jnp.int32) * MOE_BLOCK
    n_used = (pend[-1] // MOE_BLOCK).astype(jnp.int32)
    be = jnp.minimum(jnp.sum(pend[None, :] <= blk0[:, None], axis=1), N_EXPERTS - 1).astype(jnp.int32)
    last = jnp.sum(jnp.where(jnp.arange(n_blocks) == n_used - 1, be, 0))
    be = jnp.where(jnp.arange(n_blocks) < n_used, be, last)
    onehot_be = be[:, None] == experts
    be_count = jnp.sum(jnp.where(onehot_be, counts, 0), axis=1)
    be_start = jnp.sum(jnp.where(onehot_be, pstart, 0), axis=1)
    valid = jnp.clip(be_count - (blk0 - be_start), 0, MOE_BLOCK).astype(jnp.int32)
    return dest.astype(jnp.int32), be, valid, n_used.reshape(1), n_blocks


def _layer(x, norm1_g, w_in, gate_up, gate_bias, gla_norm_g, q_norm_g, k_norm_g, rel_bias, w_out,
           norm2_g, router_w, router_b, moe_w_in, moe_b_in, moe_w_out, moe_b_out):
    batch, seq, d = x.shape
    n = batch * seq
    x2 = x.reshape(n, d)

    pieces = jnp.split(w_in, np.cumsum(IN_SIZES)[:-1].tolist(), axis=-1)
    pieces[3] = jnp.pad(pieces[3], ((0, 0), (0, LANE - GLA_GATE_RANK)))
    widths = [p.shape[-1] for p in pieces]
    w_all = jnp.concatenate(pieces, axis=-1).astype(bf16)
    gq, gk, gv, glr, gog, aq, ak, av = _inproj(x2, norm1_g.reshape(1, d), w_all, widths)

    gup = jnp.pad(gate_up, ((0, LANE - GLA_GATE_RANK), (0, 0))).astype(bf16)
    o_gla = _gla(gq, gk, gv, glr, gog, gup, gate_bias.reshape(1, -1), gla_norm_g.reshape(1, -1),
                 batch, seq)
    tile2 = lambda g: jnp.tile(g.reshape(1, -1), (1, LANE // ATT_DH))
    o_att = _attention(aq, ak, av, _rel_bias_table(rel_bias), tile2(q_norm_g), tile2(k_norm_g),
                       batch, seq)

    rw = jnp.pad(router_w, ((0, 0), (0, LANE - N_EXPERTS)))
    rw_hi = rw.astype(bf16)
    rw_lo = (rw - rw_hi.astype(f32)).astype(bf16)
    rb = jnp.pad(router_b, (0, LANE - N_EXPERTS)).reshape(1, LANE)
    xm, hp, meta, gates, cnt = _mid(o_gla, o_att, x2, w_out.astype(bf16), norm2_g.reshape(1, d),
                                    rw_hi, rw_lo, rb)

    counts = cnt[0, :N_EXPERTS].astype(jnp.int32)
    dest, be, valid, n_used, n_blocks = _routing_tables(counts, meta, n)
    xb = _dispatch(dest.reshape(n // ROW_TILE, 1, ROW_TILE * TOP_K), hp, n_blocks * MOE_BLOCK)
    y = _experts(be, valid, n_used, xb, moe_w_in, moe_b_in, moe_w_out, moe_b_out)
    out = _combine(dest.reshape(n // COMBINE_TILE, 1, COMBINE_TILE * TOP_K), gates, xm, y)
    return out.reshape(batch, seq, d)


def kernel(x, norm1_g, w_in, gla_gate_up, gla_gate_bias, gla_norm_g, q_norm_g, k_norm_g, rel_bias, w_out, norm2_g, router_w, router_b, moe_w_in, moe_b_in, moe_w_out, moe_b_out):
    for l in range(norm1_g.shape[0]):
        x = _layer(x, norm1_g[l], w_in[l], gla_gate_up[l], gla_gate_bias[l], gla_norm_g[l],
                   q_norm_g[l], k_norm_g[l], rel_bias[l], w_out[l], norm2_g[l], router_w[l],
                   router_b[l], moe_w_in[l], moe_b_in[l], moe_w_out[l], moe_b_out[l])
    return x
```

```python
import functools

import numpy as np
import jax
import jax.numpy as jnp
from jax import lax
from jax.experimental import pallas as pl
from jax.experimental.pallas import tpu as pltpu

CHUNK = 64
EPS = 1e-6
GLA_HEADS = 4
GLA_DK = 64
GLA_DV = 128
GLA_GATE_RANK = 16
GLA_GATE_TAU = 16.0
ATT_HEADS = 8
ATT_DH = 64
N_BACK_CHUNKS = 8
BAND_CHUNKS = N_BACK_CHUNKS + 1
REL_CLIP = 256
MASK_VALUE = -1e30
N_EXPERTS = 32
TOP_K = 4
SWIGLU_ALPHA = 1.702
SWIGLU_LIMIT = 7.0
MOE_BLOCK = 256

LANE = 128
GLA_QK_W = GLA_HEADS * GLA_DK
GLA_V_W = GLA_HEADS * GLA_DV
ATT_W = ATT_HEADS * ATT_DH
IN_SIZES = (GLA_QK_W, GLA_QK_W, GLA_V_W, GLA_GATE_RANK, GLA_V_W, ATT_W, ATT_W, ATT_W)
SEQ_TILE = N_BACK_CHUNKS * CHUNK
ROW_TILE = 512
MOE_TILE = 256
DMA_ROWS = 8
LOCAL_PIECES = 160
LOCAL_ROWS = LOCAL_PIECES * DMA_ROWS
VMEM_LIMIT = 48 * 1024 * 1024

f32 = jnp.float32
bf16 = jnp.bfloat16


def _rms(x, g):
    return x * lax.rsqrt(jnp.mean(x * x, axis=-1, keepdims=True) + EPS) * g


def _dot(a, b):
    return jnp.dot(a, b, preferred_element_type=f32)


def _dot_nt(a, b):
    return lax.dot_general(a, b, (((1,), (1,)), ((), ())), preferred_element_type=f32)


def _dot_tn(a, b):
    return lax.dot_general(a, b, (((0,), (0,)), ((), ())), preferred_element_type=f32)


def _split_bf16(x):
    hi = x.astype(bf16)
    lo = (x - hi.astype(f32)).astype(bf16)
    return hi, lo


def _inproj_kernel(x_ref, g_ref, w_ref, *out_refs):
    h = _rms(x_ref[...], g_ref[...]).astype(bf16)
    off = 0
    for o_ref in out_refs:
        w = o_ref.shape[-1]
        o_ref[...] = _dot(h, w_ref[:, off:off + w]).astype(o_ref.dtype)
        off += w


def _inproj(x2, g, w, widths):
    n, d = x2.shape
    return pl.pallas_call(
        _inproj_kernel,
        grid=(n // ROW_TILE,),
        in_specs=[
            pl.BlockSpec((ROW_TILE, d), lambda i: (i, 0)),
            pl.BlockSpec((1, d), lambda i: (0, 0)),
            pl.BlockSpec(w.shape, lambda i: (0, 0)),
        ],
        out_specs=[pl.BlockSpec((ROW_TILE, wd), lambda i: (i, 0)) for wd in widths],
        out_shape=[jax.ShapeDtypeStruct((n, wd), bf16) for wd in widths],
        compiler_params=pltpu.CompilerParams(
            dimension_semantics=("arbitrary",), vmem_limit_bytes=VMEM_LIMIT),
        name="inproj",
    )(x2, g, w)


def _gla_kernel(q_ref, k_ref, v_ref, lr_ref, og_ref, gup_ref, gb_ref, ng_ref, o_ref, st_ref):
    t = q_ref.shape[0]

    @pl.when(pl.program_id(1) == 0)
    def _():
        st_ref[...] = jnp.zeros_like(st_ref)

    z = _dot(lr_ref[...], gup_ref[...]) + gb_ref[...]
    log_a = (jnp.minimum(z, 0.0) - jnp.log1p(jnp.exp(-jnp.abs(z)))) * (1.0 / GLA_GATE_TAU)
    row = lax.broadcasted_iota(jnp.int32, (t, t), 0)
    col = lax.broadcasted_iota(jnp.int32, (t, t), 1)
    tri = jnp.where((col <= row) & ((col // CHUNK) == (row // CHUNK)), 1.0, 0.0).astype(bf16)
    la_hi, la_lo = _split_bf16(log_a)
    cum_all = _dot(tri, la_hi) + _dot(tri, la_lo)
    lane = lax.broadcasted_iota(jnp.int32, (1, LANE), 1)
    half_mask = (lane < GLA_DK, lane >= GLA_DK)

    for c in range(t // CHUNK):
        rows = slice(c * CHUNK, (c + 1) * CHUNK)
        cum = cum_all[rows]
        tot = cum[CHUNK - 1:CHUNK]
        kdec = k_ref[rows, :].astype(f32) * jnp.exp(tot - cum)
        dec = jnp.exp(tot)
        for h in range(GLA_HEADS):
            p, half = divmod(h, 2)
            pair = slice(p * LANE, (p + 1) * LANE)
            head = slice(h * GLA_DV, (h + 1) * GLA_DV)
            kd = jnp.where(half_mask[half], kdec[:, pair], 0.0).astype(bf16)
            u_t = _dot_tn(v_ref[rows, head], kd)
            st = st_ref[h] * dec[:, pair] + u_t
            st_ref[h] = st
            o = _dot_nt(q_ref[rows, pair], st.astype(bf16)) * (GLA_DK ** -0.5)
            o = _rms(o, ng_ref[...])
            g = og_ref[rows, head].astype(f32)
            o_ref[rows, head] = (o * (g / (1.0 + jnp.exp(-g)))).astype(o_ref.dtype)


def _gla(gq, gk, gv, glr, gog, gup, gb, ng, batch, seq):
    nt = seq // SEQ_TILE
    tile = lambda w: pl.BlockSpec((SEQ_TILE, w), lambda b, i: (b * nt + i, 0))
    full = lambda a: pl.BlockSpec(a.shape, lambda b, i: (0,) * a.ndim)
    return pl.pallas_call(
        _gla_kernel,
        grid=(batch, nt),
        in_specs=[tile(GLA_QK_W), tile(GLA_QK_W), tile(GLA_V_W), tile(LANE), tile(GLA_V_W),
                  full(gup), full(gb), full(ng)],
        out_specs=tile(GLA_V_W),
        out_shape=jax.ShapeDtypeStruct((batch * seq, GLA_V_W), bf16),
        scratch_shapes=[pltpu.VMEM((GLA_HEADS, GLA_DV, LANE), f32)],
        compiler_params=pltpu.CompilerParams(
            dimension_semantics=("arbitrary", "arbitrary"), vmem_limit_bytes=VMEM_LIMIT),
        name="gla",
    )(gq, gk, gv, glr, gog, gup, gb, ng)


def _head_norm(x, g, ones_bd):
    sq_hi, sq_lo = _split_bf16(x * x)
    ssq = _dot(sq_hi, ones_bd) + _dot(sq_lo, ones_bd)
    return x * lax.rsqrt(ssq * (1.0 / ATT_DH) + EPS) * g


def _att_kernel(q_ref, k_ref, v_ref, bias_ref, qg_ref, kg_ref, o_ref, qs_ref, kb_ref, vb_ref,
                sn_ref):
    t = q_ref.shape[0]
    n_pairs = ATT_W // LANE
    band = BAND_CHUNKS * CHUNK
    first = pl.program_id(1) == 0

    @pl.when(first)
    def _():
        kb_ref[0:t, :] = jnp.zeros((t, ATT_W), bf16)
        vb_ref[0:t, :] = jnp.zeros((t, 2 * ATT_W), bf16)

    @pl.when(jnp.logical_not(first))
    def _():
        kb_ref[0:t, :] = kb_ref[t:2 * t, :]
        vb_ref[0:t, :] = vb_ref[t:2 * t, :]

    lane = lax.broadcasted_iota(jnp.int32, (1, LANE), 1)
    lo = lane < ATT_DH
    r_i = lax.broadcasted_iota(jnp.int32, (LANE, LANE), 0)
    c_i = lax.broadcasted_iota(jnp.int32, (LANE, LANE), 1)
    ones_bd = jnp.where((r_i < ATT_DH) == (c_i < ATT_DH), 1.0, 0.0).astype(bf16)

    for p in range(n_pairs):
        pair = slice(p * LANE, (p + 1) * LANE)
        kb_ref[t:2 * t, pair] = _head_norm(k_ref[:, pair].astype(f32), kg_ref[...], ones_bd).astype(bf16)
        qn = _head_norm(q_ref[:, pair].astype(f32), qg_ref[...], ones_bd) * (ATT_DH ** -0.5)
        q_lo = jnp.where(lo, qn, 0.0).astype(bf16)
        q_hi = jnp.where(lo, 0.0, qn).astype(bf16)
        for c in range(t // CHUNK):
            rows = slice(c * CHUNK, (c + 1) * CHUNK)
            qs_ref[c * n_pairs + p, 0:CHUNK, :] = q_lo[rows]
            qs_ref[c * n_pairs + p, CHUNK:2 * CHUNK, :] = q_hi[rows]
        vb_ref[t:2 * t, 2 * p * LANE:(2 * p + 1) * LANE] = v_ref[:, pair]
        vb_ref[t:2 * t, (2 * p + 1) * LANE:(2 * p + 2) * LANE] = jnp.ones((t, LANE), bf16)

    colk = lax.broadcasted_iota(jnp.int32, (1, band), 1)

    def chunk_loop(masked):
        n_chunks = t // CHUNK

        def scores(c, p):
            k2 = kb_ref[pl.ds(pl.multiple_of(c * CHUNK, CHUNK), band), p * LANE:(p + 1) * LANE]
            return _dot_nt(qs_ref[c * n_pairs + p], k2)

        def chunk_body(c, carry):
            r0 = pl.multiple_of(c * CHUNK, CHUNK)
            s_next = sn_ref[...]
            for p in range(n_pairs):
                s = s_next + bias_ref[p]
                if p + 1 < n_pairs:
                    s_next = scores(c, p + 1)
                else:
                    sn_ref[...] = scores(jnp.minimum(c + 1, n_chunks - 1), 0)
                if masked:
                    s = jnp.where(colk >= t - c * CHUNK, s, MASK_VALUE)
                e = jnp.exp(s - jnp.max(s, axis=-1, keepdims=True))
                v2 = vb_ref[pl.ds(r0, band), 2 * p * LANE:(2 * p + 2) * LANE]
                pvl = _dot(e.astype(bf16), v2)
                pv = pvl[:, 0:LANE] / pvl[:, LANE:2 * LANE]
                o_ref[pl.ds(r0, CHUNK), p * LANE:(p + 1) * LANE] = jnp.where(
                    lo, pv[0:CHUNK], pv[CHUNK:2 * CHUNK]).astype(o_ref.dtype)
            return carry

        sn_ref[...] = scores(jnp.int32(0), 0)
        lax.fori_loop(0, n_chunks, chunk_body, 0)

    @pl.when(first)
    def _():
        chunk_loop(True)

    @pl.when(jnp.logical_not(first))
    def _():
        chunk_loop(False)


def _attention(aq, ak, av, bias, qg, kg, batch, seq):
    nt = seq // SEQ_TILE
    n_pairs = ATT_W // LANE
    tile = pl.BlockSpec((SEQ_TILE, ATT_W), lambda b, i: (b * nt + i, 0))
    full = lambda a: pl.BlockSpec(a.shape, lambda b, i: (0,) * a.ndim)
    bias2 = bias.reshape(n_pairs, 2 * CHUNK, BAND_CHUNKS * CHUNK)
    return pl.pallas_call(
        _att_kernel,
        grid=(batch, nt),
        in_specs=[tile, tile, tile, full(bias2), full(qg), full(kg)],
        out_specs=tile,
        out_shape=jax.ShapeDtypeStruct((batch * seq, ATT_W), bf16),
        scratch_shapes=[pltpu.VMEM((SEQ_TILE // CHUNK * n_pairs, 2 * CHUNK, LANE), bf16),
                        pltpu.VMEM((2 * SEQ_TILE, ATT_W), bf16),
                        pltpu.VMEM((2 * SEQ_TILE, 2 * ATT_W), bf16),
                        pltpu.VMEM((2 * CHUNK, BAND_CHUNKS * CHUNK), f32)],
        compiler_params=pltpu.CompilerParams(
            dimension_semantics=("arbitrary", "arbitrary"), vmem_limit_bytes=VMEM_LIMIT),
        name="attention",
    )(aq, ak, av, bias2, qg, kg)


def _mid_kernel(og_ref, oa_ref, x_ref, wo_ref, g2_ref, rwh_ref, rwl_ref, rb_ref,
                xm_ref, h_ref, meta_ref, gate_ref, cnt_ref):
    t, d = x_ref.shape
    xm = x_ref[...] + _dot(og_ref[...], wo_ref[0:GLA_V_W, :]) + _dot(oa_ref[...], wo_ref[GLA_V_W:, :])
    xm_ref[...] = xm
    h2 = _rms(xm, g2_ref[...])
    h_hi, h_lo = _split_bf16(h2)
    h_ref[...] = h_hi

    logits = (_dot(h_hi, rwh_ref[...]) + _dot(h_lo, rwh_ref[...]) + _dot(h_hi, rwl_ref[...])
              + rb_ref[...])
    lane = lax.broadcasted_iota(jnp.int32, (t, LANE), 1)
    lane_f = lane.astype(f32)
    l = jnp.where(lane < N_EXPERTS, logits, -jnp.inf)
    vals, idxs = [], []
    for _ in range(TOP_K):
        m = jnp.max(l, axis=-1, keepdims=True)
        ik = jnp.min(jnp.where(l == m, lane_f, float(LANE)), axis=-1, keepdims=True)
        vals.append(m)
        idxs.append(ik)
        l = jnp.where(lane_f == ik, -jnp.inf, l)
    es = [jnp.exp(v - vals[0]) for v in vals]
    den = es[0] + es[1] + es[2] + es[3]

    onehots = [lane_f == ik for ik in idxs]
    sel = jnp.zeros((t, LANE), f32)
    for oh in onehots:
        sel = sel + jnp.where(oh, 1.0, 0.0)
    row = lax.broadcasted_iota(jnp.int32, (t, t), 0)
    col = lax.broadcasted_iota(jnp.int32, (t, t), 1)
    tri = jnp.where((col < row) & ((col // MOE_TILE) == (row // MOE_TILE)), 1.0, 0.0).astype(bf16)
    prefix = _dot(tri, sel.astype(bf16))
    for s in range(t // MOE_TILE):
        cnt_ref[0, s:s + 1, :] = jnp.sum(sel[s * MOE_TILE:(s + 1) * MOE_TILE], axis=0, keepdims=True)

    meta = jnp.zeros((t, LANE), f32)
    gates = jnp.zeros((t, LANE), f32)
    for k in range(TOP_K):
        rank_k = jnp.sum(jnp.where(onehots[k], prefix, 0.0), axis=-1, keepdims=True)
        meta = jnp.where(lane == k, idxs[k], meta)
        meta = jnp.where(lane == TOP_K + k, rank_k, meta)
        gates = jnp.where(lane == k, es[k] / den, gates)
    meta_ref[...] = meta.astype(jnp.int32)
    gate_ref[...] = gates


def _mid(o_gla, o_att, x2, wo, g2, rwh, rwl, rb):
    n, d = x2.shape
    sub = ROW_TILE // MOE_TILE
    tile = lambda w: pl.BlockSpec((ROW_TILE, w), lambda i: (i, 0))
    full = lambda a: pl.BlockSpec(a.shape, lambda i: (0,) * a.ndim)
    return pl.pallas_call(
        _mid_kernel,
        grid=(n // ROW_TILE,),
        in_specs=[tile(GLA_V_W), tile(ATT_W), tile(d), full(wo), full(g2), full(rwh), full(rwl),
                  full(rb)],
        out_specs=[tile(d), tile(d), tile(LANE), tile(LANE),
                   pl.BlockSpec((1, sub, LANE), lambda i: (i, 0, 0))],
        out_shape=[jax.ShapeDtypeStruct((n, d), f32),
                   jax.ShapeDtypeStruct((n, d), bf16),
                   jax.ShapeDtypeStruct((n, LANE), jnp.int32),
                   jax.ShapeDtypeStruct((n, LANE), f32),
                   jax.ShapeDtypeStruct((n // ROW_TILE, sub, LANE), f32)],
        compiler_params=pltpu.CompilerParams(
            dimension_semantics=("arbitrary",), vmem_limit_bytes=VMEM_LIMIT),
        name="mid",
    )(o_gla, o_att, x2, wo, g2, rwh, rwl, rb)


def _slot_matrix(meta_ref, ls_ref, values):
    meta = meta_ref[...]
    t = meta.shape[0]
    lane = lax.broadcasted_iota(jnp.int32, (t, LANE), 1)
    starts = ls_ref[0].astype(f32)
    col = lax.broadcasted_iota(jnp.int32, (1, LOCAL_ROWS), 1)
    out = jnp.zeros((t, LOCAL_ROWS), f32)
    for k in range(TOP_K):
        start_k = jnp.sum(jnp.where(lane == meta[:, k:k + 1], starts, 0.0), axis=-1, keepdims=True)
        slot_k = start_k.astype(jnp.int32) + meta[:, TOP_K + k:TOP_K + k + 1]
        out = jnp.where(col == slot_k, values[k], out)
    return out


def _pack_bf16_pairs(a, b):
    return (pltpu.bitcast(a, jnp.uint32) >> 16) | (pltpu.bitcast(b, jnp.uint32) & jnp.uint32(0xFFFF0000))


def _unpack_bf16_pairs(w):
    a = pltpu.bitcast(w << 16, f32).astype(bf16)
    b = pltpu.bitcast(w & jnp.uint32(0xFFFF0000), f32).astype(bf16)
    return a, b


def _piece_copy(src_ref, dst_ref, src_row, dst_row, sem):
    return pltpu.make_async_copy(src_ref.at[pl.ds(pl.multiple_of(src_row, DMA_ROWS), DMA_ROWS)],
                                 dst_ref.at[pl.ds(pl.multiple_of(dst_row, DMA_ROWS), DMA_ROWS)], sem)


def _zero_fill_padding(pad_start_ref, pad_pieces_ref, n_used_ref, xb_ref, z_ref, sem):
    z_ref[...] = jnp.zeros_like(z_ref)
    n_blocks = xb_ref.shape[0] // MOE_BLOCK

    def tail_piece(e, q):
        return _piece_copy(z_ref, xb_ref, 0, pad_start_ref[e] + q * DMA_ROWS, sem)

    def block_copy(b):
        row = pl.multiple_of(b * MOE_BLOCK, MOE_BLOCK)
        return pltpu.make_async_copy(z_ref, xb_ref.at[pl.ds(row, MOE_BLOCK)], sem)

    def each_tail_piece(fn):
        def per_expert(e, c):
            lax.fori_loop(0, pad_pieces_ref[e], lambda q, cc: (fn(tail_piece(e, q)), cc)[1], 0)
            return c
        lax.fori_loop(0, N_EXPERTS, per_expert, 0)

    def each_block(fn):
        lax.fori_loop(n_used_ref[0], n_blocks, lambda b, c: (fn(block_copy(b)), c)[1], 0)

    each_tail_piece(lambda cp: cp.start())
    each_block(lambda cp: cp.start())
    each_tail_piece(lambda cp: cp.wait())
    each_block(lambda cp: cp.wait())


def _dispatch_kernel(nq_ref, pad_start_ref, pad_pieces_ref, n_used_ref, dst_ref, meta_ref, ls_ref,
                     h_ref, xb_ref, l_ref, z_ref, sem):
    i = pl.program_id(0)
    hw = l_ref.shape[1]
    perm = _slot_matrix(meta_ref, ls_ref, [1.0] * TOP_K).T.astype(bf16)
    packed = _pack_bf16_pairs(_dot(perm, h_ref[:, 0:hw]), _dot(perm, h_ref[:, hw:]))

    def piece(q):
        return _piece_copy(l_ref, xb_ref, q * DMA_ROWS, dst_ref[0, 0, q], sem.at[0])

    def drain(n):
        lax.fori_loop(0, n, lambda q, c: (piece(q).wait(), c)[1], 0)

    @pl.when(i > 0)
    def _():
        drain(nq_ref[jnp.maximum(i - 1, 0)])

    l_ref[...] = packed
    lax.fori_loop(0, nq_ref[i], lambda q, c: (piece(q).start(), c)[1], 0)

    @pl.when(i == pl.num_programs(0) - 1)
    def _():
        _zero_fill_padding(pad_start_ref, pad_pieces_ref, n_used_ref, xb_ref, z_ref, sem.at[1])
        drain(nq_ref[i])


def _dispatch(nq, pad_start, pad_pieces, n_used, dst, meta, lstart, h, n_rows):
    n, d = h.shape
    hw = d // 2
    t = MOE_TILE
    grid_spec = pltpu.PrefetchScalarGridSpec(
        num_scalar_prefetch=4,
        grid=(n // t,),
        in_specs=[pl.BlockSpec((1, 1, LOCAL_PIECES), lambda i, *_: (i, 0, 0), memory_space=pltpu.SMEM),
                  pl.BlockSpec((t, LANE), lambda i, *_: (i, 0)),
                  pl.BlockSpec((1, 1, LANE), lambda i, *_: (i, 0, 0)),
                  pl.BlockSpec((t, d), lambda i, *_: (i, 0))],
        out_specs=pl.BlockSpec(memory_space=pl.ANY),
        scratch_shapes=[pltpu.VMEM((LOCAL_ROWS, hw), jnp.uint32),
                        pltpu.VMEM((MOE_BLOCK, hw), jnp.uint32),
                        pltpu.SemaphoreType.DMA((2,))],
    )
    return pl.pallas_call(
        _dispatch_kernel,
        grid_spec=grid_spec,
        out_shape=jax.ShapeDtypeStruct((n_rows, hw), jnp.uint32),
        compiler_params=pltpu.CompilerParams(
            dimension_semantics=("arbitrary",), vmem_limit_bytes=VMEM_LIMIT),
        name="dispatch",
    )(nq, pad_start, pad_pieces, n_used, dst, meta, lstart, h)


def _expert_kernel(be_ref, nv_ref, nu_ref, x_ref, wi_ref, bi_ref, wo_ref, bo_ref, y_ref,
                   wib_ref, wob_ref):
    b = pl.program_id(0)
    blk, hw = x_ref.shape
    dff = wo_ref.shape[1]

    @pl.when(b < nu_ref[0])
    def _():
        changed = jnp.logical_or(b == 0, be_ref[b] != be_ref[jnp.maximum(b - 1, 0)])

        @pl.when(changed)
        def _():
            wib_ref[...] = wi_ref[0].astype(bf16)
            wob_ref[...] = wo_ref[0].astype(bf16)

        rows = lax.broadcasted_iota(jnp.int32, (blk, hw), 0)
        w = jnp.where(rows < nv_ref[b], x_ref[...], jnp.uint32(0))
        xa, xb = _unpack_bf16_pairs(w)
        hc = _dot(xa, wib_ref[0:hw, :]) + _dot(xb, wib_ref[hw:, :]) + bi_ref[0]
        glu = jnp.minimum(hc[:, :dff], SWIGLU_LIMIT)
        lin = jnp.clip(hc[:, dff:], -SWIGLU_LIMIT, SWIGLU_LIMIT)
        act = glu * (1.0 / (1.0 + jnp.exp(-SWIGLU_ALPHA * glu))) * (lin + 1.0)
        y = (_dot(act.astype(bf16), wob_ref[...]) + bo_ref[0]).astype(bf16).astype(f32)
        y_ref[...] = _pack_bf16_pairs(y[:, 0:hw], y[:, hw:])

    @pl.when(b >= nu_ref[0])
    def _():
        y_ref[...] = jnp.zeros_like(y_ref)


def _experts(block_expert, block_valid, n_used, xb, w_in, b_in, w_out, b_out):
    n_rows, hw = xb.shape
    e, d, dff2 = w_in.shape
    dff = w_out.shape[1]
    n_blocks = n_rows // MOE_BLOCK
    grid_spec = pltpu.PrefetchScalarGridSpec(
        num_scalar_prefetch=3,
        grid=(n_blocks,),
        in_specs=[
            pl.BlockSpec((MOE_BLOCK, hw), lambda b, be, nv, nu: (b, 0)),
            pl.BlockSpec((1, d, dff2), lambda b, be, nv, nu: (be[b], 0, 0)),
            pl.BlockSpec((1, 1, dff2), lambda b, be, nv, nu: (be[b], 0, 0)),
            pl.BlockSpec((1, dff, d), lambda b, be, nv, nu: (be[b], 0, 0)),
            pl.BlockSpec((1, 1, d), lambda b, be, nv, nu: (be[b], 0, 0)),
        ],
        out_specs=pl.BlockSpec((MOE_BLOCK, hw), lambda b, be, nv, nu: (b, 0)),
        scratch_shapes=[pltpu.VMEM((d, dff2), bf16), pltpu.VMEM((dff, d), bf16)],
    )
    return pl.pallas_call(
        _expert_kernel,
        grid_spec=grid_spec,
        out_shape=jax.ShapeDtypeStruct((n_rows, hw), jnp.uint32),
        compiler_params=pltpu.CompilerParams(
            dimension_semantics=("arbitrary",), vmem_limit_bytes=VMEM_LIMIT),
        name="experts",
    )(block_expert, block_valid, n_used, xb, w_in, b_in.reshape(e, 1, dff2), w_out,
      b_out.reshape(e, 1, d))


def _combine_kernel(nq_ref, dst_ref, dstn_ref, meta_ref, gate_ref, ls_ref, xm_ref, y_ref, o_ref,
                    ly_ref, sem):
    i = pl.program_id(0)
    last = pl.num_programs(0) - 1
    hw = ly_ref.shape[2]
    slot = i % 2

    def fetch(tile, table_ref, buf):
        ly_ref[buf, MOE_TILE * TOP_K:, :] = jnp.zeros((LOCAL_ROWS - MOE_TILE * TOP_K, hw), jnp.uint32)

        def start(q, c):
            _piece_copy(y_ref, ly_ref.at[buf], table_ref[0, 0, q], q * DMA_ROWS, sem.at[buf]).start()
            return c

        lax.fori_loop(0, nq_ref[tile], start, 0)

    @pl.when(i == 0)
    def _():
        fetch(i, dst_ref, slot)

    @pl.when(i < last)
    def _():
        fetch(jnp.minimum(i + 1, last), dstn_ref, 1 - slot)

    def wait(q, c):
        _piece_copy(y_ref, ly_ref.at[slot], dst_ref[0, 0, q], q * DMA_ROWS, sem.at[slot]).wait()
        return c

    lax.fori_loop(0, nq_ref[i], wait, 0)
    gates = gate_ref[...]
    g = _slot_matrix(meta_ref, ls_ref, [gates[:, k:k + 1] for k in range(TOP_K)]).astype(bf16)
    ya, yb = _unpack_bf16_pairs(ly_ref[slot])
    o_ref[:, 0:hw] = xm_ref[:, 0:hw] + _dot(g, ya)
    o_ref[:, hw:] = xm_ref[:, hw:] + _dot(g, yb)


def _combine(nq, dst, meta, gates, lstart, xm, y):
    n, d = xm.shape
    hw = d // 2
    t = MOE_TILE
    n_tiles = n // t
    grid_spec = pltpu.PrefetchScalarGridSpec(
        num_scalar_prefetch=1,
        grid=(n_tiles,),
        in_specs=[pl.BlockSpec((1, 1, LOCAL_PIECES), lambda i, nq: (i, 0, 0), memory_space=pltpu.SMEM),
                  pl.BlockSpec((1, 1, LOCAL_PIECES), lambda i, nq: (jnp.minimum(i + 1, n_tiles - 1), 0, 0),
                               memory_space=pltpu.SMEM),
                  pl.BlockSpec((t, LANE), lambda i, nq: (i, 0)),
                  pl.BlockSpec((t, LANE), lambda i, nq: (i, 0)),
                  pl.BlockSpec((1, 1, LANE), lambda i, nq: (i, 0, 0)),
                  pl.BlockSpec((t, d), lambda i, nq: (i, 0)),
                  pl.BlockSpec(memory_space=pl.ANY)],
        out_specs=pl.BlockSpec((t, d), lambda i, nq: (i, 0)),
        scratch_shapes=[pltpu.VMEM((2, LOCAL_ROWS, hw), jnp.uint32), pltpu.SemaphoreType.DMA((2,))],
    )
    return pl.pallas_call(
        _combine_kernel,
        grid_spec=grid_spec,
        out_shape=jax.ShapeDtypeStruct((n, d), f32),
        compiler_params=pltpu.CompilerParams(
            dimension_semantics=("arbitrary",), vmem_limit_bytes=VMEM_LIMIT),
        name="combine",
    )(nq, dst, dst, meta, gates, lstart, xm, y)


def _rel_bias_table(rel_bias):
    band = BAND_CHUNKS * CHUNK
    dist = np.arange(band + CHUNK - 1) - (CHUNK - 1)
    ext = rel_bias[:, np.clip(dist, -REL_CLIP, REL_CLIP) + REL_CLIP][:, ::-1]
    rows = [lax.slice_in_dim(ext, CHUNK - 1 - i, CHUNK - 1 - i + band, axis=1) for i in range(CHUNK)]
    return jnp.stack(rows, axis=1).astype(f32)


def _round_up(x, m):
    return (x + m - 1) // m * m


def _routing_tables(cnt, n_tokens):
    experts = jnp.arange(N_EXPERTS, dtype=jnp.int32)
    c = cnt.reshape(-1, LANE)[:, :N_EXPERTS].astype(jnp.int32)
    n_tiles = c.shape[0]
    cp = _round_up(c, DMA_ROWS)
    lend = jnp.cumsum(cp, axis=1)
    lstart = lend - cp
    nq = (lend[:, -1] // DMA_ROWS).astype(jnp.int32)
    region = jnp.sum(cp, axis=0)
    padded = _round_up(region, MOE_BLOCK)
    pend = jnp.cumsum(padded)
    pstart = pend - padded
    base = pstart[None, :] + jnp.cumsum(cp, axis=0) - cp
    q0 = jnp.arange(LOCAL_PIECES, dtype=jnp.int32) * DMA_ROWS
    e_q = jnp.minimum(jnp.sum(lend[:, None, :] <= q0[None, :, None], axis=-1), N_EXPERTS - 1)
    shift = jnp.sum(jnp.where(e_q[:, :, None] == experts, (base - lstart)[:, None, :], 0), axis=-1)
    dst = jnp.where(q0[None, :] < lend[:, -1:], shift + q0[None, :], 0).astype(jnp.int32)

    n_blocks = -(-(n_tokens * TOP_K + n_tiles * N_EXPERTS * (DMA_ROWS - 1)) // MOE_BLOCK) + N_EXPERTS
    blk0 = jnp.arange(n_blocks, dtype=jnp.int32) * MOE_BLOCK
    n_used = (pend[-1] // MOE_BLOCK).astype(jnp.int32)
    be = jnp.minimum(jnp.sum(pend[None, :] <= blk0[:, None], axis=1), N_EXPERTS - 1).astype(jnp.int32)
    last = jnp.sum(jnp.where(jnp.arange(n_blocks) == n_used - 1, be, 0))
    be = jnp.where(jnp.arange(n_blocks) < n_used, be, last)
    onehot_be = be[:, None] == experts
    be_rows = jnp.sum(jnp.where(onehot_be, region, 0), axis=1)
    be_start = jnp.sum(jnp.where(onehot_be, pstart, 0), axis=1)
    valid = jnp.clip(be_rows - (blk0 - be_start), 0, MOE_BLOCK).astype(jnp.int32)
    lstart_rows = jnp.pad(lstart, ((0, 0), (0, LANE - N_EXPERTS))).reshape(n_tiles, 1, LANE)
    pad_start = (pstart + region).astype(jnp.int32)
    pad_pieces = ((padded - region) // DMA_ROWS).astype(jnp.int32)
    return (nq, pad_start, pad_pieces, dst.reshape(n_tiles, 1, LOCAL_PIECES),
            lstart_rows.astype(jnp.int32), be, valid, n_used.reshape(1), n_blocks)


def _layer(x, norm1_g, w_in, gate_up, gate_bias, gla_norm_g, q_norm_g, k_norm_g, rel_bias, w_out,
           norm2_g, router_w, router_b, moe_w_in, moe_b_in, moe_w_out, moe_b_out):
    batch, seq, d = x.shape
    n = batch * seq
    x2 = x.reshape(n, d)

    pieces = jnp.split(w_in, np.cumsum(IN_SIZES)[:-1].tolist(), axis=-1)
    pieces[3] = jnp.pad(pieces[3], ((0, 0), (0, LANE - GLA_GATE_RANK)))
    widths = [p.shape[-1] for p in pieces]
    w_all = jnp.concatenate(pieces, axis=-1).astype(bf16)
    gq, gk, gv, glr, gog, aq, ak, av = _inproj(x2, norm1_g.reshape(1, d), w_all, widths)

    gup = jnp.pad(gate_up, ((0, LANE - GLA_GATE_RANK), (0, 0))).astype(bf16)
    o_gla = _gla(gq, gk, gv, glr, gog, gup, gate_bias.reshape(1, -1), gla_norm_g.reshape(1, -1),
                 batch, seq)
    tile2 = lambda g: jnp.tile(g.reshape(1, -1), (1, LANE // ATT_DH))
    o_att = _attention(aq, ak, av, _rel_bias_table(rel_bias), tile2(q_norm_g), tile2(k_norm_g),
                       batch, seq)

    rw = jnp.pad(router_w, ((0, 0), (0, LANE - N_EXPERTS)))
    rw_hi = rw.astype(bf16)
    rw_lo = (rw - rw_hi.astype(f32)).astype(bf16)
    rb = jnp.pad(router_b, (0, LANE - N_EXPERTS)).reshape(1, LANE)
    xm, h2, meta, gates, cnt = _mid(o_gla, o_att, x2, w_out.astype(bf16), norm2_g.reshape(1, d),
                                    rw_hi, rw_lo, rb)

    nq, pad_start, pad_pieces, dst, lstart, be, valid, n_used, n_blocks = _routing_tables(cnt, n)
    xb = _dispatch(nq, pad_start, pad_pieces, n_used, dst, meta, lstart, h2, n_blocks * MOE_BLOCK)
    y = _experts(be, valid, n_used, xb, moe_w_in, moe_b_in, moe_w_out, moe_b_out)
    out = _combine(nq, dst, meta, gates, lstart, xm, y)
    return out.reshape(batch, seq, d)


def kernel(x, norm1_g, w_in, gla_gate_up, gla_gate_bias, gla_norm_g, q_norm_g, k_norm_g, rel_bias, w_out, norm2_g, router_w, router_b, moe_w_in, moe_b_in, moe_w_out, moe_b_out):
    for l in range(norm1_g.shape[0]):
        x = _layer(x, norm1_g[l], w_in[l], gla_gate_up[l], gla_gate_bias[l], gla_norm_g[l],
                   q_norm_g[l], k_norm_g[l], rel_bias[l], w_out[l], norm2_g[l], router_w[l],
                   router_b[l], moe_w_in[l], moe_b_in[l], moe_w_out[l], moe_b_out[l])
    return x
```

```python
import functools

import numpy as np
import jax
import jax.numpy as jnp
from jax import lax
from jax.experimental import pallas as pl
from jax.experimental.pallas import tpu as pltpu

CHUNK = 64
EPS = 1e-6
GLA_HEADS = 4
GLA_DK = 64
GLA_DV = 128
GLA_GATE_RANK = 16
GLA_GATE_TAU = 16.0
ATT_HEADS = 8
ATT_DH = 64
N_BACK_CHUNKS = 8
BAND_CHUNKS = N_BACK_CHUNKS + 1
REL_CLIP = 256
MASK_VALUE = -1e30
N_EXPERTS = 32
TOP_K = 4
SWIGLU_ALPHA = 1.702
SWIGLU_LIMIT = 7.0
MOE_BLOCK = 256

LANE = 128
GLA_QK_W = GLA_HEADS * GLA_DK
GLA_V_W = GLA_HEADS * GLA_DV
ATT_W = ATT_HEADS * ATT_DH
IN_SIZES = (GLA_QK_W, GLA_QK_W, GLA_V_W, GLA_GATE_RANK, GLA_V_W, ATT_W, ATT_W, ATT_W)
SEQ_TILE = N_BACK_CHUNKS * CHUNK
ROW_TILE = 512
MOE_TILE = 256
DMA_ROWS = 8
LOCAL_PIECES = 160
LOCAL_ROWS = LOCAL_PIECES * DMA_ROWS
MIN_PIECES = MOE_TILE * TOP_K // DMA_ROWS
ISSUE_UNROLL = 4
VMEM_LIMIT = 48 * 1024 * 1024

f32 = jnp.float32
bf16 = jnp.bfloat16


def _rms(x, g):
    return x * lax.rsqrt(jnp.mean(x * x, axis=-1, keepdims=True) + EPS) * g


def _dot(a, b):
    return jnp.dot(a, b, preferred_element_type=f32)


def _dot_nt(a, b):
    return lax.dot_general(a, b, (((1,), (1,)), ((), ())), preferred_element_type=f32)


def _dot_tn(a, b):
    return lax.dot_general(a, b, (((0,), (0,)), ((), ())), preferred_element_type=f32)


def _split_bf16(x):
    hi = x.astype(bf16)
    lo = (x - hi.astype(f32)).astype(bf16)
    return hi, lo


def _inproj_kernel(x_ref, g_ref, w_ref, *out_refs):
    h = _rms(x_ref[...], g_ref[...]).astype(bf16)
    off = 0
    for o_ref in out_refs:
        w = o_ref.shape[-1]
        o_ref[...] = _dot(h, w_ref[:, off:off + w]).astype(o_ref.dtype)
        off += w


def _inproj(x2, g, w, widths):
    n, d = x2.shape
    return pl.pallas_call(
        _inproj_kernel,
        grid=(n // ROW_TILE,),
        in_specs=[
            pl.BlockSpec((ROW_TILE, d), lambda i: (i, 0)),
            pl.BlockSpec((1, d), lambda i: (0, 0)),
            pl.BlockSpec(w.shape, lambda i: (0, 0)),
        ],
        out_specs=[pl.BlockSpec((ROW_TILE, wd), lambda i: (i, 0)) for wd in widths],
        out_shape=[jax.ShapeDtypeStruct((n, wd), bf16) for wd in widths],
        compiler_params=pltpu.CompilerParams(
            dimension_semantics=("arbitrary",), vmem_limit_bytes=VMEM_LIMIT),
        name="inproj",
    )(x2, g, w)


def _gla_kernel(q_ref, k_ref, v_ref, lr_ref, og_ref, gup_ref, gb_ref, ng_ref, o_ref, st_ref,
                u_ref, sb_ref):
    t = q_ref.shape[0]

    @pl.when(pl.program_id(1) == 0)
    def _():
        st_ref[...] = jnp.zeros_like(st_ref)

    z = _dot(lr_ref[...], gup_ref[...]) + gb_ref[...]
    log_a = (jnp.minimum(z, 0.0) - jnp.log1p(jnp.exp(-jnp.abs(z)))) * (1.0 / GLA_GATE_TAU)
    row = lax.broadcasted_iota(jnp.int32, (t, t), 0)
    col = lax.broadcasted_iota(jnp.int32, (t, t), 1)
    tri = jnp.where((col <= row) & ((col // CHUNK) == (row // CHUNK)), 1.0, 0.0).astype(bf16)
    la_hi, la_lo = _split_bf16(log_a)
    cum_all = _dot(tri, la_hi) + _dot(tri, la_lo)
    lane = lax.broadcasted_iota(jnp.int32, (1, LANE), 1)
    half_mask = (lane < GLA_DK, lane >= GLA_DK)
    n_chunks = t // CHUNK
    chunk_rows = [slice(c * CHUNK, (c + 1) * CHUNK) for c in range(n_chunks)]
    pair_of = lambda h: slice((h // 2) * LANE, (h // 2 + 1) * LANE)
    head_of = lambda h: slice(h * GLA_DV, (h + 1) * GLA_DV)

    decs = []
    for c, rows in enumerate(chunk_rows):
        cum = cum_all[rows]
        tot = cum[CHUNK - 1:CHUNK]
        kdec = k_ref[rows, :].astype(f32) * jnp.exp(tot - cum)
        decs.append(jnp.exp(tot))
        for h in range(GLA_HEADS):
            kd = jnp.where(half_mask[h % 2], kdec[:, pair_of(h)], 0.0).astype(bf16)
            u_ref[c, h] = _dot_tn(v_ref[rows, head_of(h)], kd)

    for h in range(GLA_HEADS):
        st = st_ref[h]
        for c in range(n_chunks):
            st = st * decs[c][:, pair_of(h)] + u_ref[c, h]
            sb_ref[c, h] = st.astype(bf16)
        st_ref[h] = st

    for c, rows in enumerate(chunk_rows):
        for h in range(GLA_HEADS):
            o = _dot_nt(q_ref[rows, pair_of(h)], sb_ref[c, h]) * (GLA_DK ** -0.5)
            o = _rms(o, ng_ref[...])
            g = og_ref[rows, head_of(h)].astype(f32)
            o_ref[rows, head_of(h)] = (o * (g / (1.0 + jnp.exp(-g)))).astype(o_ref.dtype)


def _gla(gq, gk, gv, glr, gog, gup, gb, ng, batch, seq):
    nt = seq // SEQ_TILE
    tile = lambda w: pl.BlockSpec((SEQ_TILE, w), lambda b, i: (b * nt + i, 0))
    full = lambda a: pl.BlockSpec(a.shape, lambda b, i: (0,) * a.ndim)
    return pl.pallas_call(
        _gla_kernel,
        grid=(batch, nt),
        in_specs=[tile(GLA_QK_W), tile(GLA_QK_W), tile(GLA_V_W), tile(LANE), tile(GLA_V_W),
                  full(gup), full(gb), full(ng)],
        out_specs=tile(GLA_V_W),
        out_shape=jax.ShapeDtypeStruct((batch * seq, GLA_V_W), bf16),
        scratch_shapes=[pltpu.VMEM((GLA_HEADS, GLA_DV, LANE), f32),
                        pltpu.VMEM((SEQ_TILE // CHUNK, GLA_HEADS, GLA_DV, LANE), f32),
                        pltpu.VMEM((SEQ_TILE // CHUNK, GLA_HEADS, GLA_DV, LANE), bf16)],
        compiler_params=pltpu.CompilerParams(
            dimension_semantics=("arbitrary", "arbitrary"), vmem_limit_bytes=VMEM_LIMIT),
        name="gla",
    )(gq, gk, gv, glr, gog, gup, gb, ng)


def _head_norm(x, g, ones_bd):
    sq_hi, sq_lo = _split_bf16(x * x)
    ssq = _dot(sq_hi, ones_bd) + _dot(sq_lo, ones_bd)
    return x * lax.rsqrt(ssq * (1.0 / ATT_DH) + EPS) * g


def _att_kernel(q_ref, k_ref, v_ref, bias_ref, qg_ref, kg_ref, o_ref, qs_ref, kb_ref, vb_ref,
                sn_ref):
    t = q_ref.shape[0]
    n_pairs = ATT_W // LANE
    band = BAND_CHUNKS * CHUNK
    first = pl.program_id(1) == 0

    @pl.when(first)
    def _():
        kb_ref[0:t, :] = jnp.zeros((t, ATT_W), bf16)
        vb_ref[0:t, :] = jnp.zeros((t, 2 * ATT_W), bf16)

    @pl.when(jnp.logical_not(first))
    def _():
        kb_ref[0:t, :] = kb_ref[t:2 * t, :]
        vb_ref[0:t, :] = vb_ref[t:2 * t, :]

    lane = lax.broadcasted_iota(jnp.int32, (1, LANE), 1)
    lo = lane < ATT_DH
    r_i = lax.broadcasted_iota(jnp.int32, (LANE, LANE), 0)
    c_i = lax.broadcasted_iota(jnp.int32, (LANE, LANE), 1)
    ones_bd = jnp.where((r_i < ATT_DH) == (c_i < ATT_DH), 1.0, 0.0).astype(bf16)

    for p in range(n_pairs):
        pair = slice(p * LANE, (p + 1) * LANE)
        kb_ref[t:2 * t, pair] = _head_norm(k_ref[:, pair].astype(f32), kg_ref[...], ones_bd).astype(bf16)
        qn = _head_norm(q_ref[:, pair].astype(f32), qg_ref[...], ones_bd) * (ATT_DH ** -0.5)
        q_lo = jnp.where(lo, qn, 0.0).astype(bf16)
        q_hi = jnp.where(lo, 0.0, qn).astype(bf16)
        for c in range(t // CHUNK):
            rows = slice(c * CHUNK, (c + 1) * CHUNK)
            qs_ref[c * n_pairs + p, 0:CHUNK, :] = q_lo[rows]
            qs_ref[c * n_pairs + p, CHUNK:2 * CHUNK, :] = q_hi[rows]
        vb_ref[t:2 * t, 2 * p * LANE:(2 * p + 1) * LANE] = v_ref[:, pair]
        vb_ref[t:2 * t, (2 * p + 1) * LANE:(2 * p + 2) * LANE] = jnp.ones((t, LANE), bf16)

    colk = lax.broadcasted_iota(jnp.int32, (1, band), 1)

    def chunk_loop(masked):
        n_chunks = t // CHUNK

        def scores(c, p):
            k2 = kb_ref[pl.ds(pl.multiple_of(c * CHUNK, CHUNK), band), p * LANE:(p + 1) * LANE]
            return _dot_nt(qs_ref[c * n_pairs + p], k2)

        def chunk_body(c, carry):
            r0 = pl.multiple_of(c * CHUNK, CHUNK)
            s_next = sn_ref[...]
            for p in range(n_pairs):
                s = s_next + bias_ref[p]
                if p + 1 < n_pairs:
                    s_next = scores(c, p + 1)
                else:
                    sn_ref[...] = scores(jnp.minimum(c + 1, n_chunks - 1), 0)
                if masked:
                    s = jnp.where(colk >= t - c * CHUNK, s, MASK_VALUE)
                e = jnp.exp(s - jnp.max(s, axis=-1, keepdims=True))
                v2 = vb_ref[pl.ds(r0, band), 2 * p * LANE:(2 * p + 2) * LANE]
                pvl = _dot(e.astype(bf16), v2)
                pv = pvl[:, 0:LANE] / pvl[:, LANE:2 * LANE]
                o_ref[pl.ds(r0, CHUNK), p * LANE:(p + 1) * LANE] = jnp.where(
                    lo, pv[0:CHUNK], pv[CHUNK:2 * CHUNK]).astype(o_ref.dtype)
            return carry

        sn_ref[...] = scores(jnp.int32(0), 0)
        lax.fori_loop(0, n_chunks, chunk_body, 0)

    @pl.when(first)
    def _():
        chunk_loop(True)

    @pl.when(jnp.logical_not(first))
    def _():
        chunk_loop(False)


def _attention(aq, ak, av, bias, qg, kg, batch, seq):
    nt = seq // SEQ_TILE
    n_pairs = ATT_W // LANE
    tile = pl.BlockSpec((SEQ_TILE, ATT_W), lambda b, i: (b * nt + i, 0))
    full = lambda a: pl.BlockSpec(a.shape, lambda b, i: (0,) * a.ndim)
    bias2 = bias.reshape(n_pairs, 2 * CHUNK, BAND_CHUNKS * CHUNK)
    return pl.pallas_call(
        _att_kernel,
        grid=(batch, nt),
        in_specs=[tile, tile, tile, full(bias2), full(qg), full(kg)],
        out_specs=tile,
        out_shape=jax.ShapeDtypeStruct((batch * seq, ATT_W), bf16),
        scratch_shapes=[pltpu.VMEM((SEQ_TILE // CHUNK * n_pairs, 2 * CHUNK, LANE), bf16),
                        pltpu.VMEM((2 * SEQ_TILE, ATT_W), bf16),
                        pltpu.VMEM((2 * SEQ_TILE, 2 * ATT_W), bf16),
                        pltpu.VMEM((2 * CHUNK, BAND_CHUNKS * CHUNK), f32)],
        compiler_params=pltpu.CompilerParams(
            dimension_semantics=("arbitrary", "arbitrary"), vmem_limit_bytes=VMEM_LIMIT),
        name="attention",
    )(aq, ak, av, bias2, qg, kg)


def _mid_kernel(og_ref, oa_ref, x_ref, wo_ref, g2_ref, rwh_ref, rwl_ref, rb_ref,
                xm_ref, h_ref, meta_ref, gate_ref, cnt_ref):
    t, d = x_ref.shape
    xm = x_ref[...] + _dot(og_ref[...], wo_ref[0:GLA_V_W, :]) + _dot(oa_ref[...], wo_ref[GLA_V_W:, :])
    xm_ref[...] = xm
    h2 = _rms(xm, g2_ref[...])
    h_hi, h_lo = _split_bf16(h2)
    h_ref[...] = h_hi

    logits = (_dot(h_hi, rwh_ref[...]) + _dot(h_lo, rwh_ref[...]) + _dot(h_hi, rwl_ref[...])
              + rb_ref[...])
    lane = lax.broadcasted_iota(jnp.int32, (t, LANE), 1)
    lane_f = lane.astype(f32)
    l = jnp.where(lane < N_EXPERTS, logits, -jnp.inf)
    vals, idxs = [], []
    for _ in range(TOP_K):
        m = jnp.max(l, axis=-1, keepdims=True)
        ik = jnp.min(jnp.where(l == m, lane_f, float(LANE)), axis=-1, keepdims=True)
        vals.append(m)
        idxs.append(ik)
        l = jnp.where(lane_f == ik, -jnp.inf, l)
    es = [jnp.exp(v - vals[0]) for v in vals]
    den = es[0] + es[1] + es[2] + es[3]

    onehots = [lane_f == ik for ik in idxs]
    sel = jnp.zeros((t, LANE), f32)
    for oh in onehots:
        sel = sel + jnp.where(oh, 1.0, 0.0)
    row = lax.broadcasted_iota(jnp.int32, (t, t), 0)
    col = lax.broadcasted_iota(jnp.int32, (t, t), 1)
    tri = jnp.where((col < row) & ((col // MOE_TILE) == (row // MOE_TILE)), 1.0, 0.0).astype(bf16)
    prefix = _dot(tri, sel.astype(bf16))
    for s in range(t // MOE_TILE):
        cnt_ref[0, s:s + 1, :] = jnp.sum(sel[s * MOE_TILE:(s + 1) * MOE_TILE], axis=0, keepdims=True)

    meta = jnp.zeros((t, LANE), f32)
    gates = jnp.zeros((t, LANE), f32)
    for k in range(TOP_K):
        rank_k = jnp.sum(jnp.where(onehots[k], prefix, 0.0), axis=-1, keepdims=True)
        meta = jnp.where(lane == k, idxs[k], meta)
        meta = jnp.where(lane == TOP_K + k, rank_k, meta)
        gates = jnp.where(lane == k, es[k] / den, gates)
    meta_ref[...] = meta.astype(jnp.int32)
    gate_ref[...] = gates


def _mid(o_gla, o_att, x2, wo, g2, rwh, rwl, rb):
    n, d = x2.shape
    sub = ROW_TILE // MOE_TILE
    tile = lambda w: pl.BlockSpec((ROW_TILE, w), lambda i: (i, 0))
    full = lambda a: pl.BlockSpec(a.shape, lambda i: (0,) * a.ndim)
    return pl.pallas_call(
        _mid_kernel,
        grid=(n // ROW_TILE,),
        in_specs=[tile(GLA_V_W), tile(ATT_W), tile(d), full(wo), full(g2), full(rwh), full(rwl),
                  full(rb)],
        out_specs=[tile(d), tile(d), tile(LANE), tile(LANE),
                   pl.BlockSpec((1, sub, LANE), lambda i: (i, 0, 0))],
        out_shape=[jax.ShapeDtypeStruct((n, d), f32),
                   jax.ShapeDtypeStruct((n, d), bf16),
                   jax.ShapeDtypeStruct((n, LANE), jnp.int32),
                   jax.ShapeDtypeStruct((n, LANE), f32),
                   jax.ShapeDtypeStruct((n // ROW_TILE, sub, LANE), f32)],
        compiler_params=pltpu.CompilerParams(
            dimension_semantics=("arbitrary",), vmem_limit_bytes=VMEM_LIMIT),
        name="mid",
    )(o_gla, o_att, x2, wo, g2, rwh, rwl, rb)


def _slot_matrix(meta_ref, ls_ref, values):
    meta = meta_ref[...]
    t = meta.shape[0]
    lane = lax.broadcasted_iota(jnp.int32, (t, LANE), 1)
    starts = ls_ref[0].astype(f32)
    col = lax.broadcasted_iota(jnp.int32, (1, LOCAL_ROWS), 1)
    out = jnp.zeros((t, LOCAL_ROWS), f32)
    for k in range(TOP_K):
        start_k = jnp.sum(jnp.where(lane == meta[:, k:k + 1], starts, 0.0), axis=-1, keepdims=True)
        slot_k = start_k.astype(jnp.int32) + meta[:, TOP_K + k:TOP_K + k + 1]
        out = jnp.where(col == slot_k, values[k], out)
    return out


def _pack_bf16_pairs(a, b):
    return (pltpu.bitcast(a, jnp.uint32) >> 16) | (pltpu.bitcast(b, jnp.uint32) & jnp.uint32(0xFFFF0000))


def _unpack_bf16_pairs(w):
    a = pltpu.bitcast(w << 16, f32).astype(bf16)
    b = pltpu.bitcast(w & jnp.uint32(0xFFFF0000), f32).astype(bf16)
    return a, b


def _piece_copy(src_ref, dst_ref, src_row, dst_row, sem):
    return pltpu.make_async_copy(src_ref.at[pl.ds(pl.multiple_of(src_row, DMA_ROWS), DMA_ROWS)],
                                 dst_ref.at[pl.ds(pl.multiple_of(dst_row, DMA_ROWS), DMA_ROWS)], sem)


def _start_pieces(n, piece):
    def group(g, c):
        for j in range(ISSUE_UNROLL):
            piece(g * ISSUE_UNROLL + j).start()
        return c

    n_groups = n // ISSUE_UNROLL
    lax.fori_loop(0, n_groups, group, 0)
    lax.fori_loop(n_groups * ISSUE_UNROLL, n, lambda q, c: (piece(q).start(), c)[1], 0)


def _wait_pieces(n, piece, bulk):
    bulk.wait()
    lax.fori_loop(MIN_PIECES, n, lambda q, c: (piece(q).wait(), c)[1], 0)


def _zero_fill_padding(pad_start_ref, pad_pieces_ref, n_used_ref, xb_ref, z_ref, sem):
    z_ref[...] = jnp.zeros_like(z_ref)
    n_blocks = xb_ref.shape[0] // MOE_BLOCK

    def tail_piece(e, q):
        return _piece_copy(z_ref, xb_ref, 0, pad_start_ref[e] + q * DMA_ROWS, sem)

    def block_copy(b):
        row = pl.multiple_of(b * MOE_BLOCK, MOE_BLOCK)
        return pltpu.make_async_copy(z_ref, xb_ref.at[pl.ds(row, MOE_BLOCK)], sem)

    def each_tail_piece(fn):
        def per_expert(e, c):
            lax.fori_loop(0, pad_pieces_ref[e], lambda q, cc: (fn(tail_piece(e, q)), cc)[1], 0)
            return c
        lax.fori_loop(0, N_EXPERTS, per_expert, 0)

    def each_block(fn):
        lax.fori_loop(n_used_ref[0], n_blocks, lambda b, c: (fn(block_copy(b)), c)[1], 0)

    each_tail_piece(lambda cp: cp.start())
    each_block(lambda cp: cp.start())
    each_tail_piece(lambda cp: cp.wait())
    each_block(lambda cp: cp.wait())


def _dispatch_kernel(nq_ref, pad_start_ref, pad_pieces_ref, n_used_ref, dst_ref, meta_ref, ls_ref,
                     h_ref, xb_ref, l_ref, z_ref, sem):
    i = pl.program_id(0)
    hw = l_ref.shape[1]
    perm = _slot_matrix(meta_ref, ls_ref, [1.0] * TOP_K).T.astype(bf16)
    packed = _pack_bf16_pairs(_dot(perm, h_ref[:, 0:hw]), _dot(perm, h_ref[:, hw:]))

    def piece(q):
        return _piece_copy(l_ref, xb_ref, q * DMA_ROWS, dst_ref[0, 0, q], sem.at[0])

    def drain(n):
        min_rows = MIN_PIECES * DMA_ROWS
        bulk = pltpu.make_async_copy(l_ref.at[0:min_rows], xb_ref.at[0:min_rows], sem.at[0])
        _wait_pieces(n, piece, bulk)

    @pl.when(i > 0)
    def _():
        drain(nq_ref[jnp.maximum(i - 1, 0)])

    l_ref[...] = packed
    _start_pieces(nq_ref[i], piece)

    @pl.when(i == pl.num_programs(0) - 1)
    def _():
        _zero_fill_padding(pad_start_ref, pad_pieces_ref, n_used_ref, xb_ref, z_ref, sem.at[1])
        drain(nq_ref[i])


def _dispatch(nq, pad_start, pad_pieces, n_used, dst, meta, lstart, h, n_rows):
    n, d = h.shape
    hw = d // 2
    t = MOE_TILE
    grid_spec = pltpu.PrefetchScalarGridSpec(
        num_scalar_prefetch=4,
        grid=(n // t,),
        in_specs=[pl.BlockSpec((1, 1, LOCAL_PIECES), lambda i, *_: (i, 0, 0), memory_space=pltpu.SMEM),
                  pl.BlockSpec((t, LANE), lambda i, *_: (i, 0)),
                  pl.BlockSpec((1, 1, LANE), lambda i, *_: (i, 0, 0)),
                  pl.BlockSpec((t, d), lambda i, *_: (i, 0))],
        out_specs=pl.BlockSpec(memory_space=pl.ANY),
        scratch_shapes=[pltpu.VMEM((LOCAL_ROWS, hw), jnp.uint32),
                        pltpu.VMEM((MOE_BLOCK, hw), jnp.uint32),
                        pltpu.SemaphoreType.DMA((2,))],
    )
    return pl.pallas_call(
        _dispatch_kernel,
        grid_spec=grid_spec,
        out_shape=jax.ShapeDtypeStruct((n_rows, hw), jnp.uint32),
        compiler_params=pltpu.CompilerParams(
            dimension_semantics=("arbitrary",), vmem_limit_bytes=VMEM_LIMIT),
        name="dispatch",
    )(nq, pad_start, pad_pieces, n_used, dst, meta, lstart, h)


def _expert_kernel(be_ref, nu_ref, x_ref, wi_ref, bi_ref, wo_ref, bo_ref, y_ref,
                   wib_ref, wob_ref):
    b = pl.program_id(0)
    blk, hw = x_ref.shape
    dff = wo_ref.shape[1]

    @pl.when(b < nu_ref[0])
    def _():
        changed = jnp.logical_or(b == 0, be_ref[b] != be_ref[jnp.maximum(b - 1, 0)])

        @pl.when(changed)
        def _():
            wib_ref[...] = wi_ref[0].astype(bf16)
            wob_ref[...] = wo_ref[0].astype(bf16)

        xa, xb = _unpack_bf16_pairs(x_ref[...])
        hc = _dot(xa, wib_ref[0:hw, :]) + _dot(xb, wib_ref[hw:, :]) + bi_ref[0]
        glu = jnp.minimum(hc[:, :dff], SWIGLU_LIMIT)
        lin = jnp.clip(hc[:, dff:], -SWIGLU_LIMIT, SWIGLU_LIMIT)
        act = glu * (1.0 / (1.0 + jnp.exp(-SWIGLU_ALPHA * glu))) * (lin + 1.0)
        y = (_dot(act.astype(bf16), wob_ref[...]) + bo_ref[0]).astype(bf16).astype(f32)
        y_ref[...] = _pack_bf16_pairs(y[:, 0:hw], y[:, hw:])

    @pl.when(b >= nu_ref[0])
    def _():
        y_ref[...] = jnp.zeros_like(y_ref)


def _experts(block_expert, n_used, xb, w_in, b_in, w_out, b_out):
    n_rows, hw = xb.shape
    e, d, dff2 = w_in.shape
    dff = w_out.shape[1]
    n_blocks = n_rows // MOE_BLOCK
    grid_spec = pltpu.PrefetchScalarGridSpec(
        num_scalar_prefetch=2,
        grid=(n_blocks,),
        in_specs=[
            pl.BlockSpec((MOE_BLOCK, hw), lambda b, be, nu: (b, 0)),
            pl.BlockSpec((1, d, dff2), lambda b, be, nu: (be[b], 0, 0)),
            pl.BlockSpec((1, 1, dff2), lambda b, be, nu: (be[b], 0, 0)),
            pl.BlockSpec((1, dff, d), lambda b, be, nu: (be[b], 0, 0)),
            pl.BlockSpec((1, 1, d), lambda b, be, nu: (be[b], 0, 0)),
        ],
        out_specs=pl.BlockSpec((MOE_BLOCK, hw), lambda b, be, nu: (b, 0)),
        scratch_shapes=[pltpu.VMEM((d, dff2), bf16), pltpu.VMEM((dff, d), bf16)],
    )
    return pl.pallas_call(
        _expert_kernel,
        grid_spec=grid_spec,
        out_shape=jax.ShapeDtypeStruct((n_rows, hw), jnp.uint32),
        compiler_params=pltpu.CompilerParams(
            dimension_semantics=("arbitrary",), vmem_limit_bytes=VMEM_LIMIT),
        name="experts",
    )(block_expert, n_used, xb, w_in, b_in.reshape(e, 1, dff2), w_out, b_out.reshape(e, 1, d))


def _combine_kernel(nq_ref, dst_ref, dstn_ref, meta_ref, gate_ref, ls_ref, xm_ref, y_ref, o_ref,
                    ly_ref, sem):
    i = pl.program_id(0)
    last = pl.num_programs(0) - 1
    hw = ly_ref.shape[2]
    slot = i % 2

    def fetch(tile, table_ref, buf):
        ly_ref[buf, MOE_TILE * TOP_K:, :] = jnp.zeros((LOCAL_ROWS - MOE_TILE * TOP_K, hw), jnp.uint32)

        _start_pieces(nq_ref[tile], lambda q: _piece_copy(
            y_ref, ly_ref.at[buf], table_ref[0, 0, q], q * DMA_ROWS, sem.at[buf]))

    @pl.when(i == 0)
    def _():
        fetch(i, dst_ref, slot)

    @pl.when(i < last)
    def _():
        fetch(jnp.minimum(i + 1, last), dstn_ref, 1 - slot)

    min_rows = MIN_PIECES * DMA_ROWS
    _wait_pieces(
        nq_ref[i],
        lambda q: _piece_copy(y_ref, ly_ref.at[slot], dst_ref[0, 0, q], q * DMA_ROWS, sem.at[slot]),
        pltpu.make_async_copy(y_ref.at[0:min_rows], ly_ref.at[slot, 0:min_rows], sem.at[slot]))
    gates = gate_ref[...]
    g = _slot_matrix(meta_ref, ls_ref, [gates[:, k:k + 1] for k in range(TOP_K)]).astype(bf16)
    ya, yb = _unpack_bf16_pairs(ly_ref[slot])
    o_ref[:, 0:hw] = xm_ref[:, 0:hw] + _dot(g, ya)
    o_ref[:, hw:] = xm_ref[:, hw:] + _dot(g, yb)


def _combine(nq, dst, meta, gates, lstart, xm, y):
    n, d = xm.shape
    hw = d // 2
    t = MOE_TILE
    n_tiles = n // t
    grid_spec = pltpu.PrefetchScalarGridSpec(
        num_scalar_prefetch=1,
        grid=(n_tiles,),
        in_specs=[pl.BlockSpec((1, 1, LOCAL_PIECES), lambda i, nq: (i, 0, 0), memory_space=pltpu.SMEM),
                  pl.BlockSpec((1, 1, LOCAL_PIECES), lambda i, nq: (jnp.minimum(i + 1, n_tiles - 1), 0, 0),
                               memory_space=pltpu.SMEM),
                  pl.BlockSpec((t, LANE), lambda i, nq: (i, 0)),
                  pl.BlockSpec((t, LANE), lambda i, nq: (i, 0)),
                  pl.BlockSpec((1, 1, LANE), lambda i, nq: (i, 0, 0)),
                  pl.BlockSpec((t, d), lambda i, nq: (i, 0)),
                  pl.BlockSpec(memory_space=pl.ANY)],
        out_specs=pl.BlockSpec((t, d), lambda i, nq: (i, 0)),
        scratch_shapes=[pltpu.VMEM((2, LOCAL_ROWS, hw), jnp.uint32), pltpu.SemaphoreType.DMA((2,))],
    )
    return pl.pallas_call(
        _combine_kernel,
        grid_spec=grid_spec,
        out_shape=jax.ShapeDtypeStruct((n, d), f32),
        compiler_params=pltpu.CompilerParams(
            dimension_semantics=("arbitrary",), vmem_limit_bytes=VMEM_LIMIT),
        name="combine",
    )(nq, dst, dst, meta, gates, lstart, xm, y)


def _rel_bias_table(rel_bias):
    band = BAND_CHUNKS * CHUNK
    dist = (np.arange(band + CHUNK - 1) - (CHUNK - 1))[::-1]
    ext = rel_bias[:, np.clip(dist, -REL_CLIP, REL_CLIP) + REL_CLIP]
    rows = [lax.slice_in_dim(ext, CHUNK - 1 - i, CHUNK - 1 - i + band, axis=1) for i in range(CHUNK)]
    return jnp.stack(rows, axis=1).astype(f32)


def _round_up(x, m):
    return (x + m - 1) // m * m


def _routing_tables(cnt, n_tokens):
    experts = jnp.arange(N_EXPERTS, dtype=jnp.int32)
    c = cnt.reshape(-1, LANE)[:, :N_EXPERTS].astype(jnp.int32)
    n_tiles = c.shape[0]
    cp = _round_up(c, DMA_ROWS)
    lend = jnp.cumsum(cp, axis=1)
    lstart = lend - cp
    nq = (lend[:, -1] // DMA_ROWS).astype(jnp.int32)
    region = jnp.sum(cp, axis=0)
    padded = _round_up(region, MOE_BLOCK)
    pend = jnp.cumsum(padded)
    pstart = pend - padded
    base = pstart[None, :] + jnp.cumsum(cp, axis=0) - cp
    q0 = jnp.arange(LOCAL_PIECES, dtype=jnp.int32) * DMA_ROWS
    e_q = jnp.minimum(jnp.sum(lend[:, None, :] <= q0[None, :, None], axis=-1), N_EXPERTS - 1)
    shift = jnp.sum(jnp.where(e_q[:, :, None] == experts, (base - lstart)[:, None, :], 0), axis=-1)
    dst = jnp.where(q0[None, :] < lend[:, -1:], shift + q0[None, :], 0).astype(jnp.int32)

    n_blocks = -(-(n_tokens * TOP_K + n_tiles * N_EXPERTS * (DMA_ROWS - 1)) // MOE_BLOCK) + N_EXPERTS
    blk0 = jnp.arange(n_blocks, dtype=jnp.int32) * MOE_BLOCK
    n_used = (pend[-1] // MOE_BLOCK).astype(jnp.int32)
    be = jnp.minimum(jnp.sum(pend[None, :] <= blk0[:, None], axis=1), N_EXPERTS - 1).astype(jnp.int32)
    last = jnp.sum(jnp.where(jnp.arange(n_blocks) == n_used - 1, be, 0))
    be = jnp.where(jnp.arange(n_blocks) < n_used, be, last)
    lstart_rows = jnp.pad(lstart, ((0, 0), (0, LANE - N_EXPERTS))).reshape(n_tiles, 1, LANE)
    pad_start = (pstart + region).astype(jnp.int32)
    pad_pieces = ((padded - region) // DMA_ROWS).astype(jnp.int32)
    return (nq, pad_start, pad_pieces, dst.reshape(n_tiles, 1, LOCAL_PIECES),
            lstart_rows.astype(jnp.int32), be, n_used.reshape(1), n_blocks)


def _layer(x, norm1_g, w_in, gate_up, gate_bias, gla_norm_g, q_norm_g, k_norm_g, rel_bias, w_out,
           norm2_g, router_w, router_b, moe_w_in, moe_b_in, moe_w_out, moe_b_out):
    batch, seq, d = x.shape
    n = batch * seq
    x2 = x.reshape(n, d)

    pieces = jnp.split(w_in, np.cumsum(IN_SIZES)[:-1].tolist(), axis=-1)
    pieces[3] = jnp.pad(pieces[3], ((0, 0), (0, LANE - GLA_GATE_RANK)))
    widths = [p.shape[-1] for p in pieces]
    w_all = jnp.concatenate(pieces, axis=-1).astype(bf16)
    gq, gk, gv, glr, gog, aq, ak, av = _inproj(x2, norm1_g.reshape(1, d), w_all, widths)

    gup = jnp.pad(gate_up, ((0, LANE - GLA_GATE_RANK), (0, 0))).astype(bf16)
    o_gla = _gla(gq, gk, gv, glr, gog, gup, gate_bias.reshape(1, -1), gla_norm_g.reshape(1, -1),
                 batch, seq)
    tile2 = lambda g: jnp.tile(g.reshape(1, -1), (1, LANE // ATT_DH))
    o_att = _attention(aq, ak, av, _rel_bias_table(rel_bias), tile2(q_norm_g), tile2(k_norm_g),
                       batch, seq)

    rw = jnp.pad(router_w, ((0, 0), (0, LANE - N_EXPERTS)))
    rw_hi = rw.astype(bf16)
    rw_lo = (rw - rw_hi.astype(f32)).astype(bf16)
    rb = jnp.pad(router_b, (0, LANE - N_EXPERTS)).reshape(1, LANE)
    xm, h2, meta, gates, cnt = _mid(o_gla, o_att, x2, w_out.astype(bf16), norm2_g.reshape(1, d),
                                    rw_hi, rw_lo, rb)

    nq, pad_start, pad_pieces, dst, lstart, be, n_used, n_blocks = _routing_tables(cnt, n)
    xb = _dispatch(nq, pad_start, pad_pieces, n_used, dst, meta, lstart, h2, n_blocks * MOE_BLOCK)
    y = _experts(be, n_used, xb, moe_w_in, moe_b_in, moe_w_out, moe_b_out)
    out = _combine(nq, dst, meta, gates, lstart, xm, y)
    return out.reshape(batch, seq, d)


def kernel(x, norm1_g, w_in, gla_gate_up, gla_gate_bias, gla_norm_g, q_norm_g, k_norm_g, rel_bias, w_out, norm2_g, router_w, router_b, moe_w_in, moe_b_in, moe_w_out, moe_b_out):
    for l in range(norm1_g.shape[0]):
        x = _layer(x, norm1_g[l], w_in[l], gla_gate_up[l], gla_gate_bias[l], gla_norm_g[l],
                   q_norm_g[l], k_norm_g[l], rel_bias[l], w_out[l], norm2_g[l], router_w[l],
                   router_b[l], moe_w_in[l], moe_b_in[l], moe_w_out[l], moe_b_out[l])
    return x
```

```python
import functools

import numpy as np
import jax
import jax.numpy as jnp
from jax import lax
from jax.experimental import pallas as pl
from jax.experimental.pallas import tpu as pltpu

CHUNK = 64
EPS = 1e-6
GLA_HEADS = 4
GLA_DK = 64
GLA_DV = 128
GLA_GATE_RANK = 16
GLA_GATE_TAU = 16.0
ATT_HEADS = 8
ATT_DH = 64
N_BACK_CHUNKS = 8
BAND_CHUNKS = N_BACK_CHUNKS + 1
REL_CLIP = 256
MASK_VALUE = -1e30
N_EXPERTS = 32
TOP_K = 4
SWIGLU_ALPHA = 1.702
SWIGLU_LIMIT = 7.0
MOE_BLOCK = 512
EXPERT_SUB_BLOCK = 256

LANE = 128
GLA_QK_W = GLA_HEADS * GLA_DK
GLA_V_W = GLA_HEADS * GLA_DV
ATT_W = ATT_HEADS * ATT_DH
IN_SIZES = (GLA_QK_W, GLA_QK_W, GLA_V_W, GLA_GATE_RANK, GLA_V_W, ATT_W, ATT_W, ATT_W)
SEQ_TILE = N_BACK_CHUNKS * CHUNK
ROW_TILE = 512
MOE_TILE = 256
DMA_ROWS = 8
LOCAL_PIECES = 160
LOCAL_ROWS = LOCAL_PIECES * DMA_ROWS
MIN_PIECES = MOE_TILE * TOP_K // DMA_ROWS
ISSUE_UNROLL = 4
VMEM_LIMIT = 48 * 1024 * 1024

f32 = jnp.float32
bf16 = jnp.bfloat16


def _rms(x, g):
    return x * lax.rsqrt(jnp.mean(x * x, axis=-1, keepdims=True) + EPS) * g


def _dot(a, b):
    return jnp.dot(a, b, preferred_element_type=f32)


def _dot_nt(a, b):
    return lax.dot_general(a, b, (((1,), (1,)), ((), ())), preferred_element_type=f32)


def _dot_tn(a, b):
    return lax.dot_general(a, b, (((0,), (0,)), ((), ())), preferred_element_type=f32)


def _split_bf16(x):
    hi = x.astype(bf16)
    lo = (x - hi.astype(f32)).astype(bf16)
    return hi, lo


def _inproj_kernel(x_ref, g_ref, w_ref, *out_refs):
    h = _rms(x_ref[...], g_ref[...]).astype(bf16)
    off = 0
    for o_ref in out_refs:
        w = o_ref.shape[-1]
        o_ref[...] = _dot(h, w_ref[:, off:off + w]).astype(o_ref.dtype)
        off += w


def _inproj(x2, g, w, widths):
    n, d = x2.shape
    return pl.pallas_call(
        _inproj_kernel,
        grid=(n // ROW_TILE,),
        in_specs=[
            pl.BlockSpec((ROW_TILE, d), lambda i: (i, 0)),
            pl.BlockSpec((1, d), lambda i: (0, 0)),
            pl.BlockSpec(w.shape, lambda i: (0, 0)),
        ],
        out_specs=[pl.BlockSpec((ROW_TILE, wd), lambda i: (i, 0)) for wd in widths],
        out_shape=[jax.ShapeDtypeStruct((n, wd), bf16) for wd in widths],
        compiler_params=pltpu.CompilerParams(
            dimension_semantics=("arbitrary",), vmem_limit_bytes=VMEM_LIMIT),
        name="inproj",
    )(x2, g, w)


def _gla_kernel(q_ref, k_ref, v_ref, lr_ref, og_ref, gup_ref, gb_ref, ng_ref, o_ref, st_ref,
                u_ref, sb_ref):
    t = q_ref.shape[0]

    @pl.when(pl.program_id(1) == 0)
    def _():
        st_ref[...] = jnp.zeros_like(st_ref)

    z = _dot(lr_ref[...], gup_ref[...]) + gb_ref[...]
    log_a = (jnp.minimum(z, 0.0) - jnp.log1p(jnp.exp(-jnp.abs(z)))) * (1.0 / GLA_GATE_TAU)
    row = lax.broadcasted_iota(jnp.int32, (t, t), 0)
    col = lax.broadcasted_iota(jnp.int32, (t, t), 1)
    tri = jnp.where((col <= row) & ((col // CHUNK) == (row // CHUNK)), 1.0, 0.0).astype(bf16)
    la_hi, la_lo = _split_bf16(log_a)
    cum_all = _dot(tri, la_hi) + _dot(tri, la_lo)
    lane = lax.broadcasted_iota(jnp.int32, (1, LANE), 1)
    half_mask = (lane < GLA_DK, lane >= GLA_DK)
    n_chunks = t // CHUNK
    chunk_rows = [slice(c * CHUNK, (c + 1) * CHUNK) for c in range(n_chunks)]
    pair_of = lambda h: slice((h // 2) * LANE, (h // 2 + 1) * LANE)
    head_of = lambda h: slice(h * GLA_DV, (h + 1) * GLA_DV)

    decs = []
    for c, rows in enumerate(chunk_rows):
        cum = cum_all[rows]
        tot = cum[CHUNK - 1:CHUNK]
        kdec = k_ref[rows, :].astype(f32) * jnp.exp(tot - cum)
        decs.append(jnp.exp(tot))
        for h in range(GLA_HEADS):
            kd = jnp.where(half_mask[h % 2], kdec[:, pair_of(h)], 0.0).astype(bf16)
            u_ref[c, h] = _dot_tn(v_ref[rows, head_of(h)], kd)

    for h in range(GLA_HEADS):
        st = st_ref[h]
        for c in range(n_chunks):
            st = st * decs[c][:, pair_of(h)] + u_ref[c, h]
            sb_ref[c, h] = st.astype(bf16)
        st_ref[h] = st

    for c, rows in enumerate(chunk_rows):
        for h in range(GLA_HEADS):
            o = _dot_nt(q_ref[rows, pair_of(h)], sb_ref[c, h]) * (GLA_DK ** -0.5)
            o = _rms(o, ng_ref[...])
            g = og_ref[rows, head_of(h)].astype(f32)
            o_ref[rows, head_of(h)] = (o * (g / (1.0 + jnp.exp(-g)))).astype(o_ref.dtype)


def _gla(gq, gk, gv, glr, gog, gup, gb, ng, batch, seq):
    nt = seq // SEQ_TILE
    tile = lambda w: pl.BlockSpec((SEQ_TILE, w), lambda b, i: (b * nt + i, 0))
    full = lambda a: pl.BlockSpec(a.shape, lambda b, i: (0,) * a.ndim)
    return pl.pallas_call(
        _gla_kernel,
        grid=(batch, nt),
        in_specs=[tile(GLA_QK_W), tile(GLA_QK_W), tile(GLA_V_W), tile(LANE), tile(GLA_V_W),
                  full(gup), full(gb), full(ng)],
        out_specs=tile(GLA_V_W),
        out_shape=jax.ShapeDtypeStruct((batch * seq, GLA_V_W), bf16),
        scratch_shapes=[pltpu.VMEM((GLA_HEADS, GLA_DV, LANE), f32),
                        pltpu.VMEM((SEQ_TILE // CHUNK, GLA_HEADS, GLA_DV, LANE), f32),
                        pltpu.VMEM((SEQ_TILE // CHUNK, GLA_HEADS, GLA_DV, LANE), bf16)],
        compiler_params=pltpu.CompilerParams(
            dimension_semantics=("arbitrary", "arbitrary"), vmem_limit_bytes=VMEM_LIMIT),
        name="gla",
    )(gq, gk, gv, glr, gog, gup, gb, ng)


def _head_norm(x, g, ones_bd):
    sq_hi, sq_lo = _split_bf16(x * x)
    ssq = _dot(sq_hi, ones_bd) + _dot(sq_lo, ones_bd)
    return x * lax.rsqrt(ssq * (1.0 / ATT_DH) + EPS) * g


def _att_kernel(q_ref, k_ref, v_ref, bias_ref, qg_ref, kg_ref, o_ref, qs_ref, kb_ref, vb_ref,
                sn_ref, en_ref):
    t = q_ref.shape[0]
    n_pairs = ATT_W // LANE
    band = BAND_CHUNKS * CHUNK
    first = pl.program_id(1) == 0

    @pl.when(first)
    def _():
        kb_ref[0:t, :] = jnp.zeros((t, ATT_W), bf16)
        vb_ref[0:t, :] = jnp.zeros((t, 2 * ATT_W), bf16)

    @pl.when(jnp.logical_not(first))
    def _():
        kb_ref[0:t, :] = kb_ref[t:2 * t, :]
        vb_ref[0:t, :] = vb_ref[t:2 * t, :]

    lane = lax.broadcasted_iota(jnp.int32, (1, LANE), 1)
    lo = lane < ATT_DH
    r_i = lax.broadcasted_iota(jnp.int32, (LANE, LANE), 0)
    c_i = lax.broadcasted_iota(jnp.int32, (LANE, LANE), 1)
    ones_bd = jnp.where((r_i < ATT_DH) == (c_i < ATT_DH), 1.0, 0.0).astype(bf16)

    for p in range(n_pairs):
        pair = slice(p * LANE, (p + 1) * LANE)
        kb_ref[t:2 * t, pair] = _head_norm(k_ref[:, pair].astype(f32), kg_ref[...], ones_bd).astype(bf16)
        qn = _head_norm(q_ref[:, pair].astype(f32), qg_ref[...], ones_bd) * (ATT_DH ** -0.5)
        q_lo = jnp.where(lo, qn, 0.0).astype(bf16)
        q_hi = jnp.where(lo, 0.0, qn).astype(bf16)
        for c in range(t // CHUNK):
            rows = slice(c * CHUNK, (c + 1) * CHUNK)
            qs_ref[c * n_pairs + p, 0:CHUNK, :] = q_lo[rows]
            qs_ref[c * n_pairs + p, CHUNK:2 * CHUNK, :] = q_hi[rows]
        vb_ref[t:2 * t, 2 * p * LANE:(2 * p + 1) * LANE] = v_ref[:, pair]
        vb_ref[t:2 * t, (2 * p + 1) * LANE:(2 * p + 2) * LANE] = jnp.ones((t, LANE), bf16)

    colk = lax.broadcasted_iota(jnp.int32, (1, band), 1)

    def chunk_loop(masked):
        n_chunks = t // CHUNK

        def scores(c, p):
            k2 = kb_ref[pl.ds(pl.multiple_of(c * CHUNK, CHUNK), band), p * LANE:(p + 1) * LANE]
            return _dot_nt(qs_ref[c * n_pairs + p], k2)

        def weighted_values(e, r, p):
            v2 = vb_ref[pl.ds(r, band), 2 * p * LANE:(2 * p + 2) * LANE]
            pvl = _dot(e, v2)
            pv = pvl[:, 0:LANE] / pvl[:, LANE:2 * LANE]
            o_ref[pl.ds(r, CHUNK), p * LANE:(p + 1) * LANE] = jnp.where(
                lo, pv[0:CHUNK], pv[CHUNK:2 * CHUNK]).astype(o_ref.dtype)

        def chunk_body(c, carry):
            r0 = pl.multiple_of(c * CHUNK, CHUNK)
            r_prev = pl.multiple_of(jnp.maximum(c - 1, 0) * CHUNK, CHUNK)
            s_next = sn_ref[...]
            e_prev = en_ref[...]
            for p in range(n_pairs):
                s = s_next + bias_ref[p]
                if p + 1 < n_pairs:
                    s_next = scores(c, p + 1)
                else:
                    sn_ref[...] = scores(jnp.minimum(c + 1, n_chunks - 1), 0)
                if masked:
                    s = jnp.where(colk >= t - c * CHUNK, s, MASK_VALUE)
                e = jnp.exp(s - jnp.max(s, axis=-1, keepdims=True)).astype(bf16)
                if p == 0:
                    weighted_values(e_prev, r_prev, n_pairs - 1)
                else:
                    weighted_values(e_prev, r0, p - 1)
                e_prev = e
            en_ref[...] = e_prev
            return carry

        sn_ref[...] = scores(jnp.int32(0), 0)
        en_ref[...] = jnp.ones_like(en_ref)
        lax.fori_loop(0, n_chunks, chunk_body, 0)
        weighted_values(en_ref[...], (n_chunks - 1) * CHUNK, n_pairs - 1)

    @pl.when(first)
    def _():
        chunk_loop(True)

    @pl.when(jnp.logical_not(first))
    def _():
        chunk_loop(False)


def _attention(aq, ak, av, bias, qg, kg, batch, seq):
    nt = seq // SEQ_TILE
    n_pairs = ATT_W // LANE
    tile = pl.BlockSpec((SEQ_TILE, ATT_W), lambda b, i: (b * nt + i, 0))
    full = lambda a: pl.BlockSpec(a.shape, lambda b, i: (0,) * a.ndim)
    bias2 = bias.reshape(n_pairs, 2 * CHUNK, BAND_CHUNKS * CHUNK)
    return pl.pallas_call(
        _att_kernel,
        grid=(batch, nt),
        in_specs=[tile, tile, tile, full(bias2), full(qg), full(kg)],
        out_specs=tile,
        out_shape=jax.ShapeDtypeStruct((batch * seq, ATT_W), bf16),
        scratch_shapes=[pltpu.VMEM((SEQ_TILE // CHUNK * n_pairs, 2 * CHUNK, LANE), bf16),
                        pltpu.VMEM((2 * SEQ_TILE, ATT_W), bf16),
                        pltpu.VMEM((2 * SEQ_TILE, 2 * ATT_W), bf16),
                        pltpu.VMEM((2 * CHUNK, BAND_CHUNKS * CHUNK), f32),
                        pltpu.VMEM((2 * CHUNK, BAND_CHUNKS * CHUNK), bf16)],
        compiler_params=pltpu.CompilerParams(
            dimension_semantics=("arbitrary", "arbitrary"), vmem_limit_bytes=VMEM_LIMIT),
        name="attention",
    )(aq, ak, av, bias2, qg, kg)


def _mid_kernel(og_ref, oa_ref, x_ref, wo_ref, g2_ref, rwh_ref, rwl_ref, rb_ref,
                xm_ref, h_ref, meta_ref, gate_ref, cnt_ref):
    t, d = x_ref.shape
    xm = x_ref[...] + _dot(og_ref[...], wo_ref[0:GLA_V_W, :]) + _dot(oa_ref[...], wo_ref[GLA_V_W:, :])
    xm_ref[...] = xm
    h2 = _rms(xm, g2_ref[...])
    h_hi, h_lo = _split_bf16(h2)
    h_ref[...] = h_hi

    logits = (_dot(h_hi, rwh_ref[...]) + _dot(h_lo, rwh_ref[...]) + _dot(h_hi, rwl_ref[...])
              + rb_ref[...])
    lane = lax.broadcasted_iota(jnp.int32, (t, LANE), 1)
    lane_f = lane.astype(f32)
    l = jnp.where(lane < N_EXPERTS, logits, -jnp.inf)
    vals, idxs = [], []
    for _ in range(TOP_K):
        m = jnp.max(l, axis=-1, keepdims=True)
        ik = jnp.min(jnp.where(l == m, lane_f, float(LANE)), axis=-1, keepdims=True)
        vals.append(m)
        idxs.append(ik)
        l = jnp.where(lane_f == ik, -jnp.inf, l)
    es = [jnp.exp(v - vals[0]) for v in vals]
    den = es[0] + es[1] + es[2] + es[3]

    onehots = [lane_f == ik for ik in idxs]
    sel = jnp.zeros((t, LANE), f32)
    for oh in onehots:
        sel = sel + jnp.where(oh, 1.0, 0.0)
    row = lax.broadcasted_iota(jnp.int32, (t, t), 0)
    col = lax.broadcasted_iota(jnp.int32, (t, t), 1)
    tri = jnp.where((col < row) & ((col // MOE_TILE) == (row // MOE_TILE)), 1.0, 0.0).astype(bf16)
    prefix = _dot(tri, sel.astype(bf16))
    for s in range(t // MOE_TILE):
        cnt_ref[0, s:s + 1, :] = jnp.sum(sel[s * MOE_TILE:(s + 1) * MOE_TILE], axis=0, keepdims=True)

    meta = jnp.zeros((t, LANE), f32)
    gates = jnp.zeros((t, LANE), f32)
    for k in range(TOP_K):
        rank_k = jnp.sum(jnp.where(onehots[k], prefix, 0.0), axis=-1, keepdims=True)
        meta = jnp.where(lane == k, idxs[k], meta)
        meta = jnp.where(lane == TOP_K + k, rank_k, meta)
        gates = jnp.where(lane == k, es[k] / den, gates)
    meta_ref[...] = meta.astype(jnp.int32)
    gate_ref[...] = gates


def _mid(o_gla, o_att, x2, wo, g2, rwh, rwl, rb):
    n, d = x2.shape
    sub = ROW_TILE // MOE_TILE
    tile = lambda w: pl.BlockSpec((ROW_TILE, w), lambda i: (i, 0))
    full = lambda a: pl.BlockSpec(a.shape, lambda i: (0,) * a.ndim)
    return pl.pallas_call(
        _mid_kernel,
        grid=(n // ROW_TILE,),
        in_specs=[tile(GLA_V_W), tile(ATT_W), tile(d), full(wo), full(g2), full(rwh), full(rwl),
                  full(rb)],
        out_specs=[tile(d), tile(d), tile(LANE), tile(LANE),
                   pl.BlockSpec((1, sub, LANE), lambda i: (i, 0, 0))],
        out_shape=[jax.ShapeDtypeStruct((n, d), f32),
                   jax.ShapeDtypeStruct((n, d), bf16),
                   jax.ShapeDtypeStruct((n, LANE), jnp.int32),
                   jax.ShapeDtypeStruct((n, LANE), f32),
                   jax.ShapeDtypeStruct((n // ROW_TILE, sub, LANE), f32)],
        compiler_params=pltpu.CompilerParams(
            dimension_semantics=("arbitrary",), vmem_limit_bytes=VMEM_LIMIT),
        name="mid",
    )(o_gla, o_att, x2, wo, g2, rwh, rwl, rb)


def _slot_matrix(meta_ref, ls_ref, values):
    meta = meta_ref[...]
    t = meta.shape[0]
    lane = lax.broadcasted_iota(jnp.int32, (t, LANE), 1)
    starts = ls_ref[0].astype(f32)
    col = lax.broadcasted_iota(jnp.int32, (1, LOCAL_ROWS), 1)
    out = jnp.zeros((t, LOCAL_ROWS), f32)
    for k in range(TOP_K):
        start_k = jnp.sum(jnp.where(lane == meta[:, k:k + 1], starts, 0.0), axis=-1, keepdims=True)
        slot_k = start_k.astype(jnp.int32) + meta[:, TOP_K + k:TOP_K + k + 1]
        out = jnp.where(col == slot_k, values[k], out)
    return out


def _pack_bf16_pairs(a, b):
    return (pltpu.bitcast(a, jnp.uint32) >> 16) | (pltpu.bitcast(b, jnp.uint32) & jnp.uint32(0xFFFF0000))


def _unpack_bf16_pairs(w):
    a = pltpu.bitcast(w << 16, f32).astype(bf16)
    b = pltpu.bitcast(w & jnp.uint32(0xFFFF0000), f32).astype(bf16)
    return a, b


def _piece_copy(src_ref, dst_ref, src_row, dst_row, sem):
    return pltpu.make_async_copy(src_ref.at[pl.ds(pl.multiple_of(src_row, DMA_ROWS), DMA_ROWS)],
                                 dst_ref.at[pl.ds(pl.multiple_of(dst_row, DMA_ROWS), DMA_ROWS)], sem)


def _start_pieces(n, piece):
    def group(g, c):
        for j in range(ISSUE_UNROLL):
            piece(g * ISSUE_UNROLL + j).start()
        return c

    n_groups = n // ISSUE_UNROLL
    lax.fori_loop(0, n_groups, group, 0)
    lax.fori_loop(n_groups * ISSUE_UNROLL, n, lambda q, c: (piece(q).start(), c)[1], 0)


def _wait_pieces(n, piece, bulk):
    bulk.wait()
    lax.fori_loop(MIN_PIECES, n, lambda q, c: (piece(q).wait(), c)[1], 0)


def _zero_fill_padding(pad_start_ref, pad_pieces_ref, n_used_ref, xb_ref, z_ref, sem):
    z_ref[...] = jnp.zeros_like(z_ref)
    n_blocks = xb_ref.shape[0] // MOE_BLOCK

    def tail_piece(e, q):
        return _piece_copy(z_ref, xb_ref, 0, pad_start_ref[e] + q * DMA_ROWS, sem)

    def block_copy(b):
        row = pl.multiple_of(b * MOE_BLOCK, MOE_BLOCK)
        return pltpu.make_async_copy(z_ref, xb_ref.at[pl.ds(row, MOE_BLOCK)], sem)

    def each_tail_piece(fn):
        def per_expert(e, c):
            lax.fori_loop(0, pad_pieces_ref[e], lambda q, cc: (fn(tail_piece(e, q)), cc)[1], 0)
            return c
        lax.fori_loop(0, N_EXPERTS, per_expert, 0)

    def each_block(fn):
        lax.fori_loop(n_used_ref[0], n_blocks, lambda b, c: (fn(block_copy(b)), c)[1], 0)

    each_tail_piece(lambda cp: cp.start())
    each_block(lambda cp: cp.start())
    each_tail_piece(lambda cp: cp.wait())
    each_block(lambda cp: cp.wait())


def _dispatch_kernel(nq_ref, pad_start_ref, pad_pieces_ref, n_used_ref, dst_ref, meta_ref, ls_ref,
                     h_ref, xb_ref, l_ref, z_ref, sem):
    i = pl.program_id(0)
    hw = l_ref.shape[1]
    perm = _slot_matrix(meta_ref, ls_ref, [1.0] * TOP_K).T.astype(bf16)
    packed = _pack_bf16_pairs(_dot(perm, h_ref[:, 0:hw]), _dot(perm, h_ref[:, hw:]))

    def piece(q):
        return _piece_copy(l_ref, xb_ref, q * DMA_ROWS, dst_ref[0, 0, q], sem.at[0])

    def drain(n):
        min_rows = MIN_PIECES * DMA_ROWS
        bulk = pltpu.make_async_copy(l_ref.at[0:min_rows], xb_ref.at[0:min_rows], sem.at[0])
        _wait_pieces(n, piece, bulk)

    @pl.when(i > 0)
    def _():
        drain(nq_ref[jnp.maximum(i - 1, 0)])

    l_ref[...] = packed
    _start_pieces(nq_ref[i], piece)

    @pl.when(i == pl.num_programs(0) - 1)
    def _():
        _zero_fill_padding(pad_start_ref, pad_pieces_ref, n_used_ref, xb_ref, z_ref, sem.at[1])
        drain(nq_ref[i])


def _dispatch(nq, pad_start, pad_pieces, n_used, dst, meta, lstart, h, n_rows):
    n, d = h.shape
    hw = d // 2
    t = MOE_TILE
    grid_spec = pltpu.PrefetchScalarGridSpec(
        num_scalar_prefetch=4,
        grid=(n // t,),
        in_specs=[pl.BlockSpec((1, 1, LOCAL_PIECES), lambda i, *_: (i, 0, 0), memory_space=pltpu.SMEM),
                  pl.BlockSpec((t, LANE), lambda i, *_: (i, 0)),
                  pl.BlockSpec((1, 1, LANE), lambda i, *_: (i, 0, 0)),
                  pl.BlockSpec((t, d), lambda i, *_: (i, 0))],
        out_specs=pl.BlockSpec(memory_space=pl.ANY),
        scratch_shapes=[pltpu.VMEM((LOCAL_ROWS, hw), jnp.uint32),
                        pltpu.VMEM((MOE_BLOCK, hw), jnp.uint32),
                        pltpu.SemaphoreType.DMA((2,))],
    )
    return pl.pallas_call(
        _dispatch_kernel,
        grid_spec=grid_spec,
        out_shape=jax.ShapeDtypeStruct((n_rows, hw), jnp.uint32),
        compiler_params=pltpu.CompilerParams(
            dimension_semantics=("arbitrary",), vmem_limit_bytes=VMEM_LIMIT),
        name="dispatch",
    )(nq, pad_start, pad_pieces, n_used, dst, meta, lstart, h)


def _expert_kernel(be_ref, nu_ref, x_ref, wi_ref, bi_ref, wo_ref, bo_ref, y_ref,
                   wib_ref, wob_ref):
    b = pl.program_id(0)
    blk, hw = x_ref.shape
    dff = wo_ref.shape[1]

    @pl.when(b < nu_ref[0])
    def _():
        changed = jnp.logical_or(b == 0, be_ref[b] != be_ref[jnp.maximum(b - 1, 0)])

        @pl.when(changed)
        def _():
            wib_ref[...] = wi_ref[0].astype(bf16)
            wob_ref[...] = wo_ref[0].astype(bf16)

        def up(rows):
            xa, xb = _unpack_bf16_pairs(x_ref[rows, :])
            return _dot(xa, wib_ref[0:hw, :]) + _dot(xb, wib_ref[hw:, :]) + bi_ref[0]

        def down(rows, hc):
            glu = jnp.minimum(hc[:, :dff], SWIGLU_LIMIT)
            lin = jnp.clip(hc[:, dff:], -SWIGLU_LIMIT, SWIGLU_LIMIT)
            act = glu * (1.0 / (1.0 + jnp.exp(-SWIGLU_ALPHA * glu))) * (lin + 1.0)
            y = (_dot(act.astype(bf16), wob_ref[...]) + bo_ref[0]).astype(bf16).astype(f32)
            y_ref[rows, :] = _pack_bf16_pairs(y[:, 0:hw], y[:, hw:])

        subs = [slice(s * EXPERT_SUB_BLOCK, (s + 1) * EXPERT_SUB_BLOCK)
                for s in range(blk // EXPERT_SUB_BLOCK)]
        hc_next = up(subs[0])
        for s, rows in enumerate(subs):
            hc = hc_next
            if s + 1 < len(subs):
                hc_next = up(subs[s + 1])
            down(rows, hc)

    @pl.when(b >= nu_ref[0])
    def _():
        y_ref[...] = jnp.zeros_like(y_ref)


def _experts(block_expert, n_used, xb, w_in, b_in, w_out, b_out):
    n_rows, hw = xb.shape
    e, d, dff2 = w_in.shape
    dff = w_out.shape[1]
    n_blocks = n_rows // MOE_BLOCK
    grid_spec = pltpu.PrefetchScalarGridSpec(
        num_scalar_prefetch=2,
        grid=(n_blocks,),
        in_specs=[
            pl.BlockSpec((MOE_BLOCK, hw), lambda b, be, nu: (b, 0)),
            pl.BlockSpec((1, d, dff2), lambda b, be, nu: (be[b], 0, 0)),
            pl.BlockSpec((1, 1, dff2), lambda b, be, nu: (be[b], 0, 0)),
            pl.BlockSpec((1, dff, d), lambda b, be, nu: (be[b], 0, 0)),
            pl.BlockSpec((1, 1, d), lambda b, be, nu: (be[b], 0, 0)),
        ],
        out_specs=pl.BlockSpec((MOE_BLOCK, hw), lambda b, be, nu: (b, 0)),
        scratch_shapes=[pltpu.VMEM((d, dff2), bf16), pltpu.VMEM((dff, d), bf16)],
    )
    return pl.pallas_call(
        _expert_kernel,
        grid_spec=grid_spec,
        out_shape=jax.ShapeDtypeStruct((n_rows, hw), jnp.uint32),
        compiler_params=pltpu.CompilerParams(
            dimension_semantics=("arbitrary",), vmem_limit_bytes=VMEM_LIMIT),
        name="experts",
    )(block_expert, n_used, xb, w_in, b_in.reshape(e, 1, dff2), w_out, b_out.reshape(e, 1, d))


def _combine_kernel(nq_ref, dst_ref, dstn_ref, meta_ref, gate_ref, ls_ref, xm_ref, y_ref, o_ref,
                    ly_ref, sem):
    i = pl.program_id(0)
    last = pl.num_programs(0) - 1
    hw = ly_ref.shape[2]
    slot = i % 2

    def fetch(tile, table_ref, buf):
        ly_ref[buf, MOE_TILE * TOP_K:, :] = jnp.zeros((LOCAL_ROWS - MOE_TILE * TOP_K, hw), jnp.uint32)

        _start_pieces(nq_ref[tile], lambda q: _piece_copy(
            y_ref, ly_ref.at[buf], table_ref[0, 0, q], q * DMA_ROWS, sem.at[buf]))

    @pl.when(i == 0)
    def _():
        fetch(i, dst_ref, slot)

    @pl.when(i < last)
    def _():
        fetch(jnp.minimum(i + 1, last), dstn_ref, 1 - slot)

    min_rows = MIN_PIECES * DMA_ROWS
    _wait_pieces(
        nq_ref[i],
        lambda q: _piece_copy(y_ref, ly_ref.at[slot], dst_ref[0, 0, q], q * DMA_ROWS, sem.at[slot]),
        pltpu.make_async_copy(y_ref.at[0:min_rows], ly_ref.at[slot, 0:min_rows], sem.at[slot]))
    gates = gate_ref[...]
    g = _slot_matrix(meta_ref, ls_ref, [gates[:, k:k + 1] for k in range(TOP_K)]).astype(bf16)
    ya, yb = _unpack_bf16_pairs(ly_ref[slot])
    o_ref[:, 0:hw] = xm_ref[:, 0:hw] + _dot(g, ya)
    o_ref[:, hw:] = xm_ref[:, hw:] + _dot(g, yb)


def _combine(nq, dst, meta, gates, lstart, xm, y):
    n, d = xm.shape
    hw = d // 2
    t = MOE_TILE
    n_tiles = n // t
    grid_spec = pltpu.PrefetchScalarGridSpec(
        num_scalar_prefetch=1,
        grid=(n_tiles,),
        in_specs=[pl.BlockSpec((1, 1, LOCAL_PIECES), lambda i, nq: (i, 0, 0), memory_space=pltpu.SMEM),
                  pl.BlockSpec((1, 1, LOCAL_PIECES), lambda i, nq: (jnp.minimum(i + 1, n_tiles - 1), 0, 0),
                               memory_space=pltpu.SMEM),
                  pl.BlockSpec((t, LANE), lambda i, nq: (i, 0)),
                  pl.BlockSpec((t, LANE), lambda i, nq: (i, 0)),
                  pl.BlockSpec((1, 1, LANE), lambda i, nq: (i, 0, 0)),
                  pl.BlockSpec((t, d), lambda i, nq: (i, 0)),
                  pl.BlockSpec(memory_space=pl.ANY)],
        out_specs=pl.BlockSpec((t, d), lambda i, nq: (i, 0)),
        scratch_shapes=[pltpu.VMEM((2, LOCAL_ROWS, hw), jnp.uint32), pltpu.SemaphoreType.DMA((2,))],
    )
    return pl.pallas_call(
        _combine_kernel,
        grid_spec=grid_spec,
        out_shape=jax.ShapeDtypeStruct((n, d), f32),
        compiler_params=pltpu.CompilerParams(
            dimension_semantics=("arbitrary",), vmem_limit_bytes=VMEM_LIMIT),
        name="combine",
    )(nq, dst, dst, meta, gates, lstart, xm, y)


def _rel_bias_table(rel_bias):
    band = BAND_CHUNKS * CHUNK
    dist = (np.arange(band + CHUNK - 1) - (CHUNK - 1))[::-1]
    ext = rel_bias[:, np.clip(dist, -REL_CLIP, REL_CLIP) + REL_CLIP]
    rows = [lax.slice_in_dim(ext, CHUNK - 1 - i, CHUNK - 1 - i + band, axis=1) for i in range(CHUNK)]
    return jnp.stack(rows, axis=1).astype(f32)


def _round_up(x, m):
    return (x + m - 1) // m * m


def _routing_tables(cnt, n_tokens):
    experts = jnp.arange(N_EXPERTS, dtype=jnp.int32)
    c = cnt.reshape(-1, LANE)[:, :N_EXPERTS].astype(jnp.int32)
    n_tiles = c.shape[0]
    cp = _round_up(c, DMA_ROWS)
    lend = jnp.cumsum(cp, axis=1)
    lstart = lend - cp
    nq = (lend[:, -1] // DMA_ROWS).astype(jnp.int32)
    region = jnp.sum(cp, axis=0)
    padded = _round_up(region, MOE_BLOCK)
    pend = jnp.cumsum(padded)
    pstart = pend - padded
    base = pstart[None, :] + jnp.cumsum(cp, axis=0) - cp
    q0 = jnp.arange(LOCAL_PIECES, dtype=jnp.int32) * DMA_ROWS
    e_q = jnp.minimum(jnp.sum(lend[:, None, :] <= q0[None, :, None], axis=-1), N_EXPERTS - 1)
    shift = jnp.sum(jnp.where(e_q[:, :, None] == experts, (base - lstart)[:, None, :], 0), axis=-1)
    dst = jnp.where(q0[None, :] < lend[:, -1:], shift + q0[None, :], 0).astype(jnp.int32)

    n_blocks = -(-(n_tokens * TOP_K + n_tiles * N_EXPERTS * (DMA_ROWS - 1)) // MOE_BLOCK) + N_EXPERTS
    blk0 = jnp.arange(n_blocks, dtype=jnp.int32) * MOE_BLOCK
    n_used = (pend[-1] // MOE_BLOCK).astype(jnp.int32)
    be = jnp.minimum(jnp.sum(pend[None, :] <= blk0[:, None], axis=1), N_EXPERTS - 1).astype(jnp.int32)
    last = jnp.sum(jnp.where(jnp.arange(n_blocks) == n_used - 1, be, 0))
    be = jnp.where(jnp.arange(n_blocks) < n_used, be, last)
    lstart_rows = jnp.pad(lstart, ((0, 0), (0, LANE - N_EXPERTS))).reshape(n_tiles, 1, LANE)
    pad_start = (pstart + region).astype(jnp.int32)
    pad_pieces = ((padded - region) // DMA_ROWS).astype(jnp.int32)
    return (nq, pad_start, pad_pieces, dst.reshape(n_tiles, 1, LOCAL_PIECES),
            lstart_rows.astype(jnp.int32), be, n_used.reshape(1), n_blocks)


def _layer(x, norm1_g, w_in, gate_up, gate_bias, gla_norm_g, q_norm_g, k_norm_g, rel_bias, w_out,
           norm2_g, router_w, router_b, moe_w_in, moe_b_in, moe_w_out, moe_b_out):
    batch, seq, d = x.shape
    n = batch * seq
    x2 = x.reshape(n, d)

    pieces = jnp.split(w_in, np.cumsum(IN_SIZES)[:-1].tolist(), axis=-1)
    pieces[3] = jnp.pad(pieces[3], ((0, 0), (0, LANE - GLA_GATE_RANK)))
    widths = [p.shape[-1] for p in pieces]
    w_all = jnp.concatenate(pieces, axis=-1).astype(bf16)
    gq, gk, gv, glr, gog, aq, ak, av = _inproj(x2, norm1_g.reshape(1, d), w_all, widths)

    gup = jnp.pad(gate_up, ((0, LANE - GLA_GATE_RANK), (0, 0))).astype(bf16)
    o_gla = _gla(gq, gk, gv, glr, gog, gup, gate_bias.reshape(1, -1), gla_norm_g.reshape(1, -1),
                 batch, seq)
    tile2 = lambda g: jnp.tile(g.reshape(1, -1), (1, LANE // ATT_DH))
    o_att = _attention(aq, ak, av, _rel_bias_table(rel_bias), tile2(q_norm_g), tile2(k_norm_g),
                       batch, seq)

    rw = jnp.pad(router_w, ((0, 0), (0, LANE - N_EXPERTS)))
    rw_hi = rw.astype(bf16)
    rw_lo = (rw - rw_hi.astype(f32)).astype(bf16)
    rb = jnp.pad(router_b, (0, LANE - N_EXPERTS)).reshape(1, LANE)
    xm, h2, meta, gates, cnt = _mid(o_gla, o_att, x2, w_out.astype(bf16), norm2_g.reshape(1, d),
                                    rw_hi, rw_lo, rb)

    nq, pad_start, pad_pieces, dst, lstart, be, n_used, n_blocks = _routing_tables(cnt, n)
    xb = _dispatch(nq, pad_start, pad_pieces, n_used, dst, meta, lstart, h2, n_blocks * MOE_BLOCK)
    y = _experts(be, n_used, xb, moe_w_in, moe_b_in, moe_w_out, moe_b_out)
    out = _combine(nq, dst, meta, gates, lstart, xm, y)
    return out.reshape(batch, seq, d)


def kernel(x, norm1_g, w_in, gla_gate_up, gla_gate_bias, gla_norm_g, q_norm_g, k_norm_g, rel_bias, w_out, norm2_g, router_w, router_b, moe_w_in, moe_b_in, moe_w_out, moe_b_out):
    for l in range(norm1_g.shape[0]):
        x = _layer(x, norm1_g[l], w_in[l], gla_gate_up[l], gla_gate_bias[l], gla_norm_g[l],
                   q_norm_g[l], k_norm_g[l], rel_bias[l], w_out[l], norm2_g[l], router_w[l],
                   router_b[l], moe_w_in[l], moe_b_in[l], moe_w_out[l], moe_b_out[l])
    return x
```

```python
import functools

import numpy as np
import jax
import jax.numpy as jnp
from jax import lax
from jax.experimental import pallas as pl
from jax.experimental.pallas import tpu as pltpu

CHUNK = 64
EPS = 1e-6
GLA_HEADS = 4
GLA_DK = 64
GLA_DV = 128
GLA_GATE_RANK = 16
GLA_GATE_TAU = 16.0
ATT_HEADS = 8
ATT_DH = 64
N_BACK_CHUNKS = 8
BAND_CHUNKS = N_BACK_CHUNKS + 1
REL_CLIP = 256
MASK_VALUE = -1e30
N_EXPERTS = 32
TOP_K = 4
SWIGLU_ALPHA = 1.702
SWIGLU_LIMIT = 7.0
MOE_BLOCK = 512
EXPERT_SUB_BLOCK = 256

LANE = 128
SUBLANE = 8
GLA_QK_W = GLA_HEADS * GLA_DK
GLA_V_W = GLA_HEADS * GLA_DV
ATT_W = ATT_HEADS * ATT_DH
IN_SIZES = (GLA_QK_W, GLA_QK_W, GLA_V_W, GLA_GATE_RANK, GLA_V_W, ATT_W, ATT_W, ATT_W)
SEQ_TILE = N_BACK_CHUNKS * CHUNK
ROW_TILE = 512
MOE_TILE = 256
DMA_ROWS = SUBLANE
LOCAL_PIECES = 160
LOCAL_ROWS = LOCAL_PIECES * DMA_ROWS
MIN_PIECES = MOE_TILE * TOP_K // DMA_ROWS
ISSUE_UNROLL = 4
VMEM_LIMIT = 48 * 1024 * 1024

f32 = jnp.float32
bf16 = jnp.bfloat16


def _rms(x, g):
    return x * lax.rsqrt(jnp.mean(x * x, axis=-1, keepdims=True) + EPS) * g


def _dot(a, b):
    return jnp.dot(a, b, preferred_element_type=f32)


def _dot_nt(a, b):
    return lax.dot_general(a, b, (((1,), (1,)), ((), ())), preferred_element_type=f32)


def _dot_tn(a, b):
    return lax.dot_general(a, b, (((0,), (0,)), ((), ())), preferred_element_type=f32)


def _split_bf16(x):
    hi = x.astype(bf16)
    lo = (x - hi.astype(f32)).astype(bf16)
    return hi, lo


def _inproj_kernel(x_ref, g_ref, w_ref, *out_refs):
    h = _rms(x_ref[...], g_ref[...]).astype(bf16)
    off = 0
    for o_ref in out_refs:
        w = o_ref.shape[-1]
        o_ref[...] = _dot(h, w_ref[:, off:off + w]).astype(o_ref.dtype)
        off += w


def _inproj(x2, g, w, widths):
    n, d = x2.shape
    return pl.pallas_call(
        _inproj_kernel,
        grid=(n // ROW_TILE,),
        in_specs=[
            pl.BlockSpec((ROW_TILE, d), lambda i: (i, 0)),
            pl.BlockSpec((1, d), lambda i: (0, 0)),
            pl.BlockSpec(w.shape, lambda i: (0, 0)),
        ],
        out_specs=[pl.BlockSpec((ROW_TILE, wd), lambda i: (i, 0)) for wd in widths],
        out_shape=[jax.ShapeDtypeStruct((n, wd), bf16) for wd in widths],
        compiler_params=pltpu.CompilerParams(
            dimension_semantics=("arbitrary",), vmem_limit_bytes=VMEM_LIMIT),
        name="inproj",
    )(x2, g, w)


def _gla_kernel(q_ref, k_ref, v_ref, lr_ref, og_ref, gup_ref, gb_ref, ng_ref, o_ref, st_ref,
                u_ref, sb_ref):
    t = q_ref.shape[0]

    @pl.when(pl.program_id(1) == 0)
    def _():
        st_ref[...] = jnp.zeros_like(st_ref)

    z = _dot(lr_ref[...], gup_ref[...]) + gb_ref[...]
    log_a = (jnp.minimum(z, 0.0) - jnp.log1p(jnp.exp(-jnp.abs(z)))) * (1.0 / GLA_GATE_TAU)
    row = lax.broadcasted_iota(jnp.int32, (t, t), 0)
    col = lax.broadcasted_iota(jnp.int32, (t, t), 1)
    tri = jnp.where((col <= row) & ((col // CHUNK) == (row // CHUNK)), 1.0, 0.0).astype(bf16)
    la_hi, la_lo = _split_bf16(log_a)
    cum_all = _dot(tri, la_hi) + _dot(tri, la_lo)
    lane = lax.broadcasted_iota(jnp.int32, (1, LANE), 1)
    half_mask = (lane < GLA_DK, lane >= GLA_DK)
    n_chunks = t // CHUNK
    chunk_rows = [slice(c * CHUNK, (c + 1) * CHUNK) for c in range(n_chunks)]
    pair_of = lambda h: slice((h // 2) * LANE, (h // 2 + 1) * LANE)
    head_of = lambda h: slice(h * GLA_DV, (h + 1) * GLA_DV)

    decs = []
    for c, rows in enumerate(chunk_rows):
        cum = cum_all[rows]
        tot = cum[CHUNK - 1:CHUNK]
        kdec = k_ref[rows, :].astype(f32) * jnp.exp(tot - cum)
        decs.append(jnp.exp(tot))
        for h in range(GLA_HEADS):
            kd = jnp.where(half_mask[h % 2], kdec[:, pair_of(h)], 0.0).astype(bf16)
            u_ref[c, h] = _dot_tn(v_ref[rows, head_of(h)], kd)

    for h in range(GLA_HEADS):
        st = st_ref[h]
        for c in range(n_chunks):
            st = st * decs[c][:, pair_of(h)] + u_ref[c, h]
            sb_ref[c, h] = st.astype(bf16)
        st_ref[h] = st

    for c, rows in enumerate(chunk_rows):
        for h in range(GLA_HEADS):
            o = _dot_nt(q_ref[rows, pair_of(h)], sb_ref[c, h]) * (GLA_DK ** -0.5)
            o = _rms(o, ng_ref[...])
            g = og_ref[rows, head_of(h)].astype(f32)
            o_ref[rows, head_of(h)] = (o * (g / (1.0 + jnp.exp(-g)))).astype(o_ref.dtype)


def _gla(gq, gk, gv, glr, gog, gup, gb, ng, batch, seq):
    nt = seq // SEQ_TILE
    tile = lambda w: pl.BlockSpec((SEQ_TILE, w), lambda b, i: (b * nt + i, 0))
    full = lambda a: pl.BlockSpec(a.shape, lambda b, i: (0,) * a.ndim)
    return pl.pallas_call(
        _gla_kernel,
        grid=(batch, nt),
        in_specs=[tile(GLA_QK_W), tile(GLA_QK_W), tile(GLA_V_W), tile(LANE), tile(GLA_V_W),
                  full(gup), full(gb), full(ng)],
        out_specs=tile(GLA_V_W),
        out_shape=jax.ShapeDtypeStruct((batch * seq, GLA_V_W), bf16),
        scratch_shapes=[pltpu.VMEM((GLA_HEADS, GLA_DV, LANE), f32),
                        pltpu.VMEM((SEQ_TILE // CHUNK, GLA_HEADS, GLA_DV, LANE), f32),
                        pltpu.VMEM((SEQ_TILE // CHUNK, GLA_HEADS, GLA_DV, LANE), bf16)],
        compiler_params=pltpu.CompilerParams(
            dimension_semantics=("arbitrary", "arbitrary"), vmem_limit_bytes=VMEM_LIMIT),
        name="gla",
    )(gq, gk, gv, glr, gog, gup, gb, ng)


def _head_norm(x, g, ones_bd):
    sq_hi, sq_lo = _split_bf16(x * x)
    ssq = _dot(sq_hi, ones_bd) + _dot(sq_lo, ones_bd)
    return x * lax.rsqrt(ssq * (1.0 / ATT_DH) + EPS) * g


def _att_kernel(q_ref, k_ref, v_ref, bias_ref, qg_ref, kg_ref, o_ref, qs_ref, kb_ref, vb_ref,
                sn_ref, en_ref):
    t = q_ref.shape[0]
    n_pairs = ATT_W // LANE
    band = BAND_CHUNKS * CHUNK
    first = pl.program_id(1) == 0

    @pl.when(first)
    def _():
        kb_ref[0:t, :] = jnp.zeros((t, ATT_W), bf16)
        vb_ref[0:t, :] = jnp.zeros((t, 2 * ATT_W), bf16)

    @pl.when(jnp.logical_not(first))
    def _():
        kb_ref[0:t, :] = kb_ref[t:2 * t, :]
        vb_ref[0:t, :] = vb_ref[t:2 * t, :]

    lane = lax.broadcasted_iota(jnp.int32, (1, LANE), 1)
    lo = lane < ATT_DH
    r_i = lax.broadcasted_iota(jnp.int32, (LANE, LANE), 0)
    c_i = lax.broadcasted_iota(jnp.int32, (LANE, LANE), 1)
    ones_bd = jnp.where((r_i < ATT_DH) == (c_i < ATT_DH), 1.0, 0.0).astype(bf16)

    for p in range(n_pairs):
        pair = slice(p * LANE, (p + 1) * LANE)
        kb_ref[t:2 * t, pair] = _head_norm(k_ref[:, pair].astype(f32), kg_ref[...], ones_bd).astype(bf16)
        qn = _head_norm(q_ref[:, pair].astype(f32), qg_ref[...], ones_bd) * (ATT_DH ** -0.5)
        q_lo = jnp.where(lo, qn, 0.0).astype(bf16)
        q_hi = jnp.where(lo, 0.0, qn).astype(bf16)
        for c in range(t // CHUNK):
            rows = slice(c * CHUNK, (c + 1) * CHUNK)
            qs_ref[c * n_pairs + p, 0:CHUNK, :] = q_lo[rows]
            qs_ref[c * n_pairs + p, CHUNK:2 * CHUNK, :] = q_hi[rows]
        vb_ref[t:2 * t, 2 * p * LANE:(2 * p + 1) * LANE] = v_ref[:, pair]
        vb_ref[t:2 * t, (2 * p + 1) * LANE:(2 * p + 2) * LANE] = jnp.ones((t, LANE), bf16)

    colk = lax.broadcasted_iota(jnp.int32, (1, band), 1)

    def chunk_loop(masked):
        n_chunks = t // CHUNK

        def scores(c, p):
            k2 = kb_ref[pl.ds(pl.multiple_of(c * CHUNK, CHUNK), band), p * LANE:(p + 1) * LANE]
            return _dot_nt(qs_ref[c * n_pairs + p], k2)

        def weighted_values(e, r, p):
            v2 = vb_ref[pl.ds(r, band), 2 * p * LANE:(2 * p + 2) * LANE]
            pvl = _dot(e, v2)
            pv = pvl[:, 0:LANE] / pvl[:, LANE:2 * LANE]
            o_ref[pl.ds(r, CHUNK), p * LANE:(p + 1) * LANE] = jnp.where(
                lo, pv[0:CHUNK], pv[CHUNK:2 * CHUNK]).astype(o_ref.dtype)

        def chunk_body(c, carry):
            r0 = pl.multiple_of(c * CHUNK, CHUNK)
            r_prev = pl.multiple_of(jnp.maximum(c - 1, 0) * CHUNK, CHUNK)
            s_next = sn_ref[...]
            e_prev = en_ref[...]
            for p in range(n_pairs):
                s = s_next + bias_ref[p]
                if p + 1 < n_pairs:
                    s_next = scores(c, p + 1)
                else:
                    sn_ref[...] = scores(jnp.minimum(c + 1, n_chunks - 1), 0)
                if masked:
                    s = jnp.where(colk >= t - c * CHUNK, s, MASK_VALUE)
                e = jnp.exp(s - jnp.max(s, axis=-1, keepdims=True)).astype(bf16)
                if p == 0:
                    weighted_values(e_prev, r_prev, n_pairs - 1)
                else:
                    weighted_values(e_prev, r0, p - 1)
                e_prev = e
            en_ref[...] = e_prev
            return carry

        sn_ref[...] = scores(jnp.int32(0), 0)
        en_ref[...] = jnp.ones_like(en_ref)
        lax.fori_loop(0, n_chunks, chunk_body, 0)
        weighted_values(en_ref[...], (n_chunks - 1) * CHUNK, n_pairs - 1)

    @pl.when(first)
    def _():
        chunk_loop(True)

    @pl.when(jnp.logical_not(first))
    def _():
        chunk_loop(False)


def _attention(aq, ak, av, bias, qg, kg, batch, seq):
    nt = seq // SEQ_TILE
    n_pairs = ATT_W // LANE
    tile = pl.BlockSpec((SEQ_TILE, ATT_W), lambda b, i: (b * nt + i, 0))
    full = lambda a: pl.BlockSpec(a.shape, lambda b, i: (0,) * a.ndim)
    bias2 = bias.reshape(n_pairs, 2 * CHUNK, BAND_CHUNKS * CHUNK)
    return pl.pallas_call(
        _att_kernel,
        grid=(batch, nt),
        in_specs=[tile, tile, tile, full(bias2), full(qg), full(kg)],
        out_specs=tile,
        out_shape=jax.ShapeDtypeStruct((batch * seq, ATT_W), bf16),
        scratch_shapes=[pltpu.VMEM((SEQ_TILE // CHUNK * n_pairs, 2 * CHUNK, LANE), bf16),
                        pltpu.VMEM((2 * SEQ_TILE, ATT_W), bf16),
                        pltpu.VMEM((2 * SEQ_TILE, 2 * ATT_W), bf16),
                        pltpu.VMEM((2 * CHUNK, BAND_CHUNKS * CHUNK), f32),
                        pltpu.VMEM((2 * CHUNK, BAND_CHUNKS * CHUNK), bf16)],
        compiler_params=pltpu.CompilerParams(
            dimension_semantics=("arbitrary", "arbitrary"), vmem_limit_bytes=VMEM_LIMIT),
        name="attention",
    )(aq, ak, av, bias2, qg, kg)


def _mid_kernel(og_ref, oa_ref, x_ref, wo_ref, g2_ref, rwh_ref, rwl_ref, rb_ref,
                xm_ref, h_ref, meta_ref, metat_ref, gate_ref, cnt_ref):
    t, d = x_ref.shape
    xm = x_ref[...] + _dot(og_ref[...], wo_ref[0:GLA_V_W, :]) + _dot(oa_ref[...], wo_ref[GLA_V_W:, :])
    xm_ref[...] = xm
    h2 = _rms(xm, g2_ref[...])
    h_hi, h_lo = _split_bf16(h2)
    h_ref[...] = h_hi

    logits = (_dot(h_hi, rwh_ref[...]) + _dot(h_lo, rwh_ref[...]) + _dot(h_hi, rwl_ref[...])
              + rb_ref[...])
    lane = lax.broadcasted_iota(jnp.int32, (t, LANE), 1)
    lane_f = lane.astype(f32)
    l = jnp.where(lane < N_EXPERTS, logits, -jnp.inf)
    vals, idxs = [], []
    for _ in range(TOP_K):
        m = jnp.max(l, axis=-1, keepdims=True)
        ik = jnp.min(jnp.where(l == m, lane_f, float(LANE)), axis=-1, keepdims=True)
        vals.append(m)
        idxs.append(ik)
        l = jnp.where(lane_f == ik, -jnp.inf, l)
    es = [jnp.exp(v - vals[0]) for v in vals]
    den = es[0] + es[1] + es[2] + es[3]

    onehots = [lane_f == ik for ik in idxs]
    sel = jnp.zeros((t, LANE), f32)
    for oh in onehots:
        sel = sel + jnp.where(oh, 1.0, 0.0)
    row = lax.broadcasted_iota(jnp.int32, (t, t), 0)
    col = lax.broadcasted_iota(jnp.int32, (t, t), 1)
    tri = jnp.where((col < row) & ((col // MOE_TILE) == (row // MOE_TILE)), 1.0, 0.0).astype(bf16)
    prefix = _dot(tri, sel.astype(bf16))
    e_r = lax.broadcasted_iota(jnp.int32, (LANE, LANE), 0)
    e_c = lax.broadcasted_iota(jnp.int32, (LANE, LANE), 1)
    before = jnp.where(e_r < e_c, 1.0, 0.0).astype(bf16)
    starts = []
    for s in range(t // MOE_TILE):
        c_row = jnp.sum(sel[s * MOE_TILE:(s + 1) * MOE_TILE], axis=0, keepdims=True)
        cnt_ref[0, s:s + 1, :] = c_row
        padded = jnp.ceil(c_row * (1.0 / DMA_ROWS)) * DMA_ROWS
        start_row = _dot(jnp.broadcast_to(padded, (SUBLANE, LANE)).astype(bf16), before)[0:1]
        starts.append(jnp.broadcast_to(start_row, (MOE_TILE, LANE)))
    slot_base = prefix + jnp.concatenate(starts, axis=0)

    meta = jnp.zeros((t, LANE), f32)
    gates = jnp.zeros((t, LANE), f32)
    for k in range(TOP_K):
        slot_k = jnp.sum(jnp.where(onehots[k], slot_base, 0.0), axis=-1, keepdims=True)
        meta = jnp.where(lane == k, slot_k, meta)
        gates = jnp.where(lane == k, es[k] / den, gates)
    meta_ref[...] = meta.astype(jnp.int32)
    metat_ref[...] = meta.T[0:SUBLANE, :].astype(jnp.int32)
    gate_ref[...] = gates


def _mid(o_gla, o_att, x2, wo, g2, rwh, rwl, rb):
    n, d = x2.shape
    sub = ROW_TILE // MOE_TILE
    tile = lambda w: pl.BlockSpec((ROW_TILE, w), lambda i: (i, 0))
    full = lambda a: pl.BlockSpec(a.shape, lambda i: (0,) * a.ndim)
    return pl.pallas_call(
        _mid_kernel,
        grid=(n // ROW_TILE,),
        in_specs=[tile(GLA_V_W), tile(ATT_W), tile(d), full(wo), full(g2), full(rwh), full(rwl),
                  full(rb)],
        out_specs=[tile(d), tile(d), tile(LANE),
                   pl.BlockSpec((SUBLANE, ROW_TILE), lambda i: (0, i)), tile(LANE),
                   pl.BlockSpec((1, sub, LANE), lambda i: (i, 0, 0))],
        out_shape=[jax.ShapeDtypeStruct((n, d), f32),
                   jax.ShapeDtypeStruct((n, d), bf16),
                   jax.ShapeDtypeStruct((n, LANE), jnp.int32),
                   jax.ShapeDtypeStruct((SUBLANE, n), jnp.int32),
                   jax.ShapeDtypeStruct((n, LANE), f32),
                   jax.ShapeDtypeStruct((n // ROW_TILE, sub, LANE), f32)],
        compiler_params=pltpu.CompilerParams(
            dimension_semantics=("arbitrary",), vmem_limit_bytes=VMEM_LIMIT),
        name="mid",
    )(o_gla, o_att, x2, wo, g2, rwh, rwl, rb)


def _slot_matrix(slots, values, slot_axis):
    n_tokens = slots[0].shape[1 - slot_axis]
    shape = (LOCAL_ROWS, n_tokens) if slot_axis == 0 else (n_tokens, LOCAL_ROWS)
    iota_shape = (LOCAL_ROWS, 1) if slot_axis == 0 else (1, LOCAL_ROWS)
    pos = lax.broadcasted_iota(jnp.int32, iota_shape, slot_axis)
    out = jnp.zeros(shape, f32)
    for slot_k, value_k in zip(slots, values):
        out = jnp.where(pos == slot_k, value_k, out)
    return out


def _pack_bf16_pairs(a, b):
    return (pltpu.bitcast(a, jnp.uint32) >> 16) | (pltpu.bitcast(b, jnp.uint32) & jnp.uint32(0xFFFF0000))


def _unpack_bf16_pairs(w):
    a = pltpu.bitcast(w << 16, f32).astype(bf16)
    b = pltpu.bitcast(w & jnp.uint32(0xFFFF0000), f32).astype(bf16)
    return a, b


def _piece_copy(src_ref, dst_ref, src_row, dst_row, sem):
    return pltpu.make_async_copy(src_ref.at[pl.ds(pl.multiple_of(src_row, DMA_ROWS), DMA_ROWS)],
                                 dst_ref.at[pl.ds(pl.multiple_of(dst_row, DMA_ROWS), DMA_ROWS)], sem)


def _start_pieces(n, piece):
    def group(g, c):
        for j in range(ISSUE_UNROLL):
            piece(g * ISSUE_UNROLL + j).start()
        return c

    n_groups = n // ISSUE_UNROLL
    lax.fori_loop(0, n_groups, group, 0)
    lax.fori_loop(n_groups * ISSUE_UNROLL, n, lambda q, c: (piece(q).start(), c)[1], 0)


def _wait_pieces(n, piece, bulk):
    bulk.wait()
    lax.fori_loop(MIN_PIECES, n, lambda q, c: (piece(q).wait(), c)[1], 0)


def _zero_fill_padding(pad_start_ref, pad_pieces_ref, n_used_ref, xb_ref, z_ref, sem):
    z_ref[...] = jnp.zeros_like(z_ref)
    n_blocks = xb_ref.shape[0] // MOE_BLOCK

    def tail_piece(e, q):
        return _piece_copy(z_ref, xb_ref, 0, pad_start_ref[e] + q * DMA_ROWS, sem)

    def block_copy(b):
        row = pl.multiple_of(b * MOE_BLOCK, MOE_BLOCK)
        return pltpu.make_async_copy(z_ref, xb_ref.at[pl.ds(row, MOE_BLOCK)], sem)

    def each_tail_piece(fn):
        def per_expert(e, c):
            lax.fori_loop(0, pad_pieces_ref[e], lambda q, cc: (fn(tail_piece(e, q)), cc)[1], 0)
            return c
        lax.fori_loop(0, N_EXPERTS, per_expert, 0)

    def each_block(fn):
        lax.fori_loop(n_used_ref[0], n_blocks, lambda b, c: (fn(block_copy(b)), c)[1], 0)

    each_tail_piece(lambda cp: cp.start())
    each_block(lambda cp: cp.start())
    each_tail_piece(lambda cp: cp.wait())
    each_block(lambda cp: cp.wait())


def _dispatch_kernel(nq_ref, pad_start_ref, pad_pieces_ref, n_used_ref, dst_ref, metat_ref,
                     h_ref, xb_ref, l_ref, z_ref, sem):
    i = pl.program_id(0)
    hw = l_ref.shape[1]
    slots = [metat_ref[k:k + 1, :] for k in range(TOP_K)]
    perm = _slot_matrix(slots, [1.0] * TOP_K, slot_axis=0).astype(bf16)
    packed = _pack_bf16_pairs(_dot(perm, h_ref[:, 0:hw]), _dot(perm, h_ref[:, hw:]))

    def piece(q):
        return _piece_copy(l_ref, xb_ref, q * DMA_ROWS, dst_ref[0, 0, q], sem.at[0])

    def drain(n):
        min_rows = MIN_PIECES * DMA_ROWS
        bulk = pltpu.make_async_copy(l_ref.at[0:min_rows], xb_ref.at[0:min_rows], sem.at[0])
        _wait_pieces(n, piece, bulk)

    @pl.when(i > 0)
    def _():
        drain(nq_ref[jnp.maximum(i - 1, 0)])

    l_ref[...] = packed
    _start_pieces(nq_ref[i], piece)

    @pl.when(i == pl.num_programs(0) - 1)
    def _():
        _zero_fill_padding(pad_start_ref, pad_pieces_ref, n_used_ref, xb_ref, z_ref, sem.at[1])
        drain(nq_ref[i])


def _dispatch(nq, pad_start, pad_pieces, n_used, dst, meta_t, h, n_rows):
    n, d = h.shape
    hw = d // 2
    t = MOE_TILE
    grid_spec = pltpu.PrefetchScalarGridSpec(
        num_scalar_prefetch=4,
        grid=(n // t,),
        in_specs=[pl.BlockSpec((1, 1, LOCAL_PIECES), lambda i, *_: (i, 0, 0), memory_space=pltpu.SMEM),
                  pl.BlockSpec((SUBLANE, t), lambda i, *_: (0, i)),
                  pl.BlockSpec((t, d), lambda i, *_: (i, 0))],
        out_specs=pl.BlockSpec(memory_space=pl.ANY),
        scratch_shapes=[pltpu.VMEM((LOCAL_ROWS, hw), jnp.uint32),
                        pltpu.VMEM((MOE_BLOCK, hw), jnp.uint32),
                        pltpu.SemaphoreType.DMA((2,))],
    )
    return pl.pallas_call(
        _dispatch_kernel,
        grid_spec=grid_spec,
        out_shape=jax.ShapeDtypeStruct((n_rows, hw), jnp.uint32),
        compiler_params=pltpu.CompilerParams(
            dimension_semantics=("arbitrary",), vmem_limit_bytes=VMEM_LIMIT),
        name="dispatch",
    )(nq, pad_start, pad_pieces, n_used, dst, meta_t, h)


def _expert_kernel(be_ref, nu_ref, x_ref, wi_ref, bi_ref, wo_ref, bo_ref, y_ref,
                   wib_ref, wob_ref):
    b = pl.program_id(0)
    blk, hw = x_ref.shape
    dff = wo_ref.shape[1]

    @pl.when(b < nu_ref[0])
    def _():
        changed = jnp.logical_or(b == 0, be_ref[b] != be_ref[jnp.maximum(b - 1, 0)])

        @pl.when(changed)
        def _():
            wib_ref[...] = wi_ref[0].astype(bf16)
            wob_ref[...] = wo_ref[0].astype(bf16)

        def up(rows):
            xa, xb = _unpack_bf16_pairs(x_ref[rows, :])
            return _dot(xa, wib_ref[0:hw, :]) + _dot(xb, wib_ref[hw:, :]) + bi_ref[0]

        def down(rows, hc):
            glu = jnp.minimum(hc[:, :dff], SWIGLU_LIMIT)
            lin = jnp.clip(hc[:, dff:], -SWIGLU_LIMIT, SWIGLU_LIMIT)
            act = glu * (1.0 / (1.0 + jnp.exp(-SWIGLU_ALPHA * glu))) * (lin + 1.0)
            y = (_dot(act.astype(bf16), wob_ref[...]) + bo_ref[0]).astype(bf16).astype(f32)
            y_ref[rows, :] = _pack_bf16_pairs(y[:, 0:hw], y[:, hw:])

        subs = [slice(s * EXPERT_SUB_BLOCK, (s + 1) * EXPERT_SUB_BLOCK)
                for s in range(blk // EXPERT_SUB_BLOCK)]
        hc_next = up(subs[0])
        for s, rows in enumerate(subs):
            hc = hc_next
            if s + 1 < len(subs):
                hc_next = up(subs[s + 1])
            down(rows, hc)

    @pl.when(b >= nu_ref[0])
    def _():
        y_ref[...] = jnp.zeros_like(y_ref)


def _experts(block_expert, n_used, xb, w_in, b_in, w_out, b_out):
    n_rows, hw = xb.shape
    e, d, dff2 = w_in.shape
    dff = w_out.shape[1]
    n_blocks = n_rows // MOE_BLOCK
    grid_spec = pltpu.PrefetchScalarGridSpec(
        num_scalar_prefetch=2,
        grid=(n_blocks,),
        in_specs=[
            pl.BlockSpec((MOE_BLOCK, hw), lambda b, be, nu: (b, 0)),
            pl.BlockSpec((1, d, dff2), lambda b, be, nu: (be[b], 0, 0)),
            pl.BlockSpec((1, 1, dff2), lambda b, be, nu: (be[b], 0, 0)),
            pl.BlockSpec((1, dff, d), lambda b, be, nu: (be[b], 0, 0)),
            pl.BlockSpec((1, 1, d), lambda b, be, nu: (be[b], 0, 0)),
        ],
        out_specs=pl.BlockSpec((MOE_BLOCK, hw), lambda b, be, nu: (b, 0)),
        scratch_shapes=[pltpu.VMEM((d, dff2), bf16), pltpu.VMEM((dff, d), bf16)],
    )
    return pl.pallas_call(
        _expert_kernel,
        grid_spec=grid_spec,
        out_shape=jax.ShapeDtypeStruct((n_rows, hw), jnp.uint32),
        compiler_params=pltpu.CompilerParams(
            dimension_semantics=("arbitrary",), vmem_limit_bytes=VMEM_LIMIT),
        name="experts",
    )(block_expert, n_used, xb, w_in, b_in.reshape(e, 1, dff2), w_out, b_out.reshape(e, 1, d))


def _combine_kernel(nq_ref, dst_ref, dstn_ref, meta_ref, gate_ref, xm_ref, y_ref, o_ref,
                    ly_ref, sem):
    i = pl.program_id(0)
    last = pl.num_programs(0) - 1
    hw = ly_ref.shape[2]
    slot = i % 2

    def fetch(tile, table_ref, buf):
        ly_ref[buf, MOE_TILE * TOP_K:, :] = jnp.zeros((LOCAL_ROWS - MOE_TILE * TOP_K, hw), jnp.uint32)

        _start_pieces(nq_ref[tile], lambda q: _piece_copy(
            y_ref, ly_ref.at[buf], table_ref[0, 0, q], q * DMA_ROWS, sem.at[buf]))

    @pl.when(i == 0)
    def _():
        fetch(i, dst_ref, slot)

    @pl.when(i < last)
    def _():
        fetch(jnp.minimum(i + 1, last), dstn_ref, 1 - slot)

    min_rows = MIN_PIECES * DMA_ROWS
    _wait_pieces(
        nq_ref[i],
        lambda q: _piece_copy(y_ref, ly_ref.at[slot], dst_ref[0, 0, q], q * DMA_ROWS, sem.at[slot]),
        pltpu.make_async_copy(y_ref.at[0:min_rows], ly_ref.at[slot, 0:min_rows], sem.at[slot]))
    gates = gate_ref[...]
    meta = meta_ref[...]
    g = _slot_matrix([meta[:, k:k + 1] for k in range(TOP_K)],
                     [gates[:, k:k + 1] for k in range(TOP_K)], slot_axis=1).astype(bf16)
    ya, yb = _unpack_bf16_pairs(ly_ref[slot])
    o_ref[:, 0:hw] = xm_ref[:, 0:hw] + _dot(g, ya)
    o_ref[:, hw:] = xm_ref[:, hw:] + _dot(g, yb)


def _combine(nq, dst, meta, gates, xm, y):
    n, d = xm.shape
    hw = d // 2
    t = MOE_TILE
    n_tiles = n // t
    grid_spec = pltpu.PrefetchScalarGridSpec(
        num_scalar_prefetch=1,
        grid=(n_tiles,),
        in_specs=[pl.BlockSpec((1, 1, LOCAL_PIECES), lambda i, nq: (i, 0, 0), memory_space=pltpu.SMEM),
                  pl.BlockSpec((1, 1, LOCAL_PIECES), lambda i, nq: (jnp.minimum(i + 1, n_tiles - 1), 0, 0),
                               memory_space=pltpu.SMEM),
                  pl.BlockSpec((t, LANE), lambda i, nq: (i, 0)),
                  pl.BlockSpec((t, LANE), lambda i, nq: (i, 0)),
                  pl.BlockSpec((t, d), lambda i, nq: (i, 0)),
                  pl.BlockSpec(memory_space=pl.ANY)],
        out_specs=pl.BlockSpec((t, d), lambda i, nq: (i, 0)),
        scratch_shapes=[pltpu.VMEM((2, LOCAL_ROWS, hw), jnp.uint32), pltpu.SemaphoreType.DMA((2,))],
    )
    return pl.pallas_call(
        _combine_kernel,
        grid_spec=grid_spec,
        out_shape=jax.ShapeDtypeStruct((n, d), f32),
        compiler_params=pltpu.CompilerParams(
            dimension_semantics=("arbitrary",), vmem_limit_bytes=VMEM_LIMIT),
        name="combine",
    )(nq, dst, dst, meta, gates, xm, y)


def _rel_bias_table(rel_bias):
    band = BAND_CHUNKS * CHUNK
    dist = (np.arange(band + CHUNK - 1) - (CHUNK - 1))[::-1]
    ext = rel_bias[:, np.clip(dist, -REL_CLIP, REL_CLIP) + REL_CLIP]
    rows = [lax.slice_in_dim(ext, CHUNK - 1 - i, CHUNK - 1 - i + band, axis=1) for i in range(CHUNK)]
    return jnp.stack(rows, axis=1).astype(f32)


def _round_up(x, m):
    return (x + m - 1) // m * m


def _routing_tables(cnt, n_tokens):
    experts = jnp.arange(N_EXPERTS, dtype=jnp.int32)
    c = cnt.reshape(-1, LANE)[:, :N_EXPERTS].astype(jnp.int32)
    n_tiles = c.shape[0]
    cp = _round_up(c, DMA_ROWS)
    lend = jnp.cumsum(cp, axis=1)
    lstart = lend - cp
    nq = (lend[:, -1] // DMA_ROWS).astype(jnp.int32)
    region = jnp.sum(cp, axis=0)
    padded = _round_up(region, MOE_BLOCK)
    pend = jnp.cumsum(padded)
    pstart = pend - padded
    base = pstart[None, :] + jnp.cumsum(cp, axis=0) - cp
    q0 = jnp.arange(LOCAL_PIECES, dtype=jnp.int32) * DMA_ROWS
    e_q = jnp.minimum(jnp.sum(lend[:, None, :] <= q0[None, :, None], axis=-1), N_EXPERTS - 1)
    shift = jnp.sum(jnp.where(e_q[:, :, None] == experts, (base - lstart)[:, None, :], 0), axis=-1)
    dst = jnp.where(q0[None, :] < lend[:, -1:], shift + q0[None, :], 0).astype(jnp.int32)

    n_blocks = -(-(n_tokens * TOP_K + n_tiles * N_EXPERTS * (DMA_ROWS - 1)) // MOE_BLOCK) + N_EXPERTS
    blk0 = jnp.arange(n_blocks, dtype=jnp.int32) * MOE_BLOCK
    n_used = (pend[-1] // MOE_BLOCK).astype(jnp.int32)
    be = jnp.minimum(jnp.sum(pend[None, :] <= blk0[:, None], axis=1), N_EXPERTS - 1).astype(jnp.int32)
    last = jnp.sum(jnp.where(jnp.arange(n_blocks) == n_used - 1, be, 0))
    be = jnp.where(jnp.arange(n_blocks) < n_used, be, last)
    pad_start = (pstart + region).astype(jnp.int32)
    pad_pieces = ((padded - region) // DMA_ROWS).astype(jnp.int32)
    return (nq, pad_start, pad_pieces, dst.reshape(n_tiles, 1, LOCAL_PIECES), be,
            n_used.reshape(1), n_blocks)


def _layer(x, norm1_g, w_in, gate_up, gate_bias, gla_norm_g, q_norm_g, k_norm_g, rel_bias, w_out,
           norm2_g, router_w, router_b, moe_w_in, moe_b_in, moe_w_out, moe_b_out):
    batch, seq, d = x.shape
    n = batch * seq
    x2 = x.reshape(n, d)

    pieces = jnp.split(w_in, np.cumsum(IN_SIZES)[:-1].tolist(), axis=-1)
    pieces[3] = jnp.pad(pieces[3], ((0, 0), (0, LANE - GLA_GATE_RANK)))
    widths = [p.shape[-1] for p in pieces]
    w_all = jnp.concatenate(pieces, axis=-1).astype(bf16)
    gq, gk, gv, glr, gog, aq, ak, av = _inproj(x2, norm1_g.reshape(1, d), w_all, widths)

    gup = jnp.pad(gate_up, ((0, LANE - GLA_GATE_RANK), (0, 0))).astype(bf16)
    o_gla = _gla(gq, gk, gv, glr, gog, gup, gate_bias.reshape(1, -1), gla_norm_g.reshape(1, -1),
                 batch, seq)
    tile2 = lambda g: jnp.tile(g.reshape(1, -1), (1, LANE // ATT_DH))
    o_att = _attention(aq, ak, av, _rel_bias_table(rel_bias), tile2(q_norm_g), tile2(k_norm_g),
                       batch, seq)

    rw = jnp.pad(router_w, ((0, 0), (0, LANE - N_EXPERTS)))
    rw_hi = rw.astype(bf16)
    rw_lo = (rw - rw_hi.astype(f32)).astype(bf16)
    rb = jnp.pad(router_b, (0, LANE - N_EXPERTS)).reshape(1, LANE)
    xm, h2, meta, meta_t, gates, cnt = _mid(o_gla, o_att, x2, w_out.astype(bf16),
                                            norm2_g.reshape(1, d), rw_hi, rw_lo, rb)

    nq, pad_start, pad_pieces, dst, be, n_used, n_blocks = _routing_tables(cnt, n)
    xb = _dispatch(nq, pad_start, pad_pieces, n_used, dst, meta_t, h2, n_blocks * MOE_BLOCK)
    y = _experts(be, n_used, xb, moe_w_in, moe_b_in, moe_w_out, moe_b_out)
    out = _combine(nq, dst, meta, gates, xm, y)
    return out.reshape(batch, seq, d)


def kernel(x, norm1_g, w_in, gla_gate_up, gla_gate_bias, gla_norm_g, q_norm_g, k_norm_g, rel_bias, w_out, norm2_g, router_w, router_b, moe_w_in, moe_b_in, moe_w_out, moe_b_out):
    for l in range(norm1_g.shape[0]):
        x = _layer(x, norm1_g[l], w_in[l], gla_gate_up[l], gla_gate_bias[l], gla_norm_g[l],
                   q_norm_g[l], k_norm_g[l], rel_bias[l], w_out[l], norm2_g[l], router_w[l],
                   router_b[l], moe_w_in[l], moe_b_in[l], moe_w_out[l], moe_b_out[l])
    return x
```

```python
import functools

import numpy as np
import jax
import jax.numpy as jnp
from jax import lax
from jax.experimental import pallas as pl
from jax.experimental.pallas import tpu as pltpu

CHUNK = 64
EPS = 1e-6
GLA_HEADS = 4
GLA_DK = 64
GLA_DV = 128
GLA_GATE_RANK = 16
GLA_GATE_TAU = 16.0
ATT_HEADS = 8
ATT_DH = 64
N_BACK_CHUNKS = 8
BAND_CHUNKS = N_BACK_CHUNKS + 1
REL_CLIP = 256
MASK_VALUE = -1e30
N_EXPERTS = 32
TOP_K = 4
SWIGLU_ALPHA = 1.702
SWIGLU_LIMIT = 7.0
MOE_BLOCK = 1024
EXPERT_SUB_BLOCK = 256

LANE = 128
SUBLANE = 8
GLA_QK_W = GLA_HEADS * GLA_DK
GLA_V_W = GLA_HEADS * GLA_DV
ATT_W = ATT_HEADS * ATT_DH
IN_SIZES = (GLA_QK_W, GLA_QK_W, GLA_V_W, GLA_GATE_RANK, GLA_V_W, ATT_W, ATT_W, ATT_W)
SEQ_TILE = N_BACK_CHUNKS * CHUNK
ROW_TILE = 512
MOE_TILE = 256
DMA_ROWS = SUBLANE
LOCAL_PIECES = 160
LOCAL_ROWS = LOCAL_PIECES * DMA_ROWS
MIN_PIECES = MOE_TILE * TOP_K // DMA_ROWS
ISSUE_UNROLL = 4
VMEM_LIMIT = 48 * 1024 * 1024

f32 = jnp.float32
bf16 = jnp.bfloat16


def _rms(x, g):
    return x * lax.rsqrt(jnp.mean(x * x, axis=-1, keepdims=True) + EPS) * g


def _dot(a, b):
    return jnp.dot(a, b, preferred_element_type=f32)


def _dot_nt(a, b):
    return lax.dot_general(a, b, (((1,), (1,)), ((), ())), preferred_element_type=f32)


def _dot_tn(a, b):
    return lax.dot_general(a, b, (((0,), (0,)), ((), ())), preferred_element_type=f32)


def _split_bf16(x):
    hi = x.astype(bf16)
    lo = (x - hi.astype(f32)).astype(bf16)
    return hi, lo


def _inproj_kernel(x_ref, g_ref, w_ref, *out_refs):
    h = _rms(x_ref[...], g_ref[...]).astype(bf16)
    off = 0
    for o_ref in out_refs:
        w = o_ref.shape[-1]
        o_ref[...] = _dot(h, w_ref[:, off:off + w]).astype(o_ref.dtype)
        off += w


def _inproj(x2, g, w, widths):
    n, d = x2.shape
    return pl.pallas_call(
        _inproj_kernel,
        grid=(n // ROW_TILE,),
        in_specs=[
            pl.BlockSpec((ROW_TILE, d), lambda i: (i, 0)),
            pl.BlockSpec((1, d), lambda i: (0, 0)),
            pl.BlockSpec(w.shape, lambda i: (0, 0)),
        ],
        out_specs=[pl.BlockSpec((ROW_TILE, wd), lambda i: (i, 0)) for wd in widths],
        out_shape=[jax.ShapeDtypeStruct((n, wd), bf16) for wd in widths],
        compiler_params=pltpu.CompilerParams(
            dimension_semantics=("arbitrary",), vmem_limit_bytes=VMEM_LIMIT),
        name="inproj",
    )(x2, g, w)


def _gla_kernel(q_ref, k_ref, v_ref, lr_ref, og_ref, gup_ref, gb_ref, ng_ref, o_ref, st_ref,
                u_ref, sb_ref):
    t = q_ref.shape[0]

    @pl.when(pl.program_id(1) == 0)
    def _():
        st_ref[...] = jnp.zeros_like(st_ref)

    z = _dot(lr_ref[...], gup_ref[...]) + gb_ref[...]
    log_a = (jnp.minimum(z, 0.0) - jnp.log1p(jnp.exp(-jnp.abs(z)))) * (1.0 / GLA_GATE_TAU)
    row = lax.broadcasted_iota(jnp.int32, (t, t), 0)
    col = lax.broadcasted_iota(jnp.int32, (t, t), 1)
    tri = jnp.where((col <= row) & ((col // CHUNK) == (row // CHUNK)), 1.0, 0.0).astype(bf16)
    la_hi, la_lo = _split_bf16(log_a)
    cum_all = _dot(tri, la_hi) + _dot(tri, la_lo)
    lane = lax.broadcasted_iota(jnp.int32, (1, LANE), 1)
    half_mask = (lane < GLA_DK, lane >= GLA_DK)
    n_chunks = t // CHUNK
    chunk_rows = [slice(c * CHUNK, (c + 1) * CHUNK) for c in range(n_chunks)]
    pair_of = lambda h: slice((h // 2) * LANE, (h // 2 + 1) * LANE)
    head_of = lambda h: slice(h * GLA_DV, (h + 1) * GLA_DV)

    decs = []
    for c, rows in enumerate(chunk_rows):
        cum = cum_all[rows]
        tot = cum[CHUNK - 1:CHUNK]
        kdec = k_ref[rows, :].astype(f32) * jnp.exp(tot - cum)
        decs.append(jnp.exp(tot))
        for h in range(GLA_HEADS):
            kd = jnp.where(half_mask[h % 2], kdec[:, pair_of(h)], 0.0).astype(bf16)
            u_ref[c, h] = _dot_tn(v_ref[rows, head_of(h)], kd)

    for h in range(GLA_HEADS):
        st = st_ref[h]
        for c in range(n_chunks):
            st = st * decs[c][:, pair_of(h)] + u_ref[c, h]
            sb_ref[c, h] = st.astype(bf16)
        st_ref[h] = st

    for c, rows in enumerate(chunk_rows):
        for h in range(GLA_HEADS):
            o = _dot_nt(q_ref[rows, pair_of(h)], sb_ref[c, h]) * (GLA_DK ** -0.5)
            o = _rms(o, ng_ref[...])
            g = og_ref[rows, head_of(h)].astype(f32)
            o_ref[rows, head_of(h)] = (o * (g / (1.0 + jnp.exp(-g)))).astype(o_ref.dtype)


def _gla(gq, gk, gv, glr, gog, gup, gb, ng, batch, seq):
    nt = seq // SEQ_TILE
    tile = lambda w: pl.BlockSpec((SEQ_TILE, w), lambda b, i: (b * nt + i, 0))
    full = lambda a: pl.BlockSpec(a.shape, lambda b, i: (0,) * a.ndim)
    return pl.pallas_call(
        _gla_kernel,
        grid=(batch, nt),
        in_specs=[tile(GLA_QK_W), tile(GLA_QK_W), tile(GLA_V_W), tile(LANE), tile(GLA_V_W),
                  full(gup), full(gb), full(ng)],
        out_specs=tile(GLA_V_W),
        out_shape=jax.ShapeDtypeStruct((batch * seq, GLA_V_W), bf16),
        scratch_shapes=[pltpu.VMEM((GLA_HEADS, GLA_DV, LANE), f32),
                        pltpu.VMEM((SEQ_TILE // CHUNK, GLA_HEADS, GLA_DV, LANE), f32),
                        pltpu.VMEM((SEQ_TILE // CHUNK, GLA_HEADS, GLA_DV, LANE), bf16)],
        compiler_params=pltpu.CompilerParams(
            dimension_semantics=("arbitrary", "arbitrary"), vmem_limit_bytes=VMEM_LIMIT),
        name="gla",
    )(gq, gk, gv, glr, gog, gup, gb, ng)


def _head_norm(x, g, ones_bd):
    sq_hi, sq_lo = _split_bf16(x * x)
    ssq = _dot(sq_hi, ones_bd) + _dot(sq_lo, ones_bd)
    return x * lax.rsqrt(ssq * (1.0 / ATT_DH) + EPS) * g


def _att_kernel(q_ref, k_ref, v_ref, bias_ref, qg_ref, kg_ref, o_ref, qs_ref, kb_ref, vb_ref,
                sn_ref, en_ref):
    t = q_ref.shape[0]
    n_pairs = ATT_W // LANE
    band = BAND_CHUNKS * CHUNK
    first = pl.program_id(1) == 0

    @pl.when(first)
    def _():
        kb_ref[0:t, :] = jnp.zeros((t, ATT_W), bf16)
        vb_ref[0:t, :] = jnp.zeros((t, 2 * ATT_W), bf16)

    @pl.when(jnp.logical_not(first))
    def _():
        kb_ref[0:t, :] = kb_ref[t:2 * t, :]
        vb_ref[0:t, :] = vb_ref[t:2 * t, :]

    lane = lax.broadcasted_iota(jnp.int32, (1, LANE), 1)
    lo = lane < ATT_DH
    r_i = lax.broadcasted_iota(jnp.int32, (LANE, LANE), 0)
    c_i = lax.broadcasted_iota(jnp.int32, (LANE, LANE), 1)
    ones_bd = jnp.where((r_i < ATT_DH) == (c_i < ATT_DH), 1.0, 0.0).astype(bf16)

    for p in range(n_pairs):
        pair = slice(p * LANE, (p + 1) * LANE)
        kb_ref[t:2 * t, pair] = _head_norm(k_ref[:, pair].astype(f32), kg_ref[...], ones_bd).astype(bf16)
        qn = _head_norm(q_ref[:, pair].astype(f32), qg_ref[...], ones_bd) * (ATT_DH ** -0.5)
        q_lo = jnp.where(lo, qn, 0.0).astype(bf16)
        q_hi = jnp.where(lo, 0.0, qn).astype(bf16)
        for c in range(t // CHUNK):
            rows = slice(c * CHUNK, (c + 1) * CHUNK)
            qs_ref[c * n_pairs + p, 0:CHUNK, :] = q_lo[rows]
            qs_ref[c * n_pairs + p, CHUNK:2 * CHUNK, :] = q_hi[rows]
        vb_ref[t:2 * t, 2 * p * LANE:(2 * p + 1) * LANE] = v_ref[:, pair]
        vb_ref[t:2 * t, (2 * p + 1) * LANE:(2 * p + 2) * LANE] = jnp.ones((t, LANE), bf16)

    colk = lax.broadcasted_iota(jnp.int32, (1, band), 1)

    def chunk_loop(masked):
        n_chunks = t // CHUNK

        def scores(c, p):
            k2 = kb_ref[pl.ds(pl.multiple_of(c * CHUNK, CHUNK), band), p * LANE:(p + 1) * LANE]
            return _dot_nt(qs_ref[c * n_pairs + p], k2)

        def weighted_values(e, r, p):
            v2 = vb_ref[pl.ds(r, band), 2 * p * LANE:(2 * p + 2) * LANE]
            pvl = _dot(e, v2)
            pv = pvl[:, 0:LANE] / pvl[:, LANE:2 * LANE]
            o_ref[pl.ds(r, CHUNK), p * LANE:(p + 1) * LANE] = jnp.where(
                lo, pv[0:CHUNK], pv[CHUNK:2 * CHUNK]).astype(o_ref.dtype)

        def chunk_body(c, carry):
            r0 = pl.multiple_of(c * CHUNK, CHUNK)
            r_prev = pl.multiple_of(jnp.maximum(c - 1, 0) * CHUNK, CHUNK)
            s_next = sn_ref[...]
            e_prev = en_ref[...]
            for p in range(n_pairs):
                s = s_next + bias_ref[p]
                if p + 1 < n_pairs:
                    s_next = scores(c, p + 1)
                else:
                    sn_ref[...] = scores(jnp.minimum(c + 1, n_chunks - 1), 0)
                if masked:
                    s = jnp.where(colk >= t - c * CHUNK, s, MASK_VALUE)
                e = jnp.exp(s - jnp.max(s, axis=-1, keepdims=True)).astype(bf16)
                if p == 0:
                    weighted_values(e_prev, r_prev, n_pairs - 1)
                else:
                    weighted_values(e_prev, r0, p - 1)
                e_prev = e
            en_ref[...] = e_prev
            return carry

        sn_ref[...] = scores(jnp.int32(0), 0)
        en_ref[...] = jnp.ones_like(en_ref)
        lax.fori_loop(0, n_chunks, chunk_body, 0)
        weighted_values(en_ref[...], (n_chunks - 1) * CHUNK, n_pairs - 1)

    @pl.when(first)
    def _():
        chunk_loop(True)

    @pl.when(jnp.logical_not(first))
    def _():
        chunk_loop(False)


def _attention(aq, ak, av, bias, qg, kg, batch, seq):
    nt = seq // SEQ_TILE
    n_pairs = ATT_W // LANE
    tile = pl.BlockSpec((SEQ_TILE, ATT_W), lambda b, i: (b * nt + i, 0))
    full = lambda a: pl.BlockSpec(a.shape, lambda b, i: (0,) * a.ndim)
    bias2 = bias.reshape(n_pairs, 2 * CHUNK, BAND_CHUNKS * CHUNK)
    return pl.pallas_call(
        _att_kernel,
        grid=(batch, nt),
        in_specs=[tile, tile, tile, full(bias2), full(qg), full(kg)],
        out_specs=tile,
        out_shape=jax.ShapeDtypeStruct((batch * seq, ATT_W), bf16),
        scratch_shapes=[pltpu.VMEM((SEQ_TILE // CHUNK * n_pairs, 2 * CHUNK, LANE), bf16),
                        pltpu.VMEM((2 * SEQ_TILE, ATT_W), bf16),
                        pltpu.VMEM((2 * SEQ_TILE, 2 * ATT_W), bf16),
                        pltpu.VMEM((2 * CHUNK, BAND_CHUNKS * CHUNK), f32),
                        pltpu.VMEM((2 * CHUNK, BAND_CHUNKS * CHUNK), bf16)],
        compiler_params=pltpu.CompilerParams(
            dimension_semantics=("arbitrary", "arbitrary"), vmem_limit_bytes=VMEM_LIMIT),
        name="attention",
    )(aq, ak, av, bias2, qg, kg)


def _mid_kernel(og_ref, oa_ref, x_ref, wo_ref, g2_ref, rwh_ref, rwl_ref, rb_ref,
                xm_ref, h_ref, meta_ref, metat_ref, gate_ref, cnt_ref):
    t, d = x_ref.shape
    xm = x_ref[...] + _dot(og_ref[...], wo_ref[0:GLA_V_W, :]) + _dot(oa_ref[...], wo_ref[GLA_V_W:, :])
    xm_ref[...] = xm
    h2 = _rms(xm, g2_ref[...])
    h_hi, h_lo = _split_bf16(h2)
    h_ref[...] = h_hi

    logits = (_dot(h_hi, rwh_ref[...]) + _dot(h_lo, rwh_ref[...]) + _dot(h_hi, rwl_ref[...])
              + rb_ref[...])
    lane = lax.broadcasted_iota(jnp.int32, (t, LANE), 1)
    lane_f = lane.astype(f32)
    l = jnp.where(lane < N_EXPERTS, logits, -jnp.inf)
    vals, idxs = [], []
    for _ in range(TOP_K):
        m = jnp.max(l, axis=-1, keepdims=True)
        ik = jnp.min(jnp.where(l == m, lane_f, float(LANE)), axis=-1, keepdims=True)
        vals.append(m)
        idxs.append(ik)
        l = jnp.where(lane_f == ik, -jnp.inf, l)
    es = [jnp.exp(v - vals[0]) for v in vals]
    den = es[0] + es[1] + es[2] + es[3]

    onehots = [lane_f == ik for ik in idxs]
    sel = jnp.zeros((t, LANE), f32)
    for oh in onehots:
        sel = sel + jnp.where(oh, 1.0, 0.0)
    row = lax.broadcasted_iota(jnp.int32, (t, t), 0)
    col = lax.broadcasted_iota(jnp.int32, (t, t), 1)
    tri = jnp.where((col < row) & ((col // MOE_TILE) == (row // MOE_TILE)), 1.0, 0.0).astype(bf16)
    prefix = _dot(tri, sel.astype(bf16))
    e_r = lax.broadcasted_iota(jnp.int32, (LANE, LANE), 0)
    e_c = lax.broadcasted_iota(jnp.int32, (LANE, LANE), 1)
    before = jnp.where(e_r < e_c, 1.0, 0.0).astype(bf16)
    starts = []
    for s in range(t // MOE_TILE):
        c_row = jnp.sum(sel[s * MOE_TILE:(s + 1) * MOE_TILE], axis=0, keepdims=True)
        cnt_ref[0, s:s + 1, :] = c_row
        padded = jnp.ceil(c_row * (1.0 / DMA_ROWS)) * DMA_ROWS
        start_row = _dot(jnp.broadcast_to(padded, (SUBLANE, LANE)).astype(bf16), before)[0:1]
        starts.append(jnp.broadcast_to(start_row, (MOE_TILE, LANE)))
    slot_base = prefix + jnp.concatenate(starts, axis=0)

    meta = jnp.zeros((t, LANE), f32)
    gates = jnp.zeros((t, LANE), f32)
    for k in range(TOP_K):
        slot_k = jnp.sum(jnp.where(onehots[k], slot_base, 0.0), axis=-1, keepdims=True)
        meta = jnp.where(lane == k, slot_k, meta)
        gates = jnp.where(lane == k, es[k] / den, gates)
    meta_ref[...] = meta.astype(jnp.int32)
    metat_ref[...] = meta.T[0:SUBLANE, :].astype(jnp.int32)
    gate_ref[...] = gates


def _mid(o_gla, o_att, x2, wo, g2, rwh, rwl, rb):
    n, d = x2.shape
    sub = ROW_TILE // MOE_TILE
    tile = lambda w: pl.BlockSpec((ROW_TILE, w), lambda i: (i, 0))
    full = lambda a: pl.BlockSpec(a.shape, lambda i: (0,) * a.ndim)
    return pl.pallas_call(
        _mid_kernel,
        grid=(n // ROW_TILE,),
        in_specs=[tile(GLA_V_W), tile(ATT_W), tile(d), full(wo), full(g2), full(rwh), full(rwl),
                  full(rb)],
        out_specs=[tile(d), tile(d), tile(LANE),
                   pl.BlockSpec((SUBLANE, ROW_TILE), lambda i: (0, i)), tile(LANE),
                   pl.BlockSpec((1, sub, LANE), lambda i: (i, 0, 0))],
        out_shape=[jax.ShapeDtypeStruct((n, d), f32),
                   jax.ShapeDtypeStruct((n, d), bf16),
                   jax.ShapeDtypeStruct((n, LANE), jnp.int32),
                   jax.ShapeDtypeStruct((SUBLANE, n), jnp.int32),
                   jax.ShapeDtypeStruct((n, LANE), f32),
                   jax.ShapeDtypeStruct((n // ROW_TILE, sub, LANE), f32)],
        compiler_params=pltpu.CompilerParams(
            dimension_semantics=("arbitrary",), vmem_limit_bytes=VMEM_LIMIT),
        name="mid",
    )(o_gla, o_att, x2, wo, g2, rwh, rwl, rb)


def _slot_matrix(slots, values, slot_axis):
    n_tokens = slots[0].shape[1 - slot_axis]
    shape = (LOCAL_ROWS, n_tokens) if slot_axis == 0 else (n_tokens, LOCAL_ROWS)
    iota_shape = (LOCAL_ROWS, 1) if slot_axis == 0 else (1, LOCAL_ROWS)
    pos = lax.broadcasted_iota(jnp.int32, iota_shape, slot_axis)
    out = jnp.zeros(shape, f32)
    for slot_k, value_k in zip(slots, values):
        out = jnp.where(pos == slot_k, value_k, out)
    return out


def _pack_bf16_pairs(a, b):
    return (pltpu.bitcast(a, jnp.uint32) >> 16) | (pltpu.bitcast(b, jnp.uint32) & jnp.uint32(0xFFFF0000))


def _unpack_bf16_pairs(w):
    a = pltpu.bitcast(w << 16, f32).astype(bf16)
    b = pltpu.bitcast(w & jnp.uint32(0xFFFF0000), f32).astype(bf16)
    return a, b


def _piece_copy(src_ref, dst_ref, src_row, dst_row, sem):
    return pltpu.make_async_copy(src_ref.at[pl.ds(pl.multiple_of(src_row, DMA_ROWS), DMA_ROWS)],
                                 dst_ref.at[pl.ds(pl.multiple_of(dst_row, DMA_ROWS), DMA_ROWS)], sem)


def _start_pieces(n, piece):
    def group(g, c):
        for j in range(ISSUE_UNROLL):
            piece(g * ISSUE_UNROLL + j).start()
        return c

    n_groups = n // ISSUE_UNROLL
    lax.fori_loop(0, n_groups, group, 0)
    lax.fori_loop(n_groups * ISSUE_UNROLL, n, lambda q, c: (piece(q).start(), c)[1], 0)


def _wait_pieces(n, piece, bulk):
    bulk.wait()
    lax.fori_loop(MIN_PIECES, n, lambda q, c: (piece(q).wait(), c)[1], 0)


def _zero_fill_padding(pad_start_ref, pad_pieces_ref, n_used_ref, xb_ref, z_ref, sem):
    z_ref[...] = jnp.zeros_like(z_ref)
    n_blocks = xb_ref.shape[0] // MOE_BLOCK

    def tail_piece(e, q):
        return _piece_copy(z_ref, xb_ref, 0, pad_start_ref[e] + q * DMA_ROWS, sem)

    def block_copy(b):
        row = pl.multiple_of(b * MOE_BLOCK, MOE_BLOCK)
        return pltpu.make_async_copy(z_ref, xb_ref.at[pl.ds(row, MOE_BLOCK)], sem)

    def each_tail_piece(fn):
        def per_expert(e, c):
            lax.fori_loop(0, pad_pieces_ref[e], lambda q, cc: (fn(tail_piece(e, q)), cc)[1], 0)
            return c
        lax.fori_loop(0, N_EXPERTS, per_expert, 0)

    def each_block(fn):
        lax.fori_loop(n_used_ref[0], n_blocks, lambda b, c: (fn(block_copy(b)), c)[1], 0)

    each_tail_piece(lambda cp: cp.start())
    each_block(lambda cp: cp.start())
    each_tail_piece(lambda cp: cp.wait())
    each_block(lambda cp: cp.wait())


def _dispatch_kernel(nq_ref, pad_start_ref, pad_pieces_ref, n_used_ref, dst_ref, metat_ref,
                     h_ref, xb_ref, l_ref, z_ref, sem):
    i = pl.program_id(0)
    hw = l_ref.shape[1]
    slots = [metat_ref[k:k + 1, :] for k in range(TOP_K)]
    perm = _slot_matrix(slots, [1.0] * TOP_K, slot_axis=0).astype(bf16)
    packed = _pack_bf16_pairs(_dot(perm, h_ref[:, 0:hw]), _dot(perm, h_ref[:, hw:]))

    def piece(q):
        return _piece_copy(l_ref, xb_ref, q * DMA_ROWS, dst_ref[0, 0, q], sem.at[0])

    def drain(n):
        min_rows = MIN_PIECES * DMA_ROWS
        bulk = pltpu.make_async_copy(l_ref.at[0:min_rows], xb_ref.at[0:min_rows], sem.at[0])
        _wait_pieces(n, piece, bulk)

    @pl.when(i > 0)
    def _():
        drain(nq_ref[jnp.maximum(i - 1, 0)])

    l_ref[...] = packed
    _start_pieces(nq_ref[i], piece)

    @pl.when(i == pl.num_programs(0) - 1)
    def _():
        _zero_fill_padding(pad_start_ref, pad_pieces_ref, n_used_ref, xb_ref, z_ref, sem.at[1])
        drain(nq_ref[i])


def _dispatch(nq, pad_start, pad_pieces, n_used, dst, meta_t, h, n_rows):
    n, d = h.shape
    hw = d // 2
    t = MOE_TILE
    grid_spec = pltpu.PrefetchScalarGridSpec(
        num_scalar_prefetch=4,
        grid=(n // t,),
        in_specs=[pl.BlockSpec((1, 1, LOCAL_PIECES), lambda i, *_: (i, 0, 0), memory_space=pltpu.SMEM),
                  pl.BlockSpec((SUBLANE, t), lambda i, *_: (0, i)),
                  pl.BlockSpec((t, d), lambda i, *_: (i, 0))],
        out_specs=pl.BlockSpec(memory_space=pl.ANY),
        scratch_shapes=[pltpu.VMEM((LOCAL_ROWS, hw), jnp.uint32),
                        pltpu.VMEM((MOE_BLOCK, hw), jnp.uint32),
                        pltpu.SemaphoreType.DMA((2,))],
    )
    return pl.pallas_call(
        _dispatch_kernel,
        grid_spec=grid_spec,
        out_shape=jax.ShapeDtypeStruct((n_rows, hw), jnp.uint32),
        compiler_params=pltpu.CompilerParams(
            dimension_semantics=("arbitrary",), vmem_limit_bytes=VMEM_LIMIT),
        name="dispatch",
    )(nq, pad_start, pad_pieces, n_used, dst, meta_t, h)


def _expert_kernel(be_ref, ns_ref, x_ref, wi_ref, bi_ref, wo_ref, bo_ref, y_ref,
                   wib_ref, wob_ref):
    b = pl.program_id(0)
    blk, hw = x_ref.shape
    dff = wo_ref.shape[1]
    subs = [slice(s * EXPERT_SUB_BLOCK, (s + 1) * EXPERT_SUB_BLOCK)
            for s in range(blk // EXPERT_SUB_BLOCK)]
    n_live = ns_ref[b]

    @pl.when(jnp.logical_and(n_live > 0,
                             jnp.logical_or(b == 0, be_ref[b] != be_ref[jnp.maximum(b - 1, 0)])))
    def _():
        wib_ref[...] = wi_ref[0].astype(bf16)
        wob_ref[...] = wo_ref[0].astype(bf16)

    def up(rows):
        xa, xb = _unpack_bf16_pairs(x_ref[rows, :])
        return _dot(xa, wib_ref[0:hw, :]) + _dot(xb, wib_ref[hw:, :]) + bi_ref[0]

    def down(rows, hc):
        glu = jnp.minimum(hc[:, :dff], SWIGLU_LIMIT)
        lin = jnp.clip(hc[:, dff:], -SWIGLU_LIMIT, SWIGLU_LIMIT)
        act = glu * (1.0 / (1.0 + jnp.exp(-SWIGLU_ALPHA * glu))) * (lin + 1.0)
        y = (_dot(act.astype(bf16), wob_ref[...]) + bo_ref[0]).astype(bf16).astype(f32)
        y_ref[rows, :] = _pack_bf16_pairs(y[:, 0:hw], y[:, hw:])

    def run(n):
        if n > 0:
            hc_next = up(subs[0])
        for s in range(n):
            hc = hc_next
            if s + 1 < n:
                hc_next = up(subs[s + 1])
            down(subs[s], hc)
        for rows in subs[n:]:
            y_ref[rows, :] = jnp.zeros((EXPERT_SUB_BLOCK, hw), y_ref.dtype)

    for n in range(len(subs) + 1):
        pl.when(n_live == n)(functools.partial(run, n))


def _experts(block_expert, block_live, xb, w_in, b_in, w_out, b_out):
    n_rows, hw = xb.shape
    e, d, dff2 = w_in.shape
    dff = w_out.shape[1]
    n_blocks = n_rows // MOE_BLOCK
    grid_spec = pltpu.PrefetchScalarGridSpec(
        num_scalar_prefetch=2,
        grid=(n_blocks,),
        in_specs=[
            pl.BlockSpec((MOE_BLOCK, hw), lambda b, be, nu: (b, 0)),
            pl.BlockSpec((1, d, dff2), lambda b, be, nu: (be[b], 0, 0)),
            pl.BlockSpec((1, 1, dff2), lambda b, be, nu: (be[b], 0, 0)),
            pl.BlockSpec((1, dff, d), lambda b, be, nu: (be[b], 0, 0)),
            pl.BlockSpec((1, 1, d), lambda b, be, nu: (be[b], 0, 0)),
        ],
        out_specs=pl.BlockSpec((MOE_BLOCK, hw), lambda b, be, nu: (b, 0)),
        scratch_shapes=[pltpu.VMEM((d, dff2), bf16), pltpu.VMEM((dff, d), bf16)],
    )
    return pl.pallas_call(
        _expert_kernel,
        grid_spec=grid_spec,
        out_shape=jax.ShapeDtypeStruct((n_rows, hw), jnp.uint32),
        compiler_params=pltpu.CompilerParams(
            dimension_semantics=("arbitrary",), vmem_limit_bytes=VMEM_LIMIT),
        name="experts",
    )(block_expert, block_live, xb, w_in, b_in.reshape(e, 1, dff2), w_out, b_out.reshape(e, 1, d))


def _combine_kernel(nq_ref, dst_ref, dstn_ref, meta_ref, gate_ref, xm_ref, y_ref, o_ref,
                    ly_ref, sem):
    i = pl.program_id(0)
    last = pl.num_programs(0) - 1
    hw = ly_ref.shape[2]
    slot = i % 2

    def fetch(tile, table_ref, buf):
        ly_ref[buf, MOE_TILE * TOP_K:, :] = jnp.zeros((LOCAL_ROWS - MOE_TILE * TOP_K, hw), jnp.uint32)

        _start_pieces(nq_ref[tile], lambda q: _piece_copy(
            y_ref, ly_ref.at[buf], table_ref[0, 0, q], q * DMA_ROWS, sem.at[buf]))

    @pl.when(i == 0)
    def _():
        fetch(i, dst_ref, slot)

    @pl.when(i < last)
    def _():
        fetch(jnp.minimum(i + 1, last), dstn_ref, 1 - slot)

    min_rows = MIN_PIECES * DMA_ROWS
    _wait_pieces(
        nq_ref[i],
        lambda q: _piece_copy(y_ref, ly_ref.at[slot], dst_ref[0, 0, q], q * DMA_ROWS, sem.at[slot]),
        pltpu.make_async_copy(y_ref.at[0:min_rows], ly_ref.at[slot, 0:min_rows], sem.at[slot]))
    gates = gate_ref[...]
    meta = meta_ref[...]
    g = _slot_matrix([meta[:, k:k + 1] for k in range(TOP_K)],
                     [gates[:, k:k + 1] for k in range(TOP_K)], slot_axis=1).astype(bf16)
    ya, yb = _unpack_bf16_pairs(ly_ref[slot])
    o_ref[:, 0:hw] = xm_ref[:, 0:hw] + _dot(g, ya)
    o_ref[:, hw:] = xm_ref[:, hw:] + _dot(g, yb)


def _combine(nq, dst, meta, gates, xm, y):
    n, d = xm.shape
    hw = d // 2
    t = MOE_TILE
    n_tiles = n // t
    grid_spec = pltpu.PrefetchScalarGridSpec(
        num_scalar_prefetch=1,
        grid=(n_tiles,),
        in_specs=[pl.BlockSpec((1, 1, LOCAL_PIECES), lambda i, nq: (i, 0, 0), memory_space=pltpu.SMEM),
                  pl.BlockSpec((1, 1, LOCAL_PIECES), lambda i, nq: (jnp.minimum(i + 1, n_tiles - 1), 0, 0),
                               memory_space=pltpu.SMEM),
                  pl.BlockSpec((t, LANE), lambda i, nq: (i, 0)),
                  pl.BlockSpec((t, LANE), lambda i, nq: (i, 0)),
                  pl.BlockSpec((t, d), lambda i, nq: (i, 0)),
                  pl.BlockSpec(memory_space=pl.ANY)],
        out_specs=pl.BlockSpec((t, d), lambda i, nq: (i, 0)),
        scratch_shapes=[pltpu.VMEM((2, LOCAL_ROWS, hw), jnp.uint32), pltpu.SemaphoreType.DMA((2,))],
    )
    return pl.pallas_call(
        _combine_kernel,
        grid_spec=grid_spec,
        out_shape=jax.ShapeDtypeStruct((n, d), f32),
        compiler_params=pltpu.CompilerParams(
            dimension_semantics=("arbitrary",), vmem_limit_bytes=VMEM_LIMIT),
        name="combine",
    )(nq, dst, dst, meta, gates, xm, y)


def _rel_bias_table(rel_bias):
    band = BAND_CHUNKS * CHUNK
    width = band + CHUNK
    dist = (np.arange(width) - CHUNK)[::-1]
    ext = rel_bias[:, np.clip(dist, -REL_CLIP, REL_CLIP) + REL_CLIP].astype(f32)
    heads = ext.shape[0]
    tiled = jnp.broadcast_to(ext[:, None, :], (heads, CHUNK, width)).reshape(heads, CHUNK * width)
    skewed = tiled[:, :CHUNK * (width - 1)].reshape(heads, CHUNK, width - 1)
    return skewed[:, :, CHUNK - 1:CHUNK - 1 + band]


def _round_up(x, m):
    return (x + m - 1) // m * m


def _routing_tables(cnt, n_tokens):
    experts = jnp.arange(N_EXPERTS, dtype=jnp.int32)
    c = cnt.reshape(-1, LANE)[:, :N_EXPERTS].astype(jnp.int32)
    n_tiles = c.shape[0]
    cp = _round_up(c, DMA_ROWS)
    lend = jnp.cumsum(cp, axis=1)
    lstart = lend - cp
    nq = (lend[:, -1] // DMA_ROWS).astype(jnp.int32)
    region = jnp.sum(cp, axis=0)
    padded = _round_up(region, MOE_BLOCK)
    pend = jnp.cumsum(padded)
    pstart = pend - padded
    base = pstart[None, :] + jnp.cumsum(cp, axis=0) - cp
    q0 = jnp.arange(LOCAL_PIECES, dtype=jnp.int32) * DMA_ROWS
    e_q = jnp.minimum(jnp.sum(lend[:, None, :] <= q0[None, :, None], axis=-1), N_EXPERTS - 1)
    shift = jnp.sum(jnp.where(e_q[:, :, None] == experts, (base - lstart)[:, None, :], 0), axis=-1)
    dst = jnp.where(q0[None, :] < lend[:, -1:], shift + q0[None, :], 0).astype(jnp.int32)

    n_blocks = -(-(n_tokens * TOP_K + n_tiles * N_EXPERTS * (DMA_ROWS - 1)) // MOE_BLOCK) + N_EXPERTS
    blk0 = jnp.arange(n_blocks, dtype=jnp.int32) * MOE_BLOCK
    n_used = (pend[-1] // MOE_BLOCK).astype(jnp.int32)
    be = jnp.minimum(jnp.sum(pend[None, :] <= blk0[:, None], axis=1), N_EXPERTS - 1).astype(jnp.int32)
    last = jnp.sum(jnp.where(jnp.arange(n_blocks) == n_used - 1, be, 0))
    be = jnp.where(jnp.arange(n_blocks) < n_used, be, last)
    onehot_be = be[:, None] == experts
    region_end = jnp.sum(jnp.where(onehot_be, pstart + region, 0), axis=1)
    live_rows = jnp.clip(region_end - blk0, 0, MOE_BLOCK)
    live = jnp.where(jnp.arange(n_blocks) < n_used, -(-live_rows // EXPERT_SUB_BLOCK), 0)
    pad_start = (pstart + region).astype(jnp.int32)
    pad_pieces = ((padded - region) // DMA_ROWS).astype(jnp.int32)
    return (nq, pad_start, pad_pieces, dst.reshape(n_tiles, 1, LOCAL_PIECES), be,
            live.astype(jnp.int32), n_used.reshape(1), n_blocks)


def _layer(x, norm1_g, w_in, gate_up, gate_bias, gla_norm_g, q_norm_g, k_norm_g, rel_bias, w_out,
           norm2_g, router_w, router_b, moe_w_in, moe_b_in, moe_w_out, moe_b_out):
    batch, seq, d = x.shape
    n = batch * seq
    x2 = x.reshape(n, d)

    pieces = jnp.split(w_in, np.cumsum(IN_SIZES)[:-1].tolist(), axis=-1)
    pieces[3] = jnp.pad(pieces[3], ((0, 0), (0, LANE - GLA_GATE_RANK)))
    widths = [p.shape[-1] for p in pieces]
    w_all = jnp.concatenate(pieces, axis=-1).astype(bf16)
    gq, gk, gv, glr, gog, aq, ak, av = _inproj(x2, norm1_g.reshape(1, d), w_all, widths)

    gup = jnp.pad(gate_up, ((0, LANE - GLA_GATE_RANK), (0, 0))).astype(bf16)
    o_gla = _gla(gq, gk, gv, glr, gog, gup, gate_bias.reshape(1, -1), gla_norm_g.reshape(1, -1),
                 batch, seq)
    tile2 = lambda g: jnp.tile(g.reshape(1, -1), (1, LANE // ATT_DH))
    o_att = _attention(aq, ak, av, _rel_bias_table(rel_bias), tile2(q_norm_g), tile2(k_norm_g),
                       batch, seq)

    rw = jnp.pad(router_w, ((0, 0), (0, LANE - N_EXPERTS)))
    rw_hi = rw.astype(bf16)
    rw_lo = (rw - rw_hi.astype(f32)).astype(bf16)
    rb = jnp.pad(router_b, (0, LANE - N_EXPERTS)).reshape(1, LANE)
    xm, h2, meta, meta_t, gates, cnt = _mid(o_gla, o_att, x2, w_out.astype(bf16),
                                            norm2_g.reshape(1, d), rw_hi, rw_lo, rb)

    nq, pad_start, pad_pieces, dst, be, live, n_used, n_blocks = _routing_tables(cnt, n)
    xb = _dispatch(nq, pad_start, pad_pieces, n_used, dst, meta_t, h2, n_blocks * MOE_BLOCK)
    y = _experts(be, live, xb, moe_w_in, moe_b_in, moe_w_out, moe_b_out)
    out = _combine(nq, dst, meta, gates, xm, y)
    return out.reshape(batch, seq, d)


def kernel(x, norm1_g, w_in, gla_gate_up, gla_gate_bias, gla_norm_g, q_norm_g, k_norm_g, rel_bias, w_out, norm2_g, router_w, router_b, moe_w_in, moe_b_in, moe_w_out, moe_b_out):
    for l in range(norm1_g.shape[0]):
        x = _layer(x, norm1_g[l], w_in[l], gla_gate_up[l], gla_gate_bias[l], gla_norm_g[l],
                   q_norm_g[l], k_norm_g[l], rel_bias[l], w_out[l], norm2_g[l], router_w[l],
                   router_b[l], moe_w_in[l], moe_b_in[l], moe_w_out[l], moe_b_out[l])
    return x
```

```python
import functools

import numpy as np
import jax
import jax.numpy as jnp
from jax import lax
from jax.experimental import pallas as pl
from jax.experimental.pallas import tpu as pltpu

CHUNK = 64
EPS = 1e-6
GLA_HEADS = 4
GLA_DK = 64
GLA_DV = 128
GLA_GATE_RANK = 16
GLA_GATE_TAU = 16.0
ATT_HEADS = 8
ATT_DH = 64
N_BACK_CHUNKS = 8
BAND_CHUNKS = N_BACK_CHUNKS + 1
REL_CLIP = 256
MASK_VALUE = -1e30
N_EXPERTS = 32
TOP_K = 4
SWIGLU_ALPHA = 1.702
SWIGLU_LIMIT = 7.0
MOE_BLOCK = 1024
EXPERT_SUB_BLOCK = 256

LANE = 128
SUBLANE = 8
GLA_QK_W = GLA_HEADS * GLA_DK
GLA_V_W = GLA_HEADS * GLA_DV
ATT_W = ATT_HEADS * ATT_DH
IN_SIZES = (GLA_QK_W, GLA_QK_W, GLA_V_W, GLA_GATE_RANK, GLA_V_W, ATT_W, ATT_W, ATT_W)
SEQ_TILE = N_BACK_CHUNKS * CHUNK
ROW_TILE = 512
MOE_TILE = 256
DMA_ROWS = SUBLANE
LOCAL_PIECES = 160
LOCAL_ROWS = LOCAL_PIECES * DMA_ROWS
MIN_PIECES = MOE_TILE * TOP_K // DMA_ROWS
ISSUE_UNROLL = 4
VMEM_LIMIT = 48 * 1024 * 1024

f32 = jnp.float32
bf16 = jnp.bfloat16


def _rms(x, g):
    return x * lax.rsqrt(jnp.mean(x * x, axis=-1, keepdims=True) + EPS) * g


def _dot(a, b):
    return jnp.dot(a, b, preferred_element_type=f32)


def _dot_nt(a, b):
    return lax.dot_general(a, b, (((1,), (1,)), ((), ())), preferred_element_type=f32)


def _dot_tn(a, b):
    return lax.dot_general(a, b, (((0,), (0,)), ((), ())), preferred_element_type=f32)


def _split_bf16(x):
    hi = x.astype(bf16)
    lo = (x - hi.astype(f32)).astype(bf16)
    return hi, lo


def _inproj_kernel(x_ref, g_ref, w_ref, *out_refs):
    h = _rms(x_ref[...], g_ref[...]).astype(bf16)
    off = 0
    for o_ref in out_refs:
        w = o_ref.shape[-1]
        o_ref[...] = _dot(h, w_ref[:, off:off + w]).astype(o_ref.dtype)
        off += w


def _inproj(x2, g, w, widths):
    n, d = x2.shape
    return pl.pallas_call(
        _inproj_kernel,
        grid=(n // ROW_TILE,),
        in_specs=[
            pl.BlockSpec((ROW_TILE, d), lambda i: (i, 0)),
            pl.BlockSpec((1, d), lambda i: (0, 0)),
            pl.BlockSpec(w.shape, lambda i: (0, 0)),
        ],
        out_specs=[pl.BlockSpec((ROW_TILE, wd), lambda i: (i, 0)) for wd in widths],
        out_shape=[jax.ShapeDtypeStruct((n, wd), bf16) for wd in widths],
        compiler_params=pltpu.CompilerParams(
            dimension_semantics=("arbitrary",), vmem_limit_bytes=VMEM_LIMIT),
        name="inproj",
    )(x2, g, w)


def _gla_kernel(q_ref, k_ref, v_ref, lr_ref, og_ref, gup_ref, gb_ref, ng_ref, o_ref, st_ref,
                u_ref, sb_ref):
    t = q_ref.shape[0]

    @pl.when(pl.program_id(1) == 0)
    def _():
        st_ref[...] = jnp.zeros_like(st_ref)

    z = _dot(lr_ref[...], gup_ref[...]) + gb_ref[...]
    log_a = (jnp.minimum(z, 0.0) - jnp.log1p(jnp.exp(-jnp.abs(z)))) * (1.0 / GLA_GATE_TAU)
    row = lax.broadcasted_iota(jnp.int32, (t, t), 0)
    col = lax.broadcasted_iota(jnp.int32, (t, t), 1)
    tri = jnp.where((col <= row) & ((col // CHUNK) == (row // CHUNK)), 1.0, 0.0).astype(bf16)
    la_hi, la_lo = _split_bf16(log_a)
    cum_all = _dot(tri, la_hi) + _dot(tri, la_lo)
    lane = lax.broadcasted_iota(jnp.int32, (1, LANE), 1)
    half_mask = (lane < GLA_DK, lane >= GLA_DK)
    n_chunks = t // CHUNK
    chunk_rows = [slice(c * CHUNK, (c + 1) * CHUNK) for c in range(n_chunks)]
    pair_of = lambda h: slice((h // 2) * LANE, (h // 2 + 1) * LANE)
    head_of = lambda h: slice(h * GLA_DV, (h + 1) * GLA_DV)

    decs = []
    for c, rows in enumerate(chunk_rows):
        cum = cum_all[rows]
        tot = cum[CHUNK - 1:CHUNK]
        kdec = k_ref[rows, :].astype(f32) * jnp.exp(tot - cum)
        decs.append(jnp.exp(tot))
        for h in range(GLA_HEADS):
            kd = jnp.where(half_mask[h % 2], kdec[:, pair_of(h)], 0.0).astype(bf16)
            u_ref[c, h] = _dot_tn(v_ref[rows, head_of(h)], kd)

    for h in range(GLA_HEADS):
        st = st_ref[h]
        for c in range(n_chunks):
            st = st * decs[c][:, pair_of(h)] + u_ref[c, h]
            sb_ref[c, h] = st.astype(bf16)
        st_ref[h] = st

    for c, rows in enumerate(chunk_rows):
        for h in range(GLA_HEADS):
            o = _dot_nt(q_ref[rows, pair_of(h)], sb_ref[c, h]) * (GLA_DK ** -0.5)
            o = _rms(o, ng_ref[...])
            g = og_ref[rows, head_of(h)].astype(f32)
            o_ref[rows, head_of(h)] = (o * (g / (1.0 + jnp.exp(-g)))).astype(o_ref.dtype)


def _gla(gq, gk, gv, glr, gog, gup, gb, ng, batch, seq):
    nt = seq // SEQ_TILE
    tile = lambda w: pl.BlockSpec((SEQ_TILE, w), lambda b, i: (b * nt + i, 0))
    full = lambda a: pl.BlockSpec(a.shape, lambda b, i: (0,) * a.ndim)
    return pl.pallas_call(
        _gla_kernel,
        grid=(batch, nt),
        in_specs=[tile(GLA_QK_W), tile(GLA_QK_W), tile(GLA_V_W), tile(LANE), tile(GLA_V_W),
                  full(gup), full(gb), full(ng)],
        out_specs=tile(GLA_V_W),
        out_shape=jax.ShapeDtypeStruct((batch * seq, GLA_V_W), bf16),
        scratch_shapes=[pltpu.VMEM((GLA_HEADS, GLA_DV, LANE), f32),
                        pltpu.VMEM((SEQ_TILE // CHUNK, GLA_HEADS, GLA_DV, LANE), f32),
                        pltpu.VMEM((SEQ_TILE // CHUNK, GLA_HEADS, GLA_DV, LANE), bf16)],
        compiler_params=pltpu.CompilerParams(
            dimension_semantics=("arbitrary", "arbitrary"), vmem_limit_bytes=VMEM_LIMIT),
        name="gla",
    )(gq, gk, gv, glr, gog, gup, gb, ng)


def _head_norm(x, g, ones_bd):
    sq_hi, sq_lo = _split_bf16(x * x)
    ssq = _dot(sq_hi, ones_bd) + _dot(sq_lo, ones_bd)
    return x * lax.rsqrt(ssq * (1.0 / ATT_DH) + EPS) * g


def _att_kernel(q_ref, k_ref, v_ref, bias_ref, qg_ref, kg_ref, o_ref, qs_ref, kb_ref, vb_ref,
                sn_ref, en_ref):
    t = q_ref.shape[0]
    n_pairs = ATT_W // LANE
    band = BAND_CHUNKS * CHUNK
    first = pl.program_id(1) == 0

    @pl.when(first)
    def _():
        kb_ref[0:t, :] = jnp.zeros((t, ATT_W), bf16)
        vb_ref[0:t, :] = jnp.zeros((t, 2 * ATT_W), bf16)

    @pl.when(jnp.logical_not(first))
    def _():
        kb_ref[0:t, :] = kb_ref[t:2 * t, :]
        vb_ref[0:t, :] = vb_ref[t:2 * t, :]

    lane = lax.broadcasted_iota(jnp.int32, (1, LANE), 1)
    lo = lane < ATT_DH
    r_i = lax.broadcasted_iota(jnp.int32, (LANE, LANE), 0)
    c_i = lax.broadcasted_iota(jnp.int32, (LANE, LANE), 1)
    ones_bd = jnp.where((r_i < ATT_DH) == (c_i < ATT_DH), 1.0, 0.0).astype(bf16)

    for p in range(n_pairs):
        pair = slice(p * LANE, (p + 1) * LANE)
        kb_ref[t:2 * t, pair] = _head_norm(k_ref[:, pair].astype(f32), kg_ref[...], ones_bd).astype(bf16)
        qn = _head_norm(q_ref[:, pair].astype(f32), qg_ref[...], ones_bd) * (ATT_DH ** -0.5)
        q_lo = jnp.where(lo, qn, 0.0).astype(bf16)
        q_hi = jnp.where(lo, 0.0, qn).astype(bf16)
        for c in range(t // CHUNK):
            rows = slice(c * CHUNK, (c + 1) * CHUNK)
            qs_ref[c * n_pairs + p, 0:CHUNK, :] = q_lo[rows]
            qs_ref[c * n_pairs + p, CHUNK:2 * CHUNK, :] = q_hi[rows]
        vb_ref[t:2 * t, 2 * p * LANE:(2 * p + 1) * LANE] = v_ref[:, pair]
        vb_ref[t:2 * t, (2 * p + 1) * LANE:(2 * p + 2) * LANE] = jnp.ones((t, LANE), bf16)

    colk = lax.broadcasted_iota(jnp.int32, (1, band), 1)

    def chunk_loop(masked):
        n_chunks = t // CHUNK

        def scores(c, p):
            k2 = kb_ref[pl.ds(pl.multiple_of(c * CHUNK, CHUNK), band), p * LANE:(p + 1) * LANE]
            return _dot_nt(qs_ref[c * n_pairs + p], k2)

        def weighted_values(e, r, p):
            v2 = vb_ref[pl.ds(r, band), 2 * p * LANE:(2 * p + 2) * LANE]
            pvl = _dot(e, v2)
            pv = pvl[:, 0:LANE] / pvl[:, LANE:2 * LANE]
            o_ref[pl.ds(r, CHUNK), p * LANE:(p + 1) * LANE] = jnp.where(
                lo, pv[0:CHUNK], pv[CHUNK:2 * CHUNK]).astype(o_ref.dtype)

        def chunk_body(c, carry):
            r0 = pl.multiple_of(c * CHUNK, CHUNK)
            r_prev = pl.multiple_of(jnp.maximum(c - 1, 0) * CHUNK, CHUNK)
            s_next = sn_ref[...]
            e_prev = en_ref[...]
            for p in range(n_pairs):
                s = s_next + bias_ref[p]
                if p + 1 < n_pairs:
                    s_next = scores(c, p + 1)
                else:
                    sn_ref[...] = scores(jnp.minimum(c + 1, n_chunks - 1), 0)
                if masked:
                    s = jnp.where(colk >= t - c * CHUNK, s, MASK_VALUE)
                e = jnp.exp(s - jnp.max(s, axis=-1, keepdims=True)).astype(bf16)
                if p == 0:
                    weighted_values(e_prev, r_prev, n_pairs - 1)
                else:
                    weighted_values(e_prev, r0, p - 1)
                e_prev = e
            en_ref[...] = e_prev
            return carry

        sn_ref[...] = scores(jnp.int32(0), 0)
        en_ref[...] = jnp.ones_like(en_ref)
        lax.fori_loop(0, n_chunks, chunk_body, 0)
        weighted_values(en_ref[...], (n_chunks - 1) * CHUNK, n_pairs - 1)

    @pl.when(first)
    def _():
        chunk_loop(True)

    @pl.when(jnp.logical_not(first))
    def _():
        chunk_loop(False)


def _attention(aq, ak, av, bias, qg, kg, batch, seq):
    nt = seq // SEQ_TILE
    n_pairs = ATT_W // LANE
    tile = pl.BlockSpec((SEQ_TILE, ATT_W), lambda b, i: (b * nt + i, 0))
    full = lambda a: pl.BlockSpec(a.shape, lambda b, i: (0,) * a.ndim)
    bias2 = bias.reshape(n_pairs, 2 * CHUNK, BAND_CHUNKS * CHUNK)
    return pl.pallas_call(
        _att_kernel,
        grid=(batch, nt),
        in_specs=[tile, tile, tile, full(bias2), full(qg), full(kg)],
        out_specs=tile,
        out_shape=jax.ShapeDtypeStruct((batch * seq, ATT_W), bf16),
        scratch_shapes=[pltpu.VMEM((SEQ_TILE // CHUNK * n_pairs, 2 * CHUNK, LANE), bf16),
                        pltpu.VMEM((2 * SEQ_TILE, ATT_W), bf16),
                        pltpu.VMEM((2 * SEQ_TILE, 2 * ATT_W), bf16),
                        pltpu.VMEM((2 * CHUNK, BAND_CHUNKS * CHUNK), f32),
                        pltpu.VMEM((2 * CHUNK, BAND_CHUNKS * CHUNK), bf16)],
        compiler_params=pltpu.CompilerParams(
            dimension_semantics=("arbitrary", "arbitrary"), vmem_limit_bytes=VMEM_LIMIT),
        name="attention",
    )(aq, ak, av, bias2, qg, kg)


def _mid_kernel(og_ref, oa_ref, x_ref, wo_ref, g2_ref, rwh_ref, rwl_ref, rb_ref,
                xm_ref, h_ref, meta_ref, metat_ref, gate_ref, cnt_ref):
    t, d = x_ref.shape
    xm = x_ref[...] + _dot(og_ref[...], wo_ref[0:GLA_V_W, :]) + _dot(oa_ref[...], wo_ref[GLA_V_W:, :])
    xm_ref[...] = xm
    h2 = _rms(xm, g2_ref[...])
    h_hi, h_lo = _split_bf16(h2)
    h_ref[...] = h_hi

    logits = (_dot(h_hi, rwh_ref[...]) + _dot(h_lo, rwh_ref[...]) + _dot(h_hi, rwl_ref[...])
              + rb_ref[...])
    lane = lax.broadcasted_iota(jnp.int32, (t, LANE), 1)
    lane_f = lane.astype(f32)
    l = jnp.where(lane < N_EXPERTS, logits, -jnp.inf)
    vals, idxs = [], []
    for _ in range(TOP_K):
        m = jnp.max(l, axis=-1, keepdims=True)
        ik = jnp.min(jnp.where(l == m, lane_f, float(LANE)), axis=-1, keepdims=True)
        vals.append(m)
        idxs.append(ik)
        l = jnp.where(lane_f == ik, -jnp.inf, l)
    es = [jnp.exp(v - vals[0]) for v in vals]
    den = es[0] + es[1] + es[2] + es[3]

    onehots = [lane_f == ik for ik in idxs]
    sel = jnp.zeros((t, LANE), f32)
    for oh in onehots:
        sel = sel + jnp.where(oh, 1.0, 0.0)
    row = lax.broadcasted_iota(jnp.int32, (t, t), 0)
    col = lax.broadcasted_iota(jnp.int32, (t, t), 1)
    tri = jnp.where((col < row) & ((col // MOE_TILE) == (row // MOE_TILE)), 1.0, 0.0).astype(bf16)
    prefix = _dot(tri, sel.astype(bf16))
    e_r = lax.broadcasted_iota(jnp.int32, (LANE, LANE), 0)
    e_c = lax.broadcasted_iota(jnp.int32, (LANE, LANE), 1)
    before = jnp.where(e_r < e_c, 1.0, 0.0).astype(bf16)
    starts = []
    for s in range(t // MOE_TILE):
        c_row = jnp.sum(sel[s * MOE_TILE:(s + 1) * MOE_TILE], axis=0, keepdims=True)
        cnt_ref[0, s:s + 1, :] = c_row
        padded = jnp.ceil(c_row * (1.0 / DMA_ROWS)) * DMA_ROWS
        start_row = _dot(jnp.broadcast_to(padded, (SUBLANE, LANE)).astype(bf16), before)[0:1]
        starts.append(jnp.broadcast_to(start_row, (MOE_TILE, LANE)))
    slot_base = prefix + jnp.concatenate(starts, axis=0)

    meta = jnp.zeros((t, LANE), f32)
    gates = jnp.zeros((t, LANE), f32)
    for k in range(TOP_K):
        slot_k = jnp.sum(jnp.where(onehots[k], slot_base, 0.0), axis=-1, keepdims=True)
        meta = jnp.where(lane == k, slot_k, meta)
        gates = jnp.where(lane == k, es[k] / den, gates)
    meta_ref[...] = meta.astype(jnp.int32)
    metat_ref[...] = meta.T[0:SUBLANE, :].astype(jnp.int32)
    gate_ref[...] = gates


def _mid(o_gla, o_att, x2, wo, g2, rwh, rwl, rb):
    n, d = x2.shape
    sub = ROW_TILE // MOE_TILE
    tile = lambda w: pl.BlockSpec((ROW_TILE, w), lambda i: (i, 0))
    full = lambda a: pl.BlockSpec(a.shape, lambda i: (0,) * a.ndim)
    return pl.pallas_call(
        _mid_kernel,
        grid=(n // ROW_TILE,),
        in_specs=[tile(GLA_V_W), tile(ATT_W), tile(d), full(wo), full(g2), full(rwh), full(rwl),
                  full(rb)],
        out_specs=[tile(d), tile(d), tile(LANE),
                   pl.BlockSpec((SUBLANE, ROW_TILE), lambda i: (0, i)), tile(LANE),
                   pl.BlockSpec((1, sub, LANE), lambda i: (i, 0, 0))],
        out_shape=[jax.ShapeDtypeStruct((n, d), f32),
                   jax.ShapeDtypeStruct((n, d), bf16),
                   jax.ShapeDtypeStruct((n, LANE), jnp.int32),
                   jax.ShapeDtypeStruct((SUBLANE, n), jnp.int32),
                   jax.ShapeDtypeStruct((n, LANE), f32),
                   jax.ShapeDtypeStruct((n // ROW_TILE, sub, LANE), f32)],
        compiler_params=pltpu.CompilerParams(
            dimension_semantics=("arbitrary",), vmem_limit_bytes=VMEM_LIMIT),
        name="mid",
    )(o_gla, o_att, x2, wo, g2, rwh, rwl, rb)


def _slot_matrix(slots, values, slot_axis):
    n_tokens = slots[0].shape[1 - slot_axis]
    shape = (LOCAL_ROWS, n_tokens) if slot_axis == 0 else (n_tokens, LOCAL_ROWS)
    iota_shape = (LOCAL_ROWS, 1) if slot_axis == 0 else (1, LOCAL_ROWS)
    pos = lax.broadcasted_iota(jnp.int32, iota_shape, slot_axis)
    out = jnp.zeros(shape, f32)
    for slot_k, value_k in zip(slots, values):
        out = jnp.where(pos == slot_k, value_k, out)
    return out


def _pack_bf16_pairs(a, b):
    return (pltpu.bitcast(a, jnp.uint32) >> 16) | (pltpu.bitcast(b, jnp.uint32) & jnp.uint32(0xFFFF0000))


def _unpack_bf16_pairs(w):
    a = pltpu.bitcast(w << 16, f32).astype(bf16)
    b = pltpu.bitcast(w & jnp.uint32(0xFFFF0000), f32).astype(bf16)
    return a, b


def _piece_copy(src_ref, dst_ref, src_row, dst_row, sem):
    return pltpu.make_async_copy(src_ref.at[pl.ds(pl.multiple_of(src_row, DMA_ROWS), DMA_ROWS)],
                                 dst_ref.at[pl.ds(pl.multiple_of(dst_row, DMA_ROWS), DMA_ROWS)], sem)


def _start_pieces(n, piece):
    def group(g, c):
        for j in range(ISSUE_UNROLL):
            piece(g * ISSUE_UNROLL + j).start()
        return c

    n_groups = n // ISSUE_UNROLL
    lax.fori_loop(0, n_groups, group, 0)
    lax.fori_loop(n_groups * ISSUE_UNROLL, n, lambda q, c: (piece(q).start(), c)[1], 0)


def _wait_pieces(n, piece, bulk):
    bulk.wait()
    lax.fori_loop(MIN_PIECES, n, lambda q, c: (piece(q).wait(), c)[1], 0)


def _zero_fill_padding(pad_start_ref, pad_pieces_ref, n_used_ref, xb_ref, z_ref, sem):
    z_ref[...] = jnp.zeros_like(z_ref)
    n_blocks = xb_ref.shape[0] // MOE_BLOCK

    def tail_piece(e, q):
        return _piece_copy(z_ref, xb_ref, 0, pad_start_ref[e] + q * DMA_ROWS, sem)

    def block_copy(b):
        row = pl.multiple_of(b * MOE_BLOCK, MOE_BLOCK)
        return pltpu.make_async_copy(z_ref, xb_ref.at[pl.ds(row, MOE_BLOCK)], sem)

    def each_tail_piece(fn):
        def per_expert(e, c):
            lax.fori_loop(0, pad_pieces_ref[e], lambda q, cc: (fn(tail_piece(e, q)), cc)[1], 0)
            return c
        lax.fori_loop(0, N_EXPERTS, per_expert, 0)

    def each_block(fn):
        lax.fori_loop(n_used_ref[0], n_blocks, lambda b, c: (fn(block_copy(b)), c)[1], 0)

    each_tail_piece(lambda cp: cp.start())
    each_block(lambda cp: cp.start())
    each_tail_piece(lambda cp: cp.wait())
    each_block(lambda cp: cp.wait())


def _dispatch_kernel(nq_ref, pad_start_ref, pad_pieces_ref, n_used_ref, dst_ref, metat_ref,
                     h_ref, xb_ref, l_ref, z_ref, sem):
    i = pl.program_id(0)
    hw = l_ref.shape[1]
    slots = [metat_ref[k:k + 1, :] for k in range(TOP_K)]
    perm = _slot_matrix(slots, [1.0] * TOP_K, slot_axis=0).astype(bf16)
    packed = _pack_bf16_pairs(_dot(perm, h_ref[:, 0:hw]), _dot(perm, h_ref[:, hw:]))

    def piece(q):
        return _piece_copy(l_ref, xb_ref, q * DMA_ROWS, dst_ref[0, 0, q], sem.at[0])

    def drain(n):
        min_rows = MIN_PIECES * DMA_ROWS
        bulk = pltpu.make_async_copy(l_ref.at[0:min_rows], xb_ref.at[0:min_rows], sem.at[0])
        _wait_pieces(n, piece, bulk)

    @pl.when(i > 0)
    def _():
        drain(nq_ref[jnp.maximum(i - 1, 0)])

    l_ref[...] = packed
    _start_pieces(nq_ref[i], piece)

    @pl.when(i == pl.num_programs(0) - 1)
    def _():
        _zero_fill_padding(pad_start_ref, pad_pieces_ref, n_used_ref, xb_ref, z_ref, sem.at[1])
        drain(nq_ref[i])


def _dispatch(nq, pad_start, pad_pieces, n_used, dst, meta_t, h, n_rows):
    n, d = h.shape
    hw = d // 2
    t = MOE_TILE
    grid_spec = pltpu.PrefetchScalarGridSpec(
        num_scalar_prefetch=4,
        grid=(n // t,),
        in_specs=[pl.BlockSpec((1, 1, LOCAL_PIECES), lambda i, *_: (i, 0, 0), memory_space=pltpu.SMEM),
                  pl.BlockSpec((SUBLANE, t), lambda i, *_: (0, i)),
                  pl.BlockSpec((t, d), lambda i, *_: (i, 0))],
        out_specs=pl.BlockSpec(memory_space=pl.ANY),
        scratch_shapes=[pltpu.VMEM((LOCAL_ROWS, hw), jnp.uint32),
                        pltpu.VMEM((MOE_BLOCK, hw), jnp.uint32),
                        pltpu.SemaphoreType.DMA((2,))],
    )
    return pl.pallas_call(
        _dispatch_kernel,
        grid_spec=grid_spec,
        out_shape=jax.ShapeDtypeStruct((n_rows, hw), jnp.uint32),
        compiler_params=pltpu.CompilerParams(
            dimension_semantics=("arbitrary",), vmem_limit_bytes=VMEM_LIMIT),
        name="dispatch",
    )(nq, pad_start, pad_pieces, n_used, dst, meta_t, h)


def _expert_kernel(be_ref, ns_ref, x_ref, wi_ref, bi_ref, wo_ref, bo_ref, y_ref,
                   wib_ref, wob_ref):
    b = pl.program_id(0)
    blk, hw = x_ref.shape
    dff = wo_ref.shape[1]
    subs = [slice(s * EXPERT_SUB_BLOCK, (s + 1) * EXPERT_SUB_BLOCK)
            for s in range(blk // EXPERT_SUB_BLOCK)]
    n_live = ns_ref[b]

    @pl.when(jnp.logical_and(n_live > 0,
                             jnp.logical_or(b == 0, be_ref[b] != be_ref[jnp.maximum(b - 1, 0)])))
    def _():
        for j in range(dff // LANE):
            wib_ref[:, 2 * j * LANE:(2 * j + 1) * LANE] = wi_ref[0, :, j * LANE:(j + 1) * LANE].astype(bf16)
            wib_ref[:, (2 * j + 1) * LANE:(2 * j + 2) * LANE] = (
                wi_ref[0, :, dff + j * LANE:dff + (j + 1) * LANE].astype(bf16))
        wob_ref[...] = wo_ref[0].astype(bf16)

    def up(rows):
        xa, xb = _unpack_bf16_pairs(x_ref[rows, :])
        return _dot(xa, wib_ref[0:hw, :]) + _dot(xb, wib_ref[hw:, :]) + bi_ref[0]

    def down(rows, hc):
        acts = []
        for j in range(dff // LANE):
            glu = jnp.minimum(hc[:, 2 * j * LANE:(2 * j + 1) * LANE], SWIGLU_LIMIT)
            lin = jnp.clip(hc[:, (2 * j + 1) * LANE:(2 * j + 2) * LANE], -SWIGLU_LIMIT, SWIGLU_LIMIT)
            acts.append((glu * (1.0 / (1.0 + jnp.exp(-SWIGLU_ALPHA * glu))) * (lin + 1.0)).astype(bf16))
        act = jnp.concatenate(acts, axis=1)
        y = (_dot(act, wob_ref[...]) + bo_ref[0]).astype(bf16).astype(f32)
        y_ref[rows, :] = _pack_bf16_pairs(y[:, 0:hw], y[:, hw:])

    def run(n):
        if n > 0:
            hc_next = up(subs[0])
        for s in range(n):
            hc = hc_next
            if s + 1 < n:
                hc_next = up(subs[s + 1])
            down(subs[s], hc)
        for rows in subs[n:]:
            y_ref[rows, :] = jnp.zeros((EXPERT_SUB_BLOCK, hw), y_ref.dtype)

    for n in range(len(subs) + 1):
        pl.when(n_live == n)(functools.partial(run, n))


def _experts(block_expert, block_live, xb, w_in, b_in, w_out, b_out):
    n_rows, hw = xb.shape
    e, d, dff2 = w_in.shape
    dff = w_out.shape[1]
    n_blocks = n_rows // MOE_BLOCK
    b_in_interleaved = b_in.reshape(e, 2, dff // LANE, LANE).transpose(0, 2, 1, 3)
    grid_spec = pltpu.PrefetchScalarGridSpec(
        num_scalar_prefetch=2,
        grid=(n_blocks,),
        in_specs=[
            pl.BlockSpec((MOE_BLOCK, hw), lambda b, be, nu: (b, 0)),
            pl.BlockSpec((1, d, dff2), lambda b, be, nu: (be[b], 0, 0)),
            pl.BlockSpec((1, 1, dff2), lambda b, be, nu: (be[b], 0, 0)),
            pl.BlockSpec((1, dff, d), lambda b, be, nu: (be[b], 0, 0)),
            pl.BlockSpec((1, 1, d), lambda b, be, nu: (be[b], 0, 0)),
        ],
        out_specs=pl.BlockSpec((MOE_BLOCK, hw), lambda b, be, nu: (b, 0)),
        scratch_shapes=[pltpu.VMEM((d, dff2), bf16), pltpu.VMEM((dff, d), bf16)],
    )
    return pl.pallas_call(
        _expert_kernel,
        grid_spec=grid_spec,
        out_shape=jax.ShapeDtypeStruct((n_rows, hw), jnp.uint32),
        compiler_params=pltpu.CompilerParams(
            dimension_semantics=("arbitrary",), vmem_limit_bytes=VMEM_LIMIT),
        name="experts",
    )(block_expert, block_live, xb, w_in, b_in_interleaved.reshape(e, 1, dff2), w_out,
      b_out.reshape(e, 1, d))


def _combine_kernel(nq_ref, dst_ref, dstn_ref, meta_ref, gate_ref, xm_ref, y_ref, o_ref,
                    ly_ref, sem):
    i = pl.program_id(0)
    last = pl.num_programs(0) - 1
    hw = ly_ref.shape[2]
    slot = i % 2

    def fetch(tile, table_ref, buf):
        ly_ref[buf, MOE_TILE * TOP_K:, :] = jnp.zeros((LOCAL_ROWS - MOE_TILE * TOP_K, hw), jnp.uint32)

        _start_pieces(nq_ref[tile], lambda q: _piece_copy(
            y_ref, ly_ref.at[buf], table_ref[0, 0, q], q * DMA_ROWS, sem.at[buf]))

    @pl.when(i == 0)
    def _():
        fetch(i, dst_ref, slot)

    @pl.when(i < last)
    def _():
        fetch(jnp.minimum(i + 1, last), dstn_ref, 1 - slot)

    min_rows = MIN_PIECES * DMA_ROWS
    _wait_pieces(
        nq_ref[i],
        lambda q: _piece_copy(y_ref, ly_ref.at[slot], dst_ref[0, 0, q], q * DMA_ROWS, sem.at[slot]),
        pltpu.make_async_copy(y_ref.at[0:min_rows], ly_ref.at[slot, 0:min_rows], sem.at[slot]))
    gates = gate_ref[...]
    meta = meta_ref[...]
    g = _slot_matrix([meta[:, k:k + 1] for k in range(TOP_K)],
                     [gates[:, k:k + 1] for k in range(TOP_K)], slot_axis=1).astype(bf16)
    ya, yb = _unpack_bf16_pairs(ly_ref[slot])
    o_ref[:, 0:hw] = xm_ref[:, 0:hw] + _dot(g, ya)
    o_ref[:, hw:] = xm_ref[:, hw:] + _dot(g, yb)


def _combine(nq, dst, meta, gates, xm, y):
    n, d = xm.shape
    hw = d // 2
    t = MOE_TILE
    n_tiles = n // t
    grid_spec = pltpu.PrefetchScalarGridSpec(
        num_scalar_prefetch=1,
        grid=(n_tiles,),
        in_specs=[pl.BlockSpec((1, 1, LOCAL_PIECES), lambda i, nq: (i, 0, 0), memory_space=pltpu.SMEM),
                  pl.BlockSpec((1, 1, LOCAL_PIECES), lambda i, nq: (jnp.minimum(i + 1, n_tiles - 1), 0, 0),
                               memory_space=pltpu.SMEM),
                  pl.BlockSpec((t, LANE), lambda i, nq: (i, 0)),
                  pl.BlockSpec((t, LANE), lambda i, nq: (i, 0)),
                  pl.BlockSpec((t, d), lambda i, nq: (i, 0)),
                  pl.BlockSpec(memory_space=pl.ANY)],
        out_specs=pl.BlockSpec((t, d), lambda i, nq: (i, 0)),
        scratch_shapes=[pltpu.VMEM((2, LOCAL_ROWS, hw), jnp.uint32), pltpu.SemaphoreType.DMA((2,))],
    )
    return pl.pallas_call(
        _combine_kernel,
        grid_spec=grid_spec,
        out_shape=jax.ShapeDtypeStruct((n, d), f32),
        compiler_params=pltpu.CompilerParams(
            dimension_semantics=("arbitrary",), vmem_limit_bytes=VMEM_LIMIT),
        name="combine",
    )(nq, dst, dst, meta, gates, xm, y)


def _rel_bias_table(rel_bias):
    band = BAND_CHUNKS * CHUNK
    width = band + CHUNK
    dist = (np.arange(width) - CHUNK)[::-1]
    ext = rel_bias[:, np.clip(dist, -REL_CLIP, REL_CLIP) + REL_CLIP].astype(f32)
    heads = ext.shape[0]
    tiled = jnp.broadcast_to(ext[:, None, :], (heads, CHUNK, width)).reshape(heads, CHUNK * width)
    skewed = tiled[:, :CHUNK * (width - 1)].reshape(heads, CHUNK, width - 1)
    return skewed[:, :, CHUNK - 1:CHUNK - 1 + band]


def _round_up(x, m):
    return (x + m - 1) // m * m


def _routing_tables(cnt, n_tokens):
    experts = jnp.arange(N_EXPERTS, dtype=jnp.int32)
    c = cnt.reshape(-1, LANE)[:, :N_EXPERTS].astype(jnp.int32)
    n_tiles = c.shape[0]
    cp = _round_up(c, DMA_ROWS)
    lend = jnp.cumsum(cp, axis=1)
    lstart = lend - cp
    nq = (lend[:, -1] // DMA_ROWS).astype(jnp.int32)
    region = jnp.sum(cp, axis=0)
    padded = _round_up(region, MOE_BLOCK)
    pend = jnp.cumsum(padded)
    pstart = pend - padded
    base = pstart[None, :] + jnp.cumsum(cp, axis=0) - cp
    q0 = jnp.arange(LOCAL_PIECES, dtype=jnp.int32) * DMA_ROWS
    e_q = jnp.minimum(jnp.sum(lend[:, None, :] <= q0[None, :, None], axis=-1), N_EXPERTS - 1)
    shift = jnp.sum(jnp.where(e_q[:, :, None] == experts, (base - lstart)[:, None, :], 0), axis=-1)
    dst = jnp.where(q0[None, :] < lend[:, -1:], shift + q0[None, :], 0).astype(jnp.int32)

    n_blocks = -(-(n_tokens * TOP_K + n_tiles * N_EXPERTS * (DMA_ROWS - 1)) // MOE_BLOCK) + N_EXPERTS
    blk0 = jnp.arange(n_blocks, dtype=jnp.int32) * MOE_BLOCK
    n_used = (pend[-1] // MOE_BLOCK).astype(jnp.int32)
    be = jnp.minimum(jnp.sum(pend[None, :] <= blk0[:, None], axis=1), N_EXPERTS - 1).astype(jnp.int32)
    last = jnp.sum(jnp.where(jnp.arange(n_blocks) == n_used - 1, be, 0))
    be = jnp.where(jnp.arange(n_blocks) < n_used, be, last)
    onehot_be = be[:, None] == experts
    region_end = jnp.sum(jnp.where(onehot_be, pstart + region, 0), axis=1)
    live_rows = jnp.clip(region_end - blk0, 0, MOE_BLOCK)
    live = jnp.where(jnp.arange(n_blocks) < n_used, -(-live_rows // EXPERT_SUB_BLOCK), 0)
    pad_start = (pstart + region).astype(jnp.int32)
    pad_pieces = ((padded - region) // DMA_ROWS).astype(jnp.int32)
    return (nq, pad_start, pad_pieces, dst.reshape(n_tiles, 1, LOCAL_PIECES), be,
            live.astype(jnp.int32), n_used.reshape(1), n_blocks)


def _layer(x, norm1_g, w_in, gate_up, gate_bias, gla_norm_g, q_norm_g, k_norm_g, rel_bias, w_out,
           norm2_g, router_w, router_b, moe_w_in, moe_b_in, moe_w_out, moe_b_out):
    batch, seq, d = x.shape
    n = batch * seq
    x2 = x.reshape(n, d)

    pieces = jnp.split(w_in, np.cumsum(IN_SIZES)[:-1].tolist(), axis=-1)
    pieces[3] = jnp.pad(pieces[3], ((0, 0), (0, LANE - GLA_GATE_RANK)))
    widths = [p.shape[-1] for p in pieces]
    w_all = jnp.concatenate(pieces, axis=-1).astype(bf16)
    gq, gk, gv, glr, gog, aq, ak, av = _inproj(x2, norm1_g.reshape(1, d), w_all, widths)

    gup = jnp.pad(gate_up, ((0, LANE - GLA_GATE_RANK), (0, 0))).astype(bf16)
    o_gla = _gla(gq, gk, gv, glr, gog, gup, gate_bias.reshape(1, -1), gla_norm_g.reshape(1, -1),
                 batch, seq)
    tile2 = lambda g: jnp.tile(g.reshape(1, -1), (1, LANE // ATT_DH))
    o_att = _attention(aq, ak, av, _rel_bias_table(rel_bias), tile2(q_norm_g), tile2(k_norm_g),
                       batch, seq)

    rw = jnp.pad(router_w, ((0, 0), (0, LANE - N_EXPERTS)))
    rw_hi = rw.astype(bf16)
    rw_lo = (rw - rw_hi.astype(f32)).astype(bf16)
    rb = jnp.pad(router_b, (0, LANE - N_EXPERTS)).reshape(1, LANE)
    xm, h2, meta, meta_t, gates, cnt = _mid(o_gla, o_att, x2, w_out.astype(bf16),
                                            norm2_g.reshape(1, d), rw_hi, rw_lo, rb)

    nq, pad_start, pad_pieces, dst, be, live, n_used, n_blocks = _routing_tables(cnt, n)
    xb = _dispatch(nq, pad_start, pad_pieces, n_used, dst, meta_t, h2, n_blocks * MOE_BLOCK)
    y = _experts(be, live, xb, moe_w_in, moe_b_in, moe_w_out, moe_b_out)
    out = _combine(nq, dst, meta, gates, xm, y)
    return out.reshape(batch, seq, d)


def kernel(x, norm1_g, w_in, gla_gate_up, gla_gate_bias, gla_norm_g, q_norm_g, k_norm_g, rel_bias, w_out, norm2_g, router_w, router_b, moe_w_in, moe_b_in, moe_w_out, moe_b_out):
    for l in range(norm1_g.shape[0]):
        x = _layer(x, norm1_g[l], w_in[l], gla_gate_up[l], gla_gate_bias[l], gla_norm_g[l],
                   q_norm_g[l], k_norm_g[l], rel_bias[l], w_out[l], norm2_g[l], router_w[l],
                   router_b[l], moe_w_in[l], moe_b_in[l], moe_w_out[l], moe_b_out[l])
    return x
```

```python
import functools

import numpy as np
import jax
import jax.numpy as jnp
from jax import lax
from jax.experimental import pallas as pl
from jax.experimental.pallas import tpu as pltpu

CHUNK = 64
EPS = 1e-6
GLA_HEADS = 4
GLA_DK = 64
GLA_DV = 128
GLA_GATE_RANK = 16
GLA_GATE_TAU = 16.0
ATT_HEADS = 8
ATT_DH = 64
N_BACK_CHUNKS = 8
BAND_CHUNKS = N_BACK_CHUNKS + 1
REL_CLIP = 256
MASK_VALUE = -1e30
N_EXPERTS = 32
TOP_K = 4
SWIGLU_ALPHA = 1.702
SWIGLU_LIMIT = 7.0
MOE_BLOCK = 1024
EXPERT_SUB_BLOCK = 256

LANE = 128
SUBLANE = 8
GLA_QK_W = GLA_HEADS * GLA_DK
GLA_V_W = GLA_HEADS * GLA_DV
ATT_W = ATT_HEADS * ATT_DH
IN_SIZES = (GLA_QK_W, GLA_QK_W, GLA_V_W, GLA_GATE_RANK, GLA_V_W, ATT_W, ATT_W, ATT_W)
SEQ_TILE = N_BACK_CHUNKS * CHUNK
ROW_TILE = 512
MID_TILE = 1024
MOE_TILE = 256
DMA_ROWS = SUBLANE
LOCAL_PIECES = 160
LOCAL_ROWS = LOCAL_PIECES * DMA_ROWS
MIN_PIECES = MOE_TILE * TOP_K // DMA_ROWS
ISSUE_UNROLL = 4
VMEM_LIMIT = 48 * 1024 * 1024

f32 = jnp.float32
bf16 = jnp.bfloat16


def _rms(x, g):
    return x * lax.rsqrt(jnp.mean(x * x, axis=-1, keepdims=True) + EPS) * g


def _dot(a, b):
    return jnp.dot(a, b, preferred_element_type=f32)


def _dot_nt(a, b):
    return lax.dot_general(a, b, (((1,), (1,)), ((), ())), preferred_element_type=f32)


def _dot_tn(a, b):
    return lax.dot_general(a, b, (((0,), (0,)), ((), ())), preferred_element_type=f32)


def _split_bf16(x):
    hi = x.astype(bf16)
    lo = (x - hi.astype(f32)).astype(bf16)
    return hi, lo


def _head_norm(x, g):
    lo = lax.broadcasted_iota(jnp.int32, (1, LANE), 1) < ATT_DH
    sq = x * x
    s0 = jnp.sum(jnp.where(lo, sq, 0.0), axis=-1, keepdims=True)
    s1 = jnp.sum(jnp.where(lo, 0.0, sq), axis=-1, keepdims=True)
    r = jnp.where(lo, lax.rsqrt(s0 * (1.0 / ATT_DH) + EPS), lax.rsqrt(s1 * (1.0 / ATT_DH) + EPS))
    return x * r * g


def _inproj_kernel(x_ref, g_ref, w_ref, qg_ref, kg_ref, gq, gk, gv, glr, gog, qs_ref, ak, av):
    t = x_ref.shape[0]
    n_pairs = ATT_W // LANE
    h = _rms(x_ref[...], g_ref[...]).astype(bf16)
    offsets = np.cumsum((0,) + tuple(r.shape[-1] for r in (gq, gk, gv, glr, gog)) + (ATT_W, ATT_W))
    lo = lax.broadcasted_iota(jnp.int32, (1, LANE), 1) < ATT_DH

    q_all = _dot(h, w_ref[:, offsets[5]:offsets[5] + ATT_W])
    k_all = _dot(h, w_ref[:, offsets[6]:offsets[6] + ATT_W])
    for p in range(n_pairs):
        pair = slice(p * LANE, (p + 1) * LANE)
        qn = _head_norm(q_all[:, pair], qg_ref[...]) * (ATT_DH ** -0.5)
        q_lo = jnp.where(lo, qn, 0.0).astype(bf16)
        q_hi = jnp.where(lo, 0.0, qn).astype(bf16)
        for c in range(t // CHUNK):
            rows = slice(c * CHUNK, (c + 1) * CHUNK)
            qs_ref[c * n_pairs + p, 0:CHUNK, :] = q_lo[rows]
            qs_ref[c * n_pairs + p, CHUNK:2 * CHUNK, :] = q_hi[rows]
        ak[:, pair] = _head_norm(k_all[:, pair], kg_ref[...]).astype(ak.dtype)
    for o_ref, off in zip((gq, gk, gv, glr, gog, av), tuple(offsets[:5]) + (offsets[7],)):
        o_ref[...] = _dot(h, w_ref[:, off:off + o_ref.shape[-1]]).astype(o_ref.dtype)


def _inproj(x2, g, w, widths, qg, kg):
    n, d = x2.shape
    n_pairs = ATT_W // LANE
    blocks_per_tile = ROW_TILE // CHUNK * n_pairs
    rows = lambda wd: pl.BlockSpec((ROW_TILE, wd), lambda i: (i, 0))
    full = lambda a: pl.BlockSpec(a.shape, lambda i: (0,) * a.ndim)
    plain = lambda wd: jax.ShapeDtypeStruct((n, wd), bf16)
    return pl.pallas_call(
        _inproj_kernel,
        grid=(n // ROW_TILE,),
        in_specs=[rows(d), full(g), full(w), full(qg), full(kg)],
        out_specs=[rows(wd) for wd in widths[:5]]
        + [pl.BlockSpec((blocks_per_tile, 2 * CHUNK, LANE), lambda i: (i, 0, 0)), rows(ATT_W), rows(ATT_W)],
        out_shape=[plain(wd) for wd in widths[:5]]
        + [jax.ShapeDtypeStruct((n // CHUNK * n_pairs, 2 * CHUNK, LANE), bf16), plain(ATT_W), plain(ATT_W)],
        compiler_params=pltpu.CompilerParams(
            dimension_semantics=("arbitrary",), vmem_limit_bytes=VMEM_LIMIT),
        name="inproj",
    )(x2, g, w, qg, kg)


def _gla_kernel(q_ref, k_ref, v_ref, lr_ref, og_ref, gup_ref, gb_ref, ng_ref, o_ref, st_ref,
                u_ref, sb_ref):
    t = q_ref.shape[0]

    @pl.when(pl.program_id(1) == 0)
    def _():
        st_ref[...] = jnp.zeros_like(st_ref)

    z = _dot(lr_ref[...], gup_ref[...]) + gb_ref[...]
    log_a = (jnp.minimum(z, 0.0) - jnp.log1p(jnp.exp(-jnp.abs(z)))) * (1.0 / GLA_GATE_TAU)
    row = lax.broadcasted_iota(jnp.int32, (t, t), 0)
    col = lax.broadcasted_iota(jnp.int32, (t, t), 1)
    tri = jnp.where((col <= row) & ((col // CHUNK) == (row // CHUNK)), 1.0, 0.0).astype(bf16)
    la_hi, la_lo = _split_bf16(log_a)
    cum_all = _dot(tri, la_hi) + _dot(tri, la_lo)
    lane = lax.broadcasted_iota(jnp.int32, (1, LANE), 1)
    half_mask = (lane < GLA_DK, lane >= GLA_DK)
    n_chunks = t // CHUNK
    chunk_rows = [slice(c * CHUNK, (c + 1) * CHUNK) for c in range(n_chunks)]
    pair_of = lambda h: slice((h // 2) * LANE, (h // 2 + 1) * LANE)
    head_of = lambda h: slice(h * GLA_DV, (h + 1) * GLA_DV)

    decs = []
    for c, rows in enumerate(chunk_rows):
        cum = cum_all[rows]
        tot = cum[CHUNK - 1:CHUNK]
        kdec = k_ref[rows, :].astype(f32) * jnp.exp(tot - cum)
        decs.append(jnp.exp(tot))
        for h in range(GLA_HEADS):
            kd = jnp.where(half_mask[h % 2], kdec[:, pair_of(h)], 0.0).astype(bf16)
            u_ref[c, h] = _dot_tn(v_ref[rows, head_of(h)], kd)

    for h in range(GLA_HEADS):
        st = st_ref[h]
        for c in range(n_chunks):
            st = st * decs[c][:, pair_of(h)] + u_ref[c, h]
            sb_ref[c, h] = st.astype(bf16)
        st_ref[h] = st

    for c, rows in enumerate(chunk_rows):
        for h in range(GLA_HEADS):
            o = _dot_nt(q_ref[rows, pair_of(h)], sb_ref[c, h]) * (GLA_DK ** -0.5)
            o = _rms(o, ng_ref[...])
            g = og_ref[rows, head_of(h)].astype(f32)
            o_ref[rows, head_of(h)] = (o * (g / (1.0 + jnp.exp(-g)))).astype(o_ref.dtype)


def _gla(gq, gk, gv, glr, gog, gup, gb, ng, batch, seq):
    nt = seq // SEQ_TILE
    tile = lambda w: pl.BlockSpec((SEQ_TILE, w), lambda b, i: (b * nt + i, 0))
    full = lambda a: pl.BlockSpec(a.shape, lambda b, i: (0,) * a.ndim)
    return pl.pallas_call(
        _gla_kernel,
        grid=(batch, nt),
        in_specs=[tile(GLA_QK_W), tile(GLA_QK_W), tile(GLA_V_W), tile(LANE), tile(GLA_V_W),
                  full(gup), full(gb), full(ng)],
        out_specs=tile(GLA_V_W),
        out_shape=jax.ShapeDtypeStruct((batch * seq, GLA_V_W), bf16),
        scratch_shapes=[pltpu.VMEM((GLA_HEADS, GLA_DV, LANE), f32),
                        pltpu.VMEM((SEQ_TILE // CHUNK, GLA_HEADS, GLA_DV, LANE), f32),
                        pltpu.VMEM((SEQ_TILE // CHUNK, GLA_HEADS, GLA_DV, LANE), bf16)],
        compiler_params=pltpu.CompilerParams(
            dimension_semantics=("arbitrary", "arbitrary"), vmem_limit_bytes=VMEM_LIMIT),
        name="gla",
    )(gq, gk, gv, glr, gog, gup, gb, ng)


def _att_kernel(qs_ref, k_ref, v_ref, bias_ref, o_ref, kb_ref, vb_ref, sn_ref, en_ref):
    t = k_ref.shape[0]
    n_pairs = ATT_W // LANE
    band = BAND_CHUNKS * CHUNK
    first = pl.program_id(1) == 0

    @pl.when(first)
    def _():
        kb_ref[0:t, :] = jnp.zeros((t, ATT_W), bf16)
        vb_ref[0:t, :] = jnp.zeros((t, 2 * ATT_W), bf16)

    @pl.when(jnp.logical_not(first))
    def _():
        kb_ref[0:t, :] = kb_ref[t:2 * t, :]
        vb_ref[0:t, :] = vb_ref[t:2 * t, :]

    lane = lax.broadcasted_iota(jnp.int32, (1, LANE), 1)
    lo = lane < ATT_DH

    kb_ref[t:2 * t, :] = k_ref[...]
    for p in range(n_pairs):
        vb_ref[t:2 * t, 2 * p * LANE:(2 * p + 1) * LANE] = v_ref[:, p * LANE:(p + 1) * LANE]
        vb_ref[t:2 * t, (2 * p + 1) * LANE:(2 * p + 2) * LANE] = jnp.ones((t, LANE), bf16)

    colk = lax.broadcasted_iota(jnp.int32, (1, band), 1)

    def chunk_loop(masked):
        n_chunks = t // CHUNK

        def scores(c, p):
            k2 = kb_ref[pl.ds(pl.multiple_of(c * CHUNK, CHUNK), band), p * LANE:(p + 1) * LANE]
            return _dot_nt(qs_ref[c * n_pairs + p], k2)

        def weighted_values(e, r, p):
            v2 = vb_ref[pl.ds(r, band), 2 * p * LANE:(2 * p + 2) * LANE]
            pvl = _dot(e, v2)
            pv = pvl[:, 0:LANE] / pvl[:, LANE:2 * LANE]
            o_ref[pl.ds(r, CHUNK), p * LANE:(p + 1) * LANE] = jnp.where(
                lo, pv[0:CHUNK], pv[CHUNK:2 * CHUNK]).astype(o_ref.dtype)

        def chunk_body(c, carry):
            r0 = pl.multiple_of(c * CHUNK, CHUNK)
            r_prev = pl.multiple_of(jnp.maximum(c - 1, 0) * CHUNK, CHUNK)
            s_next = sn_ref[...]
            e_prev = en_ref[...]
            for p in range(n_pairs):
                s = s_next + bias_ref[p]
                if p + 1 < n_pairs:
                    s_next = scores(c, p + 1)
                else:
                    sn_ref[...] = scores(jnp.minimum(c + 1, n_chunks - 1), 0)
                if masked:
                    s = jnp.where(colk >= t - c * CHUNK, s, MASK_VALUE)
                e = jnp.exp(s - jnp.max(s, axis=-1, keepdims=True)).astype(bf16)
                if p == 0:
                    weighted_values(e_prev, r_prev, n_pairs - 1)
                else:
                    weighted_values(e_prev, r0, p - 1)
                e_prev = e
            en_ref[...] = e_prev
            return carry

        sn_ref[...] = scores(jnp.int32(0), 0)
        en_ref[...] = jnp.ones_like(en_ref)
        lax.fori_loop(0, n_chunks, chunk_body, 0)
        weighted_values(en_ref[...], (n_chunks - 1) * CHUNK, n_pairs - 1)

    @pl.when(first)
    def _():
        chunk_loop(True)

    @pl.when(jnp.logical_not(first))
    def _():
        chunk_loop(False)


def _attention(qs, ak, av, bias, batch, seq):
    nt = seq // SEQ_TILE
    n_pairs = ATT_W // LANE
    tile = pl.BlockSpec((SEQ_TILE, ATT_W), lambda b, i: (b * nt + i, 0))
    q_tile = pl.BlockSpec((SEQ_TILE // CHUNK * n_pairs, 2 * CHUNK, LANE), lambda b, i: (b * nt + i, 0, 0))
    full = lambda a: pl.BlockSpec(a.shape, lambda b, i: (0,) * a.ndim)
    bias2 = bias.reshape(n_pairs, 2 * CHUNK, BAND_CHUNKS * CHUNK)
    return pl.pallas_call(
        _att_kernel,
        grid=(batch, nt),
        in_specs=[q_tile, tile, tile, full(bias2)],
        out_specs=tile,
        out_shape=jax.ShapeDtypeStruct((batch * seq, ATT_W), bf16),
        scratch_shapes=[pltpu.VMEM((2 * SEQ_TILE, ATT_W), bf16),
                        pltpu.VMEM((2 * SEQ_TILE, 2 * ATT_W), bf16),
                        pltpu.VMEM((2 * CHUNK, BAND_CHUNKS * CHUNK), f32),
                        pltpu.VMEM((2 * CHUNK, BAND_CHUNKS * CHUNK), bf16)],
        compiler_params=pltpu.CompilerParams(
            dimension_semantics=("arbitrary", "arbitrary"), vmem_limit_bytes=VMEM_LIMIT),
        name="attention",
    )(qs, ak, av, bias2)


def _mid_kernel(og_ref, oa_ref, x_ref, wo_ref, g2_ref, rwh_ref, rwl_ref, rb_ref,
                xm_ref, h_ref, meta_ref, metat_ref, gate_ref, cnt_ref):
    t, d = x_ref.shape
    mt = MOE_TILE
    tiles = [slice(s * mt, (s + 1) * mt) for s in range(t // mt)]
    lane = lax.broadcasted_iota(jnp.int32, (mt, LANE), 1)
    lane_f = lane.astype(f32)
    row = lax.broadcasted_iota(jnp.int32, (mt, mt), 0)
    col = lax.broadcasted_iota(jnp.int32, (mt, mt), 1)
    earlier = jnp.where(col < row, 1.0, 0.0).astype(bf16)
    e_r = lax.broadcasted_iota(jnp.int32, (LANE, LANE), 0)
    e_c = lax.broadcasted_iota(jnp.int32, (LANE, LANE), 1)
    before = jnp.where(e_r < e_c, 1.0, 0.0).astype(bf16)

    def project(rows):
        xm = (x_ref[rows, :] + _dot(og_ref[rows, :], wo_ref[0:GLA_V_W, :])
              + _dot(oa_ref[rows, :], wo_ref[GLA_V_W:, :]))
        xm_ref[rows, :] = xm
        h_hi, h_lo = _split_bf16(_rms(xm, g2_ref[...]))
        h_ref[rows, :] = h_hi
        return h_hi, h_lo

    def route(h_hi, h_lo):
        return (_dot(h_hi, rwh_ref[...]) + _dot(h_lo, rwh_ref[...]) + _dot(h_hi, rwl_ref[...])
                + rb_ref[...])

    def top_k(logits):
        l = jnp.where(lane < N_EXPERTS, logits, -jnp.inf)
        vals, onehots = [], []
        for _ in range(TOP_K):
            m = jnp.max(l, axis=-1, keepdims=True)
            ik = jnp.min(jnp.where(l == m, lane_f, float(LANE)), axis=-1, keepdims=True)
            vals.append(m)
            onehots.append(lane_f == ik)
            l = jnp.where(onehots[-1], -jnp.inf, l)
        es = [jnp.exp(v - vals[0]) for v in vals]
        den = es[0] + es[1] + es[2] + es[3]
        sel = jnp.zeros((mt, LANE), f32)
        for oh in onehots:
            sel = sel + jnp.where(oh, 1.0, 0.0)
        return onehots, [e / den for e in es], sel

    def slots(s, rows, onehots, gate_vals, sel):
        prefix = _dot(earlier, sel.astype(bf16))
        c_row = jnp.sum(sel, axis=0, keepdims=True)
        cnt_ref[0, s:s + 1, :] = c_row
        padded = jnp.ceil(c_row * (1.0 / DMA_ROWS)) * DMA_ROWS
        start_row = _dot(jnp.broadcast_to(padded, (SUBLANE, LANE)).astype(bf16), before)[0:1]
        slot_base = prefix + start_row
        meta = jnp.zeros((mt, LANE), f32)
        gates = jnp.zeros((mt, LANE), f32)
        for k in range(TOP_K):
            slot_k = jnp.sum(jnp.where(onehots[k], slot_base, 0.0), axis=-1, keepdims=True)
            meta = jnp.where(lane == k, slot_k, meta)
            gates = jnp.where(lane == k, gate_vals[k], gates)
        meta_ref[rows, :] = meta.astype(jnp.int32)
        metat_ref[:, rows] = meta.T[0:SUBLANE, :].astype(jnp.int32)
        gate_ref[rows, :] = gates

    hs = [project(rows) for rows in tiles]
    logits = [route(*h) for h in hs]
    routed = [top_k(lg) for lg in logits]
    for s, rows in enumerate(tiles):
        slots(s, rows, *routed[s])


def _mid(o_gla, o_att, x2, wo, g2, rwh, rwl, rb):
    n, d = x2.shape
    sub = MID_TILE // MOE_TILE
    tile = lambda w: pl.BlockSpec((MID_TILE, w), lambda i: (i, 0))
    full = lambda a: pl.BlockSpec(a.shape, lambda i: (0,) * a.ndim)
    return pl.pallas_call(
        _mid_kernel,
        grid=(n // MID_TILE,),
        in_specs=[tile(GLA_V_W), tile(ATT_W), tile(d), full(wo), full(g2), full(rwh), full(rwl),
                  full(rb)],
        out_specs=[tile(d), tile(d), tile(LANE),
                   pl.BlockSpec((SUBLANE, MID_TILE), lambda i: (0, i)), tile(LANE),
                   pl.BlockSpec((1, sub, LANE), lambda i: (i, 0, 0))],
        out_shape=[jax.ShapeDtypeStruct((n, d), f32),
                   jax.ShapeDtypeStruct((n, d), bf16),
                   jax.ShapeDtypeStruct((n, LANE), jnp.int32),
                   jax.ShapeDtypeStruct((SUBLANE, n), jnp.int32),
                   jax.ShapeDtypeStruct((n, LANE), f32),
                   jax.ShapeDtypeStruct((n // MID_TILE, sub, LANE), f32)],
        compiler_params=pltpu.CompilerParams(
            dimension_semantics=("arbitrary",), vmem_limit_bytes=VMEM_LIMIT),
        name="mid",
    )(o_gla, o_att, x2, wo, g2, rwh, rwl, rb)


def _slot_matrix(slots, values, slot_axis):
    n_tokens = slots[0].shape[1 - slot_axis]
    shape = (LOCAL_ROWS, n_tokens) if slot_axis == 0 else (n_tokens, LOCAL_ROWS)
    iota_shape = (LOCAL_ROWS, 1) if slot_axis == 0 else (1, LOCAL_ROWS)
    pos = lax.broadcasted_iota(jnp.int32, iota_shape, slot_axis)
    out = jnp.zeros(shape, f32)
    for slot_k, value_k in zip(slots, values):
        out = jnp.where(pos == slot_k, value_k, out)
    return out


def _pack_bf16_pairs(a, b):
    return (pltpu.bitcast(a, jnp.uint32) >> 16) | (pltpu.bitcast(b, jnp.uint32) & jnp.uint32(0xFFFF0000))


def _unpack_bf16_pairs(w):
    a = pltpu.bitcast(w << 16, f32).astype(bf16)
    b = pltpu.bitcast(w & jnp.uint32(0xFFFF0000), f32).astype(bf16)
    return a, b


def _piece_copy(src_ref, dst_ref, src_row, dst_row, sem):
    return pltpu.make_async_copy(src_ref.at[pl.ds(pl.multiple_of(src_row, DMA_ROWS), DMA_ROWS)],
                                 dst_ref.at[pl.ds(pl.multiple_of(dst_row, DMA_ROWS), DMA_ROWS)], sem)


def _start_pieces(n, piece):
    def group(g, c):
        for j in range(ISSUE_UNROLL):
            piece(g * ISSUE_UNROLL + j).start()
        return c

    n_groups = n // ISSUE_UNROLL
    lax.fori_loop(0, n_groups, group, 0)
    lax.fori_loop(n_groups * ISSUE_UNROLL, n, lambda q, c: (piece(q).start(), c)[1], 0)


def _wait_pieces(n, piece, bulk):
    bulk.wait()
    lax.fori_loop(MIN_PIECES, n, lambda q, c: (piece(q).wait(), c)[1], 0)


def _zero_fill_padding(pad_start_ref, pad_pieces_ref, n_used_ref, xb_ref, z_ref, sem):
    z_ref[...] = jnp.zeros_like(z_ref)
    n_blocks = xb_ref.shape[0] // MOE_BLOCK

    def tail_piece(e, q):
        return _piece_copy(z_ref, xb_ref, 0, pad_start_ref[e] + q * DMA_ROWS, sem)

    def block_copy(b):
        row = pl.multiple_of(b * MOE_BLOCK, MOE_BLOCK)
        return pltpu.make_async_copy(z_ref, xb_ref.at[pl.ds(row, MOE_BLOCK)], sem)

    def each_tail_piece(fn):
        def per_expert(e, c):
            lax.fori_loop(0, pad_pieces_ref[e], lambda q, cc: (fn(tail_piece(e, q)), cc)[1], 0)
            return c
        lax.fori_loop(0, N_EXPERTS, per_expert, 0)

    def each_block(fn):
        lax.fori_loop(n_used_ref[0], n_blocks, lambda b, c: (fn(block_copy(b)), c)[1], 0)

    each_tail_piece(lambda cp: cp.start())
    each_block(lambda cp: cp.start())
    each_tail_piece(lambda cp: cp.wait())
    each_block(lambda cp: cp.wait())


def _dispatch_kernel(nq_ref, pad_start_ref, pad_pieces_ref, n_used_ref, dst_ref, metat_ref,
                     h_ref, xb_ref, l_ref, z_ref, sem):
    i = pl.program_id(0)
    hw = l_ref.shape[1]
    slots = [metat_ref[k:k + 1, :] for k in range(TOP_K)]
    perm = _slot_matrix(slots, [1.0] * TOP_K, slot_axis=0).astype(bf16)
    packed = _pack_bf16_pairs(_dot(perm, h_ref[:, 0:hw]), _dot(perm, h_ref[:, hw:]))

    def piece(q):
        return _piece_copy(l_ref, xb_ref, q * DMA_ROWS, dst_ref[0, 0, q], sem.at[0])

    def drain(n):
        min_rows = MIN_PIECES * DMA_ROWS
        bulk = pltpu.make_async_copy(l_ref.at[0:min_rows], xb_ref.at[0:min_rows], sem.at[0])
        _wait_pieces(n, piece, bulk)

    @pl.when(i > 0)
    def _():
        drain(nq_ref[jnp.maximum(i - 1, 0)])

    l_ref[...] = packed
    _start_pieces(nq_ref[i], piece)

    @pl.when(i == pl.num_programs(0) - 1)
    def _():
        _zero_fill_padding(pad_start_ref, pad_pieces_ref, n_used_ref, xb_ref, z_ref, sem.at[1])
        drain(nq_ref[i])


def _dispatch(nq, pad_start, pad_pieces, n_used, dst, meta_t, h, n_rows):
    n, d = h.shape
    hw = d // 2
    t = MOE_TILE
    grid_spec = pltpu.PrefetchScalarGridSpec(
        num_scalar_prefetch=4,
        grid=(n // t,),
        in_specs=[pl.BlockSpec((1, 1, LOCAL_PIECES), lambda i, *_: (i, 0, 0), memory_space=pltpu.SMEM),
                  pl.BlockSpec((SUBLANE, t), lambda i, *_: (0, i)),
                  pl.BlockSpec((t, d), lambda i, *_: (i, 0))],
        out_specs=pl.BlockSpec(memory_space=pl.ANY),
        scratch_shapes=[pltpu.VMEM((LOCAL_ROWS, hw), jnp.uint32),
                        pltpu.VMEM((MOE_BLOCK, hw), jnp.uint32),
                        pltpu.SemaphoreType.DMA((2,))],
    )
    return pl.pallas_call(
        _dispatch_kernel,
        grid_spec=grid_spec,
        out_shape=jax.ShapeDtypeStruct((n_rows, hw), jnp.uint32),
        compiler_params=pltpu.CompilerParams(
            dimension_semantics=("arbitrary",), vmem_limit_bytes=VMEM_LIMIT),
        name="dispatch",
    )(nq, pad_start, pad_pieces, n_used, dst, meta_t, h)


def _expert_kernel(be_ref, ns_ref, x_ref, wi_ref, bi_ref, wo_ref, bo_ref, y_ref,
                   wib_ref, wob_ref):
    b = pl.program_id(0)
    blk, hw = x_ref.shape
    dff = wo_ref.shape[1]
    subs = [slice(s * EXPERT_SUB_BLOCK, (s + 1) * EXPERT_SUB_BLOCK)
            for s in range(blk // EXPERT_SUB_BLOCK)]
    n_live = ns_ref[b]

    @pl.when(jnp.logical_and(n_live > 0,
                             jnp.logical_or(b == 0, be_ref[b] != be_ref[jnp.maximum(b - 1, 0)])))
    def _():
        for j in range(dff // LANE):
            wib_ref[:, 2 * j * LANE:(2 * j + 1) * LANE] = wi_ref[0, :, j * LANE:(j + 1) * LANE].astype(bf16)
            wib_ref[:, (2 * j + 1) * LANE:(2 * j + 2) * LANE] = (
                wi_ref[0, :, dff + j * LANE:dff + (j + 1) * LANE].astype(bf16))
        wob_ref[...] = wo_ref[0].astype(bf16)

    def up(rows):
        xa, xb = _unpack_bf16_pairs(x_ref[rows, :])
        return _dot(xa, wib_ref[0:hw, :]) + _dot(xb, wib_ref[hw:, :]) + bi_ref[0]

    def down(rows, hc):
        acts = []
        for j in range(dff // LANE):
            glu = jnp.minimum(hc[:, 2 * j * LANE:(2 * j + 1) * LANE], SWIGLU_LIMIT)
            lin = jnp.clip(hc[:, (2 * j + 1) * LANE:(2 * j + 2) * LANE], -SWIGLU_LIMIT, SWIGLU_LIMIT)
            acts.append((glu * (1.0 / (1.0 + jnp.exp(-SWIGLU_ALPHA * glu))) * (lin + 1.0)).astype(bf16))
        act = jnp.concatenate(acts, axis=1)
        y = (_dot(act, wob_ref[...]) + bo_ref[0]).astype(bf16).astype(f32)
        y_ref[rows, :] = _pack_bf16_pairs(y[:, 0:hw], y[:, hw:])

    def run(n):
        if n > 0:
            hc_next = up(subs[0])
        for s in range(n):
            hc = hc_next
            if s + 1 < n:
                hc_next = up(subs[s + 1])
            down(subs[s], hc)
        for rows in subs[n:]:
            y_ref[rows, :] = jnp.zeros((EXPERT_SUB_BLOCK, hw), y_ref.dtype)

    for n in range(len(subs) + 1):
        pl.when(n_live == n)(functools.partial(run, n))


def _experts(block_expert, block_live, xb, w_in, b_in, w_out, b_out):
    n_rows, hw = xb.shape
    e, d, dff2 = w_in.shape
    dff = w_out.shape[1]
    n_blocks = n_rows // MOE_BLOCK
    b_in_interleaved = b_in.reshape(e, 2, dff // LANE, LANE).transpose(0, 2, 1, 3)
    grid_spec = pltpu.PrefetchScalarGridSpec(
        num_scalar_prefetch=2,
        grid=(n_blocks,),
        in_specs=[
            pl.BlockSpec((MOE_BLOCK, hw), lambda b, be, nu: (b, 0)),
            pl.BlockSpec((1, d, dff2), lambda b, be, nu: (be[b], 0, 0)),
            pl.BlockSpec((1, 1, dff2), lambda b, be, nu: (be[b], 0, 0)),
            pl.BlockSpec((1, dff, d), lambda b, be, nu: (be[b], 0, 0)),
            pl.BlockSpec((1, 1, d), lambda b, be, nu: (be[b], 0, 0)),
        ],
        out_specs=pl.BlockSpec((MOE_BLOCK, hw), lambda b, be, nu: (b, 0)),
        scratch_shapes=[pltpu.VMEM((d, dff2), bf16), pltpu.VMEM((dff, d), bf16)],
    )
    return pl.pallas_call(
        _expert_kernel,
        grid_spec=grid_spec,
        out_shape=jax.ShapeDtypeStruct((n_rows, hw), jnp.uint32),
        compiler_params=pltpu.CompilerParams(
            dimension_semantics=("arbitrary",), vmem_limit_bytes=VMEM_LIMIT),
        name="experts",
    )(block_expert, block_live, xb, w_in, b_in_interleaved.reshape(e, 1, dff2), w_out,
      b_out.reshape(e, 1, d))


def _combine_kernel(nq_ref, dst_ref, dstn_ref, meta_ref, gate_ref, xm_ref, y_ref, o_ref,
                    ly_ref, sem):
    i = pl.program_id(0)
    last = pl.num_programs(0) - 1
    hw = ly_ref.shape[2]
    slot = i % 2

    def fetch(tile, table_ref, buf):
        ly_ref[buf, MOE_TILE * TOP_K:, :] = jnp.zeros((LOCAL_ROWS - MOE_TILE * TOP_K, hw), jnp.uint32)

        _start_pieces(nq_ref[tile], lambda q: _piece_copy(
            y_ref, ly_ref.at[buf], table_ref[0, 0, q], q * DMA_ROWS, sem.at[buf]))

    @pl.when(i == 0)
    def _():
        fetch(i, dst_ref, slot)

    @pl.when(i < last)
    def _():
        fetch(jnp.minimum(i + 1, last), dstn_ref, 1 - slot)

    min_rows = MIN_PIECES * DMA_ROWS
    _wait_pieces(
        nq_ref[i],
        lambda q: _piece_copy(y_ref, ly_ref.at[slot], dst_ref[0, 0, q], q * DMA_ROWS, sem.at[slot]),
        pltpu.make_async_copy(y_ref.at[0:min_rows], ly_ref.at[slot, 0:min_rows], sem.at[slot]))
    gates = gate_ref[...]
    meta = meta_ref[...]
    g = _slot_matrix([meta[:, k:k + 1] for k in range(TOP_K)],
                     [gates[:, k:k + 1] for k in range(TOP_K)], slot_axis=1).astype(bf16)
    ya, yb = _unpack_bf16_pairs(ly_ref[slot])
    o_ref[:, 0:hw] = xm_ref[:, 0:hw] + _dot(g, ya)
    o_ref[:, hw:] = xm_ref[:, hw:] + _dot(g, yb)


def _combine(nq, dst, meta, gates, xm, y):
    n, d = xm.shape
    hw = d // 2
    t = MOE_TILE
    n_tiles = n // t
    grid_spec = pltpu.PrefetchScalarGridSpec(
        num_scalar_prefetch=1,
        grid=(n_tiles,),
        in_specs=[pl.BlockSpec((1, 1, LOCAL_PIECES), lambda i, nq: (i, 0, 0), memory_space=pltpu.SMEM),
                  pl.BlockSpec((1, 1, LOCAL_PIECES), lambda i, nq: (jnp.minimum(i + 1, n_tiles - 1), 0, 0),
                               memory_space=pltpu.SMEM),
                  pl.BlockSpec((t, LANE), lambda i, nq: (i, 0)),
                  pl.BlockSpec((t, LANE), lambda i, nq: (i, 0)),
                  pl.BlockSpec((t, d), lambda i, nq: (i, 0)),
                  pl.BlockSpec(memory_space=pl.ANY)],
        out_specs=pl.BlockSpec((t, d), lambda i, nq: (i, 0)),
        scratch_shapes=[pltpu.VMEM((2, LOCAL_ROWS, hw), jnp.uint32), pltpu.SemaphoreType.DMA((2,))],
    )
    return pl.pallas_call(
        _combine_kernel,
        grid_spec=grid_spec,
        out_shape=jax.ShapeDtypeStruct((n, d), f32),
        compiler_params=pltpu.CompilerParams(
            dimension_semantics=("arbitrary",), vmem_limit_bytes=VMEM_LIMIT),
        name="combine",
    )(nq, dst, dst, meta, gates, xm, y)


def _rel_bias_table(rel_bias):
    band = BAND_CHUNKS * CHUNK
    width = band + CHUNK
    dist = (np.arange(width) - CHUNK)[::-1]
    ext = rel_bias[:, np.clip(dist, -REL_CLIP, REL_CLIP) + REL_CLIP].astype(f32)
    heads = ext.shape[0]
    tiled = jnp.broadcast_to(ext[:, None, :], (heads, CHUNK, width)).reshape(heads, CHUNK * width)
    skewed = tiled[:, :CHUNK * (width - 1)].reshape(heads, CHUNK, width - 1)
    return skewed[:, :, CHUNK - 1:CHUNK - 1 + band]


def _round_up(x, m):
    return (x + m - 1) // m * m


def _routing_tables(cnt, n_tokens):
    experts = jnp.arange(N_EXPERTS, dtype=jnp.int32)
    c = cnt.reshape(-1, LANE)[:, :N_EXPERTS].astype(jnp.int32)
    n_tiles = c.shape[0]
    cp = _round_up(c, DMA_ROWS)
    lend = jnp.cumsum(cp, axis=1)
    lstart = lend - cp
    nq = (lend[:, -1] // DMA_ROWS).astype(jnp.int32)
    region = jnp.sum(cp, axis=0)
    padded = _round_up(region, MOE_BLOCK)
    pend = jnp.cumsum(padded)
    pstart = pend - padded
    base = pstart[None, :] + jnp.cumsum(cp, axis=0) - cp
    q0 = jnp.arange(LOCAL_PIECES, dtype=jnp.int32) * DMA_ROWS
    e_q = jnp.minimum(jnp.sum(lend[:, None, :] <= q0[None, :, None], axis=-1), N_EXPERTS - 1)
    shift = jnp.sum(jnp.where(e_q[:, :, None] == experts, (base - lstart)[:, None, :], 0), axis=-1)
    dst = jnp.where(q0[None, :] < lend[:, -1:], shift + q0[None, :], 0).astype(jnp.int32)

    n_blocks = -(-(n_tokens * TOP_K + n_tiles * N_EXPERTS * (DMA_ROWS - 1)) // MOE_BLOCK) + N_EXPERTS
    blk0 = jnp.arange(n_blocks, dtype=jnp.int32) * MOE_BLOCK
    n_used = (pend[-1] // MOE_BLOCK).astype(jnp.int32)
    be = jnp.minimum(jnp.sum(pend[None, :] <= blk0[:, None], axis=1), N_EXPERTS - 1).astype(jnp.int32)
    last = jnp.sum(jnp.where(jnp.arange(n_blocks) == n_used - 1, be, 0))
    be = jnp.where(jnp.arange(n_blocks) < n_used, be, last)
    onehot_be = be[:, None] == experts
    region_end = jnp.sum(jnp.where(onehot_be, pstart + region, 0), axis=1)
    live_rows = jnp.clip(region_end - blk0, 0, MOE_BLOCK)
    live = jnp.where(jnp.arange(n_blocks) < n_used, -(-live_rows // EXPERT_SUB_BLOCK), 0)
    pad_start = (pstart + region).astype(jnp.int32)
    pad_pieces = ((padded - region) // DMA_ROWS).astype(jnp.int32)
    return (nq, pad_start, pad_pieces, dst.reshape(n_tiles, 1, LOCAL_PIECES), be,
            live.astype(jnp.int32), n_used.reshape(1), n_blocks)


def _layer(x, norm1_g, w_in, gate_up, gate_bias, gla_norm_g, q_norm_g, k_norm_g, rel_bias, w_out,
           norm2_g, router_w, router_b, moe_w_in, moe_b_in, moe_w_out, moe_b_out):
    batch, seq, d = x.shape
    n = batch * seq
    x2 = x.reshape(n, d)

    pieces = jnp.split(w_in, np.cumsum(IN_SIZES)[:-1].tolist(), axis=-1)
    pieces[3] = jnp.pad(pieces[3], ((0, 0), (0, LANE - GLA_GATE_RANK)))
    widths = [p.shape[-1] for p in pieces]
    w_all = jnp.concatenate(pieces, axis=-1).astype(bf16)
    tile2 = lambda g: jnp.tile(g.reshape(1, -1), (1, LANE // ATT_DH))
    gq, gk, gv, glr, gog, qs, ak, av = _inproj(x2, norm1_g.reshape(1, d), w_all, widths,
                                               tile2(q_norm_g), tile2(k_norm_g))

    gup = jnp.pad(gate_up, ((0, LANE - GLA_GATE_RANK), (0, 0))).astype(bf16)
    o_gla = _gla(gq, gk, gv, glr, gog, gup, gate_bias.reshape(1, -1), gla_norm_g.reshape(1, -1),
                 batch, seq)
    o_att = _attention(qs, ak, av, _rel_bias_table(rel_bias), batch, seq)

    rw = jnp.pad(router_w, ((0, 0), (0, LANE - N_EXPERTS)))
    rw_hi = rw.astype(bf16)
    rw_lo = (rw - rw_hi.astype(f32)).astype(bf16)
    rb = jnp.pad(router_b, (0, LANE - N_EXPERTS)).reshape(1, LANE)
    xm, h2, meta, meta_t, gates, cnt = _mid(o_gla, o_att, x2, w_out.astype(bf16),
                                            norm2_g.reshape(1, d), rw_hi, rw_lo, rb)

    nq, pad_start, pad_pieces, dst, be, live, n_used, n_blocks = _routing_tables(cnt, n)
    xb = _dispatch(nq, pad_start, pad_pieces, n_used, dst, meta_t, h2, n_blocks * MOE_BLOCK)
    y = _experts(be, live, xb, moe_w_in, moe_b_in, moe_w_out, moe_b_out)
    out = _combine(nq, dst, meta, gates, xm, y)
    return out.reshape(batch, seq, d)


def kernel(x, norm1_g, w_in, gla_gate_up, gla_gate_bias, gla_norm_g, q_norm_g, k_norm_g, rel_bias, w_out, norm2_g, router_w, router_b, moe_w_in, moe_b_in, moe_w_out, moe_b_out):
    for l in range(norm1_g.shape[0]):
        x = _layer(x, norm1_g[l], w_in[l], gla_gate_up[l], gla_gate_bias[l], gla_norm_g[l],
                   q_norm_g[l], k_norm_g[l], rel_bias[l], w_out[l], norm2_g[l], router_w[l],
                   router_b[l], moe_w_in[l], moe_b_in[l], moe_w_out[l], moe_b_out[l])
    return x
```

```python
import functools

import numpy as np
import jax
import jax.numpy as jnp
from jax import lax
from jax.experimental import pallas as pl
from jax.experimental.pallas import tpu as pltpu

CHUNK = 64
EPS = 1e-6
GLA_HEADS = 4
GLA_DK = 64
GLA_DV = 128
GLA_GATE_RANK = 16
GLA_GATE_TAU = 16.0
ATT_HEADS = 8
ATT_DH = 64
N_BACK_CHUNKS = 8
BAND_CHUNKS = N_BACK_CHUNKS + 1
REL_CLIP = 256
MASK_VALUE = -1e30
N_EXPERTS = 32
TOP_K = 4
SWIGLU_ALPHA = 1.702
SWIGLU_LIMIT = 7.0
MOE_BLOCK = 1024
EXPERT_SUB_BLOCK = 256

LANE = 128
SUBLANE = 8
GLA_QK_W = GLA_HEADS * GLA_DK
GLA_V_W = GLA_HEADS * GLA_DV
ATT_W = ATT_HEADS * ATT_DH
IN_SIZES = (GLA_QK_W, GLA_QK_W, GLA_V_W, GLA_GATE_RANK, GLA_V_W, ATT_W, ATT_W, ATT_W)
SEQ_TILE = N_BACK_CHUNKS * CHUNK
ROW_TILE = 512
MID_TILE = 1024
MOE_TILE = 256
DMA_ROWS = SUBLANE
LOCAL_PIECES = 160
LOCAL_ROWS = LOCAL_PIECES * DMA_ROWS
MIN_PIECES = MOE_TILE * TOP_K // DMA_ROWS
ISSUE_UNROLL = 4
VMEM_LIMIT = 48 * 1024 * 1024

f32 = jnp.float32
bf16 = jnp.bfloat16


def _rms(x, g):
    return x * lax.rsqrt(jnp.mean(x * x, axis=-1, keepdims=True) + EPS) * g


def _dot(a, b):
    return jnp.dot(a, b, preferred_element_type=f32)


def _dot_nt(a, b):
    return lax.dot_general(a, b, (((1,), (1,)), ((), ())), preferred_element_type=f32)


def _dot_tn(a, b):
    return lax.dot_general(a, b, (((0,), (0,)), ((), ())), preferred_element_type=f32)


def _split_bf16(x):
    hi = x.astype(bf16)
    lo = (x - hi.astype(f32)).astype(bf16)
    return hi, lo


def _head_norm(x, g):
    lo = lax.broadcasted_iota(jnp.int32, (1, LANE), 1) < ATT_DH
    sq = x * x
    s0 = jnp.sum(jnp.where(lo, sq, 0.0), axis=-1, keepdims=True)
    s1 = jnp.sum(jnp.where(lo, 0.0, sq), axis=-1, keepdims=True)
    r = jnp.where(lo, lax.rsqrt(s0 * (1.0 / ATT_DH) + EPS), lax.rsqrt(s1 * (1.0 / ATT_DH) + EPS))
    return x * r * g


def _inproj_kernel(x_ref, g_ref, w_ref, qg_ref, kg_ref, gq, gk, gv, glr, gog, qs_ref, ak, av):
    t = x_ref.shape[0]
    n_pairs = ATT_W // LANE
    h = _rms(x_ref[...], g_ref[...]).astype(bf16)
    offsets = np.cumsum((0,) + tuple(r.shape[-1] for r in (gq, gk, gv, glr, gog)) + (ATT_W, ATT_W))
    lo = lax.broadcasted_iota(jnp.int32, (1, LANE), 1) < ATT_DH

    q_all = _dot(h, w_ref[:, offsets[5]:offsets[5] + ATT_W])
    k_all = _dot(h, w_ref[:, offsets[6]:offsets[6] + ATT_W])
    for p in range(n_pairs):
        pair = slice(p * LANE, (p + 1) * LANE)
        qn = _head_norm(q_all[:, pair], qg_ref[...]) * (ATT_DH ** -0.5)
        q_lo = jnp.where(lo, qn, 0.0).astype(bf16)
        q_hi = jnp.where(lo, 0.0, qn).astype(bf16)
        for c in range(t // CHUNK):
            rows = slice(c * CHUNK, (c + 1) * CHUNK)
            qs_ref[c * n_pairs + p, 0:CHUNK, :] = q_lo[rows]
            qs_ref[c * n_pairs + p, CHUNK:2 * CHUNK, :] = q_hi[rows]
        ak[:, pair] = _head_norm(k_all[:, pair], kg_ref[...]).astype(ak.dtype)
    for o_ref, off in zip((gq, gk, gv, glr, gog, av), tuple(offsets[:5]) + (offsets[7],)):
        o_ref[...] = _dot(h, w_ref[:, off:off + o_ref.shape[-1]]).astype(o_ref.dtype)


def _inproj(x2, g, w, widths, qg, kg):
    n, d = x2.shape
    n_pairs = ATT_W // LANE
    blocks_per_tile = ROW_TILE // CHUNK * n_pairs
    rows = lambda wd: pl.BlockSpec((ROW_TILE, wd), lambda i: (i, 0))
    full = lambda a: pl.BlockSpec(a.shape, lambda i: (0,) * a.ndim)
    plain = lambda wd: jax.ShapeDtypeStruct((n, wd), bf16)
    return pl.pallas_call(
        _inproj_kernel,
        grid=(n // ROW_TILE,),
        in_specs=[rows(d), full(g), full(w), full(qg), full(kg)],
        out_specs=[rows(wd) for wd in widths[:5]]
        + [pl.BlockSpec((blocks_per_tile, 2 * CHUNK, LANE), lambda i: (i, 0, 0)), rows(ATT_W), rows(ATT_W)],
        out_shape=[plain(wd) for wd in widths[:5]]
        + [jax.ShapeDtypeStruct((n // CHUNK * n_pairs, 2 * CHUNK, LANE), bf16), plain(ATT_W), plain(ATT_W)],
        compiler_params=pltpu.CompilerParams(
            dimension_semantics=("arbitrary",), vmem_limit_bytes=VMEM_LIMIT),
        name="inproj",
    )(x2, g, w, qg, kg)


def _gla_kernel(q_ref, k_ref, v_ref, lr_ref, og_ref, gup_ref, gb_ref, ng_ref, o_ref, st_ref,
                u_ref, sb_ref):
    t = q_ref.shape[0]

    @pl.when(pl.program_id(1) == 0)
    def _():
        st_ref[...] = jnp.zeros_like(st_ref)

    z = _dot(lr_ref[...], gup_ref[...]) + gb_ref[...]
    log_a = (jnp.minimum(z, 0.0) - jnp.log1p(jnp.exp(-jnp.abs(z)))) * (1.0 / GLA_GATE_TAU)
    row = lax.broadcasted_iota(jnp.int32, (t, t), 0)
    col = lax.broadcasted_iota(jnp.int32, (t, t), 1)
    tri = jnp.where((col <= row) & ((col // CHUNK) == (row // CHUNK)), 1.0, 0.0).astype(bf16)
    la_hi, la_lo = _split_bf16(log_a)
    cum_all = _dot(tri, la_hi) + _dot(tri, la_lo)
    lane = lax.broadcasted_iota(jnp.int32, (1, LANE), 1)
    half_mask = (lane < GLA_DK, lane >= GLA_DK)
    n_chunks = t // CHUNK
    chunk_rows = [slice(c * CHUNK, (c + 1) * CHUNK) for c in range(n_chunks)]
    pair_of = lambda h: slice((h // 2) * LANE, (h // 2 + 1) * LANE)
    head_of = lambda h: slice(h * GLA_DV, (h + 1) * GLA_DV)

    decs = []
    for c, rows in enumerate(chunk_rows):
        cum = cum_all[rows]
        tot = cum[CHUNK - 1:CHUNK]
        kdec = k_ref[rows, :].astype(f32) * jnp.exp(tot - cum)
        decs.append(jnp.exp(tot))
        for h in range(GLA_HEADS):
            kd = jnp.where(half_mask[h % 2], kdec[:, pair_of(h)], 0.0).astype(bf16)
            u_ref[c, h] = _dot_tn(v_ref[rows, head_of(h)], kd)

    for h in range(GLA_HEADS):
        st = st_ref[h]
        for c in range(n_chunks):
            st = st * decs[c][:, pair_of(h)] + u_ref[c, h]
            sb_ref[c, h] = st.astype(bf16)
        st_ref[h] = st

    for c, rows in enumerate(chunk_rows):
        for h in range(GLA_HEADS):
            o = _dot_nt(q_ref[rows, pair_of(h)], sb_ref[c, h]) * (GLA_DK ** -0.5)
            o = _rms(o, ng_ref[...])
            g = og_ref[rows, head_of(h)].astype(f32)
            o_ref[rows, head_of(h)] = (o * (g / (1.0 + jnp.exp(-g)))).astype(o_ref.dtype)


def _gla(gq, gk, gv, glr, gog, gup, gb, ng, batch, seq):
    nt = seq // SEQ_TILE
    tile = lambda w: pl.BlockSpec((SEQ_TILE, w), lambda b, i: (b * nt + i, 0))
    full = lambda a: pl.BlockSpec(a.shape, lambda b, i: (0,) * a.ndim)
    return pl.pallas_call(
        _gla_kernel,
        grid=(batch, nt),
        in_specs=[tile(GLA_QK_W), tile(GLA_QK_W), tile(GLA_V_W), tile(LANE), tile(GLA_V_W),
                  full(gup), full(gb), full(ng)],
        out_specs=tile(GLA_V_W),
        out_shape=jax.ShapeDtypeStruct((batch * seq, GLA_V_W), bf16),
        scratch_shapes=[pltpu.VMEM((GLA_HEADS, GLA_DV, LANE), f32),
                        pltpu.VMEM((SEQ_TILE // CHUNK, GLA_HEADS, GLA_DV, LANE), f32),
                        pltpu.VMEM((SEQ_TILE // CHUNK, GLA_HEADS, GLA_DV, LANE), bf16)],
        compiler_params=pltpu.CompilerParams(
            dimension_semantics=("arbitrary", "arbitrary"), vmem_limit_bytes=VMEM_LIMIT),
        name="gla",
    )(gq, gk, gv, glr, gog, gup, gb, ng)


def _att_kernel(qs_ref, k_ref, v_ref, bias_ref, o_ref, kb_ref, vb_ref):
    t = k_ref.shape[0]
    n_pairs = ATT_W // LANE
    band = BAND_CHUNKS * CHUNK
    first = pl.program_id(1) == 0

    @pl.when(first)
    def _():
        kb_ref[0:t, :] = jnp.zeros((t, ATT_W), bf16)
        vb_ref[0:t, :] = jnp.zeros((t, 2 * ATT_W), bf16)

    @pl.when(jnp.logical_not(first))
    def _():
        kb_ref[0:t, :] = kb_ref[t:2 * t, :]
        vb_ref[0:t, :] = vb_ref[t:2 * t, :]

    lane = lax.broadcasted_iota(jnp.int32, (1, LANE), 1)
    lo = lane < ATT_DH

    kb_ref[t:2 * t, :] = k_ref[...]
    for p in range(n_pairs):
        vb_ref[t:2 * t, 2 * p * LANE:(2 * p + 1) * LANE] = v_ref[:, p * LANE:(p + 1) * LANE]
        vb_ref[t:2 * t, (2 * p + 1) * LANE:(2 * p + 2) * LANE] = jnp.ones((t, LANE), bf16)

    colk = lax.broadcasted_iota(jnp.int32, (1, band), 1)

    def chunk_loop(masked):
        n_chunks = t // CHUNK

        def scores(c, p):
            k2 = kb_ref[c * CHUNK:c * CHUNK + band, p * LANE:(p + 1) * LANE]
            return _dot_nt(qs_ref[c * n_pairs + p], k2)

        def weighted_values(e, c, p):
            v2 = vb_ref[c * CHUNK:c * CHUNK + band, 2 * p * LANE:(2 * p + 2) * LANE]
            pvl = _dot(e, v2)
            pv = pvl[:, 0:LANE] / pvl[:, LANE:2 * LANE]
            o_ref[c * CHUNK:(c + 1) * CHUNK, p * LANE:(p + 1) * LANE] = jnp.where(
                lo, pv[0:CHUNK], pv[CHUNK:2 * CHUNK]).astype(o_ref.dtype)

        blocks = [(c, p) for c in range(n_chunks) for p in range(n_pairs)]
        s_next = scores(0, 0)
        e_prev = None
        for i, (c, p) in enumerate(blocks):
            s = s_next + bias_ref[p]
            if i + 1 < len(blocks):
                s_next = scores(*blocks[i + 1])
            if masked:
                s = jnp.where(colk >= t - c * CHUNK, s, MASK_VALUE)
            e = jnp.exp(s - jnp.max(s, axis=-1, keepdims=True)).astype(bf16)
            if i > 0:
                weighted_values(e_prev, *blocks[i - 1])
            e_prev = e
        weighted_values(e_prev, *blocks[-1])

    @pl.when(first)
    def _():
        chunk_loop(True)

    @pl.when(jnp.logical_not(first))
    def _():
        chunk_loop(False)


def _attention(qs, ak, av, bias, batch, seq):
    nt = seq // SEQ_TILE
    n_pairs = ATT_W // LANE
    tile = pl.BlockSpec((SEQ_TILE, ATT_W), lambda b, i: (b * nt + i, 0))
    q_tile = pl.BlockSpec((SEQ_TILE // CHUNK * n_pairs, 2 * CHUNK, LANE), lambda b, i: (b * nt + i, 0, 0))
    full = lambda a: pl.BlockSpec(a.shape, lambda b, i: (0,) * a.ndim)
    bias2 = bias.reshape(n_pairs, 2 * CHUNK, BAND_CHUNKS * CHUNK)
    return pl.pallas_call(
        _att_kernel,
        grid=(batch, nt),
        in_specs=[q_tile, tile, tile, full(bias2)],
        out_specs=tile,
        out_shape=jax.ShapeDtypeStruct((batch * seq, ATT_W), bf16),
        scratch_shapes=[pltpu.VMEM((2 * SEQ_TILE, ATT_W), bf16),
                        pltpu.VMEM((2 * SEQ_TILE, 2 * ATT_W), bf16)],
        compiler_params=pltpu.CompilerParams(
            dimension_semantics=("arbitrary", "arbitrary"), vmem_limit_bytes=VMEM_LIMIT),
        name="attention",
    )(qs, ak, av, bias2)


def _mid_kernel(og_ref, oa_ref, x_ref, wo_ref, g2_ref, rwh_ref, rwl_ref, rb_ref,
                xm_ref, h_ref, meta_ref, metat_ref, gate_ref, cnt_ref):
    t, d = x_ref.shape
    mt = MOE_TILE
    tiles = [slice(s * mt, (s + 1) * mt) for s in range(t // mt)]
    lane = lax.broadcasted_iota(jnp.int32, (mt, LANE), 1)
    lane_f = lane.astype(f32)
    row = lax.broadcasted_iota(jnp.int32, (mt, mt), 0)
    col = lax.broadcasted_iota(jnp.int32, (mt, mt), 1)
    earlier = jnp.where(col < row, 1.0, 0.0).astype(bf16)
    e_r = lax.broadcasted_iota(jnp.int32, (LANE, LANE), 0)
    e_c = lax.broadcasted_iota(jnp.int32, (LANE, LANE), 1)
    before = jnp.where(e_r < e_c, 1.0, 0.0).astype(bf16)

    def project(rows):
        xm = (x_ref[rows, :] + _dot(og_ref[rows, :], wo_ref[0:GLA_V_W, :])
              + _dot(oa_ref[rows, :], wo_ref[GLA_V_W:, :]))
        xm_ref[rows, :] = xm
        h_hi, h_lo = _split_bf16(_rms(xm, g2_ref[...]))
        h_ref[rows, :] = h_hi
        return h_hi, h_lo

    def route(h_hi, h_lo):
        return (_dot(h_hi, rwh_ref[...]) + _dot(h_lo, rwh_ref[...]) + _dot(h_hi, rwl_ref[...])
                + rb_ref[...])

    def top_k(logits):
        l = jnp.where(lane < N_EXPERTS, logits, -jnp.inf)
        vals, onehots = [], []
        for _ in range(TOP_K):
            m = jnp.max(l, axis=-1, keepdims=True)
            ik = jnp.min(jnp.where(l == m, lane_f, float(LANE)), axis=-1, keepdims=True)
            vals.append(m)
            onehots.append(lane_f == ik)
            l = jnp.where(onehots[-1], -jnp.inf, l)
        es = [jnp.exp(v - vals[0]) for v in vals]
        den = es[0] + es[1] + es[2] + es[3]
        sel = jnp.zeros((mt, LANE), f32)
        for oh in onehots:
            sel = sel + jnp.where(oh, 1.0, 0.0)
        return onehots, [e / den for e in es], sel

    def slots(s, rows, onehots, gate_vals, sel):
        prefix = _dot(earlier, sel.astype(bf16))
        c_row = jnp.sum(sel, axis=0, keepdims=True)
        cnt_ref[0, s:s + 1, :] = c_row
        padded = jnp.ceil(c_row * (1.0 / DMA_ROWS)) * DMA_ROWS
        start_row = _dot(jnp.broadcast_to(padded, (SUBLANE, LANE)).astype(bf16), before)[0:1]
        slot_base = prefix + start_row
        meta = jnp.zeros((mt, LANE), f32)
        gates = jnp.zeros((mt, LANE), f32)
        for k in range(TOP_K):
            slot_k = jnp.sum(jnp.where(onehots[k], slot_base, 0.0), axis=-1, keepdims=True)
            meta = jnp.where(lane == k, slot_k, meta)
            gates = jnp.where(lane == k, gate_vals[k], gates)
        meta_ref[rows, :] = meta.astype(jnp.int32)
        metat_ref[:, rows] = meta.T[0:SUBLANE, :].astype(jnp.int32)
        gate_ref[rows, :] = gates

    hs = [project(rows) for rows in tiles]
    logits = [route(*h) for h in hs]
    routed = [top_k(lg) for lg in logits]
    for s, rows in enumerate(tiles):
        slots(s, rows, *routed[s])


def _mid(o_gla, o_att, x2, wo, g2, rwh, rwl, rb):
    n, d = x2.shape
    sub = MID_TILE // MOE_TILE
    tile = lambda w: pl.BlockSpec((MID_TILE, w), lambda i: (i, 0))
    full = lambda a: pl.BlockSpec(a.shape, lambda i: (0,) * a.ndim)
    return pl.pallas_call(
        _mid_kernel,
        grid=(n // MID_TILE,),
        in_specs=[tile(GLA_V_W), tile(ATT_W), tile(d), full(wo), full(g2), full(rwh), full(rwl),
                  full(rb)],
        out_specs=[tile(d), tile(d), tile(LANE),
                   pl.BlockSpec((SUBLANE, MID_TILE), lambda i: (0, i)), tile(LANE),
                   pl.BlockSpec((1, sub, LANE), lambda i: (i, 0, 0))],
        out_shape=[jax.ShapeDtypeStruct((n, d), f32),
                   jax.ShapeDtypeStruct((n, d), bf16),
                   jax.ShapeDtypeStruct((n, LANE), jnp.int32),
                   jax.ShapeDtypeStruct((SUBLANE, n), jnp.int32),
                   jax.ShapeDtypeStruct((n, LANE), f32),
                   jax.ShapeDtypeStruct((n // MID_TILE, sub, LANE), f32)],
        compiler_params=pltpu.CompilerParams(
            dimension_semantics=("arbitrary",), vmem_limit_bytes=VMEM_LIMIT),
        name="mid",
    )(o_gla, o_att, x2, wo, g2, rwh, rwl, rb)


def _slot_matrix(slots, values, slot_axis):
    n_tokens = slots[0].shape[1 - slot_axis]
    shape = (LOCAL_ROWS, n_tokens) if slot_axis == 0 else (n_tokens, LOCAL_ROWS)
    iota_shape = (LOCAL_ROWS, 1) if slot_axis == 0 else (1, LOCAL_ROWS)
    pos = lax.broadcasted_iota(jnp.int32, iota_shape, slot_axis)
    out = jnp.zeros(shape, f32)
    for slot_k, value_k in zip(slots, values):
        out = jnp.where(pos == slot_k, value_k, out)
    return out


def _pack_bf16_pairs(a, b):
    return (pltpu.bitcast(a, jnp.uint32) >> 16) | (pltpu.bitcast(b, jnp.uint32) & jnp.uint32(0xFFFF0000))


def _unpack_bf16_pairs(w):
    a = pltpu.bitcast(w << 16, f32).astype(bf16)
    b = pltpu.bitcast(w & jnp.uint32(0xFFFF0000), f32).astype(bf16)
    return a, b


def _piece_copy(src_ref, dst_ref, src_row, dst_row, sem):
    return pltpu.make_async_copy(src_ref.at[pl.ds(pl.multiple_of(src_row, DMA_ROWS), DMA_ROWS)],
                                 dst_ref.at[pl.ds(pl.multiple_of(dst_row, DMA_ROWS), DMA_ROWS)], sem)


def _start_pieces(n, piece):
    def group(g, c):
        for j in range(ISSUE_UNROLL):
            piece(g * ISSUE_UNROLL + j).start()
        return c

    n_groups = n // ISSUE_UNROLL
    lax.fori_loop(0, n_groups, group, 0)
    lax.fori_loop(n_groups * ISSUE_UNROLL, n, lambda q, c: (piece(q).start(), c)[1], 0)


def _wait_pieces(n, piece, bulk):
    bulk.wait()
    lax.fori_loop(MIN_PIECES, n, lambda q, c: (piece(q).wait(), c)[1], 0)


def _zero_fill_padding(pad_start_ref, pad_pieces_ref, n_used_ref, xb_ref, z_ref, sem):
    z_ref[...] = jnp.zeros_like(z_ref)
    n_blocks = xb_ref.shape[0] // MOE_BLOCK

    def tail_piece(e, q):
        return _piece_copy(z_ref, xb_ref, 0, pad_start_ref[e] + q * DMA_ROWS, sem)

    def block_copy(b):
        row = pl.multiple_of(b * MOE_BLOCK, MOE_BLOCK)
        return pltpu.make_async_copy(z_ref, xb_ref.at[pl.ds(row, MOE_BLOCK)], sem)

    def each_tail_piece(fn):
        def per_expert(e, c):
            lax.fori_loop(0, pad_pieces_ref[e], lambda q, cc: (fn(tail_piece(e, q)), cc)[1], 0)
            return c
        lax.fori_loop(0, N_EXPERTS, per_expert, 0)

    def each_block(fn):
        lax.fori_loop(n_used_ref[0], n_blocks, lambda b, c: (fn(block_copy(b)), c)[1], 0)

    each_tail_piece(lambda cp: cp.start())
    each_block(lambda cp: cp.start())
    each_tail_piece(lambda cp: cp.wait())
    each_block(lambda cp: cp.wait())


def _dispatch_kernel(nq_ref, pad_start_ref, pad_pieces_ref, n_used_ref, dst_ref, metat_ref,
                     h_ref, xb_ref, l_ref, z_ref, sem):
    i = pl.program_id(0)
    hw = l_ref.shape[1]
    slots = [metat_ref[k:k + 1, :] for k in range(TOP_K)]
    perm = _slot_matrix(slots, [1.0] * TOP_K, slot_axis=0).astype(bf16)
    packed = _pack_bf16_pairs(_dot(perm, h_ref[:, 0:hw]), _dot(perm, h_ref[:, hw:]))

    def piece(q):
        return _piece_copy(l_ref, xb_ref, q * DMA_ROWS, dst_ref[0, 0, q], sem.at[0])

    def drain(n):
        min_rows = MIN_PIECES * DMA_ROWS
        bulk = pltpu.make_async_copy(l_ref.at[0:min_rows], xb_ref.at[0:min_rows], sem.at[0])
        _wait_pieces(n, piece, bulk)

    @pl.when(i > 0)
    def _():
        drain(nq_ref[jnp.maximum(i - 1, 0)])

    l_ref[...] = packed
    _start_pieces(nq_ref[i], piece)

    @pl.when(i == pl.num_programs(0) - 1)
    def _():
        _zero_fill_padding(pad_start_ref, pad_pieces_ref, n_used_ref, xb_ref, z_ref, sem.at[1])
        drain(nq_ref[i])


def _dispatch(nq, pad_start, pad_pieces, n_used, dst, meta_t, h, n_rows):
    n, d = h.shape
    hw = d // 2
    t = MOE_TILE
    grid_spec = pltpu.PrefetchScalarGridSpec(
        num_scalar_prefetch=4,
        grid=(n // t,),
        in_specs=[pl.BlockSpec((1, 1, LOCAL_PIECES), lambda i, *_: (i, 0, 0), memory_space=pltpu.SMEM),
                  pl.BlockSpec((SUBLANE, t), lambda i, *_: (0, i)),
                  pl.BlockSpec((t, d), lambda i, *_: (i, 0))],
        out_specs=pl.BlockSpec(memory_space=pl.ANY),
        scratch_shapes=[pltpu.VMEM((LOCAL_ROWS, hw), jnp.uint32),
                        pltpu.VMEM((MOE_BLOCK, hw), jnp.uint32),
                        pltpu.SemaphoreType.DMA((2,))],
    )
    return pl.pallas_call(
        _dispatch_kernel,
        grid_spec=grid_spec,
        out_shape=jax.ShapeDtypeStruct((n_rows, hw), jnp.uint32),
        compiler_params=pltpu.CompilerParams(
            dimension_semantics=("arbitrary",), vmem_limit_bytes=VMEM_LIMIT),
        name="dispatch",
    )(nq, pad_start, pad_pieces, n_used, dst, meta_t, h)


def _expert_kernel(be_ref, ns_ref, x_ref, wi_ref, bi_ref, wo_ref, bo_ref, y_ref,
                   wib_ref, wob_ref):
    b = pl.program_id(0)
    blk, hw = x_ref.shape
    dff = wo_ref.shape[1]
    subs = [slice(s * EXPERT_SUB_BLOCK, (s + 1) * EXPERT_SUB_BLOCK)
            for s in range(blk // EXPERT_SUB_BLOCK)]
    n_live = ns_ref[b]

    @pl.when(jnp.logical_and(n_live > 0,
                             jnp.logical_or(b == 0, be_ref[b] != be_ref[jnp.maximum(b - 1, 0)])))
    def _():
        for j in range(dff // LANE):
            wib_ref[:, 2 * j * LANE:(2 * j + 1) * LANE] = wi_ref[0, :, j * LANE:(j + 1) * LANE].astype(bf16)
            wib_ref[:, (2 * j + 1) * LANE:(2 * j + 2) * LANE] = (
                wi_ref[0, :, dff + j * LANE:dff + (j + 1) * LANE].astype(bf16))
        wob_ref[...] = wo_ref[0].astype(bf16)

    def up(rows):
        xa, xb = _unpack_bf16_pairs(x_ref[rows, :])
        return _dot(xa, wib_ref[0:hw, :]) + _dot(xb, wib_ref[hw:, :]) + bi_ref[0]

    def down(rows, hc):
        acts = []
        for j in range(dff // LANE):
            glu = jnp.minimum(hc[:, 2 * j * LANE:(2 * j + 1) * LANE], SWIGLU_LIMIT)
            lin = jnp.clip(hc[:, (2 * j + 1) * LANE:(2 * j + 2) * LANE], -SWIGLU_LIMIT, SWIGLU_LIMIT)
            acts.append((glu * (1.0 / (1.0 + jnp.exp(-SWIGLU_ALPHA * glu))) * (lin + 1.0)).astype(bf16))
        act = jnp.concatenate(acts, axis=1)
        y = (_dot(act, wob_ref[...]) + bo_ref[0]).astype(bf16).astype(f32)
        y_ref[rows, :] = _pack_bf16_pairs(y[:, 0:hw], y[:, hw:])

    def run(n):
        if n > 0:
            hc_next = up(subs[0])
        for s in range(n):
            hc = hc_next
            if s + 1 < n:
                hc_next = up(subs[s + 1])
            down(subs[s], hc)
        for rows in subs[n:]:
            y_ref[rows, :] = jnp.zeros((EXPERT_SUB_BLOCK, hw), y_ref.dtype)

    for n in range(len(subs) + 1):
        pl.when(n_live == n)(functools.partial(run, n))


def _experts(block_expert, block_live, xb, w_in, b_in, w_out, b_out):
    n_rows, hw = xb.shape
    e, d, dff2 = w_in.shape
    dff = w_out.shape[1]
    n_blocks = n_rows // MOE_BLOCK
    b_in_interleaved = b_in.reshape(e, 2, dff // LANE, LANE).transpose(0, 2, 1, 3)
    grid_spec = pltpu.PrefetchScalarGridSpec(
        num_scalar_prefetch=2,
        grid=(n_blocks,),
        in_specs=[
            pl.BlockSpec((MOE_BLOCK, hw), lambda b, be, nu: (b, 0)),
            pl.BlockSpec((1, d, dff2), lambda b, be, nu: (be[b], 0, 0)),
            pl.BlockSpec((1, 1, dff2), lambda b, be, nu: (be[b], 0, 0)),
            pl.BlockSpec((1, dff, d), lambda b, be, nu: (be[b], 0, 0)),
            pl.BlockSpec((1, 1, d), lambda b, be, nu: (be[b], 0, 0)),
        ],
        out_specs=pl.BlockSpec((MOE_BLOCK, hw), lambda b, be, nu: (b, 0)),
        scratch_shapes=[pltpu.VMEM((d, dff2), bf16), pltpu.VMEM((dff, d), bf16)],
    )
    return pl.pallas_call(
        _expert_kernel,
        grid_spec=grid_spec,
        out_shape=jax.ShapeDtypeStruct((n_rows, hw), jnp.uint32),
        compiler_params=pltpu.CompilerParams(
            dimension_semantics=("arbitrary",), vmem_limit_bytes=VMEM_LIMIT),
        name="experts",
    )(block_expert, block_live, xb, w_in, b_in_interleaved.reshape(e, 1, dff2), w_out,
      b_out.reshape(e, 1, d))


def _combine_kernel(nq_ref, dst_ref, dstn_ref, meta_ref, gate_ref, xm_ref, y_ref, o_ref,
                    ly_ref, sem):
    i = pl.program_id(0)
    last = pl.num_programs(0) - 1
    hw = ly_ref.shape[2]
    slot = i % 2

    def fetch(tile, table_ref, buf):
        ly_ref[buf, MOE_TILE * TOP_K:, :] = jnp.zeros((LOCAL_ROWS - MOE_TILE * TOP_K, hw), jnp.uint32)

        _start_pieces(nq_ref[tile], lambda q: _piece_copy(
            y_ref, ly_ref.at[buf], table_ref[0, 0, q], q * DMA_ROWS, sem.at[buf]))

    @pl.when(i == 0)
    def _():
        fetch(i, dst_ref, slot)

    @pl.when(i < last)
    def _():
        fetch(jnp.minimum(i + 1, last), dstn_ref, 1 - slot)

    min_rows = MIN_PIECES * DMA_ROWS
    _wait_pieces(
        nq_ref[i],
        lambda q: _piece_copy(y_ref, ly_ref.at[slot], dst_ref[0, 0, q], q * DMA_ROWS, sem.at[slot]),
        pltpu.make_async_copy(y_ref.at[0:min_rows], ly_ref.at[slot, 0:min_rows], sem.at[slot]))
    gates = gate_ref[...]
    meta = meta_ref[...]
    g = _slot_matrix([meta[:, k:k + 1] for k in range(TOP_K)],
                     [gates[:, k:k + 1] for k in range(TOP_K)], slot_axis=1).astype(bf16)
    ya, yb = _unpack_bf16_pairs(ly_ref[slot])
    o_ref[:, 0:hw] = xm_ref[:, 0:hw] + _dot(g, ya)
    o_ref[:, hw:] = xm_ref[:, hw:] + _dot(g, yb)


def _combine(nq, dst, meta, gates, xm, y):
    n, d = xm.shape
    hw = d // 2
    t = MOE_TILE
    n_tiles = n // t
    grid_spec = pltpu.PrefetchScalarGridSpec(
        num_scalar_prefetch=1,
        grid=(n_tiles,),
        in_specs=[pl.BlockSpec((1, 1, LOCAL_PIECES), lambda i, nq: (i, 0, 0), memory_space=pltpu.SMEM),
                  pl.BlockSpec((1, 1, LOCAL_PIECES), lambda i, nq: (jnp.minimum(i + 1, n_tiles - 1), 0, 0),
                               memory_space=pltpu.SMEM),
                  pl.BlockSpec((t, LANE), lambda i, nq: (i, 0)),
                  pl.BlockSpec((t, LANE), lambda i, nq: (i, 0)),
                  pl.BlockSpec((t, d), lambda i, nq: (i, 0)),
                  pl.BlockSpec(memory_space=pl.ANY)],
        out_specs=pl.BlockSpec((t, d), lambda i, nq: (i, 0)),
        scratch_shapes=[pltpu.VMEM((2, LOCAL_ROWS, hw), jnp.uint32), pltpu.SemaphoreType.DMA((2,))],
    )
    return pl.pallas_call(
        _combine_kernel,
        grid_spec=grid_spec,
        out_shape=jax.ShapeDtypeStruct((n, d), f32),
        compiler_params=pltpu.CompilerParams(
            dimension_semantics=("arbitrary",), vmem_limit_bytes=VMEM_LIMIT),
        name="combine",
    )(nq, dst, dst, meta, gates, xm, y)


def _rel_bias_table(rel_bias):
    band = BAND_CHUNKS * CHUNK
    width = band + CHUNK
    dist = (np.arange(width) - CHUNK)[::-1]
    ext = rel_bias[:, np.clip(dist, -REL_CLIP, REL_CLIP) + REL_CLIP].astype(f32)
    heads = ext.shape[0]
    tiled = jnp.broadcast_to(ext[:, None, :], (heads, CHUNK, width)).reshape(heads, CHUNK * width)
    skewed = tiled[:, :CHUNK * (width - 1)].reshape(heads, CHUNK, width - 1)
    return skewed[:, :, CHUNK - 1:CHUNK - 1 + band]


def _round_up(x, m):
    return (x + m - 1) // m * m


def _routing_tables(cnt, n_tokens):
    experts = jnp.arange(N_EXPERTS, dtype=jnp.int32)
    c = cnt.reshape(-1, LANE)[:, :N_EXPERTS].astype(jnp.int32)
    n_tiles = c.shape[0]
    cp = _round_up(c, DMA_ROWS)
    lend = jnp.cumsum(cp, axis=1)
    lstart = lend - cp
    nq = (lend[:, -1] // DMA_ROWS).astype(jnp.int32)
    region = jnp.sum(cp, axis=0)
    padded = _round_up(region, MOE_BLOCK)
    pend = jnp.cumsum(padded)
    pstart = pend - padded
    base = pstart[None, :] + jnp.cumsum(cp, axis=0) - cp
    q0 = jnp.arange(LOCAL_PIECES, dtype=jnp.int32) * DMA_ROWS
    e_q = jnp.minimum(jnp.sum(lend[:, None, :] <= q0[None, :, None], axis=-1), N_EXPERTS - 1)
    shift = jnp.sum(jnp.where(e_q[:, :, None] == experts, (base - lstart)[:, None, :], 0), axis=-1)
    dst = jnp.where(q0[None, :] < lend[:, -1:], shift + q0[None, :], 0).astype(jnp.int32)

    n_blocks = -(-(n_tokens * TOP_K + n_tiles * N_EXPERTS * (DMA_ROWS - 1)) // MOE_BLOCK) + N_EXPERTS
    blk0 = jnp.arange(n_blocks, dtype=jnp.int32) * MOE_BLOCK
    n_used = (pend[-1] // MOE_BLOCK).astype(jnp.int32)
    be = jnp.minimum(jnp.sum(pend[None, :] <= blk0[:, None], axis=1), N_EXPERTS - 1).astype(jnp.int32)
    last = jnp.sum(jnp.where(jnp.arange(n_blocks) == n_used - 1, be, 0))
    be = jnp.where(jnp.arange(n_blocks) < n_used, be, last)
    onehot_be = be[:, None] == experts
    region_end = jnp.sum(jnp.where(onehot_be, pstart + region, 0), axis=1)
    live_rows = jnp.clip(region_end - blk0, 0, MOE_BLOCK)
    live = jnp.where(jnp.arange(n_blocks) < n_used, -(-live_rows // EXPERT_SUB_BLOCK), 0)
    pad_start = (pstart + region).astype(jnp.int32)
    pad_pieces = ((padded - region) // DMA_ROWS).astype(jnp.int32)
    return (nq, pad_start, pad_pieces, dst.reshape(n_tiles, 1, LOCAL_PIECES), be,
            live.astype(jnp.int32), n_used.reshape(1), n_blocks)


def _layer(x, norm1_g, w_in, gate_up, gate_bias, gla_norm_g, q_norm_g, k_norm_g, rel_bias, w_out,
           norm2_g, router_w, router_b, moe_w_in, moe_b_in, moe_w_out, moe_b_out):
    batch, seq, d = x.shape
    n = batch * seq
    x2 = x.reshape(n, d)

    pieces = jnp.split(w_in, np.cumsum(IN_SIZES)[:-1].tolist(), axis=-1)
    pieces[3] = jnp.pad(pieces[3], ((0, 0), (0, LANE - GLA_GATE_RANK)))
    widths = [p.shape[-1] for p in pieces]
    w_all = jnp.concatenate(pieces, axis=-1).astype(bf16)
    tile2 = lambda g: jnp.tile(g.reshape(1, -1), (1, LANE // ATT_DH))
    gq, gk, gv, glr, gog, qs, ak, av = _inproj(x2, norm1_g.reshape(1, d), w_all, widths,
                                               tile2(q_norm_g), tile2(k_norm_g))

    gup = jnp.pad(gate_up, ((0, LANE - GLA_GATE_RANK), (0, 0))).astype(bf16)
    o_gla = _gla(gq, gk, gv, glr, gog, gup, gate_bias.reshape(1, -1), gla_norm_g.reshape(1, -1),
                 batch, seq)
    o_att = _attention(qs, ak, av, _rel_bias_table(rel_bias), batch, seq)

    rw = jnp.pad(router_w, ((0, 0), (0, LANE - N_EXPERTS)))
    rw_hi = rw.astype(bf16)
    rw_lo = (rw - rw_hi.astype(f32)).astype(bf16)
    rb = jnp.pad(router_b, (0, LANE - N_EXPERTS)).reshape(1, LANE)
    xm, h2, meta, meta_t, gates, cnt = _mid(o_gla, o_att, x2, w_out.astype(bf16),
                                            norm2_g.reshape(1, d), rw_hi, rw_lo, rb)

    nq, pad_start, pad_pieces, dst, be, live, n_used, n_blocks = _routing_tables(cnt, n)
    xb = _dispatch(nq, pad_start, pad_pieces, n_used, dst, meta_t, h2, n_blocks * MOE_BLOCK)
    y = _experts(be, live, xb, moe_w_in, moe_b_in, moe_w_out, moe_b_out)
    out = _combine(nq, dst, meta, gates, xm, y)
    return out.reshape(batch, seq, d)


def kernel(x, norm1_g, w_in, gla_gate_up, gla_gate_bias, gla_norm_g, q_norm_g, k_norm_g, rel_bias, w_out, norm2_g, router_w, router_b, moe_w_in, moe_b_in, moe_w_out, moe_b_out):
    for l in range(norm1_g.shape[0]):
        x = _layer(x, norm1_g[l], w_in[l], gla_gate_up[l], gla_gate_bias[l], gla_norm_g[l],
                   q_norm_g[l], k_norm_g[l], rel_bias[l], w_out[l], norm2_g[l], router_w[l],
                   router_b[l], moe_w_in[l], moe_b_in[l], moe_w_out[l], moe_b_out[l])
    return x
```

```python
import functools

import numpy as np
import jax
import jax.numpy as jnp
from jax import lax
from jax.experimental import pallas as pl
from jax.experimental.pallas import tpu as pltpu

CHUNK = 64
EPS = 1e-6
GLA_HEADS = 4
GLA_DK = 64
GLA_DV = 128
GLA_GATE_RANK = 16
GLA_GATE_TAU = 16.0
ATT_HEADS = 8
ATT_DH = 64
N_BACK_CHUNKS = 8
BAND_CHUNKS = N_BACK_CHUNKS + 1
REL_CLIP = 256
MASK_VALUE = -1e30
N_EXPERTS = 32
TOP_K = 4
SWIGLU_ALPHA = 1.702
SWIGLU_LIMIT = 7.0
MOE_BLOCK = 1024
EXPERT_SUB_BLOCK = 256

LANE = 128
SUBLANE = 8
GLA_QK_W = GLA_HEADS * GLA_DK
GLA_V_W = GLA_HEADS * GLA_DV
ATT_W = ATT_HEADS * ATT_DH
IN_SIZES = (GLA_QK_W, GLA_QK_W, GLA_V_W, GLA_GATE_RANK, GLA_V_W, ATT_W, ATT_W, ATT_W)
SEQ_TILE = N_BACK_CHUNKS * CHUNK
ROW_TILE = 512
MID_TILE = 1024
MOE_TILE = 256
DMA_ROWS = SUBLANE
LOCAL_PIECES = 160
LOCAL_ROWS = LOCAL_PIECES * DMA_ROWS
MIN_PIECES = MOE_TILE * TOP_K // DMA_ROWS
VMEM_LIMIT = 48 * 1024 * 1024

f32 = jnp.float32
bf16 = jnp.bfloat16


def _rms(x, g):
    return x * lax.rsqrt(jnp.mean(x * x, axis=-1, keepdims=True) + EPS) * g


def _dot(a, b):
    return jnp.dot(a, b, preferred_element_type=f32)


def _dot_nt(a, b):
    return lax.dot_general(a, b, (((1,), (1,)), ((), ())), preferred_element_type=f32)


def _dot_tn(a, b):
    return lax.dot_general(a, b, (((0,), (0,)), ((), ())), preferred_element_type=f32)


def _split_bf16(x):
    hi = x.astype(bf16)
    lo = (x - hi.astype(f32)).astype(bf16)
    return hi, lo


def _head_norm(x, g):
    lo = lax.broadcasted_iota(jnp.int32, (1, LANE), 1) < ATT_DH
    sq = x * x
    s0 = jnp.sum(jnp.where(lo, sq, 0.0), axis=-1, keepdims=True)
    s1 = jnp.sum(jnp.where(lo, 0.0, sq), axis=-1, keepdims=True)
    r = jnp.where(lo, lax.rsqrt(s0 * (1.0 / ATT_DH) + EPS), lax.rsqrt(s1 * (1.0 / ATT_DH) + EPS))
    return x * r * g


def _inproj_kernel(x_ref, g_ref, w_ref, qg_ref, kg_ref, gq, gk, gv, glr, gog, qs_ref, ak, av):
    t = x_ref.shape[0]
    n_pairs = ATT_W // LANE
    h = _rms(x_ref[...], g_ref[...]).astype(bf16)
    offsets = np.cumsum((0,) + tuple(r.shape[-1] for r in (gq, gk, gv, glr, gog)) + (ATT_W, ATT_W))
    lo = lax.broadcasted_iota(jnp.int32, (1, LANE), 1) < ATT_DH

    q_all = _dot(h, w_ref[:, offsets[5]:offsets[5] + ATT_W])
    k_all = _dot(h, w_ref[:, offsets[6]:offsets[6] + ATT_W])
    for p in range(n_pairs):
        pair = slice(p * LANE, (p + 1) * LANE)
        qn = _head_norm(q_all[:, pair], qg_ref[...]) * (ATT_DH ** -0.5)
        q_lo = jnp.where(lo, qn, 0.0).astype(bf16)
        q_hi = jnp.where(lo, 0.0, qn).astype(bf16)
        for c in range(t // CHUNK):
            rows = slice(c * CHUNK, (c + 1) * CHUNK)
            qs_ref[c * n_pairs + p, 0:CHUNK, :] = q_lo[rows]
            qs_ref[c * n_pairs + p, CHUNK:2 * CHUNK, :] = q_hi[rows]
        ak[:, pair] = _head_norm(k_all[:, pair], kg_ref[...]).astype(ak.dtype)
    for o_ref, off in zip((gq, gk, gv, glr, gog, av), tuple(offsets[:5]) + (offsets[7],)):
        o_ref[...] = _dot(h, w_ref[:, off:off + o_ref.shape[-1]]).astype(o_ref.dtype)


def _inproj(x2, g, w, widths, qg, kg):
    n, d = x2.shape
    n_pairs = ATT_W // LANE
    blocks_per_tile = ROW_TILE // CHUNK * n_pairs
    rows = lambda wd: pl.BlockSpec((ROW_TILE, wd), lambda i: (i, 0))
    full = lambda a: pl.BlockSpec(a.shape, lambda i: (0,) * a.ndim)
    plain = lambda wd: jax.ShapeDtypeStruct((n, wd), bf16)
    return pl.pallas_call(
        _inproj_kernel,
        grid=(n // ROW_TILE,),
        in_specs=[rows(d), full(g), full(w), full(qg), full(kg)],
        out_specs=[rows(wd) for wd in widths[:5]]
        + [pl.BlockSpec((blocks_per_tile, 2 * CHUNK, LANE), lambda i: (i, 0, 0)), rows(ATT_W), rows(ATT_W)],
        out_shape=[plain(wd) for wd in widths[:5]]
        + [jax.ShapeDtypeStruct((n // CHUNK * n_pairs, 2 * CHUNK, LANE), bf16), plain(ATT_W), plain(ATT_W)],
        compiler_params=pltpu.CompilerParams(
            dimension_semantics=("arbitrary",), vmem_limit_bytes=VMEM_LIMIT),
        name="inproj",
    )(x2, g, w, qg, kg)


def _gla_kernel(q_ref, k_ref, v_ref, lr_ref, og_ref, gup_ref, gb_ref, ng_ref, o_ref, st_ref,
                u_ref, sb_ref):
    t = q_ref.shape[0]

    @pl.when(pl.program_id(1) == 0)
    def _():
        st_ref[...] = jnp.zeros_like(st_ref)

    z = _dot(lr_ref[...], gup_ref[...]) + gb_ref[...]
    log_a = (jnp.minimum(z, 0.0) - jnp.log1p(jnp.exp(-jnp.abs(z)))) * (1.0 / GLA_GATE_TAU)
    row = lax.broadcasted_iota(jnp.int32, (t, t), 0)
    col = lax.broadcasted_iota(jnp.int32, (t, t), 1)
    tri = jnp.where((col <= row) & ((col // CHUNK) == (row // CHUNK)), 1.0, 0.0).astype(bf16)
    la_hi, la_lo = _split_bf16(log_a)
    cum_all = _dot(tri, la_hi) + _dot(tri, la_lo)
    lane = lax.broadcasted_iota(jnp.int32, (1, LANE), 1)
    half_mask = (lane < GLA_DK, lane >= GLA_DK)
    n_chunks = t // CHUNK
    chunk_rows = [slice(c * CHUNK, (c + 1) * CHUNK) for c in range(n_chunks)]
    pair_of = lambda h: slice((h // 2) * LANE, (h // 2 + 1) * LANE)
    head_of = lambda h: slice(h * GLA_DV, (h + 1) * GLA_DV)

    decs = []
    for c, rows in enumerate(chunk_rows):
        cum = cum_all[rows]
        tot = cum[CHUNK - 1:CHUNK]
        kdec = k_ref[rows, :].astype(f32) * jnp.exp(tot - cum)
        decs.append(jnp.exp(tot))
        for h in range(GLA_HEADS):
            kd = jnp.where(half_mask[h % 2], kdec[:, pair_of(h)], 0.0).astype(bf16)
            u_ref[c, h] = _dot_tn(v_ref[rows, head_of(h)], kd)

    for h in range(GLA_HEADS):
        st = st_ref[h]
        for c in range(n_chunks):
            st = st * decs[c][:, pair_of(h)] + u_ref[c, h]
            sb_ref[c, h] = st.astype(bf16)
        st_ref[h] = st

    for c, rows in enumerate(chunk_rows):
        for h in range(GLA_HEADS):
            o = _dot_nt(q_ref[rows, pair_of(h)], sb_ref[c, h]) * (GLA_DK ** -0.5)
            o = _rms(o, ng_ref[...])
            g = og_ref[rows, head_of(h)].astype(f32)
            o_ref[rows, head_of(h)] = (o * (g / (1.0 + jnp.exp(-g)))).astype(o_ref.dtype)


def _gla(gq, gk, gv, glr, gog, gup, gb, ng, batch, seq):
    nt = seq // SEQ_TILE
    tile = lambda w: pl.BlockSpec((SEQ_TILE, w), lambda b, i: (b * nt + i, 0))
    full = lambda a: pl.BlockSpec(a.shape, lambda b, i: (0,) * a.ndim)
    return pl.pallas_call(
        _gla_kernel,
        grid=(batch, nt),
        in_specs=[tile(GLA_QK_W), tile(GLA_QK_W), tile(GLA_V_W), tile(LANE), tile(GLA_V_W),
                  full(gup), full(gb), full(ng)],
        out_specs=tile(GLA_V_W),
        out_shape=jax.ShapeDtypeStruct((batch * seq, GLA_V_W), bf16),
        scratch_shapes=[pltpu.VMEM((GLA_HEADS, GLA_DV, LANE), f32),
                        pltpu.VMEM((SEQ_TILE // CHUNK, GLA_HEADS, GLA_DV, LANE), f32),
                        pltpu.VMEM((SEQ_TILE // CHUNK, GLA_HEADS, GLA_DV, LANE), bf16)],
        compiler_params=pltpu.CompilerParams(
            dimension_semantics=("arbitrary", "arbitrary"), vmem_limit_bytes=VMEM_LIMIT),
        name="gla",
    )(gq, gk, gv, glr, gog, gup, gb, ng)


def _att_kernel(qs_ref, k_ref, v_ref, bias_ref, o_ref, kb_ref, vb_ref):
    t = k_ref.shape[0]
    n_pairs = ATT_W // LANE
    band = BAND_CHUNKS * CHUNK
    first = pl.program_id(1) == 0

    @pl.when(first)
    def _():
        kb_ref[0:t, :] = jnp.zeros((t, ATT_W), bf16)
        vb_ref[0:t, :] = jnp.zeros((t, 2 * ATT_W), bf16)

    @pl.when(jnp.logical_not(first))
    def _():
        kb_ref[0:t, :] = kb_ref[t:2 * t, :]
        vb_ref[0:t, :] = vb_ref[t:2 * t, :]

    lane = lax.broadcasted_iota(jnp.int32, (1, LANE), 1)
    lo = lane < ATT_DH

    kb_ref[t:2 * t, :] = k_ref[...]
    for p in range(n_pairs):
        vb_ref[t:2 * t, 2 * p * LANE:(2 * p + 1) * LANE] = v_ref[:, p * LANE:(p + 1) * LANE]
        vb_ref[t:2 * t, (2 * p + 1) * LANE:(2 * p + 2) * LANE] = jnp.ones((t, LANE), bf16)

    colk = lax.broadcasted_iota(jnp.int32, (1, band), 1)

    def chunk_loop(masked):
        n_chunks = t // CHUNK

        def scores(c, p):
            k2 = kb_ref[c * CHUNK:c * CHUNK + band, p * LANE:(p + 1) * LANE]
            return _dot_nt(qs_ref[c * n_pairs + p], k2)

        def weighted_values(e, c, p):
            v2 = vb_ref[c * CHUNK:c * CHUNK + band, 2 * p * LANE:(2 * p + 2) * LANE]
            pvl = _dot(e, v2)
            pv = pvl[:, 0:LANE] / pvl[:, LANE:2 * LANE]
            o_ref[c * CHUNK:(c + 1) * CHUNK, p * LANE:(p + 1) * LANE] = jnp.where(
                lo, pv[0:CHUNK], pv[CHUNK:2 * CHUNK]).astype(o_ref.dtype)

        blocks = [(c, p) for c in range(n_chunks) for p in range(n_pairs)]
        s_next = scores(0, 0)
        e_prev = None
        for i, (c, p) in enumerate(blocks):
            s = s_next + bias_ref[p]
            if i + 1 < len(blocks):
                s_next = scores(*blocks[i + 1])
            if masked:
                s = jnp.where(colk >= t - c * CHUNK, s, MASK_VALUE)
            e = jnp.exp(s - jnp.max(s, axis=-1, keepdims=True)).astype(bf16)
            if i > 0:
                weighted_values(e_prev, *blocks[i - 1])
            e_prev = e
        weighted_values(e_prev, *blocks[-1])

    @pl.when(first)
    def _():
        chunk_loop(True)

    @pl.when(jnp.logical_not(first))
    def _():
        chunk_loop(False)


def _attention(qs, ak, av, bias, batch, seq):
    nt = seq // SEQ_TILE
    n_pairs = ATT_W // LANE
    tile = pl.BlockSpec((SEQ_TILE, ATT_W), lambda b, i: (b * nt + i, 0))
    q_tile = pl.BlockSpec((SEQ_TILE // CHUNK * n_pairs, 2 * CHUNK, LANE), lambda b, i: (b * nt + i, 0, 0))
    full = lambda a: pl.BlockSpec(a.shape, lambda b, i: (0,) * a.ndim)
    bias2 = bias.reshape(n_pairs, 2 * CHUNK, BAND_CHUNKS * CHUNK)
    return pl.pallas_call(
        _att_kernel,
        grid=(batch, nt),
        in_specs=[q_tile, tile, tile, full(bias2)],
        out_specs=tile,
        out_shape=jax.ShapeDtypeStruct((batch * seq, ATT_W), bf16),
        scratch_shapes=[pltpu.VMEM((2 * SEQ_TILE, ATT_W), bf16),
                        pltpu.VMEM((2 * SEQ_TILE, 2 * ATT_W), bf16)],
        compiler_params=pltpu.CompilerParams(
            dimension_semantics=("arbitrary", "arbitrary"), vmem_limit_bytes=VMEM_LIMIT),
        name="attention",
    )(qs, ak, av, bias2)


def _mid_kernel(og_ref, oa_ref, x_ref, wo_ref, g2_ref, rwh_ref, rwl_ref, rb_ref,
                xm_ref, h_ref, meta_ref, metat_ref, gate_ref, cnt_ref):
    t, d = x_ref.shape
    mt = MOE_TILE
    tiles = [slice(s * mt, (s + 1) * mt) for s in range(t // mt)]
    lane = lax.broadcasted_iota(jnp.int32, (mt, LANE), 1)
    lane_f = lane.astype(f32)
    row = lax.broadcasted_iota(jnp.int32, (mt, mt), 0)
    col = lax.broadcasted_iota(jnp.int32, (mt, mt), 1)
    earlier = jnp.where(col < row, 1.0, 0.0).astype(bf16)
    e_r = lax.broadcasted_iota(jnp.int32, (LANE, LANE), 0)
    e_c = lax.broadcasted_iota(jnp.int32, (LANE, LANE), 1)
    before = jnp.where(e_r < e_c, 1.0, 0.0).astype(bf16)

    def project(rows):
        xm = (x_ref[rows, :] + _dot(og_ref[rows, :], wo_ref[0:GLA_V_W, :])
              + _dot(oa_ref[rows, :], wo_ref[GLA_V_W:, :]))
        xm_ref[rows, :] = xm
        h_hi, h_lo = _split_bf16(_rms(xm, g2_ref[...]))
        h_ref[rows, :] = h_hi
        return h_hi, h_lo

    def route(h_hi, h_lo):
        return (_dot(h_hi, rwh_ref[...]) + _dot(h_lo, rwh_ref[...]) + _dot(h_hi, rwl_ref[...])
                + rb_ref[...])

    def top_k(logits):
        l = jnp.where(lane < N_EXPERTS, logits, -jnp.inf)
        vals, onehots = [], []
        for _ in range(TOP_K):
            m = jnp.max(l, axis=-1, keepdims=True)
            ik = jnp.min(jnp.where(l == m, lane_f, float(LANE)), axis=-1, keepdims=True)
            vals.append(m)
            onehots.append(lane_f == ik)
            l = jnp.where(onehots[-1], -jnp.inf, l)
        es = [jnp.exp(v - vals[0]) for v in vals]
        den = es[0] + es[1] + es[2] + es[3]
        sel = jnp.zeros((mt, LANE), f32)
        for oh in onehots:
            sel = sel + jnp.where(oh, 1.0, 0.0)
        return onehots, [e / den for e in es], sel

    def slots(s, rows, onehots, gate_vals, sel):
        prefix = _dot(earlier, sel.astype(bf16))
        c_row = jnp.sum(sel, axis=0, keepdims=True)
        cnt_ref[0, s:s + 1, :] = c_row
        padded = jnp.ceil(c_row * (1.0 / DMA_ROWS)) * DMA_ROWS
        start_row = _dot(jnp.broadcast_to(padded, (SUBLANE, LANE)).astype(bf16), before)[0:1]
        slot_base = prefix + start_row
        meta = jnp.zeros((mt, LANE), f32)
        gates = jnp.zeros((mt, LANE), f32)
        for k in range(TOP_K):
            slot_k = jnp.sum(jnp.where(onehots[k], slot_base, 0.0), axis=-1, keepdims=True)
            meta = jnp.where(lane == k, slot_k, meta)
            gates = jnp.where(lane == k, gate_vals[k], gates)
        meta_ref[rows, :] = meta.astype(jnp.int32)
        metat_ref[:, rows] = meta.T[0:SUBLANE, :].astype(jnp.int32)
        gate_ref[rows, :] = gates

    hs = [project(rows) for rows in tiles]
    logits = [route(*h) for h in hs]
    routed = [top_k(lg) for lg in logits]
    for s, rows in enumerate(tiles):
        slots(s, rows, *routed[s])


def _mid(o_gla, o_att, x2, wo, g2, rwh, rwl, rb):
    n, d = x2.shape
    sub = MID_TILE // MOE_TILE
    tile = lambda w: pl.BlockSpec((MID_TILE, w), lambda i: (i, 0))
    full = lambda a: pl.BlockSpec(a.shape, lambda i: (0,) * a.ndim)
    return pl.pallas_call(
        _mid_kernel,
        grid=(n // MID_TILE,),
        in_specs=[tile(GLA_V_W), tile(ATT_W), tile(d), full(wo), full(g2), full(rwh), full(rwl),
                  full(rb)],
        out_specs=[tile(d), tile(d), tile(LANE),
                   pl.BlockSpec((SUBLANE, MID_TILE), lambda i: (0, i)), tile(LANE),
                   pl.BlockSpec((1, sub, LANE), lambda i: (i, 0, 0))],
        out_shape=[jax.ShapeDtypeStruct((n, d), f32),
                   jax.ShapeDtypeStruct((n, d), bf16),
                   jax.ShapeDtypeStruct((n, LANE), jnp.int32),
                   jax.ShapeDtypeStruct((SUBLANE, n), jnp.int32),
                   jax.ShapeDtypeStruct((n, LANE), f32),
                   jax.ShapeDtypeStruct((n // MID_TILE, sub, LANE), f32)],
        compiler_params=pltpu.CompilerParams(
            dimension_semantics=("arbitrary",), vmem_limit_bytes=VMEM_LIMIT),
        name="mid",
    )(o_gla, o_att, x2, wo, g2, rwh, rwl, rb)


def _slot_matrix(slots, values, slot_axis):
    n_tokens = slots[0].shape[1 - slot_axis]
    shape = (LOCAL_ROWS, n_tokens) if slot_axis == 0 else (n_tokens, LOCAL_ROWS)
    iota_shape = (LOCAL_ROWS, 1) if slot_axis == 0 else (1, LOCAL_ROWS)
    pos = lax.broadcasted_iota(jnp.int32, iota_shape, slot_axis)
    out = jnp.zeros(shape, f32)
    for slot_k, value_k in zip(slots, values):
        out = jnp.where(pos == slot_k, value_k, out)
    return out


def _pack_bf16_pairs(a, b):
    return (pltpu.bitcast(a, jnp.uint32) >> 16) | (pltpu.bitcast(b, jnp.uint32) & jnp.uint32(0xFFFF0000))


def _unpack_bf16_pairs(w):
    a = pltpu.bitcast(w << 16, f32).astype(bf16)
    b = pltpu.bitcast(w & jnp.uint32(0xFFFF0000), f32).astype(bf16)
    return a, b


def _piece_copy(src_ref, dst_ref, src_row, dst_row, sem):
    return pltpu.make_async_copy(src_ref.at[pl.ds(pl.multiple_of(src_row, DMA_ROWS), DMA_ROWS)],
                                 dst_ref.at[pl.ds(pl.multiple_of(dst_row, DMA_ROWS), DMA_ROWS)], sem)


def _start_guaranteed_pieces(piece):
    for q in range(MIN_PIECES):
        piece(q).start()


def _start_remaining_pieces(n, piece):
    lax.fori_loop(MIN_PIECES, n, lambda q, c: (piece(q).start(), c)[1], 0)


def _wait_pieces(n, piece, bulk):
    bulk.wait()
    lax.fori_loop(MIN_PIECES, n, lambda q, c: (piece(q).wait(), c)[1], 0)


def _zero_fill_padding(pad_start_ref, pad_pieces_ref, n_used_ref, xb_ref, z_ref, sem):
    z_ref[...] = jnp.zeros_like(z_ref)
    n_blocks = xb_ref.shape[0] // MOE_BLOCK

    def tail_piece(e, q):
        return _piece_copy(z_ref, xb_ref, 0, pad_start_ref[e] + q * DMA_ROWS, sem)

    def block_copy(b):
        row = pl.multiple_of(b * MOE_BLOCK, MOE_BLOCK)
        return pltpu.make_async_copy(z_ref, xb_ref.at[pl.ds(row, MOE_BLOCK)], sem)

    def each_tail_piece(fn):
        def per_expert(e, c):
            lax.fori_loop(0, pad_pieces_ref[e], lambda q, cc: (fn(tail_piece(e, q)), cc)[1], 0)
            return c
        lax.fori_loop(0, N_EXPERTS, per_expert, 0)

    def each_block(fn):
        lax.fori_loop(n_used_ref[0], n_blocks, lambda b, c: (fn(block_copy(b)), c)[1], 0)

    each_tail_piece(lambda cp: cp.start())
    each_block(lambda cp: cp.start())
    each_tail_piece(lambda cp: cp.wait())
    each_block(lambda cp: cp.wait())


def _dispatch_kernel(nq_ref, pad_start_ref, pad_pieces_ref, n_used_ref, dst_ref, dstp_ref, metat_ref,
                     h_ref, xb_ref, l_ref, z_ref, sem):
    i = pl.program_id(0)
    last = pl.num_programs(0) - 1
    hw = l_ref.shape[2]
    cur = i % 2
    prv = 1 - cur
    n_prev = nq_ref[jnp.maximum(i - 1, 0)]
    min_rows = MIN_PIECES * DMA_ROWS

    def local_sort():
        slots = [metat_ref[k:k + 1, :] for k in range(TOP_K)]
        perm = _slot_matrix(slots, [1.0] * TOP_K, slot_axis=0).astype(bf16)
        return _pack_bf16_pairs(_dot(perm, h_ref[:, 0:hw]), _dot(perm, h_ref[:, hw:]))

    def piece(table_ref, buf):
        return lambda q: _piece_copy(l_ref.at[buf], xb_ref, q * DMA_ROWS, table_ref[0, 0, q],
                                     sem.at[buf])

    def wait(n, table_ref, buf):
        bulk = pltpu.make_async_copy(l_ref.at[buf, 0:min_rows], xb_ref.at[0:min_rows], sem.at[buf])
        _wait_pieces(n, piece(table_ref, buf), bulk)

    @pl.when(i == 0)
    def _():
        l_ref[cur] = local_sort()

    @pl.when(i > 0)
    def _():
        _start_guaranteed_pieces(piece(dstp_ref, prv))
        packed = local_sort()

        @pl.when(i > 1)
        def _():
            wait(nq_ref[jnp.maximum(i - 2, 0)], dstp_ref, cur)

        l_ref[cur] = packed
        _start_remaining_pieces(n_prev, piece(dstp_ref, prv))

    @pl.when(i == last)
    def _():
        _start_guaranteed_pieces(piece(dst_ref, cur))
        _start_remaining_pieces(nq_ref[i], piece(dst_ref, cur))
        _zero_fill_padding(pad_start_ref, pad_pieces_ref, n_used_ref, xb_ref, z_ref, sem.at[2])

        @pl.when(i > 0)
        def _():
            wait(n_prev, dstp_ref, prv)

        wait(nq_ref[i], dst_ref, cur)


def _dispatch(nq, pad_start, pad_pieces, n_used, dst, meta_t, h, n_rows):
    n, d = h.shape
    hw = d // 2
    t = MOE_TILE
    grid_spec = pltpu.PrefetchScalarGridSpec(
        num_scalar_prefetch=4,
        grid=(n // t,),
        in_specs=[pl.BlockSpec((1, 1, LOCAL_PIECES), lambda i, *_: (i, 0, 0), memory_space=pltpu.SMEM),
                  pl.BlockSpec((1, 1, LOCAL_PIECES), lambda i, *_: (jnp.maximum(i - 1, 0), 0, 0),
                               memory_space=pltpu.SMEM),
                  pl.BlockSpec((SUBLANE, t), lambda i, *_: (0, i)),
                  pl.BlockSpec((t, d), lambda i, *_: (i, 0))],
        out_specs=pl.BlockSpec(memory_space=pl.ANY),
        scratch_shapes=[pltpu.VMEM((2, LOCAL_ROWS, hw), jnp.uint32),
                        pltpu.VMEM((MOE_BLOCK, hw), jnp.uint32),
                        pltpu.SemaphoreType.DMA((3,))],
    )
    return pl.pallas_call(
        _dispatch_kernel,
        grid_spec=grid_spec,
        out_shape=jax.ShapeDtypeStruct((n_rows, hw), jnp.uint32),
        compiler_params=pltpu.CompilerParams(
            dimension_semantics=("arbitrary",), vmem_limit_bytes=VMEM_LIMIT),
        name="dispatch",
    )(nq, pad_start, pad_pieces, n_used, dst, dst, meta_t, h)


def _expert_kernel(be_ref, ns_ref, x_ref, wi_ref, bi_ref, wo_ref, bo_ref, y_ref,
                   wib_ref, wob_ref):
    b = pl.program_id(0)
    blk, hw = x_ref.shape
    dff = wo_ref.shape[1]
    subs = [slice(s * EXPERT_SUB_BLOCK, (s + 1) * EXPERT_SUB_BLOCK)
            for s in range(blk // EXPERT_SUB_BLOCK)]
    n_live = ns_ref[b]

    @pl.when(jnp.logical_and(n_live > 0,
                             jnp.logical_or(b == 0, be_ref[b] != be_ref[jnp.maximum(b - 1, 0)])))
    def _():
        for j in range(dff // LANE):
            wib_ref[:, 2 * j * LANE:(2 * j + 1) * LANE] = wi_ref[0, :, j * LANE:(j + 1) * LANE].astype(bf16)
            wib_ref[:, (2 * j + 1) * LANE:(2 * j + 2) * LANE] = (
                wi_ref[0, :, dff + j * LANE:dff + (j + 1) * LANE].astype(bf16))
        wob_ref[...] = wo_ref[0].astype(bf16)

    def up(rows):
        xa, xb = _unpack_bf16_pairs(x_ref[rows, :])
        return _dot(xa, wib_ref[0:hw, :]) + _dot(xb, wib_ref[hw:, :]) + bi_ref[0]

    def down(rows, hc):
        acts = []
        for j in range(dff // LANE):
            glu = jnp.minimum(hc[:, 2 * j * LANE:(2 * j + 1) * LANE], SWIGLU_LIMIT)
            lin = jnp.clip(hc[:, (2 * j + 1) * LANE:(2 * j + 2) * LANE], -SWIGLU_LIMIT, SWIGLU_LIMIT)
            acts.append((glu * (1.0 / (1.0 + jnp.exp(-SWIGLU_ALPHA * glu))) * (lin + 1.0)).astype(bf16))
        act = jnp.concatenate(acts, axis=1)
        y = (_dot(act, wob_ref[...]) + bo_ref[0]).astype(bf16).astype(f32)
        y_ref[rows, :] = _pack_bf16_pairs(y[:, 0:hw], y[:, hw:])

    def run(n):
        if n > 0:
            hc_next = up(subs[0])
        for s in range(n):
            hc = hc_next
            if s + 1 < n:
                hc_next = up(subs[s + 1])
            down(subs[s], hc)
        for rows in subs[n:]:
            y_ref[rows, :] = jnp.zeros((EXPERT_SUB_BLOCK, hw), y_ref.dtype)

    for n in range(len(subs) + 1):
        pl.when(n_live == n)(functools.partial(run, n))


def _experts(block_expert, block_live, xb, w_in, b_in, w_out, b_out):
    n_rows, hw = xb.shape
    e, d, dff2 = w_in.shape
    dff = w_out.shape[1]
    n_blocks = n_rows // MOE_BLOCK
    b_in_interleaved = b_in.reshape(e, 2, dff // LANE, LANE).transpose(0, 2, 1, 3)
    grid_spec = pltpu.PrefetchScalarGridSpec(
        num_scalar_prefetch=2,
        grid=(n_blocks,),
        in_specs=[
            pl.BlockSpec((MOE_BLOCK, hw), lambda b, be, nu: (b, 0)),
            pl.BlockSpec((1, d, dff2), lambda b, be, nu: (be[b], 0, 0)),
            pl.BlockSpec((1, 1, dff2), lambda b, be, nu: (be[b], 0, 0)),
            pl.BlockSpec((1, dff, d), lambda b, be, nu: (be[b], 0, 0)),
            pl.BlockSpec((1, 1, d), lambda b, be, nu: (be[b], 0, 0)),
        ],
        out_specs=pl.BlockSpec((MOE_BLOCK, hw), lambda b, be, nu: (b, 0)),
        scratch_shapes=[pltpu.VMEM((d, dff2), bf16), pltpu.VMEM((dff, d), bf16)],
    )
    return pl.pallas_call(
        _expert_kernel,
        grid_spec=grid_spec,
        out_shape=jax.ShapeDtypeStruct((n_rows, hw), jnp.uint32),
        compiler_params=pltpu.CompilerParams(
            dimension_semantics=("arbitrary",), vmem_limit_bytes=VMEM_LIMIT),
        name="experts",
    )(block_expert, block_live, xb, w_in, b_in_interleaved.reshape(e, 1, dff2), w_out,
      b_out.reshape(e, 1, d))


def _combine_kernel(nq_ref, dst_ref, dstn_ref, meta_ref, gate_ref, xm_ref, y_ref, o_ref,
                    ly_ref, sem):
    i = pl.program_id(0)
    last = pl.num_programs(0) - 1
    hw = ly_ref.shape[2]
    cur = i % 2
    nxt = 1 - cur
    next_tile = jnp.minimum(i + 1, last)
    min_rows = MIN_PIECES * DMA_ROWS

    def piece(table_ref, buf):
        return lambda q: _piece_copy(y_ref, ly_ref.at[buf], table_ref[0, 0, q], q * DMA_ROWS,
                                     sem.at[buf])

    def zero_tail(buf):
        ly_ref[buf, MOE_TILE * TOP_K:, :] = jnp.zeros((LOCAL_ROWS - MOE_TILE * TOP_K, hw), jnp.uint32)

    def wait(n, buf):
        bulk = pltpu.make_async_copy(y_ref.at[0:min_rows], ly_ref.at[buf, 0:min_rows], sem.at[buf])
        _wait_pieces(n, piece(dst_ref, buf), bulk)

    @pl.when(i == 0)
    def _():
        zero_tail(cur)
        _start_guaranteed_pieces(piece(dst_ref, cur))
        _start_remaining_pieces(nq_ref[i], piece(dst_ref, cur))

    wait(nq_ref[i], cur)
    zero_tail(nxt)
    _start_guaranteed_pieces(piece(dstn_ref, nxt))
    gates = gate_ref[...]
    meta = meta_ref[...]
    g = _slot_matrix([meta[:, k:k + 1] for k in range(TOP_K)],
                     [gates[:, k:k + 1] for k in range(TOP_K)], slot_axis=1).astype(bf16)
    ya, yb = _unpack_bf16_pairs(ly_ref[cur])
    o_ref[:, 0:hw] = xm_ref[:, 0:hw] + _dot(g, ya)
    o_ref[:, hw:] = xm_ref[:, hw:] + _dot(g, yb)
    _start_remaining_pieces(nq_ref[next_tile], piece(dstn_ref, nxt))

    @pl.when(i == last)
    def _():
        wait(nq_ref[next_tile], nxt)


def _combine(nq, dst, meta, gates, xm, y):
    n, d = xm.shape
    hw = d // 2
    t = MOE_TILE
    n_tiles = n // t
    grid_spec = pltpu.PrefetchScalarGridSpec(
        num_scalar_prefetch=1,
        grid=(n_tiles,),
        in_specs=[pl.BlockSpec((1, 1, LOCAL_PIECES), lambda i, nq: (i, 0, 0), memory_space=pltpu.SMEM),
                  pl.BlockSpec((1, 1, LOCAL_PIECES), lambda i, nq: (jnp.minimum(i + 1, n_tiles - 1), 0, 0),
                               memory_space=pltpu.SMEM),
                  pl.BlockSpec((t, LANE), lambda i, nq: (i, 0)),
                  pl.BlockSpec((t, LANE), lambda i, nq: (i, 0)),
                  pl.BlockSpec((t, d), lambda i, nq: (i, 0)),
                  pl.BlockSpec(memory_space=pl.ANY)],
        out_specs=pl.BlockSpec((t, d), lambda i, nq: (i, 0)),
        scratch_shapes=[pltpu.VMEM((2, LOCAL_ROWS, hw), jnp.uint32), pltpu.SemaphoreType.DMA((2,))],
    )
    return pl.pallas_call(
        _combine_kernel,
        grid_spec=grid_spec,
        out_shape=jax.ShapeDtypeStruct((n, d), f32),
        compiler_params=pltpu.CompilerParams(
            dimension_semantics=("arbitrary",), vmem_limit_bytes=VMEM_LIMIT),
        name="combine",
    )(nq, dst, dst, meta, gates, xm, y)


def _rel_bias_table(rel_bias):
    band = BAND_CHUNKS * CHUNK
    width = band + CHUNK
    dist = (np.arange(width) - CHUNK)[::-1]
    ext = rel_bias[:, np.clip(dist, -REL_CLIP, REL_CLIP) + REL_CLIP].astype(f32)
    heads = ext.shape[0]
    tiled = jnp.broadcast_to(ext[:, None, :], (heads, CHUNK, width)).reshape(heads, CHUNK * width)
    skewed = tiled[:, :CHUNK * (width - 1)].reshape(heads, CHUNK, width - 1)
    return skewed[:, :, CHUNK - 1:CHUNK - 1 + band]


def _round_up(x, m):
    return (x + m - 1) // m * m


def _routing_tables(cnt, n_tokens):
    experts = jnp.arange(N_EXPERTS, dtype=jnp.int32)
    c = cnt.reshape(-1, LANE)[:, :N_EXPERTS].astype(jnp.int32)
    n_tiles = c.shape[0]
    cp = _round_up(c, DMA_ROWS)
    lend = jnp.cumsum(cp, axis=1)
    lstart = lend - cp
    nq = (lend[:, -1] // DMA_ROWS).astype(jnp.int32)
    region = jnp.sum(cp, axis=0)
    padded = _round_up(region, MOE_BLOCK)
    pend = jnp.cumsum(padded)
    pstart = pend - padded
    base = pstart[None, :] + jnp.cumsum(cp, axis=0) - cp
    q0 = jnp.arange(LOCAL_PIECES, dtype=jnp.int32) * DMA_ROWS
    e_q = jnp.minimum(jnp.sum(lend[:, None, :] <= q0[None, :, None], axis=-1), N_EXPERTS - 1)
    shift = jnp.sum(jnp.where(e_q[:, :, None] == experts, (base - lstart)[:, None, :], 0), axis=-1)
    dst = jnp.where(q0[None, :] < lend[:, -1:], shift + q0[None, :], 0).astype(jnp.int32)

    n_blocks = -(-(n_tokens * TOP_K + n_tiles * N_EXPERTS * (DMA_ROWS - 1)) // MOE_BLOCK) + N_EXPERTS
    blk0 = jnp.arange(n_blocks, dtype=jnp.int32) * MOE_BLOCK
    n_used = (pend[-1] // MOE_BLOCK).astype(jnp.int32)
    be = jnp.minimum(jnp.sum(pend[None, :] <= blk0[:, None], axis=1), N_EXPERTS - 1).astype(jnp.int32)
    last = jnp.sum(jnp.where(jnp.arange(n_blocks) == n_used - 1, be, 0))
    be = jnp.where(jnp.arange(n_blocks) < n_used, be, last)
    onehot_be = be[:, None] == experts
    region_end = jnp.sum(jnp.where(onehot_be, pstart + region, 0), axis=1)
    live_rows = jnp.clip(region_end - blk0, 0, MOE_BLOCK)
    live = jnp.where(jnp.arange(n_blocks) < n_used, -(-live_rows // EXPERT_SUB_BLOCK), 0)
    pad_start = (pstart + region).astype(jnp.int32)
    pad_pieces = ((padded - region) // DMA_ROWS).astype(jnp.int32)
    return (nq, pad_start, pad_pieces, dst.reshape(n_tiles, 1, LOCAL_PIECES), be,
            live.astype(jnp.int32), n_used.reshape(1), n_blocks)


def _layer(x, norm1_g, w_in, gate_up, gate_bias, gla_norm_g, q_norm_g, k_norm_g, rel_bias, w_out,
           norm2_g, router_w, router_b, moe_w_in, moe_b_in, moe_w_out, moe_b_out):
    batch, seq, d = x.shape
    n = batch * seq
    x2 = x.reshape(n, d)

    pieces = jnp.split(w_in, np.cumsum(IN_SIZES)[:-1].tolist(), axis=-1)
    pieces[3] = jnp.pad(pieces[3], ((0, 0), (0, LANE - GLA_GATE_RANK)))
    widths = [p.shape[-1] for p in pieces]
    w_all = jnp.concatenate(pieces, axis=-1).astype(bf16)
    tile2 = lambda g: jnp.tile(g.reshape(1, -1), (1, LANE // ATT_DH))
    gq, gk, gv, glr, gog, qs, ak, av = _inproj(x2, norm1_g.reshape(1, d), w_all, widths,
                                               tile2(q_norm_g), tile2(k_norm_g))

    gup = jnp.pad(gate_up, ((0, LANE - GLA_GATE_RANK), (0, 0))).astype(bf16)
    o_gla = _gla(gq, gk, gv, glr, gog, gup, gate_bias.reshape(1, -1), gla_norm_g.reshape(1, -1),
                 batch, seq)
    o_att = _attention(qs, ak, av, _rel_bias_table(rel_bias), batch, seq)

    rw = jnp.pad(router_w, ((0, 0), (0, LANE - N_EXPERTS)))
    rw_hi = rw.astype(bf16)
    rw_lo = (rw - rw_hi.astype(f32)).astype(bf16)
    rb = jnp.pad(router_b, (0, LANE - N_EXPERTS)).reshape(1, LANE)
    xm, h2, meta, meta_t, gates, cnt = _mid(o_gla, o_att, x2, w_out.astype(bf16),
                                            norm2_g.reshape(1, d), rw_hi, rw_lo, rb)

    nq, pad_start, pad_pieces, dst, be, live, n_used, n_blocks = _routing_tables(cnt, n)
    xb = _dispatch(nq, pad_start, pad_pieces, n_used, dst, meta_t, h2, n_blocks * MOE_BLOCK)
    y = _experts(be, live, xb, moe_w_in, moe_b_in, moe_w_out, moe_b_out)
    out = _combine(nq, dst, meta, gates, xm, y)
    return out.reshape(batch, seq, d)


def kernel(x, norm1_g, w_in, gla_gate_up, gla_gate_bias, gla_norm_g, q_norm_g, k_norm_g, rel_bias, w_out, norm2_g, router_w, router_b, moe_w_in, moe_b_in, moe_w_out, moe_b_out):
    for l in range(norm1_g.shape[0]):
        x = _layer(x, norm1_g[l], w_in[l], gla_gate_up[l], gla_gate_bias[l], gla_norm_g[l],
                   q_norm_g[l], k_norm_g[l], rel_bias[l], w_out[l], norm2_g[l], router_w[l],
                   router_b[l], moe_w_in[l], moe_b_in[l], moe_w_out[l], moe_b_out[l])
    return x
```

```python
import functools

import numpy as np
import jax
import jax.numpy as jnp
from jax import lax
from jax.experimental import pallas as pl
from jax.experimental.pallas import tpu as pltpu

CHUNK = 64
EPS = 1e-6
GLA_HEADS = 4
GLA_DK = 64
GLA_DV = 128
GLA_GATE_RANK = 16
GLA_GATE_TAU = 16.0
ATT_HEADS = 8
ATT_DH = 64
N_BACK_CHUNKS = 8
BAND_CHUNKS = N_BACK_CHUNKS + 1
REL_CLIP = 256
MASK_VALUE = -1e30
N_EXPERTS = 32
TOP_K = 4
SWIGLU_ALPHA = 1.702
SWIGLU_LIMIT = 7.0
MOE_BLOCK = 1024
EXPERT_SUB_BLOCK = 256

LANE = 128
SUBLANE = 8
GLA_QK_W = GLA_HEADS * GLA_DK
GLA_V_W = GLA_HEADS * GLA_DV
ATT_W = ATT_HEADS * ATT_DH
IN_SIZES = (GLA_QK_W, GLA_QK_W, GLA_V_W, GLA_GATE_RANK, GLA_V_W, ATT_W, ATT_W, ATT_W)
SEQ_TILE = N_BACK_CHUNKS * CHUNK
ROW_TILE = 512
MID_TILE = 1024
MOE_TILE = 256
DMA_ROWS = SUBLANE
LOCAL_PIECES = 160
LOCAL_ROWS = LOCAL_PIECES * DMA_ROWS
MIN_PIECES = MOE_TILE * TOP_K // DMA_ROWS
VMEM_LIMIT = 48 * 1024 * 1024

f32 = jnp.float32
bf16 = jnp.bfloat16


def _rms(x, g):
    return x * lax.rsqrt(jnp.mean(x * x, axis=-1, keepdims=True) + EPS) * g


def _dot(a, b):
    return jnp.dot(a, b, preferred_element_type=f32)


def _dot_nt(a, b):
    return lax.dot_general(a, b, (((1,), (1,)), ((), ())), preferred_element_type=f32)


def _dot_tn(a, b):
    return lax.dot_general(a, b, (((0,), (0,)), ((), ())), preferred_element_type=f32)


def _split_bf16(x):
    hi = x.astype(bf16)
    lo = (x - hi.astype(f32)).astype(bf16)
    return hi, lo


def _head_norm(x, g):
    lo = lax.broadcasted_iota(jnp.int32, (1, LANE), 1) < ATT_DH
    sq = x * x
    s0 = jnp.sum(jnp.where(lo, sq, 0.0), axis=-1, keepdims=True)
    s1 = jnp.sum(jnp.where(lo, 0.0, sq), axis=-1, keepdims=True)
    r = jnp.where(lo, lax.rsqrt(s0 * (1.0 / ATT_DH) + EPS), lax.rsqrt(s1 * (1.0 / ATT_DH) + EPS))
    return x * r * g


def _inproj_kernel(x_ref, g_ref, w_ref, qg_ref, kg_ref, gq, gk, gv, glr, gog, qs_ref, ak, av):
    t = x_ref.shape[0]
    n_pairs = ATT_W // LANE
    h = _rms(x_ref[...], g_ref[...]).astype(bf16)
    offsets = np.cumsum((0,) + tuple(r.shape[-1] for r in (gq, gk, gv, glr, gog)) + (ATT_W, ATT_W))
    lo = lax.broadcasted_iota(jnp.int32, (1, LANE), 1) < ATT_DH

    q_all = _dot(h, w_ref[:, offsets[5]:offsets[5] + ATT_W])
    k_all = _dot(h, w_ref[:, offsets[6]:offsets[6] + ATT_W])
    for p in range(n_pairs):
        pair = slice(p * LANE, (p + 1) * LANE)
        qn = _head_norm(q_all[:, pair], qg_ref[...]) * (ATT_DH ** -0.5)
        q_lo = jnp.where(lo, qn, 0.0).astype(bf16)
        q_hi = jnp.where(lo, 0.0, qn).astype(bf16)
        for c in range(t // CHUNK):
            rows = slice(c * CHUNK, (c + 1) * CHUNK)
            qs_ref[c * n_pairs + p, 0:CHUNK, :] = q_lo[rows]
            qs_ref[c * n_pairs + p, CHUNK:2 * CHUNK, :] = q_hi[rows]
        ak[:, pair] = _head_norm(k_all[:, pair], kg_ref[...]).astype(ak.dtype)
    for o_ref, off in zip((gq, gk, gv, glr, gog, av), tuple(offsets[:5]) + (offsets[7],)):
        o_ref[...] = _dot(h, w_ref[:, off:off + o_ref.shape[-1]]).astype(o_ref.dtype)


def _inproj(x2, g, w, widths, qg, kg):
    n, d = x2.shape
    n_pairs = ATT_W // LANE
    blocks_per_tile = ROW_TILE // CHUNK * n_pairs
    rows = lambda wd: pl.BlockSpec((ROW_TILE, wd), lambda i: (i, 0))
    full = lambda a: pl.BlockSpec(a.shape, lambda i: (0,) * a.ndim)
    plain = lambda wd: jax.ShapeDtypeStruct((n, wd), bf16)
    return pl.pallas_call(
        _inproj_kernel,
        grid=(n // ROW_TILE,),
        in_specs=[rows(d), full(g), full(w), full(qg), full(kg)],
        out_specs=[rows(wd) for wd in widths[:5]]
        + [pl.BlockSpec((blocks_per_tile, 2 * CHUNK, LANE), lambda i: (i, 0, 0)), rows(ATT_W), rows(ATT_W)],
        out_shape=[plain(wd) for wd in widths[:5]]
        + [jax.ShapeDtypeStruct((n // CHUNK * n_pairs, 2 * CHUNK, LANE), bf16), plain(ATT_W), plain(ATT_W)],
        compiler_params=pltpu.CompilerParams(
            dimension_semantics=("arbitrary",), vmem_limit_bytes=VMEM_LIMIT),
        name="inproj",
    )(x2, g, w, qg, kg)


def _gla_kernel(q_ref, k_ref, v_ref, lr_ref, og_ref, gup_ref, gb_ref, ng_ref, o_ref, st_ref,
                u_ref, sb_ref):
    t = q_ref.shape[0]

    @pl.when(pl.program_id(1) == 0)
    def _():
        st_ref[...] = jnp.zeros_like(st_ref)

    z = _dot(lr_ref[...], gup_ref[...]) + gb_ref[...]
    log_a = (jnp.minimum(z, 0.0) - jnp.log1p(jnp.exp(-jnp.abs(z)))) * (1.0 / GLA_GATE_TAU)
    row = lax.broadcasted_iota(jnp.int32, (t, t), 0)
    col = lax.broadcasted_iota(jnp.int32, (t, t), 1)
    tri = jnp.where((col <= row) & ((col // CHUNK) == (row // CHUNK)), 1.0, 0.0).astype(bf16)
    la_hi, la_lo = _split_bf16(log_a)
    cum_all = _dot(tri, la_hi) + _dot(tri, la_lo)
    lane = lax.broadcasted_iota(jnp.int32, (1, LANE), 1)
    half_mask = (lane < GLA_DK, lane >= GLA_DK)
    n_chunks = t // CHUNK
    chunk_rows = [slice(c * CHUNK, (c + 1) * CHUNK) for c in range(n_chunks)]
    pair_of = lambda h: slice((h // 2) * LANE, (h // 2 + 1) * LANE)
    head_of = lambda h: slice(h * GLA_DV, (h + 1) * GLA_DV)

    decs = []
    for c, rows in enumerate(chunk_rows):
        cum = cum_all[rows]
        tot = cum[CHUNK - 1:CHUNK]
        kdec = k_ref[rows, :].astype(f32) * jnp.exp(tot - cum)
        decs.append(jnp.exp(tot))
        for h in range(GLA_HEADS):
            kd = jnp.where(half_mask[h % 2], kdec[:, pair_of(h)], 0.0).astype(bf16)
            u_ref[c, h] = _dot_tn(v_ref[rows, head_of(h)], kd)

    for h in range(GLA_HEADS):
        st = st_ref[h]
        for c in range(n_chunks):
            st = st * decs[c][:, pair_of(h)] + u_ref[c, h]
            sb_ref[c, h] = st.astype(bf16)
        st_ref[h] = st

    for c, rows in enumerate(chunk_rows):
        for h in range(GLA_HEADS):
            o = _dot_nt(q_ref[rows, pair_of(h)], sb_ref[c, h]) * (GLA_DK ** -0.5)
            o = _rms(o, ng_ref[...])
            g = og_ref[rows, head_of(h)].astype(f32)
            o_ref[rows, head_of(h)] = (o * (g / (1.0 + jnp.exp(-g)))).astype(o_ref.dtype)


def _gla(gq, gk, gv, glr, gog, gup, gb, ng, batch, seq):
    nt = seq // SEQ_TILE
    tile = lambda w: pl.BlockSpec((SEQ_TILE, w), lambda b, i: (b * nt + i, 0))
    full = lambda a: pl.BlockSpec(a.shape, lambda b, i: (0,) * a.ndim)
    return pl.pallas_call(
        _gla_kernel,
        grid=(batch, nt),
        in_specs=[tile(GLA_QK_W), tile(GLA_QK_W), tile(GLA_V_W), tile(LANE), tile(GLA_V_W),
                  full(gup), full(gb), full(ng)],
        out_specs=tile(GLA_V_W),
        out_shape=jax.ShapeDtypeStruct((batch * seq, GLA_V_W), bf16),
        scratch_shapes=[pltpu.VMEM((GLA_HEADS, GLA_DV, LANE), f32),
                        pltpu.VMEM((SEQ_TILE // CHUNK, GLA_HEADS, GLA_DV, LANE), f32),
                        pltpu.VMEM((SEQ_TILE // CHUNK, GLA_HEADS, GLA_DV, LANE), bf16)],
        compiler_params=pltpu.CompilerParams(
            dimension_semantics=("arbitrary", "arbitrary"), vmem_limit_bytes=VMEM_LIMIT),
        name="gla",
    )(gq, gk, gv, glr, gog, gup, gb, ng)


def _att_kernel(qs_ref, k_ref, v_ref, bias_ref, o_ref, kb_ref, vb_ref):
    t = k_ref.shape[0]
    n_pairs = ATT_W // LANE
    band = BAND_CHUNKS * CHUNK
    first = pl.program_id(1) == 0

    @pl.when(first)
    def _():
        kb_ref[0:t, :] = jnp.zeros((t, ATT_W), bf16)
        vb_ref[0:t, :] = jnp.zeros((t, 2 * ATT_W), bf16)

    @pl.when(jnp.logical_not(first))
    def _():
        kb_ref[0:t, :] = kb_ref[t:2 * t, :]
        vb_ref[0:t, :] = vb_ref[t:2 * t, :]

    lane = lax.broadcasted_iota(jnp.int32, (1, LANE), 1)
    lo = lane < ATT_DH

    kb_ref[t:2 * t, :] = k_ref[...]
    for p in range(n_pairs):
        vb_ref[t:2 * t, 2 * p * LANE:(2 * p + 1) * LANE] = v_ref[:, p * LANE:(p + 1) * LANE]
        vb_ref[t:2 * t, (2 * p + 1) * LANE:(2 * p + 2) * LANE] = jnp.ones((t, LANE), bf16)

    colk = lax.broadcasted_iota(jnp.int32, (1, band), 1)

    def chunk_loop(masked):
        n_chunks = t // CHUNK

        def scores(c, p):
            k2 = kb_ref[c * CHUNK:c * CHUNK + band, p * LANE:(p + 1) * LANE]
            return _dot_nt(qs_ref[c * n_pairs + p], k2)

        def weighted_values(e, c, p):
            v2 = vb_ref[c * CHUNK:c * CHUNK + band, 2 * p * LANE:(2 * p + 2) * LANE]
            pvl = _dot(e, v2)
            pv = pvl[:, 0:LANE] / pvl[:, LANE:2 * LANE]
            o_ref[c * CHUNK:(c + 1) * CHUNK, p * LANE:(p + 1) * LANE] = jnp.where(
                lo, pv[0:CHUNK], pv[CHUNK:2 * CHUNK]).astype(o_ref.dtype)

        blocks = [(c, p) for c in range(n_chunks) for p in range(n_pairs)]
        s_next = scores(0, 0)
        e_prev = None
        for i, (c, p) in enumerate(blocks):
            s = s_next + bias_ref[p]
            if i + 1 < len(blocks):
                s_next = scores(*blocks[i + 1])
            if masked:
                s = jnp.where(colk >= t - c * CHUNK, s, MASK_VALUE)
            e = jnp.exp(s - jnp.max(s, axis=-1, keepdims=True)).astype(bf16)
            if i > 0:
                weighted_values(e_prev, *blocks[i - 1])
            e_prev = e
        weighted_values(e_prev, *blocks[-1])

    @pl.when(first)
    def _():
        chunk_loop(True)

    @pl.when(jnp.logical_not(first))
    def _():
        chunk_loop(False)


def _attention(qs, ak, av, bias, batch, seq):
    nt = seq // SEQ_TILE
    n_pairs = ATT_W // LANE
    tile = pl.BlockSpec((SEQ_TILE, ATT_W), lambda b, i: (b * nt + i, 0))
    q_tile = pl.BlockSpec((SEQ_TILE // CHUNK * n_pairs, 2 * CHUNK, LANE), lambda b, i: (b * nt + i, 0, 0))
    full = lambda a: pl.BlockSpec(a.shape, lambda b, i: (0,) * a.ndim)
    bias2 = bias.reshape(n_pairs, 2 * CHUNK, BAND_CHUNKS * CHUNK)
    return pl.pallas_call(
        _att_kernel,
        grid=(batch, nt),
        in_specs=[q_tile, tile, tile, full(bias2)],
        out_specs=tile,
        out_shape=jax.ShapeDtypeStruct((batch * seq, ATT_W), bf16),
        scratch_shapes=[pltpu.VMEM((2 * SEQ_TILE, ATT_W), bf16),
                        pltpu.VMEM((2 * SEQ_TILE, 2 * ATT_W), bf16)],
        compiler_params=pltpu.CompilerParams(
            dimension_semantics=("arbitrary", "arbitrary"), vmem_limit_bytes=VMEM_LIMIT),
        name="attention",
    )(qs, ak, av, bias2)


def _mid_kernel(og_ref, oa_ref, x_ref, wo_ref, g2_ref, rwh_ref, rwl_ref, rb_ref,
                xm_ref, h_ref, meta_ref, metat_ref, gate_ref, cnt_ref):
    t, d = x_ref.shape
    mt = MOE_TILE
    tiles = [slice(s * mt, (s + 1) * mt) for s in range(t // mt)]
    lane = lax.broadcasted_iota(jnp.int32, (mt, LANE), 1)
    lane_f = lane.astype(f32)
    row = lax.broadcasted_iota(jnp.int32, (mt, mt), 0)
    col = lax.broadcasted_iota(jnp.int32, (mt, mt), 1)
    earlier = jnp.where(col < row, 1.0, 0.0).astype(bf16)
    e_r = lax.broadcasted_iota(jnp.int32, (LANE, LANE), 0)
    e_c = lax.broadcasted_iota(jnp.int32, (LANE, LANE), 1)
    before = jnp.where(e_r < e_c, 1.0, 0.0).astype(bf16)

    def project(rows):
        xm = (x_ref[rows, :] + _dot(og_ref[rows, :], wo_ref[0:GLA_V_W, :])
              + _dot(oa_ref[rows, :], wo_ref[GLA_V_W:, :]))
        xm_ref[rows, :] = xm
        h_hi, h_lo = _split_bf16(_rms(xm, g2_ref[...]))
        h_ref[rows, :] = h_hi
        return h_hi, h_lo

    def route(h_hi, h_lo):
        return (_dot(h_hi, rwh_ref[...]) + _dot(h_lo, rwh_ref[...]) + _dot(h_hi, rwl_ref[...])
                + rb_ref[...])

    def top_k(logits):
        l = jnp.where(lane < N_EXPERTS, logits, -jnp.inf)
        vals, onehots = [], []
        for _ in range(TOP_K):
            m = jnp.max(l, axis=-1, keepdims=True)
            ik = jnp.min(jnp.where(l == m, lane_f, float(LANE)), axis=-1, keepdims=True)
            vals.append(m)
            onehots.append(lane_f == ik)
            l = jnp.where(onehots[-1], -jnp.inf, l)
        es = [jnp.exp(v - vals[0]) for v in vals]
        den = es[0] + es[1] + es[2] + es[3]
        sel = jnp.zeros((mt, LANE), f32)
        for oh in onehots:
            sel = sel + jnp.where(oh, 1.0, 0.0)
        return onehots, [e / den for e in es], sel

    def slots(s, rows, onehots, gate_vals, sel):
        prefix = _dot(earlier, sel.astype(bf16))
        c_row = jnp.sum(sel, axis=0, keepdims=True)
        cnt_ref[0, s:s + 1, :] = c_row
        padded = jnp.ceil(c_row * (1.0 / DMA_ROWS)) * DMA_ROWS
        start_row = _dot(jnp.broadcast_to(padded, (SUBLANE, LANE)).astype(bf16), before)[0:1]
        slot_base = prefix + start_row
        meta = jnp.zeros((mt, LANE), f32)
        gates = jnp.zeros((mt, LANE), f32)
        for k in range(TOP_K):
            slot_k = jnp.sum(jnp.where(onehots[k], slot_base, 0.0), axis=-1, keepdims=True)
            meta = jnp.where(lane == k, slot_k, meta)
            gates = jnp.where(lane == k, gate_vals[k], gates)
        meta_ref[rows, :] = meta.astype(jnp.int32)
        metat_ref[:, rows] = meta.T[0:SUBLANE, :].astype(jnp.int32)
        gate_ref[rows, :] = gates

    hs = [project(rows) for rows in tiles]
    logits = [route(*h) for h in hs]
    routed = [top_k(lg) for lg in logits]
    for s, rows in enumerate(tiles):
        slots(s, rows, *routed[s])


def _mid(o_gla, o_att, x2, wo, g2, rwh, rwl, rb):
    n, d = x2.shape
    sub = MID_TILE // MOE_TILE
    tile = lambda w: pl.BlockSpec((MID_TILE, w), lambda i: (i, 0))
    full = lambda a: pl.BlockSpec(a.shape, lambda i: (0,) * a.ndim)
    return pl.pallas_call(
        _mid_kernel,
        grid=(n // MID_TILE,),
        in_specs=[tile(GLA_V_W), tile(ATT_W), tile(d), full(wo), full(g2), full(rwh), full(rwl),
                  full(rb)],
        out_specs=[tile(d), tile(d), tile(LANE),
                   pl.BlockSpec((SUBLANE, MID_TILE), lambda i: (0, i)), tile(LANE),
                   pl.BlockSpec((1, sub, LANE), lambda i: (i, 0, 0))],
        out_shape=[jax.ShapeDtypeStruct((n, d), f32),
                   jax.ShapeDtypeStruct((n, d), bf16),
                   jax.ShapeDtypeStruct((n, LANE), jnp.int32),
                   jax.ShapeDtypeStruct((SUBLANE, n), jnp.int32),
                   jax.ShapeDtypeStruct((n, LANE), f32),
                   jax.ShapeDtypeStruct((n // MID_TILE, sub, LANE), f32)],
        compiler_params=pltpu.CompilerParams(
            dimension_semantics=("arbitrary",), vmem_limit_bytes=VMEM_LIMIT),
        name="mid",
    )(o_gla, o_att, x2, wo, g2, rwh, rwl, rb)


def _slot_matrix(slots, values, slot_axis):
    n_tokens = slots[0].shape[1 - slot_axis]
    shape = (LOCAL_ROWS, n_tokens) if slot_axis == 0 else (n_tokens, LOCAL_ROWS)
    iota_shape = (LOCAL_ROWS, 1) if slot_axis == 0 else (1, LOCAL_ROWS)
    pos = lax.broadcasted_iota(jnp.int32, iota_shape, slot_axis)
    out = jnp.zeros(shape, f32)
    for slot_k, value_k in zip(slots, values):
        out = jnp.where(pos == slot_k, value_k, out)
    return out


def _pack_bf16_pairs(a, b):
    return (pltpu.bitcast(a, jnp.uint32) >> 16) | (pltpu.bitcast(b, jnp.uint32) & jnp.uint32(0xFFFF0000))


def _unpack_bf16_pairs(w):
    a = pltpu.bitcast(w << 16, f32).astype(bf16)
    b = pltpu.bitcast(w & jnp.uint32(0xFFFF0000), f32).astype(bf16)
    return a, b


def _piece_copy(src_ref, dst_ref, src_row, dst_row, sem):
    return pltpu.make_async_copy(src_ref.at[pl.ds(pl.multiple_of(src_row, DMA_ROWS), DMA_ROWS)],
                                 dst_ref.at[pl.ds(pl.multiple_of(dst_row, DMA_ROWS), DMA_ROWS)], sem)


def _start_guaranteed_pieces(piece):
    for q in range(MIN_PIECES):
        piece(q).start()


def _start_remaining_pieces(n, piece):
    lax.fori_loop(MIN_PIECES, n, lambda q, c: (piece(q).start(), c)[1], 0)


def _wait_pieces(n, piece, bulk):
    bulk.wait()
    lax.fori_loop(MIN_PIECES, n, lambda q, c: (piece(q).wait(), c)[1], 0)


def _zero_fill_padding(pad_start_ref, pad_pieces_ref, n_used_ref, xb_ref, z_ref, sem):
    z_ref[...] = jnp.zeros_like(z_ref)
    n_blocks = xb_ref.shape[0] // MOE_BLOCK

    def tail_piece(e, q):
        return _piece_copy(z_ref, xb_ref, 0, pad_start_ref[e] + q * DMA_ROWS, sem)

    def block_copy(b):
        row = pl.multiple_of(b * MOE_BLOCK, MOE_BLOCK)
        return pltpu.make_async_copy(z_ref, xb_ref.at[pl.ds(row, MOE_BLOCK)], sem)

    def each_tail_piece(fn):
        def per_expert(e, c):
            lax.fori_loop(0, pad_pieces_ref[e], lambda q, cc: (fn(tail_piece(e, q)), cc)[1], 0)
            return c
        lax.fori_loop(0, N_EXPERTS, per_expert, 0)

    def each_block(fn):
        lax.fori_loop(n_used_ref[0], n_blocks, lambda b, c: (fn(block_copy(b)), c)[1], 0)

    each_tail_piece(lambda cp: cp.start())
    each_block(lambda cp: cp.start())
    each_tail_piece(lambda cp: cp.wait())
    each_block(lambda cp: cp.wait())


def _dispatch_kernel(nq_ref, pad_start_ref, pad_pieces_ref, n_used_ref, dst_ref, dstp_ref, metat_ref,
                     h_ref, xb_ref, l_ref, z_ref, sem):
    i = pl.program_id(0)
    last = pl.num_programs(0) - 1
    hw = l_ref.shape[2]
    cur = i % 2
    prv = 1 - cur
    n_prev = nq_ref[jnp.maximum(i - 1, 0)]
    min_rows = MIN_PIECES * DMA_ROWS

    def local_sort():
        slots = [metat_ref[k:k + 1, :] for k in range(TOP_K)]
        perm = _slot_matrix(slots, [1.0] * TOP_K, slot_axis=0).astype(bf16)
        return _pack_bf16_pairs(_dot(perm, h_ref[:, 0:hw]), _dot(perm, h_ref[:, hw:]))

    def piece(table_ref, buf):
        return lambda q: _piece_copy(l_ref.at[buf], xb_ref, q * DMA_ROWS, table_ref[0, 0, q],
                                     sem.at[buf])

    def wait(n, table_ref, buf):
        bulk = pltpu.make_async_copy(l_ref.at[buf, 0:min_rows], xb_ref.at[0:min_rows], sem.at[buf])
        _wait_pieces(n, piece(table_ref, buf), bulk)

    @pl.when(i == 0)
    def _():
        l_ref[cur] = local_sort()

    @pl.when(i > 0)
    def _():
        _start_guaranteed_pieces(piece(dstp_ref, prv))
        packed = local_sort()

        @pl.when(i > 1)
        def _():
            wait(nq_ref[jnp.maximum(i - 2, 0)], dstp_ref, cur)

        l_ref[cur] = packed
        _start_remaining_pieces(n_prev, piece(dstp_ref, prv))

    @pl.when(i == last)
    def _():
        _start_guaranteed_pieces(piece(dst_ref, cur))
        _start_remaining_pieces(nq_ref[i], piece(dst_ref, cur))
        _zero_fill_padding(pad_start_ref, pad_pieces_ref, n_used_ref, xb_ref, z_ref, sem.at[2])

        @pl.when(i > 0)
        def _():
            wait(n_prev, dstp_ref, prv)

        wait(nq_ref[i], dst_ref, cur)


def _dispatch(nq, pad_start, pad_pieces, n_used, dst, meta_t, h, n_rows):
    n, d = h.shape
    hw = d // 2
    t = MOE_TILE
    grid_spec = pltpu.PrefetchScalarGridSpec(
        num_scalar_prefetch=4,
        grid=(n // t,),
        in_specs=[pl.BlockSpec((1, 1, LOCAL_PIECES), lambda i, *_: (i, 0, 0), memory_space=pltpu.SMEM),
                  pl.BlockSpec((1, 1, LOCAL_PIECES), lambda i, *_: (jnp.maximum(i - 1, 0), 0, 0),
                               memory_space=pltpu.SMEM),
                  pl.BlockSpec((SUBLANE, t), lambda i, *_: (0, i)),
                  pl.BlockSpec((t, d), lambda i, *_: (i, 0))],
        out_specs=pl.BlockSpec(memory_space=pl.ANY),
        scratch_shapes=[pltpu.VMEM((2, LOCAL_ROWS, hw), jnp.uint32),
                        pltpu.VMEM((MOE_BLOCK, hw), jnp.uint32),
                        pltpu.SemaphoreType.DMA((3,))],
    )
    return pl.pallas_call(
        _dispatch_kernel,
        grid_spec=grid_spec,
        out_shape=jax.ShapeDtypeStruct((n_rows, hw), jnp.uint32),
        compiler_params=pltpu.CompilerParams(
            dimension_semantics=("arbitrary",), vmem_limit_bytes=VMEM_LIMIT),
        name="dispatch",
    )(nq, pad_start, pad_pieces, n_used, dst, dst, meta_t, h)


def _expert_kernel(be_ref, ns_ref, x_ref, wi_ref, bi_ref, wo_ref, bo_ref, y_ref,
                   wib_ref, wob_ref):
    b = pl.program_id(0)
    blk, hw = x_ref.shape
    dff = wo_ref.shape[1]
    subs = [slice(s * EXPERT_SUB_BLOCK, (s + 1) * EXPERT_SUB_BLOCK)
            for s in range(blk // EXPERT_SUB_BLOCK)]
    n_live = ns_ref[b]

    @pl.when(jnp.logical_and(n_live > 0,
                             jnp.logical_or(b == 0, be_ref[b] != be_ref[jnp.maximum(b - 1, 0)])))
    def _():
        for j in range(dff // LANE):
            wib_ref[:, 2 * j * LANE:(2 * j + 1) * LANE] = wi_ref[0, :, j * LANE:(j + 1) * LANE].astype(bf16)
            wib_ref[:, (2 * j + 1) * LANE:(2 * j + 2) * LANE] = (
                wi_ref[0, :, dff + j * LANE:dff + (j + 1) * LANE].astype(bf16))
        wob_ref[...] = wo_ref[0].astype(bf16)

    def up(rows):
        xa, xb = _unpack_bf16_pairs(x_ref[rows, :])
        return _dot(xa, wib_ref[0:hw, :]) + _dot(xb, wib_ref[hw:, :]) + bi_ref[0]

    def down(rows, hc):
        acts = []
        for j in range(dff // LANE):
            glu = jnp.minimum(hc[:, 2 * j * LANE:(2 * j + 1) * LANE], SWIGLU_LIMIT)
            lin = jnp.clip(hc[:, (2 * j + 1) * LANE:(2 * j + 2) * LANE], -SWIGLU_LIMIT, SWIGLU_LIMIT)
            acts.append((glu * (1.0 / (1.0 + jnp.exp(-SWIGLU_ALPHA * glu))) * (lin + 1.0)).astype(bf16))
        act = jnp.concatenate(acts, axis=1)
        y = (_dot(act, wob_ref[...]) + bo_ref[0]).astype(bf16).astype(f32)
        y_ref[rows, :] = _pack_bf16_pairs(y[:, 0:hw], y[:, hw:])

    def run(n):
        if n > 0:
            hc_next = up(subs[0])
        for s in range(n):
            hc = hc_next
            if s + 1 < n:
                hc_next = up(subs[s + 1])
            down(subs[s], hc)
        for rows in subs[n:]:
            y_ref[rows, :] = jnp.zeros((EXPERT_SUB_BLOCK, hw), y_ref.dtype)

    for n in range(len(subs) + 1):
        pl.when(n_live == n)(functools.partial(run, n))


def _experts(block_expert, block_live, xb, w_in, b_in, w_out, b_out):
    n_rows, hw = xb.shape
    e, d, dff2 = w_in.shape
    dff = w_out.shape[1]
    n_blocks = n_rows // MOE_BLOCK
    b_in_interleaved = b_in.reshape(e, 2, dff // LANE, LANE).transpose(0, 2, 1, 3)
    grid_spec = pltpu.PrefetchScalarGridSpec(
        num_scalar_prefetch=2,
        grid=(n_blocks,),
        in_specs=[
            pl.BlockSpec((MOE_BLOCK, hw), lambda b, be, nu: (b, 0)),
            pl.BlockSpec((1, d, dff2), lambda b, be, nu: (be[b], 0, 0)),
            pl.BlockSpec((1, 1, dff2), lambda b, be, nu: (be[b], 0, 0)),
            pl.BlockSpec((1, dff, d), lambda b, be, nu: (be[b], 0, 0)),
            pl.BlockSpec((1, 1, d), lambda b, be, nu: (be[b], 0, 0)),
        ],
        out_specs=pl.BlockSpec((MOE_BLOCK, hw), lambda b, be, nu: (b, 0)),
        scratch_shapes=[pltpu.VMEM((d, dff2), bf16), pltpu.VMEM((dff, d), bf16)],
    )
    return pl.pallas_call(
        _expert_kernel,
        grid_spec=grid_spec,
        out_shape=jax.ShapeDtypeStruct((n_rows, hw), jnp.uint32),
        compiler_params=pltpu.CompilerParams(
            dimension_semantics=("arbitrary",), vmem_limit_bytes=VMEM_LIMIT),
        name="experts",
    )(block_expert, block_live, xb, w_in, b_in_interleaved.reshape(e, 1, dff2), w_out,
      b_out.reshape(e, 1, d))


def _combine_kernel(nq_ref, dst_ref, dstn_ref, meta_ref, gate_ref, xm_ref, y_ref, o_ref,
                    ly_ref, sem):
    i = pl.program_id(0)
    last = pl.num_programs(0) - 1
    hw = ly_ref.shape[2]
    cur = i % 2
    nxt = 1 - cur
    next_tile = jnp.minimum(i + 1, last)
    min_rows = MIN_PIECES * DMA_ROWS

    def piece(table_ref, buf):
        return lambda q: _piece_copy(y_ref, ly_ref.at[buf], table_ref[0, 0, q], q * DMA_ROWS,
                                     sem.at[buf])

    def zero_tail(buf):
        ly_ref[buf, MOE_TILE * TOP_K:, :] = jnp.zeros((LOCAL_ROWS - MOE_TILE * TOP_K, hw), jnp.uint32)

    def wait(n, buf):
        bulk = pltpu.make_async_copy(y_ref.at[0:min_rows], ly_ref.at[buf, 0:min_rows], sem.at[buf])
        _wait_pieces(n, piece(dst_ref, buf), bulk)

    @pl.when(i == 0)
    def _():
        zero_tail(cur)
        _start_guaranteed_pieces(piece(dst_ref, cur))
        _start_remaining_pieces(nq_ref[i], piece(dst_ref, cur))

    zero_tail(nxt)
    _start_remaining_pieces(nq_ref[next_tile], piece(dstn_ref, nxt))
    wait(nq_ref[i], cur)
    _start_guaranteed_pieces(piece(dstn_ref, nxt))
    gates = gate_ref[...]
    meta = meta_ref[...]
    g = _slot_matrix([meta[:, k:k + 1] for k in range(TOP_K)],
                     [gates[:, k:k + 1] for k in range(TOP_K)], slot_axis=1).astype(bf16)
    ya, yb = _unpack_bf16_pairs(ly_ref[cur])
    o_ref[:, 0:hw] = xm_ref[:, 0:hw] + _dot(g, ya)
    o_ref[:, hw:] = xm_ref[:, hw:] + _dot(g, yb)

    @pl.when(i == last)
    def _():
        wait(nq_ref[next_tile], nxt)


def _combine(nq, dst, meta, gates, xm, y):
    n, d = xm.shape
    hw = d // 2
    t = MOE_TILE
    n_tiles = n // t
    grid_spec = pltpu.PrefetchScalarGridSpec(
        num_scalar_prefetch=1,
        grid=(n_tiles,),
        in_specs=[pl.BlockSpec((1, 1, LOCAL_PIECES), lambda i, nq: (i, 0, 0), memory_space=pltpu.SMEM),
                  pl.BlockSpec((1, 1, LOCAL_PIECES), lambda i, nq: (jnp.minimum(i + 1, n_tiles - 1), 0, 0),
                               memory_space=pltpu.SMEM),
                  pl.BlockSpec((t, LANE), lambda i, nq: (i, 0)),
                  pl.BlockSpec((t, LANE), lambda i, nq: (i, 0)),
                  pl.BlockSpec((t, d), lambda i, nq: (i, 0)),
                  pl.BlockSpec(memory_space=pl.ANY)],
        out_specs=pl.BlockSpec((t, d), lambda i, nq: (i, 0)),
        scratch_shapes=[pltpu.VMEM((2, LOCAL_ROWS, hw), jnp.uint32), pltpu.SemaphoreType.DMA((2,))],
    )
    return pl.pallas_call(
        _combine_kernel,
        grid_spec=grid_spec,
        out_shape=jax.ShapeDtypeStruct((n, d), f32),
        compiler_params=pltpu.CompilerParams(
            dimension_semantics=("arbitrary",), vmem_limit_bytes=VMEM_LIMIT),
        name="combine",
    )(nq, dst, dst, meta, gates, xm, y)


def _rel_bias_table(rel_bias):
    band = BAND_CHUNKS * CHUNK
    width = band + CHUNK
    dist = (np.arange(width) - CHUNK)[::-1]
    ext = rel_bias[:, np.clip(dist, -REL_CLIP, REL_CLIP) + REL_CLIP].astype(f32)
    heads = ext.shape[0]
    tiled = jnp.broadcast_to(ext[:, None, :], (heads, CHUNK, width)).reshape(heads, CHUNK * width)
    skewed = tiled[:, :CHUNK * (width - 1)].reshape(heads, CHUNK, width - 1)
    return skewed[:, :, CHUNK - 1:CHUNK - 1 + band]


def _round_up(x, m):
    return (x + m - 1) // m * m


def _routing_tables(cnt, n_tokens):
    experts = jnp.arange(N_EXPERTS, dtype=jnp.int32)
    c = cnt.reshape(-1, LANE)[:, :N_EXPERTS].astype(jnp.int32)
    n_tiles = c.shape[0]
    cp = _round_up(c, DMA_ROWS)
    lend = jnp.cumsum(cp, axis=1)
    lstart = lend - cp
    nq = (lend[:, -1] // DMA_ROWS).astype(jnp.int32)
    region = jnp.sum(cp, axis=0)
    padded = _round_up(region, MOE_BLOCK)
    pend = jnp.cumsum(padded)
    pstart = pend - padded
    base = pstart[None, :] + jnp.cumsum(cp, axis=0) - cp
    q0 = jnp.arange(LOCAL_PIECES, dtype=jnp.int32) * DMA_ROWS
    e_q = jnp.minimum(jnp.sum(lend[:, None, :] <= q0[None, :, None], axis=-1), N_EXPERTS - 1)
    shift = jnp.sum(jnp.where(e_q[:, :, None] == experts, (base - lstart)[:, None, :], 0), axis=-1)
    dst = jnp.where(q0[None, :] < lend[:, -1:], shift + q0[None, :], 0).astype(jnp.int32)

    n_blocks = -(-(n_tokens * TOP_K + n_tiles * N_EXPERTS * (DMA_ROWS - 1)) // MOE_BLOCK) + N_EXPERTS
    blk0 = jnp.arange(n_blocks, dtype=jnp.int32) * MOE_BLOCK
    n_used = (pend[-1] // MOE_BLOCK).astype(jnp.int32)
    be = jnp.minimum(jnp.sum(pend[None, :] <= blk0[:, None], axis=1), N_EXPERTS - 1).astype(jnp.int32)
    last = jnp.sum(jnp.where(jnp.arange(n_blocks) == n_used - 1, be, 0))
    be = jnp.where(jnp.arange(n_blocks) < n_used, be, last)
    onehot_be = be[:, None] == experts
    region_end = jnp.sum(jnp.where(onehot_be, pstart + region, 0), axis=1)
    live_rows = jnp.clip(region_end - blk0, 0, MOE_BLOCK)
    live = jnp.where(jnp.arange(n_blocks) < n_used, -(-live_rows // EXPERT_SUB_BLOCK), 0)
    pad_start = (pstart + region).astype(jnp.int32)
    pad_pieces = ((padded - region) // DMA_ROWS).astype(jnp.int32)
    return (nq, pad_start, pad_pieces, dst.reshape(n_tiles, 1, LOCAL_PIECES), be,
            live.astype(jnp.int32), n_used.reshape(1), n_blocks)


def _layer(x, norm1_g, w_in, gate_up, gate_bias, gla_norm_g, q_norm_g, k_norm_g, rel_bias, w_out,
           norm2_g, router_w, router_b, moe_w_in, moe_b_in, moe_w_out, moe_b_out):
    batch, seq, d = x.shape
    n = batch * seq
    x2 = x.reshape(n, d)

    pieces = jnp.split(w_in, np.cumsum(IN_SIZES)[:-1].tolist(), axis=-1)
    pieces[3] = jnp.pad(pieces[3], ((0, 0), (0, LANE - GLA_GATE_RANK)))
    widths = [p.shape[-1] for p in pieces]
    w_all = jnp.concatenate(pieces, axis=-1).astype(bf16)
    tile2 = lambda g: jnp.tile(g.reshape(1, -1), (1, LANE // ATT_DH))
    gq, gk, gv, glr, gog, qs, ak, av = _inproj(x2, norm1_g.reshape(1, d), w_all, widths,
                                               tile2(q_norm_g), tile2(k_norm_g))

    gup = jnp.pad(gate_up, ((0, LANE - GLA_GATE_RANK), (0, 0))).astype(bf16)
    o_gla = _gla(gq, gk, gv, glr, gog, gup, gate_bias.reshape(1, -1), gla_norm_g.reshape(1, -1),
                 batch, seq)
    o_att = _attention(qs, ak, av, _rel_bias_table(rel_bias), batch, seq)

    rw = jnp.pad(router_w, ((0, 0), (0, LANE - N_EXPERTS)))
    rw_hi = rw.astype(bf16)
    rw_lo = (rw - rw_hi.astype(f32)).astype(bf16)
    rb = jnp.pad(router_b, (0, LANE - N_EXPERTS)).reshape(1, LANE)
    xm, h2, meta, meta_t, gates, cnt = _mid(o_gla, o_att, x2, w_out.astype(bf16),
                                            norm2_g.reshape(1, d), rw_hi, rw_lo, rb)

    nq, pad_start, pad_pieces, dst, be, live, n_used, n_blocks = _routing_tables(cnt, n)
    xb = _dispatch(nq, pad_start, pad_pieces, n_used, dst, meta_t, h2, n_blocks * MOE_BLOCK)
    y = _experts(be, live, xb, moe_w_in, moe_b_in, moe_w_out, moe_b_out)
    out = _combine(nq, dst, meta, gates, xm, y)
    return out.reshape(batch, seq, d)


def kernel(x, norm1_g, w_in, gla_gate_up, gla_gate_bias, gla_norm_g, q_norm_g, k_norm_g, rel_bias, w_out, norm2_g, router_w, router_b, moe_w_in, moe_b_in, moe_w_out, moe_b_out):
    for l in range(norm1_g.shape[0]):
        x = _layer(x, norm1_g[l], w_in[l], gla_gate_up[l], gla_gate_bias[l], gla_norm_g[l],
                   q_norm_g[l], k_norm_g[l], rel_bias[l], w_out[l], norm2_g[l], router_w[l],
                   router_b[l], moe_w_in[l], moe_b_in[l], moe_w_out[l], moe_b_out[l])
    return x
```

```python
import functools

import numpy as np
import jax
import jax.numpy as jnp
from jax import lax
from jax.experimental import pallas as pl
from jax.experimental.pallas import tpu as pltpu

CHUNK = 64
EPS = 1e-6
GLA_HEADS = 4
GLA_DK = 64
GLA_DV = 128
GLA_GATE_RANK = 16
GLA_GATE_TAU = 16.0
ATT_HEADS = 8
ATT_DH = 64
N_BACK_CHUNKS = 8
BAND_CHUNKS = N_BACK_CHUNKS + 1
REL_CLIP = 256
MASK_VALUE = -1e30
N_EXPERTS = 32
TOP_K = 4
SWIGLU_ALPHA = 1.702
SWIGLU_LIMIT = 7.0
MOE_BLOCK = 1024
EXPERT_SUB_BLOCK = 256

LANE = 128
SUBLANE = 8
GLA_QK_W = GLA_HEADS * GLA_DK
GLA_V_W = GLA_HEADS * GLA_DV
ATT_W = ATT_HEADS * ATT_DH
IN_SIZES = (GLA_QK_W, GLA_QK_W, GLA_V_W, GLA_GATE_RANK, GLA_V_W, ATT_W, ATT_W, ATT_W)
SEQ_TILE = N_BACK_CHUNKS * CHUNK
ROW_TILE = 512
MID_TILE = 1024
MOE_TILE = 256
DMA_ROWS = SUBLANE
LOCAL_PIECES = 160
LOCAL_ROWS = LOCAL_PIECES * DMA_ROWS
MIN_PIECES = MOE_TILE * TOP_K // DMA_ROWS
VMEM_LIMIT = 48 * 1024 * 1024

f32 = jnp.float32
bf16 = jnp.bfloat16


def _rms(x, g):
    return x * lax.rsqrt(jnp.mean(x * x, axis=-1, keepdims=True) + EPS) * g


def _dot(a, b):
    return jnp.dot(a, b, preferred_element_type=f32)


def _dot_nt(a, b):
    return lax.dot_general(a, b, (((1,), (1,)), ((), ())), preferred_element_type=f32)


def _dot_tn(a, b):
    return lax.dot_general(a, b, (((0,), (0,)), ((), ())), preferred_element_type=f32)


def _split_bf16(x):
    hi = x.astype(bf16)
    lo = (x - hi.astype(f32)).astype(bf16)
    return hi, lo


def _head_norm(x, g):
    lo = lax.broadcasted_iota(jnp.int32, (1, LANE), 1) < ATT_DH
    sq = x * x
    s0 = jnp.sum(jnp.where(lo, sq, 0.0), axis=-1, keepdims=True)
    s1 = jnp.sum(jnp.where(lo, 0.0, sq), axis=-1, keepdims=True)
    r = jnp.where(lo, lax.rsqrt(s0 * (1.0 / ATT_DH) + EPS), lax.rsqrt(s1 * (1.0 / ATT_DH) + EPS))
    return x * r * g


def _inproj_kernel(x_ref, g_ref, w_ref, qg_ref, kg_ref, gq, gk, gv, glr, gog, qs_ref, ak, av):
    t = x_ref.shape[0]
    n_pairs = ATT_W // LANE
    h = _rms(x_ref[...], g_ref[...]).astype(bf16)
    offsets = np.cumsum((0,) + tuple(r.shape[-1] for r in (gq, gk, gv, glr, gog)) + (ATT_W, ATT_W))
    lo = lax.broadcasted_iota(jnp.int32, (1, LANE), 1) < ATT_DH

    q_all = _dot(h, w_ref[:, offsets[5]:offsets[5] + ATT_W])
    k_all = _dot(h, w_ref[:, offsets[6]:offsets[6] + ATT_W])
    for p in range(n_pairs):
        pair = slice(p * LANE, (p + 1) * LANE)
        qn = _head_norm(q_all[:, pair], qg_ref[...]) * (ATT_DH ** -0.5)
        q_lo = jnp.where(lo, qn, 0.0).astype(bf16)
        q_hi = jnp.where(lo, 0.0, qn).astype(bf16)
        for c in range(t // CHUNK):
            rows = slice(c * CHUNK, (c + 1) * CHUNK)
            qs_ref[c * n_pairs + p, 0:CHUNK, :] = q_lo[rows]
            qs_ref[c * n_pairs + p, CHUNK:2 * CHUNK, :] = q_hi[rows]
        ak[:, pair] = _head_norm(k_all[:, pair], kg_ref[...]).astype(ak.dtype)
    for o_ref, off in zip((gq, gk, gv, glr, gog, av), tuple(offsets[:5]) + (offsets[7],)):
        o_ref[...] = _dot(h, w_ref[:, off:off + o_ref.shape[-1]]).astype(o_ref.dtype)


def _inproj(x2, g, w, widths, qg, kg):
    n, d = x2.shape
    n_pairs = ATT_W // LANE
    blocks_per_tile = ROW_TILE // CHUNK * n_pairs
    rows = lambda wd: pl.BlockSpec((ROW_TILE, wd), lambda i: (i, 0))
    full = lambda a: pl.BlockSpec(a.shape, lambda i: (0,) * a.ndim)
    plain = lambda wd: jax.ShapeDtypeStruct((n, wd), bf16)
    return pl.pallas_call(
        _inproj_kernel,
        grid=(n // ROW_TILE,),
        in_specs=[rows(d), full(g), full(w), full(qg), full(kg)],
        out_specs=[rows(wd) for wd in widths[:5]]
        + [pl.BlockSpec((blocks_per_tile, 2 * CHUNK, LANE), lambda i: (i, 0, 0)), rows(ATT_W), rows(ATT_W)],
        out_shape=[plain(wd) for wd in widths[:5]]
        + [jax.ShapeDtypeStruct((n // CHUNK * n_pairs, 2 * CHUNK, LANE), bf16), plain(ATT_W), plain(ATT_W)],
        compiler_params=pltpu.CompilerParams(
            dimension_semantics=("arbitrary",), vmem_limit_bytes=VMEM_LIMIT),
        name="inproj",
    )(x2, g, w, qg, kg)


def _gla_kernel(q_ref, k_ref, v_ref, lr_ref, og_ref, gup_ref, gb_ref, ng_ref, o_ref, st_ref,
                u_ref, sb_ref):
    t = q_ref.shape[0]

    @pl.when(pl.program_id(1) == 0)
    def _():
        st_ref[...] = jnp.zeros_like(st_ref)

    z = _dot(lr_ref[...], gup_ref[...]) + gb_ref[...]
    log_a = (jnp.minimum(z, 0.0) - jnp.log1p(jnp.exp(-jnp.abs(z)))) * (1.0 / GLA_GATE_TAU)
    row = lax.broadcasted_iota(jnp.int32, (t, t), 0)
    col = lax.broadcasted_iota(jnp.int32, (t, t), 1)
    tri = jnp.where((col <= row) & ((col // CHUNK) == (row // CHUNK)), 1.0, 0.0).astype(bf16)
    la_hi, la_lo = _split_bf16(log_a)
    cum_all = _dot(tri, la_hi) + _dot(tri, la_lo)
    lane = lax.broadcasted_iota(jnp.int32, (1, LANE), 1)
    half_mask = (lane < GLA_DK, lane >= GLA_DK)
    n_chunks = t // CHUNK
    chunk_rows = [slice(c * CHUNK, (c + 1) * CHUNK) for c in range(n_chunks)]
    pair_of = lambda h: slice((h // 2) * LANE, (h // 2 + 1) * LANE)
    head_of = lambda h: slice(h * GLA_DV, (h + 1) * GLA_DV)

    decs = []
    for c, rows in enumerate(chunk_rows):
        cum = cum_all[rows]
        tot = cum[CHUNK - 1:CHUNK]
        kdec = k_ref[rows, :].astype(f32) * jnp.exp(tot - cum)
        decs.append(jnp.exp(tot))
        for h in range(GLA_HEADS):
            kd = jnp.where(half_mask[h % 2], kdec[:, pair_of(h)], 0.0).astype(bf16)
            u_ref[c, h] = _dot_tn(v_ref[rows, head_of(h)], kd)

    for h in range(GLA_HEADS):
        st = st_ref[h]
        for c in range(n_chunks):
            st = st * decs[c][:, pair_of(h)] + u_ref[c, h]
            sb_ref[c, h] = st.astype(bf16)
        st_ref[h] = st

    for c, rows in enumerate(chunk_rows):
        for h in range(GLA_HEADS):
            o = _dot_nt(q_ref[rows, pair_of(h)], sb_ref[c, h]) * (GLA_DK ** -0.5)
            o = _rms(o, ng_ref[...])
            g = og_ref[rows, head_of(h)].astype(f32)
            o_ref[rows, head_of(h)] = (o * (g / (1.0 + jnp.exp(-g)))).astype(o_ref.dtype)


def _gla(gq, gk, gv, glr, gog, gup, gb, ng, batch, seq):
    nt = seq // SEQ_TILE
    tile = lambda w: pl.BlockSpec((SEQ_TILE, w), lambda b, i: (b * nt + i, 0))
    full = lambda a: pl.BlockSpec(a.shape, lambda b, i: (0,) * a.ndim)
    return pl.pallas_call(
        _gla_kernel,
        grid=(batch, nt),
        in_specs=[tile(GLA_QK_W), tile(GLA_QK_W), tile(GLA_V_W), tile(LANE), tile(GLA_V_W),
                  full(gup), full(gb), full(ng)],
        out_specs=tile(GLA_V_W),
        out_shape=jax.ShapeDtypeStruct((batch * seq, GLA_V_W), bf16),
        scratch_shapes=[pltpu.VMEM((GLA_HEADS, GLA_DV, LANE), f32),
                        pltpu.VMEM((SEQ_TILE // CHUNK, GLA_HEADS, GLA_DV, LANE), f32),
                        pltpu.VMEM((SEQ_TILE // CHUNK, GLA_HEADS, GLA_DV, LANE), bf16)],
        compiler_params=pltpu.CompilerParams(
            dimension_semantics=("arbitrary", "arbitrary"), vmem_limit_bytes=VMEM_LIMIT),
        name="gla",
    )(gq, gk, gv, glr, gog, gup, gb, ng)


def _att_kernel(qs_ref, k_ref, v_ref, bias_ref, o_ref, kb_ref, vb_ref):
    t = k_ref.shape[0]
    n_pairs = ATT_W // LANE
    band = BAND_CHUNKS * CHUNK
    first = pl.program_id(1) == 0

    @pl.when(first)
    def _():
        kb_ref[0:t, :] = jnp.zeros((t, ATT_W), bf16)
        vb_ref[0:t, :] = jnp.zeros((t, 2 * ATT_W), bf16)

    @pl.when(jnp.logical_not(first))
    def _():
        kb_ref[0:t, :] = kb_ref[t:2 * t, :]
        vb_ref[0:t, :] = vb_ref[t:2 * t, :]

    lane = lax.broadcasted_iota(jnp.int32, (1, LANE), 1)
    lo = lane < ATT_DH

    kb_ref[t:2 * t, :] = k_ref[...]
    for p in range(n_pairs):
        vb_ref[t:2 * t, 2 * p * LANE:(2 * p + 1) * LANE] = v_ref[:, p * LANE:(p + 1) * LANE]
        vb_ref[t:2 * t, (2 * p + 1) * LANE:(2 * p + 2) * LANE] = jnp.ones((t, LANE), bf16)

    colk = lax.broadcasted_iota(jnp.int32, (1, band), 1)

    def chunk_loop(masked):
        n_chunks = t // CHUNK

        def scores(c, p):
            k2 = kb_ref[c * CHUNK:c * CHUNK + band, p * LANE:(p + 1) * LANE]
            return _dot_nt(qs_ref[c * n_pairs + p], k2)

        def weighted_values(e, c, p):
            v2 = vb_ref[c * CHUNK:c * CHUNK + band, 2 * p * LANE:(2 * p + 2) * LANE]
            pvl = _dot(e, v2)
            pv = pvl[:, 0:LANE] / pvl[:, LANE:2 * LANE]
            o_ref[c * CHUNK:(c + 1) * CHUNK, p * LANE:(p + 1) * LANE] = jnp.where(
                lo, pv[0:CHUNK], pv[CHUNK:2 * CHUNK]).astype(o_ref.dtype)

        blocks = [(c, p) for c in range(n_chunks) for p in range(n_pairs)]
        s_next = scores(0, 0)
        e_prev = None
        for i, (c, p) in enumerate(blocks):
            s = s_next + bias_ref[p]
            if i + 1 < len(blocks):
                s_next = scores(*blocks[i + 1])
            if masked:
                s = jnp.where(colk >= t - c * CHUNK, s, MASK_VALUE)
            e = jnp.exp(s - jnp.max(s, axis=-1, keepdims=True)).astype(bf16)
            if i > 0:
                weighted_values(e_prev, *blocks[i - 1])
            e_prev = e
        weighted_values(e_prev, *blocks[-1])

    @pl.when(first)
    def _():
        chunk_loop(True)

    @pl.when(jnp.logical_not(first))
    def _():
        chunk_loop(False)


def _attention(qs, ak, av, bias, batch, seq):
    nt = seq // SEQ_TILE
    n_pairs = ATT_W // LANE
    tile = pl.BlockSpec((SEQ_TILE, ATT_W), lambda b, i: (b * nt + i, 0))
    q_tile = pl.BlockSpec((SEQ_TILE // CHUNK * n_pairs, 2 * CHUNK, LANE), lambda b, i: (b * nt + i, 0, 0))
    full = lambda a: pl.BlockSpec(a.shape, lambda b, i: (0,) * a.ndim)
    bias2 = bias.reshape(n_pairs, 2 * CHUNK, BAND_CHUNKS * CHUNK)
    return pl.pallas_call(
        _att_kernel,
        grid=(batch, nt),
        in_specs=[q_tile, tile, tile, full(bias2)],
        out_specs=tile,
        out_shape=jax.ShapeDtypeStruct((batch * seq, ATT_W), bf16),
        scratch_shapes=[pltpu.VMEM((2 * SEQ_TILE, ATT_W), bf16),
                        pltpu.VMEM((2 * SEQ_TILE, 2 * ATT_W), bf16)],
        compiler_params=pltpu.CompilerParams(
            dimension_semantics=("arbitrary", "arbitrary"), vmem_limit_bytes=VMEM_LIMIT),
        name="attention",
    )(qs, ak, av, bias2)


def _mid_kernel(og_ref, oa_ref, x_ref, wo_ref, g2_ref, rwh_ref, rwl_ref, rb_ref,
                xm_ref, h_ref, meta_ref, metat_ref, gate_ref, cnt_ref):
    t, d = x_ref.shape
    mt = MOE_TILE
    tiles = [slice(s * mt, (s + 1) * mt) for s in range(t // mt)]
    lane = lax.broadcasted_iota(jnp.int32, (mt, LANE), 1)
    lane_f = lane.astype(f32)
    row = lax.broadcasted_iota(jnp.int32, (mt, mt), 0)
    col = lax.broadcasted_iota(jnp.int32, (mt, mt), 1)
    earlier = jnp.where(col < row, 1.0, 0.0).astype(bf16)
    e_r = lax.broadcasted_iota(jnp.int32, (LANE, LANE), 0)
    e_c = lax.broadcasted_iota(jnp.int32, (LANE, LANE), 1)
    before = jnp.where(e_r < e_c, 1.0, 0.0).astype(bf16)

    def project(rows):
        xm = (x_ref[rows, :] + _dot(og_ref[rows, :], wo_ref[0:GLA_V_W, :])
              + _dot(oa_ref[rows, :], wo_ref[GLA_V_W:, :]))
        xm_ref[rows, :] = xm
        h_hi, h_lo = _split_bf16(_rms(xm, g2_ref[...]))
        h_ref[rows, :] = h_hi
        return h_hi, h_lo

    def route(h_hi, h_lo):
        return (_dot(h_hi, rwh_ref[...]) + _dot(h_lo, rwh_ref[...]) + _dot(h_hi, rwl_ref[...])
                + rb_ref[...])

    def top_k(logits):
        l = jnp.where(lane < N_EXPERTS, logits, -jnp.inf)
        vals, onehots = [], []
        for _ in range(TOP_K):
            m = jnp.max(l, axis=-1, keepdims=True)
            ik = jnp.min(jnp.where(l == m, lane_f, float(LANE)), axis=-1, keepdims=True)
            vals.append(m)
            onehots.append(lane_f == ik)
            l = jnp.where(onehots[-1], -jnp.inf, l)
        es = [jnp.exp(v - vals[0]) for v in vals]
        den = es[0] + es[1] + es[2] + es[3]
        sel = jnp.zeros((mt, LANE), f32)
        for oh in onehots:
            sel = sel + jnp.where(oh, 1.0, 0.0)
        return onehots, [e / den for e in es], sel

    def slots(s, rows, onehots, gate_vals, sel):
        prefix = _dot(earlier, sel.astype(bf16))
        c_row = jnp.sum(sel, axis=0, keepdims=True)
        cnt_ref[0, s:s + 1, :] = c_row
        padded = jnp.ceil(c_row * (1.0 / DMA_ROWS)) * DMA_ROWS
        start_row = _dot(jnp.broadcast_to(padded, (SUBLANE, LANE)).astype(bf16), before)[0:1]
        slot_base = prefix + start_row
        meta = jnp.zeros((mt, LANE), f32)
        gates = jnp.zeros((mt, LANE), f32)
        for k in range(TOP_K):
            slot_k = jnp.sum(jnp.where(onehots[k], slot_base, 0.0), axis=-1, keepdims=True)
            meta = jnp.where(lane == k, slot_k, meta)
            gates = jnp.where(lane == k, gate_vals[k], gates)
        meta_ref[rows, :] = meta.astype(jnp.int32)
        metat_ref[:, rows] = meta.T[0:SUBLANE, :].astype(jnp.int32)
        gate_ref[rows, :] = gates

    hs = [project(rows) for rows in tiles]
    logits = [route(*h) for h in hs]
    routed = [top_k(lg) for lg in logits]
    for s, rows in enumerate(tiles):
        slots(s, rows, *routed[s])


def _mid(o_gla, o_att, x2, wo, g2, rwh, rwl, rb):
    n, d = x2.shape
    sub = MID_TILE // MOE_TILE
    tile = lambda w: pl.BlockSpec((MID_TILE, w), lambda i: (i, 0))
    full = lambda a: pl.BlockSpec(a.shape, lambda i: (0,) * a.ndim)
    return pl.pallas_call(
        _mid_kernel,
        grid=(n // MID_TILE,),
        in_specs=[tile(GLA_V_W), tile(ATT_W), tile(d), full(wo), full(g2), full(rwh), full(rwl),
                  full(rb)],
        out_specs=[tile(d), tile(d), tile(LANE),
                   pl.BlockSpec((SUBLANE, MID_TILE), lambda i: (0, i)), tile(LANE),
                   pl.BlockSpec((1, sub, LANE), lambda i: (i, 0, 0))],
        out_shape=[jax.ShapeDtypeStruct((n, d), f32),
                   jax.ShapeDtypeStruct((n, d), bf16),
                   jax.ShapeDtypeStruct((n, LANE), jnp.int32),
                   jax.ShapeDtypeStruct((SUBLANE, n), jnp.int32),
                   jax.ShapeDtypeStruct((n, LANE), f32),
                   jax.ShapeDtypeStruct((n // MID_TILE, sub, LANE), f32)],
        compiler_params=pltpu.CompilerParams(
            dimension_semantics=("arbitrary",), vmem_limit_bytes=VMEM_LIMIT),
        name="mid",
    )(o_gla, o_att, x2, wo, g2, rwh, rwl, rb)


def _slot_matrix(slots, values, slot_axis):
    n_tokens = slots[0].shape[1 - slot_axis]
    shape = (LOCAL_ROWS, n_tokens) if slot_axis == 0 else (n_tokens, LOCAL_ROWS)
    iota_shape = (LOCAL_ROWS, 1) if slot_axis == 0 else (1, LOCAL_ROWS)
    pos = lax.broadcasted_iota(jnp.int32, iota_shape, slot_axis)
    out = jnp.zeros(shape, f32)
    for slot_k, value_k in zip(slots, values):
        out = jnp.where(pos == slot_k, value_k, out)
    return out


def _pack_bf16_pairs(a, b):
    return (pltpu.bitcast(a, jnp.uint32) >> 16) | (pltpu.bitcast(b, jnp.uint32) & jnp.uint32(0xFFFF0000))


def _unpack_bf16_pairs(w):
    a = pltpu.bitcast(w << 16, f32).astype(bf16)
    b = pltpu.bitcast(w & jnp.uint32(0xFFFF0000), f32).astype(bf16)
    return a, b


def _piece_copy(src_ref, dst_ref, src_row, dst_row, sem):
    return pltpu.make_async_copy(src_ref.at[pl.ds(pl.multiple_of(src_row, DMA_ROWS), DMA_ROWS)],
                                 dst_ref.at[pl.ds(pl.multiple_of(dst_row, DMA_ROWS), DMA_ROWS)], sem)


def _start_guaranteed_pieces(piece):
    for q in range(MIN_PIECES):
        piece(q).start()


def _start_remaining_pieces(n, piece):
    lax.fori_loop(MIN_PIECES, n, lambda q, c: (piece(q).start(), c)[1], 0)


def _wait_pieces(n, piece, bulk):
    bulk.wait()
    lax.fori_loop(MIN_PIECES, n, lambda q, c: (piece(q).wait(), c)[1], 0)


def _zero_fill_padding(pad_start_ref, pad_pieces_ref, n_used_ref, xb_ref, z_ref, sem):
    z_ref[...] = jnp.zeros_like(z_ref)
    n_blocks = xb_ref.shape[0] // MOE_BLOCK

    def tail_piece(e, q):
        return _piece_copy(z_ref, xb_ref, 0, pad_start_ref[e] + q * DMA_ROWS, sem)

    def block_copy(b):
        row = pl.multiple_of(b * MOE_BLOCK, MOE_BLOCK)
        return pltpu.make_async_copy(z_ref, xb_ref.at[pl.ds(row, MOE_BLOCK)], sem)

    def each_tail_piece(fn):
        def per_expert(e, c):
            lax.fori_loop(0, pad_pieces_ref[e], lambda q, cc: (fn(tail_piece(e, q)), cc)[1], 0)
            return c
        lax.fori_loop(0, N_EXPERTS, per_expert, 0)

    def each_block(fn):
        lax.fori_loop(n_used_ref[0], n_blocks, lambda b, c: (fn(block_copy(b)), c)[1], 0)

    each_tail_piece(lambda cp: cp.start())
    each_block(lambda cp: cp.start())
    each_tail_piece(lambda cp: cp.wait())
    each_block(lambda cp: cp.wait())


def _dispatch_kernel(nq_ref, pad_start_ref, pad_pieces_ref, n_used_ref, dst_ref, dstp_ref, metat_ref,
                     h_ref, xb_ref, l_ref, z_ref, sem):
    i = pl.program_id(0)
    last = pl.num_programs(0) - 1
    hw = l_ref.shape[2]
    cur = i % 2
    prv = 1 - cur
    n_prev = nq_ref[jnp.maximum(i - 1, 0)]
    min_rows = MIN_PIECES * DMA_ROWS

    def local_sort():
        slots = [metat_ref[k:k + 1, :] for k in range(TOP_K)]
        perm = _slot_matrix(slots, [1.0] * TOP_K, slot_axis=0).astype(bf16)
        return _pack_bf16_pairs(_dot(perm, h_ref[:, 0:hw]), _dot(perm, h_ref[:, hw:]))

    def piece(table_ref, buf):
        return lambda q: _piece_copy(l_ref.at[buf], xb_ref, q * DMA_ROWS, table_ref[0, 0, q],
                                     sem.at[buf])

    def wait(n, table_ref, buf):
        bulk = pltpu.make_async_copy(l_ref.at[buf, 0:min_rows], xb_ref.at[0:min_rows], sem.at[buf])
        _wait_pieces(n, piece(table_ref, buf), bulk)

    @pl.when(i == 0)
    def _():
        l_ref[cur] = local_sort()

    @pl.when(i > 0)
    def _():
        _start_guaranteed_pieces(piece(dstp_ref, prv))
        packed = local_sort()

        @pl.when(i > 1)
        def _():
            wait(nq_ref[jnp.maximum(i - 2, 0)], dstp_ref, cur)

        l_ref[cur] = packed
        _start_remaining_pieces(n_prev, piece(dstp_ref, prv))

    @pl.when(i == last)
    def _():
        _start_guaranteed_pieces(piece(dst_ref, cur))
        _start_remaining_pieces(nq_ref[i], piece(dst_ref, cur))
        _zero_fill_padding(pad_start_ref, pad_pieces_ref, n_used_ref, xb_ref, z_ref, sem.at[2])

        @pl.when(i > 0)
        def _():
            wait(n_prev, dstp_ref, prv)

        wait(nq_ref[i], dst_ref, cur)


def _dispatch(nq, pad_start, pad_pieces, n_used, dst, meta_t, h, n_rows):
    n, d = h.shape
    hw = d // 2
    t = MOE_TILE
    grid_spec = pltpu.PrefetchScalarGridSpec(
        num_scalar_prefetch=4,
        grid=(n // t,),
        in_specs=[pl.BlockSpec((1, 1, LOCAL_PIECES), lambda i, *_: (i, 0, 0), memory_space=pltpu.SMEM),
                  pl.BlockSpec((1, 1, LOCAL_PIECES), lambda i, *_: (jnp.maximum(i - 1, 0), 0, 0),
                               memory_space=pltpu.SMEM),
                  pl.BlockSpec((SUBLANE, t), lambda i, *_: (0, i)),
                  pl.BlockSpec((t, d), lambda i, *_: (i, 0))],
        out_specs=pl.BlockSpec(memory_space=pl.ANY),
        scratch_shapes=[pltpu.VMEM((2, LOCAL_ROWS, hw), jnp.uint32),
                        pltpu.VMEM((MOE_BLOCK, hw), jnp.uint32),
                        pltpu.SemaphoreType.DMA((3,))],
    )
    return pl.pallas_call(
        _dispatch_kernel,
        grid_spec=grid_spec,
        out_shape=jax.ShapeDtypeStruct((n_rows, hw), jnp.uint32),
        compiler_params=pltpu.CompilerParams(
            dimension_semantics=("arbitrary",), vmem_limit_bytes=VMEM_LIMIT),
        name="dispatch",
    )(nq, pad_start, pad_pieces, n_used, dst, dst, meta_t, h)


def _expert_kernel(be_ref, ns_ref, x_ref, wi_ref, bi_ref, wo_ref, bo_ref, y_ref,
                   wib_ref, wob_ref):
    b = pl.program_id(0)
    blk, hw = x_ref.shape
    dff = wo_ref.shape[1]
    subs = [slice(s * EXPERT_SUB_BLOCK, (s + 1) * EXPERT_SUB_BLOCK)
            for s in range(blk // EXPERT_SUB_BLOCK)]
    n_live = ns_ref[b]

    @pl.when(jnp.logical_and(n_live > 0,
                             jnp.logical_or(b == 0, be_ref[b] != be_ref[jnp.maximum(b - 1, 0)])))
    def _():
        for j in range(dff // LANE):
            wib_ref[:, 2 * j * LANE:(2 * j + 1) * LANE] = wi_ref[0, :, j * LANE:(j + 1) * LANE].astype(bf16)
            wib_ref[:, (2 * j + 1) * LANE:(2 * j + 2) * LANE] = (
                wi_ref[0, :, dff + j * LANE:dff + (j + 1) * LANE].astype(bf16))
        wob_ref[...] = wo_ref[0].astype(bf16)

    def up(rows):
        xa, xb = _unpack_bf16_pairs(x_ref[rows, :])
        return _dot(xa, wib_ref[0:hw, :]) + _dot(xb, wib_ref[hw:, :]) + bi_ref[0]

    def down(rows, hc):
        acts = []
        for j in range(dff // LANE):
            glu = jnp.minimum(hc[:, 2 * j * LANE:(2 * j + 1) * LANE], SWIGLU_LIMIT)
            lin = jnp.clip(hc[:, (2 * j + 1) * LANE:(2 * j + 2) * LANE], -SWIGLU_LIMIT, SWIGLU_LIMIT)
            acts.append((glu * (1.0 / (1.0 + jnp.exp(-SWIGLU_ALPHA * glu))) * (lin + 1.0)).astype(bf16))
        act = jnp.concatenate(acts, axis=1)
        y = (_dot(act, wob_ref[...]) + bo_ref[0]).astype(bf16).astype(f32)
        y_ref[rows, :] = _pack_bf16_pairs(y[:, 0:hw], y[:, hw:])

    def run(n):
        if n > 0:
            hc_next = up(subs[0])
        for s in range(n):
            hc = hc_next
            if s + 1 < n:
                hc_next = up(subs[s + 1])
            down(subs[s], hc)
        for rows in subs[n:]:
            y_ref[rows, :] = jnp.zeros((EXPERT_SUB_BLOCK, hw), y_ref.dtype)

    for n in range(len(subs) + 1):
        pl.when(n_live == n)(functools.partial(run, n))


def _experts(block_expert, block_live, xb, w_in, b_in, w_out, b_out):
    n_rows, hw = xb.shape
    e, d, dff2 = w_in.shape
    dff = w_out.shape[1]
    n_blocks = n_rows // MOE_BLOCK
    b_in_interleaved = b_in.reshape(e, 2, dff // LANE, LANE).transpose(0, 2, 1, 3)
    grid_spec = pltpu.PrefetchScalarGridSpec(
        num_scalar_prefetch=2,
        grid=(n_blocks,),
        in_specs=[
            pl.BlockSpec((MOE_BLOCK, hw), lambda b, be, nu: (b, 0)),
            pl.BlockSpec((1, d, dff2), lambda b, be, nu: (be[b], 0, 0)),
            pl.BlockSpec((1, 1, dff2), lambda b, be, nu: (be[b], 0, 0)),
            pl.BlockSpec((1, dff, d), lambda b, be, nu: (be[b], 0, 0)),
            pl.BlockSpec((1, 1, d), lambda b, be, nu: (be[b], 0, 0)),
        ],
        out_specs=pl.BlockSpec((MOE_BLOCK, hw), lambda b, be, nu: (b, 0)),
        scratch_shapes=[pltpu.VMEM((d, dff2), bf16), pltpu.VMEM((dff, d), bf16)],
    )
    return pl.pallas_call(
        _expert_kernel,
        grid_spec=grid_spec,
        out_shape=jax.ShapeDtypeStruct((n_rows, hw), jnp.uint32),
        compiler_params=pltpu.CompilerParams(
            dimension_semantics=("arbitrary",), vmem_limit_bytes=VMEM_LIMIT),
        name="experts",
    )(block_expert, block_live, xb, w_in, b_in_interleaved.reshape(e, 1, dff2), w_out,
      b_out.reshape(e, 1, d))


def _combine_kernel(nq_ref, dst_ref, dstn_ref, meta_ref, gate_ref, xm_ref, y_ref, o_ref,
                    ly_ref, sem):
    i = pl.program_id(0)
    last = pl.num_programs(0) - 1
    hw = ly_ref.shape[2]
    cur = i % 2
    nxt = 1 - cur
    next_tile = jnp.minimum(i + 1, last)
    min_rows = MIN_PIECES * DMA_ROWS

    def piece(table_ref, buf):
        return lambda q: _piece_copy(y_ref, ly_ref.at[buf], table_ref[0, 0, q], q * DMA_ROWS,
                                     sem.at[buf])

    def zero_tail(buf):
        ly_ref[buf, MOE_TILE * TOP_K:, :] = jnp.zeros((LOCAL_ROWS - MOE_TILE * TOP_K, hw), jnp.uint32)

    def wait(n, buf):
        bulk = pltpu.make_async_copy(y_ref.at[0:min_rows], ly_ref.at[buf, 0:min_rows], sem.at[buf])
        _wait_pieces(n, piece(dst_ref, buf), bulk)

    @pl.when(i == 0)
    def _():
        zero_tail(cur)
        _start_guaranteed_pieces(piece(dst_ref, cur))
        _start_remaining_pieces(nq_ref[i], piece(dst_ref, cur))

    zero_tail(nxt)
    _start_remaining_pieces(nq_ref[next_tile], piece(dstn_ref, nxt))
    _start_guaranteed_pieces(piece(dstn_ref, nxt))
    gates = gate_ref[...]
    meta = meta_ref[...]
    g = _slot_matrix([meta[:, k:k + 1] for k in range(TOP_K)],
                     [gates[:, k:k + 1] for k in range(TOP_K)], slot_axis=1).astype(bf16)
    wait(nq_ref[i], cur)
    ya, yb = _unpack_bf16_pairs(ly_ref[cur])
    o_ref[:, 0:hw] = xm_ref[:, 0:hw] + _dot(g, ya)
    o_ref[:, hw:] = xm_ref[:, hw:] + _dot(g, yb)

    @pl.when(i == last)
    def _():
        wait(nq_ref[next_tile], nxt)


def _combine(nq, dst, meta, gates, xm, y):
    n, d = xm.shape
    hw = d // 2
    t = MOE_TILE
    n_tiles = n // t
    grid_spec = pltpu.PrefetchScalarGridSpec(
        num_scalar_prefetch=1,
        grid=(n_tiles,),
        in_specs=[pl.BlockSpec((1, 1, LOCAL_PIECES), lambda i, nq: (i, 0, 0), memory_space=pltpu.SMEM),
                  pl.BlockSpec((1, 1, LOCAL_PIECES), lambda i, nq: (jnp.minimum(i + 1, n_tiles - 1), 0, 0),
                               memory_space=pltpu.SMEM),
                  pl.BlockSpec((t, LANE), lambda i, nq: (i, 0)),
                  pl.BlockSpec((t, LANE), lambda i, nq: (i, 0)),
                  pl.BlockSpec((t, d), lambda i, nq: (i, 0)),
                  pl.BlockSpec(memory_space=pl.ANY)],
        out_specs=pl.BlockSpec((t, d), lambda i, nq: (i, 0)),
        scratch_shapes=[pltpu.VMEM((2, LOCAL_ROWS, hw), jnp.uint32), pltpu.SemaphoreType.DMA((2,))],
    )
    return pl.pallas_call(
        _combine_kernel,
        grid_spec=grid_spec,
        out_shape=jax.ShapeDtypeStruct((n, d), f32),
        compiler_params=pltpu.CompilerParams(
            dimension_semantics=("arbitrary",), vmem_limit_bytes=VMEM_LIMIT),
        name="combine",
    )(nq, dst, dst, meta, gates, xm, y)


def _rel_bias_table(rel_bias):
    band = BAND_CHUNKS * CHUNK
    width = band + CHUNK
    dist = (np.arange(width) - CHUNK)[::-1]
    ext = rel_bias[:, np.clip(dist, -REL_CLIP, REL_CLIP) + REL_CLIP].astype(f32)
    heads = ext.shape[0]
    tiled = jnp.broadcast_to(ext[:, None, :], (heads, CHUNK, width)).reshape(heads, CHUNK * width)
    skewed = tiled[:, :CHUNK * (width - 1)].reshape(heads, CHUNK, width - 1)
    return skewed[:, :, CHUNK - 1:CHUNK - 1 + band]


def _round_up(x, m):
    return (x + m - 1) // m * m


def _routing_tables(cnt, n_tokens):
    experts = jnp.arange(N_EXPERTS, dtype=jnp.int32)
    c = cnt.reshape(-1, LANE)[:, :N_EXPERTS].astype(jnp.int32)
    n_tiles = c.shape[0]
    cp = _round_up(c, DMA_ROWS)
    lend = jnp.cumsum(cp, axis=1)
    lstart = lend - cp
    nq = (lend[:, -1] // DMA_ROWS).astype(jnp.int32)
    region = jnp.sum(cp, axis=0)
    padded = _round_up(region, MOE_BLOCK)
    pend = jnp.cumsum(padded)
    pstart = pend - padded
    base = pstart[None, :] + jnp.cumsum(cp, axis=0) - cp
    q0 = jnp.arange(LOCAL_PIECES, dtype=jnp.int32) * DMA_ROWS
    e_q = jnp.minimum(jnp.sum(lend[:, None, :] <= q0[None, :, None], axis=-1), N_EXPERTS - 1)
    shift = jnp.sum(jnp.where(e_q[:, :, None] == experts, (base - lstart)[:, None, :], 0), axis=-1)
    dst = jnp.where(q0[None, :] < lend[:, -1:], shift + q0[None, :], 0).astype(jnp.int32)

    n_blocks = -(-(n_tokens * TOP_K + n_tiles * N_EXPERTS * (DMA_ROWS - 1)) // MOE_BLOCK) + N_EXPERTS
    blk0 = jnp.arange(n_blocks, dtype=jnp.int32) * MOE_BLOCK
    n_used = (pend[-1] // MOE_BLOCK).astype(jnp.int32)
    be = jnp.minimum(jnp.sum(pend[None, :] <= blk0[:, None], axis=1), N_EXPERTS - 1).astype(jnp.int32)
    last = jnp.sum(jnp.where(jnp.arange(n_blocks) == n_used - 1, be, 0))
    be = jnp.where(jnp.arange(n_blocks) < n_used, be, last)
    onehot_be = be[:, None] == experts
    region_end = jnp.sum(jnp.where(onehot_be, pstart + region, 0), axis=1)
    live_rows = jnp.clip(region_end - blk0, 0, MOE_BLOCK)
    live = jnp.where(jnp.arange(n_blocks) < n_used, -(-live_rows // EXPERT_SUB_BLOCK), 0)
    pad_start = (pstart + region).astype(jnp.int32)
    pad_pieces = ((padded - region) // DMA_ROWS).astype(jnp.int32)
    return (nq, pad_start, pad_pieces, dst.reshape(n_tiles, 1, LOCAL_PIECES), be,
            live.astype(jnp.int32), n_used.reshape(1), n_blocks)


def _layer(x, norm1_g, w_in, gate_up, gate_bias, gla_norm_g, q_norm_g, k_norm_g, rel_bias, w_out,
           norm2_g, router_w, router_b, moe_w_in, moe_b_in, moe_w_out, moe_b_out):
    batch, seq, d = x.shape
    n = batch * seq
    x2 = x.reshape(n, d)

    pieces = jnp.split(w_in, np.cumsum(IN_SIZES)[:-1].tolist(), axis=-1)
    pieces[3] = jnp.pad(pieces[3], ((0, 0), (0, LANE - GLA_GATE_RANK)))
    widths = [p.shape[-1] for p in pieces]
    w_all = jnp.concatenate(pieces, axis=-1).astype(bf16)
    tile2 = lambda g: jnp.tile(g.reshape(1, -1), (1, LANE // ATT_DH))
    gq, gk, gv, glr, gog, qs, ak, av = _inproj(x2, norm1_g.reshape(1, d), w_all, widths,
                                               tile2(q_norm_g), tile2(k_norm_g))

    gup = jnp.pad(gate_up, ((0, LANE - GLA_GATE_RANK), (0, 0))).astype(bf16)
    o_gla = _gla(gq, gk, gv, glr, gog, gup, gate_bias.reshape(1, -1), gla_norm_g.reshape(1, -1),
                 batch, seq)
    o_att = _attention(qs, ak, av, _rel_bias_table(rel_bias), batch, seq)

    rw = jnp.pad(router_w, ((0, 0), (0, LANE - N_EXPERTS)))
    rw_hi = rw.astype(bf16)
    rw_lo = (rw - rw_hi.astype(f32)).astype(bf16)
    rb = jnp.pad(router_b, (0, LANE - N_EXPERTS)).reshape(1, LANE)
    xm, h2, meta, meta_t, gates, cnt = _mid(o_gla, o_att, x2, w_out.astype(bf16),
                                            norm2_g.reshape(1, d), rw_hi, rw_lo, rb)

    nq, pad_start, pad_pieces, dst, be, live, n_used, n_blocks = _routing_tables(cnt, n)
    xb = _dispatch(nq, pad_start, pad_pieces, n_used, dst, meta_t, h2, n_blocks * MOE_BLOCK)
    y = _experts(be, live, xb, moe_w_in, moe_b_in, moe_w_out, moe_b_out)
    out = _combine(nq, dst, meta, gates, xm, y)
    return out.reshape(batch, seq, d)


def kernel(x, norm1_g, w_in, gla_gate_up, gla_gate_bias, gla_norm_g, q_norm_g, k_norm_g, rel_bias, w_out, norm2_g, router_w, router_b, moe_w_in, moe_b_in, moe_w_out, moe_b_out):
    for l in range(norm1_g.shape[0]):
        x = _layer(x, norm1_g[l], w_in[l], gla_gate_up[l], gla_gate_bias[l], gla_norm_g[l],
                   q_norm_g[l], k_norm_g[l], rel_bias[l], w_out[l], norm2_g[l], router_w[l],
                   router_b[l], moe_w_in[l], moe_b_in[l], moe_w_out[l], moe_b_out[l])
    return x
```

```python
import functools

import numpy as np
import jax
import jax.numpy as jnp
from jax import lax
from jax.experimental import pallas as pl
from jax.experimental.pallas import tpu as pltpu

CHUNK = 64
EPS = 1e-6
GLA_HEADS = 4
GLA_DK = 64
GLA_DV = 128
GLA_GATE_RANK = 16
GLA_GATE_TAU = 16.0
ATT_HEADS = 8
ATT_DH = 64
N_BACK_CHUNKS = 8
BAND_CHUNKS = N_BACK_CHUNKS + 1
REL_CLIP = 256
MASK_VALUE = -1e30
N_EXPERTS = 32
TOP_K = 4
SWIGLU_ALPHA = 1.702
SWIGLU_LIMIT = 7.0
MOE_BLOCK = 1024
EXPERT_SUB_BLOCK = 256

LANE = 128
SUBLANE = 8
GLA_QK_W = GLA_HEADS * GLA_DK
GLA_V_W = GLA_HEADS * GLA_DV
ATT_W = ATT_HEADS * ATT_DH
IN_SIZES = (GLA_QK_W, GLA_QK_W, GLA_V_W, GLA_GATE_RANK, GLA_V_W, ATT_W, ATT_W, ATT_W)
SEQ_TILE = N_BACK_CHUNKS * CHUNK
ROW_TILE = 512
MID_TILE = 1024
MOE_TILE = 256
DMA_ROWS = SUBLANE
LOCAL_PIECES = 160
LOCAL_ROWS = LOCAL_PIECES * DMA_ROWS
MIN_PIECES = MOE_TILE * TOP_K // DMA_ROWS
VMEM_LIMIT = 48 * 1024 * 1024

f32 = jnp.float32
bf16 = jnp.bfloat16


def _rms(x, g):
    return x * lax.rsqrt(jnp.mean(x * x, axis=-1, keepdims=True) + EPS) * g


def _dot(a, b):
    return jnp.dot(a, b, preferred_element_type=f32)


def _dot_nt(a, b):
    return lax.dot_general(a, b, (((1,), (1,)), ((), ())), preferred_element_type=f32)


def _dot_tn(a, b):
    return lax.dot_general(a, b, (((0,), (0,)), ((), ())), preferred_element_type=f32)


def _split_bf16(x):
    hi = x.astype(bf16)
    lo = (x - hi.astype(f32)).astype(bf16)
    return hi, lo


def _head_norm(x, g):
    lo = lax.broadcasted_iota(jnp.int32, (1, LANE), 1) < ATT_DH
    sq = x * x
    s0 = jnp.sum(jnp.where(lo, sq, 0.0), axis=-1, keepdims=True)
    s1 = jnp.sum(jnp.where(lo, 0.0, sq), axis=-1, keepdims=True)
    r = jnp.where(lo, lax.rsqrt(s0 * (1.0 / ATT_DH) + EPS), lax.rsqrt(s1 * (1.0 / ATT_DH) + EPS))
    return x * r * g


def _inproj_kernel(x_ref, g_ref, w_ref, qg_ref, kg_ref, gq, gk, gv, glr, gog, qs_ref, ak, av):
    t = x_ref.shape[0]
    n_pairs = ATT_W // LANE
    h = _rms(x_ref[...], g_ref[...]).astype(bf16)
    offsets = np.cumsum((0,) + tuple(r.shape[-1] for r in (gq, gk, gv, glr, gog)) + (ATT_W, ATT_W))
    lo = lax.broadcasted_iota(jnp.int32, (1, LANE), 1) < ATT_DH

    q_all = _dot(h, w_ref[:, offsets[5]:offsets[5] + ATT_W])
    k_all = _dot(h, w_ref[:, offsets[6]:offsets[6] + ATT_W])
    for p in range(n_pairs):
        pair = slice(p * LANE, (p + 1) * LANE)
        qn = _head_norm(q_all[:, pair], qg_ref[...]) * (ATT_DH ** -0.5)
        q_lo = jnp.where(lo, qn, 0.0).astype(bf16)
        q_hi = jnp.where(lo, 0.0, qn).astype(bf16)
        for c in range(t // CHUNK):
            rows = slice(c * CHUNK, (c + 1) * CHUNK)
            qs_ref[c * n_pairs + p, 0:CHUNK, :] = q_lo[rows]
            qs_ref[c * n_pairs + p, CHUNK:2 * CHUNK, :] = q_hi[rows]
        ak[:, pair] = _head_norm(k_all[:, pair], kg_ref[...]).astype(ak.dtype)
    for o_ref, off in zip((gq, gk, gv, glr, gog, av), tuple(offsets[:5]) + (offsets[7],)):
        o_ref[...] = _dot(h, w_ref[:, off:off + o_ref.shape[-1]]).astype(o_ref.dtype)


def _inproj(x2, g, w, widths, qg, kg):
    n, d = x2.shape
    n_pairs = ATT_W // LANE
    blocks_per_tile = ROW_TILE // CHUNK * n_pairs
    rows = lambda wd: pl.BlockSpec((ROW_TILE, wd), lambda i: (i, 0))
    full = lambda a: pl.BlockSpec(a.shape, lambda i: (0,) * a.ndim)
    plain = lambda wd: jax.ShapeDtypeStruct((n, wd), bf16)
    return pl.pallas_call(
        _inproj_kernel,
        grid=(n // ROW_TILE,),
        in_specs=[rows(d), full(g), full(w), full(qg), full(kg)],
        out_specs=[rows(wd) for wd in widths[:5]]
        + [pl.BlockSpec((blocks_per_tile, 2 * CHUNK, LANE), lambda i: (i, 0, 0)), rows(ATT_W), rows(ATT_W)],
        out_shape=[plain(wd) for wd in widths[:5]]
        + [jax.ShapeDtypeStruct((n // CHUNK * n_pairs, 2 * CHUNK, LANE), bf16), plain(ATT_W), plain(ATT_W)],
        compiler_params=pltpu.CompilerParams(
            dimension_semantics=("arbitrary",), vmem_limit_bytes=VMEM_LIMIT),
        name="inproj",
    )(x2, g, w, qg, kg)


def _gla_kernel(q_ref, k_ref, v_ref, lr_ref, og_ref, gup_ref, gb_ref, ng_ref, o_ref, st_ref,
                u_ref, sb_ref):
    t = q_ref.shape[0]

    @pl.when(pl.program_id(1) == 0)
    def _():
        st_ref[...] = jnp.zeros_like(st_ref)

    z = _dot(lr_ref[...], gup_ref[...]) + gb_ref[...]
    log_a = (jnp.minimum(z, 0.0) - jnp.log1p(jnp.exp(-jnp.abs(z)))) * (1.0 / GLA_GATE_TAU)
    row = lax.broadcasted_iota(jnp.int32, (t, t), 0)
    col = lax.broadcasted_iota(jnp.int32, (t, t), 1)
    tri = jnp.where((col <= row) & ((col // CHUNK) == (row // CHUNK)), 1.0, 0.0).astype(bf16)
    la_hi, la_lo = _split_bf16(log_a)
    cum_all = _dot(tri, la_hi) + _dot(tri, la_lo)
    lane = lax.broadcasted_iota(jnp.int32, (1, LANE), 1)
    half_mask = (lane < GLA_DK, lane >= GLA_DK)
    n_chunks = t // CHUNK
    chunk_rows = [slice(c * CHUNK, (c + 1) * CHUNK) for c in range(n_chunks)]
    pair_of = lambda h: slice((h // 2) * LANE, (h // 2 + 1) * LANE)
    head_of = lambda h: slice(h * GLA_DV, (h + 1) * GLA_DV)

    decs = []
    for c, rows in enumerate(chunk_rows):
        cum = cum_all[rows]
        tot = cum[CHUNK - 1:CHUNK]
        kdec = k_ref[rows, :].astype(f32) * jnp.exp(tot - cum)
        decs.append(jnp.exp(tot))
        for h in range(GLA_HEADS):
            kd = jnp.where(half_mask[h % 2], kdec[:, pair_of(h)], 0.0).astype(bf16)
            u_ref[c, h] = _dot_tn(v_ref[rows, head_of(h)], kd)

    for h in range(GLA_HEADS):
        st = st_ref[h]
        for c in range(n_chunks):
            st = st * decs[c][:, pair_of(h)] + u_ref[c, h]
            sb_ref[c, h] = st.astype(bf16)
        st_ref[h] = st

    for c, rows in enumerate(chunk_rows):
        for h in range(GLA_HEADS):
            o = _dot_nt(q_ref[rows, pair_of(h)], sb_ref[c, h]) * (GLA_DK ** -0.5)
            o = _rms(o, ng_ref[...])
            g = og_ref[rows, head_of(h)].astype(f32)
            o_ref[rows, head_of(h)] = (o * (g / (1.0 + jnp.exp(-g)))).astype(o_ref.dtype)


def _gla(gq, gk, gv, glr, gog, gup, gb, ng, batch, seq):
    nt = seq // SEQ_TILE
    tile = lambda w: pl.BlockSpec((SEQ_TILE, w), lambda b, i: (b * nt + i, 0))
    full = lambda a: pl.BlockSpec(a.shape, lambda b, i: (0,) * a.ndim)
    return pl.pallas_call(
        _gla_kernel,
        grid=(batch, nt),
        in_specs=[tile(GLA_QK_W), tile(GLA_QK_W), tile(GLA_V_W), tile(LANE), tile(GLA_V_W),
                  full(gup), full(gb), full(ng)],
        out_specs=tile(GLA_V_W),
        out_shape=jax.ShapeDtypeStruct((batch * seq, GLA_V_W), bf16),
        scratch_shapes=[pltpu.VMEM((GLA_HEADS, GLA_DV, LANE), f32),
                        pltpu.VMEM((SEQ_TILE // CHUNK, GLA_HEADS, GLA_DV, LANE), f32),
                        pltpu.VMEM((SEQ_TILE // CHUNK, GLA_HEADS, GLA_DV, LANE), bf16)],
        compiler_params=pltpu.CompilerParams(
            dimension_semantics=("arbitrary", "arbitrary"), vmem_limit_bytes=VMEM_LIMIT),
        name="gla",
    )(gq, gk, gv, glr, gog, gup, gb, ng)


def _att_kernel(qs_ref, k_ref, v_ref, bias_ref, o_ref, kb_ref, vb_ref):
    t = k_ref.shape[0]
    n_pairs = ATT_W // LANE
    band = BAND_CHUNKS * CHUNK
    first = pl.program_id(1) == 0

    @pl.when(first)
    def _():
        kb_ref[0:t, :] = jnp.zeros((t, ATT_W), bf16)
        vb_ref[0:t, :] = jnp.zeros((t, 2 * ATT_W), bf16)

    @pl.when(jnp.logical_not(first))
    def _():
        kb_ref[0:t, :] = kb_ref[t:2 * t, :]
        vb_ref[0:t, :] = vb_ref[t:2 * t, :]

    lane = lax.broadcasted_iota(jnp.int32, (1, LANE), 1)
    lo = lane < ATT_DH

    kb_ref[t:2 * t, :] = k_ref[...]
    for p in range(n_pairs):
        vb_ref[t:2 * t, 2 * p * LANE:(2 * p + 1) * LANE] = v_ref[:, p * LANE:(p + 1) * LANE]
        vb_ref[t:2 * t, (2 * p + 1) * LANE:(2 * p + 2) * LANE] = jnp.ones((t, LANE), bf16)

    colk = lax.broadcasted_iota(jnp.int32, (1, band), 1)

    def chunk_loop(masked):
        n_chunks = t // CHUNK

        def scores(c, p):
            k2 = kb_ref[c * CHUNK:c * CHUNK + band, p * LANE:(p + 1) * LANE]
            return _dot_nt(qs_ref[c * n_pairs + p], k2)

        def weighted_values(e, c, p):
            v2 = vb_ref[c * CHUNK:c * CHUNK + band, 2 * p * LANE:(2 * p + 2) * LANE]
            pvl = _dot(e, v2)
            pv = pvl[:, 0:LANE] / pvl[:, LANE:2 * LANE]
            o_ref[c * CHUNK:(c + 1) * CHUNK, p * LANE:(p + 1) * LANE] = jnp.where(
                lo, pv[0:CHUNK], pv[CHUNK:2 * CHUNK]).astype(o_ref.dtype)

        blocks = [(c, p) for c in range(n_chunks) for p in range(n_pairs)]
        s_next = scores(0, 0)
        e_prev = None
        for i, (c, p) in enumerate(blocks):
            s = s_next + bias_ref[p]
            if i + 1 < len(blocks):
                s_next = scores(*blocks[i + 1])
            if masked:
                s = jnp.where(colk >= t - c * CHUNK, s, MASK_VALUE)
            e = jnp.exp(s - jnp.max(s, axis=-1, keepdims=True)).astype(bf16)
            if i > 0:
                weighted_values(e_prev, *blocks[i - 1])
            e_prev = e
        weighted_values(e_prev, *blocks[-1])

    @pl.when(first)
    def _():
        chunk_loop(True)

    @pl.when(jnp.logical_not(first))
    def _():
        chunk_loop(False)


def _attention(qs, ak, av, bias, batch, seq):
    nt = seq // SEQ_TILE
    n_pairs = ATT_W // LANE
    tile = pl.BlockSpec((SEQ_TILE, ATT_W), lambda b, i: (b * nt + i, 0))
    q_tile = pl.BlockSpec((SEQ_TILE // CHUNK * n_pairs, 2 * CHUNK, LANE), lambda b, i: (b * nt + i, 0, 0))
    full = lambda a: pl.BlockSpec(a.shape, lambda b, i: (0,) * a.ndim)
    bias2 = bias.reshape(n_pairs, 2 * CHUNK, BAND_CHUNKS * CHUNK)
    return pl.pallas_call(
        _att_kernel,
        grid=(batch, nt),
        in_specs=[q_tile, tile, tile, full(bias2)],
        out_specs=tile,
        out_shape=jax.ShapeDtypeStruct((batch * seq, ATT_W), bf16),
        scratch_shapes=[pltpu.VMEM((2 * SEQ_TILE, ATT_W), bf16),
                        pltpu.VMEM((2 * SEQ_TILE, 2 * ATT_W), bf16)],
        compiler_params=pltpu.CompilerParams(
            dimension_semantics=("arbitrary", "arbitrary"), vmem_limit_bytes=VMEM_LIMIT),
        name="attention",
    )(qs, ak, av, bias2)


def _mid_kernel(og_ref, oa_ref, x_ref, wo_ref, g2_ref, rw_ref, rb_ref,
                xm_ref, h_ref, meta_ref, metat_ref, gate_ref, cnt_ref):
    t, d = x_ref.shape
    mt = MOE_TILE
    tiles = [slice(s * mt, (s + 1) * mt) for s in range(t // mt)]
    lane = lax.broadcasted_iota(jnp.int32, (mt, LANE), 1)
    lane_f = lane.astype(f32)
    row = lax.broadcasted_iota(jnp.int32, (mt, mt), 0)
    col = lax.broadcasted_iota(jnp.int32, (mt, mt), 1)
    earlier = jnp.where(col < row, 1.0, 0.0).astype(bf16)
    e_r = lax.broadcasted_iota(jnp.int32, (LANE, LANE), 0)
    e_c = lax.broadcasted_iota(jnp.int32, (LANE, LANE), 1)
    before = jnp.where(e_r < e_c, 1.0, 0.0).astype(bf16)

    def project(rows):
        xm = (x_ref[rows, :] + _dot(og_ref[rows, :], wo_ref[0:GLA_V_W, :])
              + _dot(oa_ref[rows, :], wo_ref[GLA_V_W:, :]))
        xm_ref[rows, :] = xm
        h_hi, h_lo = _split_bf16(_rms(xm, g2_ref[...]))
        h_ref[rows, :] = h_hi
        return h_hi, h_lo

    def route(h_hi, h_lo):
        both = _dot(h_hi, rw_ref[...])
        return both[:, 0:LANE] + both[:, LANE:2 * LANE] + _dot(h_lo, rw_ref[:, 0:LANE]) + rb_ref[...]

    def top_k(logits):
        l = jnp.where(lane < N_EXPERTS, logits, -jnp.inf)
        vals, onehots = [], []
        for _ in range(TOP_K):
            m = jnp.max(l, axis=-1, keepdims=True)
            ik = jnp.min(jnp.where(l == m, lane_f, float(LANE)), axis=-1, keepdims=True)
            vals.append(m)
            onehots.append(lane_f == ik)
            l = jnp.where(onehots[-1], -jnp.inf, l)
        es = [jnp.exp(v - vals[0]) for v in vals]
        den = es[0] + es[1] + es[2] + es[3]
        sel = jnp.zeros((mt, LANE), f32)
        for oh in onehots:
            sel = sel + jnp.where(oh, 1.0, 0.0)
        return onehots, [e / den for e in es], sel

    def slots(s, rows, onehots, gate_vals, sel):
        prefix = _dot(earlier, sel.astype(bf16))
        c_row = jnp.sum(sel, axis=0, keepdims=True)
        cnt_ref[0, s:s + 1, :] = c_row
        padded = jnp.ceil(c_row * (1.0 / DMA_ROWS)) * DMA_ROWS
        start_row = _dot(jnp.broadcast_to(padded, (SUBLANE, LANE)).astype(bf16), before)[0:1]
        slot_base = prefix + start_row
        meta = jnp.zeros((mt, LANE), f32)
        gates = jnp.zeros((mt, LANE), f32)
        for k in range(TOP_K):
            slot_k = jnp.sum(jnp.where(onehots[k], slot_base, 0.0), axis=-1, keepdims=True)
            meta = jnp.where(lane == k, slot_k, meta)
            gates = jnp.where(lane == k, gate_vals[k], gates)
        meta_ref[rows, :] = meta.astype(jnp.int32)
        metat_ref[:, rows] = meta.T[0:SUBLANE, :].astype(jnp.int32)
        gate_ref[rows, :] = gates

    hs = [project(rows) for rows in tiles]
    logits = [route(*h) for h in hs]
    routed = [top_k(lg) for lg in logits]
    for s, rows in enumerate(tiles):
        slots(s, rows, *routed[s])


def _mid(o_gla, o_att, x2, wo, g2, rw, rb):
    n, d = x2.shape
    sub = MID_TILE // MOE_TILE
    tile = lambda w: pl.BlockSpec((MID_TILE, w), lambda i: (i, 0))
    full = lambda a: pl.BlockSpec(a.shape, lambda i: (0,) * a.ndim)
    return pl.pallas_call(
        _mid_kernel,
        grid=(n // MID_TILE,),
        in_specs=[tile(GLA_V_W), tile(ATT_W), tile(d), full(wo), full(g2), full(rw), full(rb)],
        out_specs=[tile(d), tile(d), tile(LANE),
                   pl.BlockSpec((SUBLANE, MID_TILE), lambda i: (0, i)), tile(LANE),
                   pl.BlockSpec((1, sub, LANE), lambda i: (i, 0, 0))],
        out_shape=[jax.ShapeDtypeStruct((n, d), f32),
                   jax.ShapeDtypeStruct((n, d), bf16),
                   jax.ShapeDtypeStruct((n, LANE), jnp.int32),
                   jax.ShapeDtypeStruct((SUBLANE, n), jnp.int32),
                   jax.ShapeDtypeStruct((n, LANE), f32),
                   jax.ShapeDtypeStruct((n // MID_TILE, sub, LANE), f32)],
        compiler_params=pltpu.CompilerParams(
            dimension_semantics=("arbitrary",), vmem_limit_bytes=VMEM_LIMIT),
        name="mid",
    )(o_gla, o_att, x2, wo, g2, rw, rb)


def _slot_matrix(slots, values, slot_axis):
    n_tokens = slots[0].shape[1 - slot_axis]
    shape = (LOCAL_ROWS, n_tokens) if slot_axis == 0 else (n_tokens, LOCAL_ROWS)
    iota_shape = (LOCAL_ROWS, 1) if slot_axis == 0 else (1, LOCAL_ROWS)
    pos = lax.broadcasted_iota(jnp.int32, iota_shape, slot_axis)
    out = jnp.zeros(shape, f32)
    for slot_k, value_k in zip(slots, values):
        out = jnp.where(pos == slot_k, value_k, out)
    return out


def _pack_bf16_pairs(a, b):
    return (pltpu.bitcast(a, jnp.uint32) >> 16) | (pltpu.bitcast(b, jnp.uint32) & jnp.uint32(0xFFFF0000))


def _unpack_bf16_pairs(w):
    a = pltpu.bitcast(w << 16, f32).astype(bf16)
    b = pltpu.bitcast(w & jnp.uint32(0xFFFF0000), f32).astype(bf16)
    return a, b


def _piece_copy(src_ref, dst_ref, src_row, dst_row, sem):
    return pltpu.make_async_copy(src_ref.at[pl.ds(pl.multiple_of(src_row, DMA_ROWS), DMA_ROWS)],
                                 dst_ref.at[pl.ds(pl.multiple_of(dst_row, DMA_ROWS), DMA_ROWS)], sem)


def _start_guaranteed_pieces(piece):
    for q in range(MIN_PIECES):
        piece(q).start()


def _start_remaining_pieces(n, piece):
    lax.fori_loop(MIN_PIECES, n, lambda q, c: (piece(q).start(), c)[1], 0)


def _wait_pieces(n, piece, bulk):
    bulk.wait()
    lax.fori_loop(MIN_PIECES, n, lambda q, c: (piece(q).wait(), c)[1], 0)


def _zero_fill_padding(pad_start_ref, pad_pieces_ref, n_used_ref, xb_ref, z_ref, sem):
    z_ref[...] = jnp.zeros_like(z_ref)
    n_blocks = xb_ref.shape[0] // MOE_BLOCK

    def tail_piece(e, q):
        return _piece_copy(z_ref, xb_ref, 0, pad_start_ref[e] + q * DMA_ROWS, sem)

    def block_copy(b):
        row = pl.multiple_of(b * MOE_BLOCK, MOE_BLOCK)
        return pltpu.make_async_copy(z_ref, xb_ref.at[pl.ds(row, MOE_BLOCK)], sem)

    def each_tail_piece(fn):
        def per_expert(e, c):
            lax.fori_loop(0, pad_pieces_ref[e], lambda q, cc: (fn(tail_piece(e, q)), cc)[1], 0)
            return c
        lax.fori_loop(0, N_EXPERTS, per_expert, 0)

    def each_block(fn):
        lax.fori_loop(n_used_ref[0], n_blocks, lambda b, c: (fn(block_copy(b)), c)[1], 0)

    each_tail_piece(lambda cp: cp.start())
    each_block(lambda cp: cp.start())
    each_tail_piece(lambda cp: cp.wait())
    each_block(lambda cp: cp.wait())


def _dispatch_kernel(nq_ref, pad_start_ref, pad_pieces_ref, n_used_ref, dst_ref, dstp_ref, metat_ref,
                     h_ref, xb_ref, l_ref, z_ref, sem):
    i = pl.program_id(0)
    last = pl.num_programs(0) - 1
    hw = l_ref.shape[2]
    cur = i % 2
    prv = 1 - cur
    n_prev = nq_ref[jnp.maximum(i - 1, 0)]
    min_rows = MIN_PIECES * DMA_ROWS

    def local_sort():
        slots = [metat_ref[k:k + 1, :] for k in range(TOP_K)]
        perm = _slot_matrix(slots, [1.0] * TOP_K, slot_axis=0).astype(bf16)
        return _pack_bf16_pairs(_dot(perm, h_ref[:, 0:hw]), _dot(perm, h_ref[:, hw:]))

    def piece(table_ref, buf):
        return lambda q: _piece_copy(l_ref.at[buf], xb_ref, q * DMA_ROWS, table_ref[0, 0, q],
                                     sem.at[buf])

    def wait(n, table_ref, buf):
        bulk = pltpu.make_async_copy(l_ref.at[buf, 0:min_rows], xb_ref.at[0:min_rows], sem.at[buf])
        _wait_pieces(n, piece(table_ref, buf), bulk)

    @pl.when(i == 0)
    def _():
        l_ref[cur] = local_sort()

    @pl.when(i > 0)
    def _():
        _start_guaranteed_pieces(piece(dstp_ref, prv))
        packed = local_sort()

        @pl.when(i > 1)
        def _():
            wait(nq_ref[jnp.maximum(i - 2, 0)], dstp_ref, cur)

        l_ref[cur] = packed
        _start_remaining_pieces(n_prev, piece(dstp_ref, prv))

    @pl.when(i == last)
    def _():
        _start_guaranteed_pieces(piece(dst_ref, cur))
        _start_remaining_pieces(nq_ref[i], piece(dst_ref, cur))
        _zero_fill_padding(pad_start_ref, pad_pieces_ref, n_used_ref, xb_ref, z_ref, sem.at[2])

        @pl.when(i > 0)
        def _():
            wait(n_prev, dstp_ref, prv)

        wait(nq_ref[i], dst_ref, cur)


def _dispatch(nq, pad_start, pad_pieces, n_used, dst, meta_t, h, n_rows):
    n, d = h.shape
    hw = d // 2
    t = MOE_TILE
    grid_spec = pltpu.PrefetchScalarGridSpec(
        num_scalar_prefetch=4,
        grid=(n // t,),
        in_specs=[pl.BlockSpec((1, 1, LOCAL_PIECES), lambda i, *_: (i, 0, 0), memory_space=pltpu.SMEM),
                  pl.BlockSpec((1, 1, LOCAL_PIECES), lambda i, *_: (jnp.maximum(i - 1, 0), 0, 0),
                               memory_space=pltpu.SMEM),
                  pl.BlockSpec((SUBLANE, t), lambda i, *_: (0, i)),
                  pl.BlockSpec((t, d), lambda i, *_: (i, 0))],
        out_specs=pl.BlockSpec(memory_space=pl.ANY),
        scratch_shapes=[pltpu.VMEM((2, LOCAL_ROWS, hw), jnp.uint32),
                        pltpu.VMEM((MOE_BLOCK, hw), jnp.uint32),
                        pltpu.SemaphoreType.DMA((3,))],
    )
    return pl.pallas_call(
        _dispatch_kernel,
        grid_spec=grid_spec,
        out_shape=jax.ShapeDtypeStruct((n_rows, hw), jnp.uint32),
        compiler_params=pltpu.CompilerParams(
            dimension_semantics=("arbitrary",), vmem_limit_bytes=VMEM_LIMIT),
        name="dispatch",
    )(nq, pad_start, pad_pieces, n_used, dst, dst, meta_t, h)


def _expert_kernel(be_ref, ns_ref, nu_ref, x_ref, wi_ref, bi_ref, wo_ref, bo_ref, y_ref,
                   wib_ref, wob_ref):
    b = pl.program_id(0)
    blk, hw = x_ref.shape
    dff = wo_ref.shape[1]
    subs = [slice(s * EXPERT_SUB_BLOCK, (s + 1) * EXPERT_SUB_BLOCK)
            for s in range(blk // EXPERT_SUB_BLOCK)]
    n_live = ns_ref[b]

    @pl.when(jnp.logical_and(n_live > 0,
                             jnp.logical_or(b == 0, be_ref[b] != be_ref[jnp.maximum(b - 1, 0)])))
    def _():
        for j in range(dff // LANE):
            wib_ref[:, 2 * j * LANE:(2 * j + 1) * LANE] = wi_ref[0, :, j * LANE:(j + 1) * LANE].astype(bf16)
            wib_ref[:, (2 * j + 1) * LANE:(2 * j + 2) * LANE] = (
                wi_ref[0, :, dff + j * LANE:dff + (j + 1) * LANE].astype(bf16))
        wob_ref[...] = wo_ref[0].astype(bf16)

    def up(rows):
        xa, xb = _unpack_bf16_pairs(x_ref[rows, :])
        return _dot(xa, wib_ref[0:hw, :]) + _dot(xb, wib_ref[hw:, :]) + bi_ref[0]

    def down(rows, hc):
        acts = []
        for j in range(dff // LANE):
            glu = jnp.minimum(hc[:, 2 * j * LANE:(2 * j + 1) * LANE], SWIGLU_LIMIT)
            lin = jnp.clip(hc[:, (2 * j + 1) * LANE:(2 * j + 2) * LANE], -SWIGLU_LIMIT, SWIGLU_LIMIT)
            acts.append((glu * (1.0 / (1.0 + jnp.exp(-SWIGLU_ALPHA * glu))) * (lin + 1.0)).astype(bf16))
        act = jnp.concatenate(acts, axis=1)
        y = (_dot(act, wob_ref[...]) + bo_ref[0]).astype(bf16).astype(f32)
        y_ref[rows, :] = _pack_bf16_pairs(y[:, 0:hw], y[:, hw:])

    def run(n):
        if n > 0:
            hc_next = up(subs[0])
        for s in range(n):
            hc = hc_next
            if s + 1 < n:
                hc_next = up(subs[s + 1])
            down(subs[s], hc)
        for rows in subs[n:]:
            y_ref[rows, :] = jnp.zeros((EXPERT_SUB_BLOCK, hw), y_ref.dtype)

    for n in range(len(subs) + 1):
        pl.when(n_live == n)(functools.partial(run, n))


def _experts(block_expert, block_live, n_used, xb, w_in, b_in, w_out, b_out):
    n_rows, hw = xb.shape
    e, d, dff2 = w_in.shape
    dff = w_out.shape[1]
    n_blocks = n_rows // MOE_BLOCK
    b_in_interleaved = b_in.reshape(e, 2, dff // LANE, LANE).transpose(0, 2, 1, 3)
    grid_spec = pltpu.PrefetchScalarGridSpec(
        num_scalar_prefetch=3,
        grid=(n_blocks,),
        in_specs=[
            pl.BlockSpec((MOE_BLOCK, hw), lambda b, be, ns, nu: (jnp.minimum(b, nu[0] - 1), 0)),
            pl.BlockSpec((1, d, dff2), lambda b, be, ns, nu: (be[b], 0, 0)),
            pl.BlockSpec((1, 1, dff2), lambda b, be, ns, nu: (be[b], 0, 0)),
            pl.BlockSpec((1, dff, d), lambda b, be, ns, nu: (be[b], 0, 0)),
            pl.BlockSpec((1, 1, d), lambda b, be, ns, nu: (be[b], 0, 0)),
        ],
        out_specs=pl.BlockSpec((MOE_BLOCK, hw), lambda b, be, ns, nu: (b, 0)),
        scratch_shapes=[pltpu.VMEM((d, dff2), bf16), pltpu.VMEM((dff, d), bf16)],
    )
    return pl.pallas_call(
        _expert_kernel,
        grid_spec=grid_spec,
        out_shape=jax.ShapeDtypeStruct((n_rows, hw), jnp.uint32),
        compiler_params=pltpu.CompilerParams(
            dimension_semantics=("arbitrary",), vmem_limit_bytes=VMEM_LIMIT),
        name="experts",
    )(block_expert, block_live, n_used, xb, w_in, b_in_interleaved.reshape(e, 1, dff2), w_out,
      b_out.reshape(e, 1, d))


def _combine_kernel(nq_ref, dst_ref, dstn_ref, meta_ref, gate_ref, xm_ref, y_ref, o_ref,
                    ly_ref, sem):
    i = pl.program_id(0)
    last = pl.num_programs(0) - 1
    hw = ly_ref.shape[2]
    cur = i % 2
    nxt = 1 - cur
    next_tile = jnp.minimum(i + 1, last)
    min_rows = MIN_PIECES * DMA_ROWS

    def piece(table_ref, buf):
        return lambda q: _piece_copy(y_ref, ly_ref.at[buf], table_ref[0, 0, q], q * DMA_ROWS,
                                     sem.at[buf])

    def zero_tail(buf):
        ly_ref[buf, MOE_TILE * TOP_K:, :] = jnp.zeros((LOCAL_ROWS - MOE_TILE * TOP_K, hw), jnp.uint32)

    def wait(n, buf):
        bulk = pltpu.make_async_copy(y_ref.at[0:min_rows], ly_ref.at[buf, 0:min_rows], sem.at[buf])
        _wait_pieces(n, piece(dst_ref, buf), bulk)

    @pl.when(i == 0)
    def _():
        zero_tail(cur)
        _start_guaranteed_pieces(piece(dst_ref, cur))
        _start_remaining_pieces(nq_ref[i], piece(dst_ref, cur))

    zero_tail(nxt)
    _start_remaining_pieces(nq_ref[next_tile], piece(dstn_ref, nxt))
    _start_guaranteed_pieces(piece(dstn_ref, nxt))
    gates = gate_ref[...]
    meta = meta_ref[...]
    g = _slot_matrix([meta[:, k:k + 1] for k in range(TOP_K)],
                     [gates[:, k:k + 1] for k in range(TOP_K)], slot_axis=1).astype(bf16)
    wait(nq_ref[i], cur)
    ya, yb = _unpack_bf16_pairs(ly_ref[cur])
    o_ref[:, 0:hw] = xm_ref[:, 0:hw] + _dot(g, ya)
    o_ref[:, hw:] = xm_ref[:, hw:] + _dot(g, yb)

    @pl.when(i == last)
    def _():
        wait(nq_ref[next_tile], nxt)


def _combine(nq, dst, meta, gates, xm, y):
    n, d = xm.shape
    hw = d // 2
    t = MOE_TILE
    n_tiles = n // t
    grid_spec = pltpu.PrefetchScalarGridSpec(
        num_scalar_prefetch=1,
        grid=(n_tiles,),
        in_specs=[pl.BlockSpec((1, 1, LOCAL_PIECES), lambda i, nq: (i, 0, 0), memory_space=pltpu.SMEM),
                  pl.BlockSpec((1, 1, LOCAL_PIECES), lambda i, nq: (jnp.minimum(i + 1, n_tiles - 1), 0, 0),
                               memory_space=pltpu.SMEM),
                  pl.BlockSpec((t, LANE), lambda i, nq: (i, 0)),
                  pl.BlockSpec((t, LANE), lambda i, nq: (i, 0)),
                  pl.BlockSpec((t, d), lambda i, nq: (i, 0)),
                  pl.BlockSpec(memory_space=pl.ANY)],
        out_specs=pl.BlockSpec((t, d), lambda i, nq: (i, 0)),
        scratch_shapes=[pltpu.VMEM((2, LOCAL_ROWS, hw), jnp.uint32), pltpu.SemaphoreType.DMA((2,))],
    )
    return pl.pallas_call(
        _combine_kernel,
        grid_spec=grid_spec,
        out_shape=jax.ShapeDtypeStruct((n, d), f32),
        compiler_params=pltpu.CompilerParams(
            dimension_semantics=("arbitrary",), vmem_limit_bytes=VMEM_LIMIT),
        name="combine",
    )(nq, dst, dst, meta, gates, xm, y)


def _rel_bias_table(rel_bias):
    band = BAND_CHUNKS * CHUNK
    width = band + CHUNK
    dist = (np.arange(width) - CHUNK)[::-1]
    ext = rel_bias[:, np.clip(dist, -REL_CLIP, REL_CLIP) + REL_CLIP].astype(f32)
    heads = ext.shape[0]
    tiled = jnp.broadcast_to(ext[:, None, :], (heads, CHUNK, width)).reshape(heads, CHUNK * width)
    skewed = tiled[:, :CHUNK * (width - 1)].reshape(heads, CHUNK, width - 1)
    return skewed[:, :, CHUNK - 1:CHUNK - 1 + band]


def _round_up(x, m):
    return (x + m - 1) // m * m


def _routing_tables(cnt, n_tokens):
    experts = jnp.arange(N_EXPERTS, dtype=jnp.int32)
    c = cnt.reshape(-1, LANE)[:, :N_EXPERTS].astype(jnp.int32)
    n_tiles = c.shape[0]
    cp = _round_up(c, DMA_ROWS)
    lend = jnp.cumsum(cp, axis=1)
    lstart = lend - cp
    nq = (lend[:, -1] // DMA_ROWS).astype(jnp.int32)
    region = jnp.sum(cp, axis=0)
    padded = _round_up(region, MOE_BLOCK)
    pend = jnp.cumsum(padded)
    pstart = pend - padded
    base = pstart[None, :] + jnp.cumsum(cp, axis=0) - cp
    q0 = jnp.arange(LOCAL_PIECES, dtype=jnp.int32) * DMA_ROWS
    e_q = jnp.minimum(jnp.sum(lend[:, None, :] <= q0[None, :, None], axis=-1), N_EXPERTS - 1)
    shift = jnp.sum(jnp.where(e_q[:, :, None] == experts, (base - lstart)[:, None, :], 0), axis=-1)
    dst = jnp.where(q0[None, :] < lend[:, -1:], shift + q0[None, :], 0).astype(jnp.int32)

    n_blocks = -(-(n_tokens * TOP_K + n_tiles * N_EXPERTS * (DMA_ROWS - 1)) // MOE_BLOCK) + N_EXPERTS
    blk0 = jnp.arange(n_blocks, dtype=jnp.int32) * MOE_BLOCK
    n_used = (pend[-1] // MOE_BLOCK).astype(jnp.int32)
    be = jnp.minimum(jnp.sum(pend[None, :] <= blk0[:, None], axis=1), N_EXPERTS - 1).astype(jnp.int32)
    last = jnp.sum(jnp.where(jnp.arange(n_blocks) == n_used - 1, be, 0))
    be = jnp.where(jnp.arange(n_blocks) < n_used, be, last)
    onehot_be = be[:, None] == experts
    region_end = jnp.sum(jnp.where(onehot_be, pstart + region, 0), axis=1)
    live_rows = jnp.clip(region_end - blk0, 0, MOE_BLOCK)
    live = jnp.where(jnp.arange(n_blocks) < n_used, -(-live_rows // EXPERT_SUB_BLOCK), 0)
    pad_start = (pstart + region).astype(jnp.int32)
    pad_pieces = ((padded - region) // DMA_ROWS).astype(jnp.int32)
    return (nq, pad_start, pad_pieces, dst.reshape(n_tiles, 1, LOCAL_PIECES), be,
            live.astype(jnp.int32), n_used.reshape(1), n_blocks)


def _layer(x, norm1_g, w_in, gate_up, gate_bias, gla_norm_g, q_norm_g, k_norm_g, rel_bias, w_out,
           norm2_g, router_w, router_b, moe_w_in, moe_b_in, moe_w_out, moe_b_out):
    batch, seq, d = x.shape
    n = batch * seq
    x2 = x.reshape(n, d)

    pieces = jnp.split(w_in, np.cumsum(IN_SIZES)[:-1].tolist(), axis=-1)
    pieces[3] = jnp.pad(pieces[3], ((0, 0), (0, LANE - GLA_GATE_RANK)))
    widths = [p.shape[-1] for p in pieces]
    w_all = jnp.concatenate(pieces, axis=-1).astype(bf16)
    tile2 = lambda g: jnp.tile(g.reshape(1, -1), (1, LANE // ATT_DH))
    gq, gk, gv, glr, gog, qs, ak, av = _inproj(x2, norm1_g.reshape(1, d), w_all, widths,
                                               tile2(q_norm_g), tile2(k_norm_g))

    gup = jnp.pad(gate_up, ((0, LANE - GLA_GATE_RANK), (0, 0))).astype(bf16)
    o_gla = _gla(gq, gk, gv, glr, gog, gup, gate_bias.reshape(1, -1), gla_norm_g.reshape(1, -1),
                 batch, seq)
    o_att = _attention(qs, ak, av, _rel_bias_table(rel_bias), batch, seq)

    rw = jnp.pad(router_w, ((0, 0), (0, LANE - N_EXPERTS)))
    rw_hi = rw.astype(bf16)
    rw_lo = (rw - rw_hi.astype(f32)).astype(bf16)
    rb = jnp.pad(router_b, (0, LANE - N_EXPERTS)).reshape(1, LANE)
    xm, h2, meta, meta_t, gates, cnt = _mid(o_gla, o_att, x2, w_out.astype(bf16),
                                            norm2_g.reshape(1, d),
                                            jnp.concatenate([rw_hi, rw_lo], axis=1), rb)

    nq, pad_start, pad_pieces, dst, be, live, n_used, n_blocks = _routing_tables(cnt, n)
    xb = _dispatch(nq, pad_start, pad_pieces, n_used, dst, meta_t, h2, n_blocks * MOE_BLOCK)
    y = _experts(be, live, n_used, xb, moe_w_in, moe_b_in, moe_w_out, moe_b_out)
    out = _combine(nq, dst, meta, gates, xm, y)
    return out.reshape(batch, seq, d)


def kernel(x, norm1_g, w_in, gla_gate_up, gla_gate_bias, gla_norm_g, q_norm_g, k_norm_g, rel_bias, w_out, norm2_g, router_w, router_b, moe_w_in, moe_b_in, moe_w_out, moe_b_out):
    for l in range(norm1_g.shape[0]):
        x = _layer(x, norm1_g[l], w_in[l], gla_gate_up[l], gla_gate_bias[l], gla_norm_g[l],
                   q_norm_g[l], k_norm_g[l], rel_bias[l], w_out[l], norm2_g[l], router_w[l],
                   router_b[l], moe_w_in[l], moe_b_in[l], moe_w_out[l], moe_b_out[l])
    return x
```

```python
import functools

import numpy as np
import jax
import jax.numpy as jnp
from jax import lax
from jax.experimental import pallas as pl
from jax.experimental.pallas import tpu as pltpu

CHUNK = 64
EPS = 1e-6
GLA_HEADS = 4
GLA_DK = 64
GLA_DV = 128
GLA_GATE_RANK = 16
GLA_GATE_TAU = 16.0
ATT_HEADS = 8
ATT_DH = 64
N_BACK_CHUNKS = 8
BAND_CHUNKS = N_BACK_CHUNKS + 1
REL_CLIP = 256
MASK_VALUE = -1e30
N_EXPERTS = 32
TOP_K = 4
SWIGLU_ALPHA = 1.702
SWIGLU_LIMIT = 7.0
MOE_BLOCK = 1024
EXPERT_SUB_BLOCK = 256

LANE = 128
SUBLANE = 8
GLA_QK_W = GLA_HEADS * GLA_DK
GLA_V_W = GLA_HEADS * GLA_DV
ATT_W = ATT_HEADS * ATT_DH
IN_SIZES = (GLA_QK_W, GLA_QK_W, GLA_V_W, GLA_GATE_RANK, GLA_V_W, ATT_W, ATT_W, ATT_W)
SEQ_TILE = N_BACK_CHUNKS * CHUNK
ROW_TILE = 512
MID_TILE = 1024
MOE_TILE = 256
DMA_ROWS = SUBLANE
LOCAL_PIECES = 160
LOCAL_ROWS = LOCAL_PIECES * DMA_ROWS
MIN_PIECES = MOE_TILE * TOP_K // DMA_ROWS
COMBINE_BUFFERS = 3
VMEM_LIMIT = 48 * 1024 * 1024

f32 = jnp.float32
bf16 = jnp.bfloat16


def _rms(x, g):
    return x * lax.rsqrt(jnp.mean(x * x, axis=-1, keepdims=True) + EPS) * g


def _dot(a, b):
    return jnp.dot(a, b, preferred_element_type=f32)


def _dot_nt(a, b):
    return lax.dot_general(a, b, (((1,), (1,)), ((), ())), preferred_element_type=f32)


def _dot_tn(a, b):
    return lax.dot_general(a, b, (((0,), (0,)), ((), ())), preferred_element_type=f32)


def _split_bf16(x):
    hi = x.astype(bf16)
    lo = (x - hi.astype(f32)).astype(bf16)
    return hi, lo


def _head_norm(x, g):
    lo = lax.broadcasted_iota(jnp.int32, (1, LANE), 1) < ATT_DH
    sq = x * x
    s0 = jnp.sum(jnp.where(lo, sq, 0.0), axis=-1, keepdims=True)
    s1 = jnp.sum(jnp.where(lo, 0.0, sq), axis=-1, keepdims=True)
    r = jnp.where(lo, lax.rsqrt(s0 * (1.0 / ATT_DH) + EPS), lax.rsqrt(s1 * (1.0 / ATT_DH) + EPS))
    return x * r * g


def _inproj_kernel(x_ref, g_ref, w_ref, qg_ref, kg_ref, gq, gk, gv, glr, gog, qs_ref, ak, av):
    t = x_ref.shape[0]
    n_pairs = ATT_W // LANE
    h = _rms(x_ref[...], g_ref[...]).astype(bf16)
    offsets = np.cumsum((0,) + tuple(r.shape[-1] for r in (gq, gk, gv, glr, gog)) + (ATT_W, ATT_W))
    lo = lax.broadcasted_iota(jnp.int32, (1, LANE), 1) < ATT_DH

    q_all = _dot(h, w_ref[:, offsets[5]:offsets[5] + ATT_W])
    k_all = _dot(h, w_ref[:, offsets[6]:offsets[6] + ATT_W])
    for p in range(n_pairs):
        pair = slice(p * LANE, (p + 1) * LANE)
        qn = _head_norm(q_all[:, pair], qg_ref[...]) * (ATT_DH ** -0.5)
        q_lo = jnp.where(lo, qn, 0.0).astype(bf16)
        q_hi = jnp.where(lo, 0.0, qn).astype(bf16)
        for c in range(t // CHUNK):
            rows = slice(c * CHUNK, (c + 1) * CHUNK)
            qs_ref[c * n_pairs + p, 0:CHUNK, :] = q_lo[rows]
            qs_ref[c * n_pairs + p, CHUNK:2 * CHUNK, :] = q_hi[rows]
        ak[:, pair] = _head_norm(k_all[:, pair], kg_ref[...]).astype(ak.dtype)
    for o_ref, off in zip((gq, gk, gv, glr, gog, av), tuple(offsets[:5]) + (offsets[7],)):
        o_ref[...] = _dot(h, w_ref[:, off:off + o_ref.shape[-1]]).astype(o_ref.dtype)


def _inproj(x2, g, w, widths, qg, kg):
    n, d = x2.shape
    n_pairs = ATT_W // LANE
    blocks_per_tile = ROW_TILE // CHUNK * n_pairs
    rows = lambda wd: pl.BlockSpec((ROW_TILE, wd), lambda i: (i, 0))
    full = lambda a: pl.BlockSpec(a.shape, lambda i: (0,) * a.ndim)
    plain = lambda wd: jax.ShapeDtypeStruct((n, wd), bf16)
    return pl.pallas_call(
        _inproj_kernel,
        grid=(n // ROW_TILE,),
        in_specs=[rows(d), full(g), full(w), full(qg), full(kg)],
        out_specs=[rows(wd) for wd in widths[:5]]
        + [pl.BlockSpec((blocks_per_tile, 2 * CHUNK, LANE), lambda i: (i, 0, 0)), rows(ATT_W), rows(ATT_W)],
        out_shape=[plain(wd) for wd in widths[:5]]
        + [jax.ShapeDtypeStruct((n // CHUNK * n_pairs, 2 * CHUNK, LANE), bf16), plain(ATT_W), plain(ATT_W)],
        compiler_params=pltpu.CompilerParams(
            dimension_semantics=("arbitrary",), vmem_limit_bytes=VMEM_LIMIT),
        name="inproj",
    )(x2, g, w, qg, kg)


def _gla_kernel(q_ref, k_ref, v_ref, lr_ref, og_ref, gup_ref, gb_ref, ng_ref, o_ref, st_ref,
                u_ref, sb_ref):
    t = q_ref.shape[0]

    @pl.when(pl.program_id(1) == 0)
    def _():
        st_ref[...] = jnp.zeros_like(st_ref)

    z = _dot(lr_ref[...], gup_ref[...]) + gb_ref[...]
    log_a = (jnp.minimum(z, 0.0) - jnp.log1p(jnp.exp(-jnp.abs(z)))) * (1.0 / GLA_GATE_TAU)
    row = lax.broadcasted_iota(jnp.int32, (t, t), 0)
    col = lax.broadcasted_iota(jnp.int32, (t, t), 1)
    tri = jnp.where((col <= row) & ((col // CHUNK) == (row // CHUNK)), 1.0, 0.0).astype(bf16)
    la_hi, la_lo = _split_bf16(log_a)
    cum_all = _dot(tri, la_hi) + _dot(tri, la_lo)
    lane = lax.broadcasted_iota(jnp.int32, (1, LANE), 1)
    half_mask = (lane < GLA_DK, lane >= GLA_DK)
    n_chunks = t // CHUNK
    chunk_rows = [slice(c * CHUNK, (c + 1) * CHUNK) for c in range(n_chunks)]
    pair_of = lambda h: slice((h // 2) * LANE, (h // 2 + 1) * LANE)
    head_of = lambda h: slice(h * GLA_DV, (h + 1) * GLA_DV)

    decs = []
    for c, rows in enumerate(chunk_rows):
        cum = cum_all[rows]
        tot = cum[CHUNK - 1:CHUNK]
        kdec = k_ref[rows, :].astype(f32) * jnp.exp(tot - cum)
        decs.append(jnp.exp(tot))
        for h in range(GLA_HEADS):
            kd = jnp.where(half_mask[h % 2], kdec[:, pair_of(h)], 0.0).astype(bf16)
            u_ref[c, h] = _dot_tn(v_ref[rows, head_of(h)], kd)

    for h in range(GLA_HEADS):
        st = st_ref[h]
        for c in range(n_chunks):
            st = st * decs[c][:, pair_of(h)] + u_ref[c, h]
            sb_ref[c, h] = st.astype(bf16)
        st_ref[h] = st

    for c, rows in enumerate(chunk_rows):
        for h in range(GLA_HEADS):
            o = _dot_nt(q_ref[rows, pair_of(h)], sb_ref[c, h]) * (GLA_DK ** -0.5)
            o = _rms(o, ng_ref[...])
            g = og_ref[rows, head_of(h)].astype(f32)
            o_ref[rows, head_of(h)] = (o * (g / (1.0 + jnp.exp(-g)))).astype(o_ref.dtype)


def _gla(gq, gk, gv, glr, gog, gup, gb, ng, batch, seq):
    nt = seq // SEQ_TILE
    tile = lambda w: pl.BlockSpec((SEQ_TILE, w), lambda b, i: (b * nt + i, 0))
    full = lambda a: pl.BlockSpec(a.shape, lambda b, i: (0,) * a.ndim)
    return pl.pallas_call(
        _gla_kernel,
        grid=(batch, nt),
        in_specs=[tile(GLA_QK_W), tile(GLA_QK_W), tile(GLA_V_W), tile(LANE), tile(GLA_V_W),
                  full(gup), full(gb), full(ng)],
        out_specs=tile(GLA_V_W),
        out_shape=jax.ShapeDtypeStruct((batch * seq, GLA_V_W), bf16),
        scratch_shapes=[pltpu.VMEM((GLA_HEADS, GLA_DV, LANE), f32),
                        pltpu.VMEM((SEQ_TILE // CHUNK, GLA_HEADS, GLA_DV, LANE), f32),
                        pltpu.VMEM((SEQ_TILE // CHUNK, GLA_HEADS, GLA_DV, LANE), bf16)],
        compiler_params=pltpu.CompilerParams(
            dimension_semantics=("arbitrary", "arbitrary"), vmem_limit_bytes=VMEM_LIMIT),
        name="gla",
    )(gq, gk, gv, glr, gog, gup, gb, ng)


def _att_kernel(qs_ref, k_ref, v_ref, bias_ref, o_ref, kb_ref, vb_ref):
    t = k_ref.shape[0]
    n_pairs = ATT_W // LANE
    band = BAND_CHUNKS * CHUNK
    first = pl.program_id(1) == 0

    @pl.when(first)
    def _():
        kb_ref[0:t, :] = jnp.zeros((t, ATT_W), bf16)
        vb_ref[0:t, :] = jnp.zeros((t, 2 * ATT_W), bf16)

    @pl.when(jnp.logical_not(first))
    def _():
        kb_ref[0:t, :] = kb_ref[t:2 * t, :]
        vb_ref[0:t, :] = vb_ref[t:2 * t, :]

    lane = lax.broadcasted_iota(jnp.int32, (1, LANE), 1)
    lo = lane < ATT_DH

    kb_ref[t:2 * t, :] = k_ref[...]
    for p in range(n_pairs):
        vb_ref[t:2 * t, 2 * p * LANE:(2 * p + 1) * LANE] = v_ref[:, p * LANE:(p + 1) * LANE]
        vb_ref[t:2 * t, (2 * p + 1) * LANE:(2 * p + 2) * LANE] = jnp.ones((t, LANE), bf16)

    colk = lax.broadcasted_iota(jnp.int32, (1, band), 1)

    def chunk_loop(masked):
        n_chunks = t // CHUNK

        def scores(c, p):
            k2 = kb_ref[c * CHUNK:c * CHUNK + band, p * LANE:(p + 1) * LANE]
            return _dot_nt(qs_ref[c * n_pairs + p], k2)

        def weighted_values(e, c, p):
            v2 = vb_ref[c * CHUNK:c * CHUNK + band, 2 * p * LANE:(2 * p + 2) * LANE]
            pvl = _dot(e, v2)
            pv = pvl[:, 0:LANE] / pvl[:, LANE:2 * LANE]
            o_ref[c * CHUNK:(c + 1) * CHUNK, p * LANE:(p + 1) * LANE] = jnp.where(
                lo, pv[0:CHUNK], pv[CHUNK:2 * CHUNK]).astype(o_ref.dtype)

        blocks = [(c, p) for c in range(n_chunks) for p in range(n_pairs)]
        s_next = scores(0, 0)
        e_prev = None
        for i, (c, p) in enumerate(blocks):
            s = s_next + bias_ref[p]
            if i + 1 < len(blocks):
                s_next = scores(*blocks[i + 1])
            if masked:
                s = jnp.where(colk >= t - c * CHUNK, s, MASK_VALUE)
            e = jnp.exp(s - jnp.max(s, axis=-1, keepdims=True)).astype(bf16)
            if i > 0:
                weighted_values(e_prev, *blocks[i - 1])
            e_prev = e
        weighted_values(e_prev, *blocks[-1])

    @pl.when(first)
    def _():
        chunk_loop(True)

    @pl.when(jnp.logical_not(first))
    def _():
        chunk_loop(False)


def _attention(qs, ak, av, bias, batch, seq):
    nt = seq // SEQ_TILE
    n_pairs = ATT_W // LANE
    tile = pl.BlockSpec((SEQ_TILE, ATT_W), lambda b, i: (b * nt + i, 0))
    q_tile = pl.BlockSpec((SEQ_TILE // CHUNK * n_pairs, 2 * CHUNK, LANE), lambda b, i: (b * nt + i, 0, 0))
    full = lambda a: pl.BlockSpec(a.shape, lambda b, i: (0,) * a.ndim)
    bias2 = bias.reshape(n_pairs, 2 * CHUNK, BAND_CHUNKS * CHUNK)
    return pl.pallas_call(
        _att_kernel,
        grid=(batch, nt),
        in_specs=[q_tile, tile, tile, full(bias2)],
        out_specs=tile,
        out_shape=jax.ShapeDtypeStruct((batch * seq, ATT_W), bf16),
        scratch_shapes=[pltpu.VMEM((2 * SEQ_TILE, ATT_W), bf16),
                        pltpu.VMEM((2 * SEQ_TILE, 2 * ATT_W), bf16)],
        compiler_params=pltpu.CompilerParams(
            dimension_semantics=("arbitrary", "arbitrary"), vmem_limit_bytes=VMEM_LIMIT),
        name="attention",
    )(qs, ak, av, bias2)


def _mid_kernel(og_ref, oa_ref, x_ref, wo_ref, g2_ref, rw_ref, rb_ref,
                xm_ref, h_ref, meta_ref, metat_ref, gate_ref, cnt_ref):
    t, d = x_ref.shape
    mt = MOE_TILE
    tiles = [slice(s * mt, (s + 1) * mt) for s in range(t // mt)]
    lane = lax.broadcasted_iota(jnp.int32, (mt, LANE), 1)
    lane_f = lane.astype(f32)
    row = lax.broadcasted_iota(jnp.int32, (mt, mt), 0)
    col = lax.broadcasted_iota(jnp.int32, (mt, mt), 1)
    earlier = jnp.where(col < row, 1.0, 0.0).astype(bf16)
    e_r = lax.broadcasted_iota(jnp.int32, (LANE, LANE), 0)
    e_c = lax.broadcasted_iota(jnp.int32, (LANE, LANE), 1)
    before = jnp.where(e_r < e_c, 1.0, 0.0).astype(bf16)

    def project(rows):
        xm = (x_ref[rows, :] + _dot(og_ref[rows, :], wo_ref[0:GLA_V_W, :])
              + _dot(oa_ref[rows, :], wo_ref[GLA_V_W:, :]))
        xm_ref[rows, :] = xm
        h_hi, h_lo = _split_bf16(_rms(xm, g2_ref[...]))
        h_ref[rows, :] = h_hi
        return h_hi, h_lo

    def route(h_hi, h_lo):
        both = _dot(h_hi, rw_ref[...])
        return both[:, 0:LANE] + both[:, LANE:2 * LANE] + _dot(h_lo, rw_ref[:, 0:LANE]) + rb_ref[...]

    def top_k(logits):
        l = jnp.where(lane < N_EXPERTS, logits, -jnp.inf)
        vals, onehots = [], []
        for _ in range(TOP_K):
            m = jnp.max(l, axis=-1, keepdims=True)
            ik = jnp.min(jnp.where(l == m, lane_f, float(LANE)), axis=-1, keepdims=True)
            vals.append(m)
            onehots.append(lane_f == ik)
            l = jnp.where(onehots[-1], -jnp.inf, l)
        es = [jnp.exp(v - vals[0]) for v in vals]
        den = es[0] + es[1] + es[2] + es[3]
        sel = jnp.zeros((mt, LANE), f32)
        for oh in onehots:
            sel = sel + jnp.where(oh, 1.0, 0.0)
        return onehots, [e / den for e in es], sel

    def slots(s, rows, onehots, gate_vals, sel):
        prefix = _dot(earlier, sel.astype(bf16))
        c_row = jnp.sum(sel, axis=0, keepdims=True)
        cnt_ref[0, s:s + 1, :] = c_row
        padded = jnp.ceil(c_row * (1.0 / DMA_ROWS)) * DMA_ROWS
        start_row = _dot(jnp.broadcast_to(padded, (SUBLANE, LANE)).astype(bf16), before)[0:1]
        slot_base = prefix + start_row
        meta = jnp.zeros((mt, LANE), f32)
        gates = jnp.zeros((mt, LANE), f32)
        for k in range(TOP_K):
            slot_k = jnp.sum(jnp.where(onehots[k], slot_base, 0.0), axis=-1, keepdims=True)
            meta = jnp.where(lane == k, slot_k, meta)
            gates = jnp.where(lane == k, gate_vals[k], gates)
        meta_ref[rows, :] = meta.astype(jnp.int32)
        metat_ref[:, rows] = meta.T[0:SUBLANE, :].astype(jnp.int32)
        gate_ref[rows, :] = gates

    hs = [project(rows) for rows in tiles]
    logits = [route(*h) for h in hs]
    routed = [top_k(lg) for lg in logits]
    for s, rows in enumerate(tiles):
        slots(s, rows, *routed[s])


def _mid(o_gla, o_att, x2, wo, g2, rw, rb):
    n, d = x2.shape
    sub = MID_TILE // MOE_TILE
    tile = lambda w: pl.BlockSpec((MID_TILE, w), lambda i: (i, 0))
    full = lambda a: pl.BlockSpec(a.shape, lambda i: (0,) * a.ndim)
    return pl.pallas_call(
        _mid_kernel,
        grid=(n // MID_TILE,),
        in_specs=[tile(GLA_V_W), tile(ATT_W), tile(d), full(wo), full(g2), full(rw), full(rb)],
        out_specs=[tile(d), tile(d), tile(LANE),
                   pl.BlockSpec((SUBLANE, MID_TILE), lambda i: (0, i)), tile(LANE),
                   pl.BlockSpec((1, sub, LANE), lambda i: (i, 0, 0))],
        out_shape=[jax.ShapeDtypeStruct((n, d), f32),
                   jax.ShapeDtypeStruct((n, d), bf16),
                   jax.ShapeDtypeStruct((n, LANE), jnp.int32),
                   jax.ShapeDtypeStruct((SUBLANE, n), jnp.int32),
                   jax.ShapeDtypeStruct((n, LANE), f32),
                   jax.ShapeDtypeStruct((n // MID_TILE, sub, LANE), f32)],
        compiler_params=pltpu.CompilerParams(
            dimension_semantics=("arbitrary",), vmem_limit_bytes=VMEM_LIMIT),
        name="mid",
    )(o_gla, o_att, x2, wo, g2, rw, rb)


def _slot_matrix(slots, values, slot_axis):
    n_tokens = slots[0].shape[1 - slot_axis]
    shape = (LOCAL_ROWS, n_tokens) if slot_axis == 0 else (n_tokens, LOCAL_ROWS)
    iota_shape = (LOCAL_ROWS, 1) if slot_axis == 0 else (1, LOCAL_ROWS)
    pos = lax.broadcasted_iota(jnp.int32, iota_shape, slot_axis)
    out = jnp.zeros(shape, f32)
    for slot_k, value_k in zip(slots, values):
        out = jnp.where(pos == slot_k, value_k, out)
    return out


def _pack_bf16_pairs(a, b):
    return (pltpu.bitcast(a, jnp.uint32) >> 16) | (pltpu.bitcast(b, jnp.uint32) & jnp.uint32(0xFFFF0000))


def _unpack_bf16_pairs(w):
    a = pltpu.bitcast(w << 16, f32).astype(bf16)
    b = pltpu.bitcast(w & jnp.uint32(0xFFFF0000), f32).astype(bf16)
    return a, b


def _piece_copy(src_ref, dst_ref, src_row, dst_row, sem):
    return pltpu.make_async_copy(src_ref.at[pl.ds(pl.multiple_of(src_row, DMA_ROWS), DMA_ROWS)],
                                 dst_ref.at[pl.ds(pl.multiple_of(dst_row, DMA_ROWS), DMA_ROWS)], sem)


def _start_guaranteed_pieces(piece):
    for q in range(MIN_PIECES):
        piece(q).start()


def _start_remaining_pieces(n, piece):
    lax.fori_loop(MIN_PIECES, n, lambda q, c: (piece(q).start(), c)[1], 0)


def _wait_pieces(n, piece, bulk):
    bulk.wait()
    lax.fori_loop(MIN_PIECES, n, lambda q, c: (piece(q).wait(), c)[1], 0)


def _zero_fill_padding(pad_start_ref, pad_pieces_ref, n_used_ref, xb_ref, z_ref, sem):
    z_ref[...] = jnp.zeros_like(z_ref)
    n_blocks = xb_ref.shape[0] // MOE_BLOCK

    def tail_piece(e, q):
        return _piece_copy(z_ref, xb_ref, 0, pad_start_ref[e] + q * DMA_ROWS, sem)

    def block_copy(b):
        row = pl.multiple_of(b * MOE_BLOCK, MOE_BLOCK)
        return pltpu.make_async_copy(z_ref, xb_ref.at[pl.ds(row, MOE_BLOCK)], sem)

    def each_tail_piece(fn):
        def per_expert(e, c):
            lax.fori_loop(0, pad_pieces_ref[e], lambda q, cc: (fn(tail_piece(e, q)), cc)[1], 0)
            return c
        lax.fori_loop(0, N_EXPERTS, per_expert, 0)

    def each_block(fn):
        lax.fori_loop(n_used_ref[0], n_blocks, lambda b, c: (fn(block_copy(b)), c)[1], 0)

    each_tail_piece(lambda cp: cp.start())
    each_block(lambda cp: cp.start())
    each_tail_piece(lambda cp: cp.wait())
    each_block(lambda cp: cp.wait())


def _dispatch_kernel(nq_ref, pad_start_ref, pad_pieces_ref, n_used_ref, dst_ref, dstp_ref, metat_ref,
                     h_ref, xb_ref, l_ref, z_ref, sem):
    i = pl.program_id(0)
    last = pl.num_programs(0) - 1
    hw = l_ref.shape[2]
    cur = i % 2
    prv = 1 - cur
    n_prev = nq_ref[jnp.maximum(i - 1, 0)]
    min_rows = MIN_PIECES * DMA_ROWS

    def local_sort():
        slots = [metat_ref[k:k + 1, :] for k in range(TOP_K)]
        perm = _slot_matrix(slots, [1.0] * TOP_K, slot_axis=0).astype(bf16)
        return _pack_bf16_pairs(_dot(perm, h_ref[:, 0:hw]), _dot(perm, h_ref[:, hw:]))

    def piece(table_ref, buf):
        return lambda q: _piece_copy(l_ref.at[buf], xb_ref, q * DMA_ROWS, table_ref[0, 0, q],
                                     sem.at[buf])

    def wait(n, table_ref, buf):
        bulk = pltpu.make_async_copy(l_ref.at[buf, 0:min_rows], xb_ref.at[0:min_rows], sem.at[buf])
        _wait_pieces(n, piece(table_ref, buf), bulk)

    @pl.when(i == 0)
    def _():
        l_ref[cur] = local_sort()

    @pl.when(i > 0)
    def _():
        _start_guaranteed_pieces(piece(dstp_ref, prv))
        packed = local_sort()

        @pl.when(i > 1)
        def _():
            wait(nq_ref[jnp.maximum(i - 2, 0)], dstp_ref, cur)

        l_ref[cur] = packed
        _start_remaining_pieces(n_prev, piece(dstp_ref, prv))

    @pl.when(i == last)
    def _():
        _start_guaranteed_pieces(piece(dst_ref, cur))
        _start_remaining_pieces(nq_ref[i], piece(dst_ref, cur))
        _zero_fill_padding(pad_start_ref, pad_pieces_ref, n_used_ref, xb_ref, z_ref, sem.at[2])

        @pl.when(i > 0)
        def _():
            wait(n_prev, dstp_ref, prv)

        wait(nq_ref[i], dst_ref, cur)


def _dispatch(nq, pad_start, pad_pieces, n_used, dst, meta_t, h, n_rows):
    n, d = h.shape
    hw = d // 2
    t = MOE_TILE
    grid_spec = pltpu.PrefetchScalarGridSpec(
        num_scalar_prefetch=4,
        grid=(n // t,),
        in_specs=[pl.BlockSpec((1, 1, LOCAL_PIECES), lambda i, *_: (i, 0, 0), memory_space=pltpu.SMEM),
                  pl.BlockSpec((1, 1, LOCAL_PIECES), lambda i, *_: (jnp.maximum(i - 1, 0), 0, 0),
                               memory_space=pltpu.SMEM),
                  pl.BlockSpec((SUBLANE, t), lambda i, *_: (0, i)),
                  pl.BlockSpec((t, d), lambda i, *_: (i, 0))],
        out_specs=pl.BlockSpec(memory_space=pl.ANY),
        scratch_shapes=[pltpu.VMEM((2, LOCAL_ROWS, hw), jnp.uint32),
                        pltpu.VMEM((MOE_BLOCK, hw), jnp.uint32),
                        pltpu.SemaphoreType.DMA((3,))],
    )
    return pl.pallas_call(
        _dispatch_kernel,
        grid_spec=grid_spec,
        out_shape=jax.ShapeDtypeStruct((n_rows, hw), jnp.uint32),
        compiler_params=pltpu.CompilerParams(
            dimension_semantics=("arbitrary",), vmem_limit_bytes=VMEM_LIMIT),
        name="dispatch",
    )(nq, pad_start, pad_pieces, n_used, dst, dst, meta_t, h)


def _expert_kernel(be_ref, ns_ref, nu_ref, x_ref, wi_ref, bi_ref, wo_ref, bo_ref, y_ref,
                   wib_ref, wob_ref):
    b = pl.program_id(0)
    blk, hw = x_ref.shape
    dff = wo_ref.shape[1]
    subs = [slice(s * EXPERT_SUB_BLOCK, (s + 1) * EXPERT_SUB_BLOCK)
            for s in range(blk // EXPERT_SUB_BLOCK)]
    n_live = ns_ref[b]

    @pl.when(jnp.logical_and(n_live > 0,
                             jnp.logical_or(b == 0, be_ref[b] != be_ref[jnp.maximum(b - 1, 0)])))
    def _():
        for j in range(dff // LANE):
            wib_ref[:, 2 * j * LANE:(2 * j + 1) * LANE] = wi_ref[0, :, j * LANE:(j + 1) * LANE].astype(bf16)
            wib_ref[:, (2 * j + 1) * LANE:(2 * j + 2) * LANE] = (
                wi_ref[0, :, dff + j * LANE:dff + (j + 1) * LANE].astype(bf16))
        wob_ref[...] = wo_ref[0].astype(bf16)

    def up(rows):
        xa, xb = _unpack_bf16_pairs(x_ref[rows, :])
        return _dot(xa, wib_ref[0:hw, :]) + _dot(xb, wib_ref[hw:, :]) + bi_ref[0]

    def down(rows, hc):
        acts = []
        for j in range(dff // LANE):
            glu = jnp.minimum(hc[:, 2 * j * LANE:(2 * j + 1) * LANE], SWIGLU_LIMIT)
            lin = jnp.clip(hc[:, (2 * j + 1) * LANE:(2 * j + 2) * LANE], -SWIGLU_LIMIT, SWIGLU_LIMIT)
            acts.append((glu * (1.0 / (1.0 + jnp.exp(-SWIGLU_ALPHA * glu))) * (lin + 1.0)).astype(bf16))
        act = jnp.concatenate(acts, axis=1)
        y = (_dot(act, wob_ref[...]) + bo_ref[0]).astype(bf16).astype(f32)
        y_ref[rows, :] = _pack_bf16_pairs(y[:, 0:hw], y[:, hw:])

    def run(n):
        if n > 0:
            hc_next = up(subs[0])
        for s in range(n):
            hc = hc_next
            if s + 1 < n:
                hc_next = up(subs[s + 1])
            down(subs[s], hc)
        for rows in subs[n:]:
            y_ref[rows, :] = jnp.zeros((EXPERT_SUB_BLOCK, hw), y_ref.dtype)

    for n in range(len(subs) + 1):
        pl.when(n_live == n)(functools.partial(run, n))


def _experts(block_expert, block_live, n_used, xb, w_in, b_in, w_out, b_out):
    n_rows, hw = xb.shape
    e, d, dff2 = w_in.shape
    dff = w_out.shape[1]
    n_blocks = n_rows // MOE_BLOCK
    b_in_interleaved = b_in.reshape(e, 2, dff // LANE, LANE).transpose(0, 2, 1, 3)
    grid_spec = pltpu.PrefetchScalarGridSpec(
        num_scalar_prefetch=3,
        grid=(n_blocks,),
        in_specs=[
            pl.BlockSpec((MOE_BLOCK, hw), lambda b, be, ns, nu: (jnp.minimum(b, nu[0] - 1), 0)),
            pl.BlockSpec((1, d, dff2), lambda b, be, ns, nu: (be[b], 0, 0)),
            pl.BlockSpec((1, 1, dff2), lambda b, be, ns, nu: (be[b], 0, 0)),
            pl.BlockSpec((1, dff, d), lambda b, be, ns, nu: (be[b], 0, 0)),
            pl.BlockSpec((1, 1, d), lambda b, be, ns, nu: (be[b], 0, 0)),
        ],
        out_specs=pl.BlockSpec((MOE_BLOCK, hw), lambda b, be, ns, nu: (b, 0)),
        scratch_shapes=[pltpu.VMEM((d, dff2), bf16), pltpu.VMEM((dff, d), bf16)],
    )
    return pl.pallas_call(
        _expert_kernel,
        grid_spec=grid_spec,
        out_shape=jax.ShapeDtypeStruct((n_rows, hw), jnp.uint32),
        compiler_params=pltpu.CompilerParams(
            dimension_semantics=("arbitrary",), vmem_limit_bytes=VMEM_LIMIT),
        name="experts",
    )(block_expert, block_live, n_used, xb, w_in, b_in_interleaved.reshape(e, 1, dff2), w_out,
      b_out.reshape(e, 1, d))


def _combine_kernel(nq_ref, dst_ref, dst1_ref, dst2_ref, meta_ref, gate_ref, xm_ref, y_ref, o_ref,
                    ly_ref, sem):
    i = pl.program_id(0)
    last = pl.num_programs(0) - 1
    hw = ly_ref.shape[2]
    cur = i % COMBINE_BUFFERS
    ahead1 = (i + 1) % COMBINE_BUFFERS
    ahead2 = (i + 2) % COMBINE_BUFFERS
    tile1 = jnp.minimum(i + 1, last)
    tile2 = jnp.minimum(i + 2, last)
    min_rows = MIN_PIECES * DMA_ROWS

    def piece(table_ref, buf):
        return lambda q: _piece_copy(y_ref, ly_ref.at[buf], table_ref[0, 0, q], q * DMA_ROWS,
                                     sem.at[buf])

    def zero_tail(buf):
        ly_ref[buf, MOE_TILE * TOP_K:, :] = jnp.zeros((LOCAL_ROWS - MOE_TILE * TOP_K, hw), jnp.uint32)

    def wait(n, buf):
        bulk = pltpu.make_async_copy(y_ref.at[0:min_rows], ly_ref.at[buf, 0:min_rows], sem.at[buf])
        _wait_pieces(n, piece(dst_ref, buf), bulk)

    @pl.when(i == 0)
    def _():
        for table_ref, tile, buf in ((dst_ref, i, cur), (dst1_ref, tile1, ahead1)):
            zero_tail(buf)
            _start_guaranteed_pieces(piece(table_ref, buf))
            _start_remaining_pieces(nq_ref[tile], piece(table_ref, buf))

    zero_tail(ahead2)
    _start_remaining_pieces(nq_ref[tile2], piece(dst2_ref, ahead2))
    _start_guaranteed_pieces(piece(dst2_ref, ahead2))
    gates = gate_ref[...]
    meta = meta_ref[...]
    g = _slot_matrix([meta[:, k:k + 1] for k in range(TOP_K)],
                     [gates[:, k:k + 1] for k in range(TOP_K)], slot_axis=1).astype(bf16)
    wait(nq_ref[i], cur)
    ya, yb = _unpack_bf16_pairs(ly_ref[cur])
    o_ref[:, 0:hw] = xm_ref[:, 0:hw] + _dot(g, ya)
    o_ref[:, hw:] = xm_ref[:, hw:] + _dot(g, yb)

    @pl.when(i == last)
    def _():
        wait(nq_ref[last], ahead1)
        wait(nq_ref[last], ahead2)


def _combine(nq, dst, meta, gates, xm, y):
    n, d = xm.shape
    hw = d // 2
    t = MOE_TILE
    n_tiles = n // t

    def table_of_tile(ahead):
        return pl.BlockSpec((1, 1, LOCAL_PIECES),
                            lambda i, nq: (jnp.minimum(i + ahead, n_tiles - 1), 0, 0),
                            memory_space=pltpu.SMEM)

    grid_spec = pltpu.PrefetchScalarGridSpec(
        num_scalar_prefetch=1,
        grid=(n_tiles,),
        in_specs=[table_of_tile(k) for k in range(COMBINE_BUFFERS)]
        + [pl.BlockSpec((t, LANE), lambda i, nq: (i, 0)),
           pl.BlockSpec((t, LANE), lambda i, nq: (i, 0)),
           pl.BlockSpec((t, d), lambda i, nq: (i, 0)),
           pl.BlockSpec(memory_space=pl.ANY)],
        out_specs=pl.BlockSpec((t, d), lambda i, nq: (i, 0)),
        scratch_shapes=[pltpu.VMEM((COMBINE_BUFFERS, LOCAL_ROWS, hw), jnp.uint32),
                        pltpu.SemaphoreType.DMA((COMBINE_BUFFERS,))],
    )
    return pl.pallas_call(
        _combine_kernel,
        grid_spec=grid_spec,
        out_shape=jax.ShapeDtypeStruct((n, d), f32),
        compiler_params=pltpu.CompilerParams(
            dimension_semantics=("arbitrary",), vmem_limit_bytes=VMEM_LIMIT),
        name="combine",
    )(nq, dst, dst, dst, meta, gates, xm, y)


def _rel_bias_table(rel_bias):
    band = BAND_CHUNKS * CHUNK
    width = band + CHUNK
    dist = (np.arange(width) - CHUNK)[::-1]
    ext = rel_bias[:, np.clip(dist, -REL_CLIP, REL_CLIP) + REL_CLIP].astype(f32)
    heads = ext.shape[0]
    tiled = jnp.broadcast_to(ext[:, None, :], (heads, CHUNK, width)).reshape(heads, CHUNK * width)
    skewed = tiled[:, :CHUNK * (width - 1)].reshape(heads, CHUNK, width - 1)
    return skewed[:, :, CHUNK - 1:CHUNK - 1 + band]


def _round_up(x, m):
    return (x + m - 1) // m * m


def _routing_tables(cnt, n_tokens):
    experts = jnp.arange(N_EXPERTS, dtype=jnp.int32)
    c = cnt.reshape(-1, LANE)[:, :N_EXPERTS].astype(jnp.int32)
    n_tiles = c.shape[0]
    cp = _round_up(c, DMA_ROWS)
    lend = jnp.cumsum(cp, axis=1)
    lstart = lend - cp
    nq = (lend[:, -1] // DMA_ROWS).astype(jnp.int32)
    region = jnp.sum(cp, axis=0)
    padded = _round_up(region, MOE_BLOCK)
    pend = jnp.cumsum(padded)
    pstart = pend - padded
    base = pstart[None, :] + jnp.cumsum(cp, axis=0) - cp
    q0 = jnp.arange(LOCAL_PIECES, dtype=jnp.int32) * DMA_ROWS
    e_q = jnp.minimum(jnp.sum(lend[:, None, :] <= q0[None, :, None], axis=-1), N_EXPERTS - 1)
    shift = jnp.sum(jnp.where(e_q[:, :, None] == experts, (base - lstart)[:, None, :], 0), axis=-1)
    dst = jnp.where(q0[None, :] < lend[:, -1:], shift + q0[None, :], 0).astype(jnp.int32)

    n_blocks = -(-(n_tokens * TOP_K + n_tiles * N_EXPERTS * (DMA_ROWS - 1)) // MOE_BLOCK) + N_EXPERTS
    blk0 = jnp.arange(n_blocks, dtype=jnp.int32) * MOE_BLOCK
    n_used = (pend[-1] // MOE_BLOCK).astype(jnp.int32)
    be = jnp.minimum(jnp.sum(pend[None, :] <= blk0[:, None], axis=1), N_EXPERTS - 1).astype(jnp.int32)
    last = jnp.sum(jnp.where(jnp.arange(n_blocks) == n_used - 1, be, 0))
    be = jnp.where(jnp.arange(n_blocks) < n_used, be, last)
    onehot_be = be[:, None] == experts
    region_end = jnp.sum(jnp.where(onehot_be, pstart + region, 0), axis=1)
    live_rows = jnp.clip(region_end - blk0, 0, MOE_BLOCK)
    live = jnp.where(jnp.arange(n_blocks) < n_used, -(-live_rows // EXPERT_SUB_BLOCK), 0)
    pad_start = (pstart + region).astype(jnp.int32)
    pad_pieces = ((padded - region) // DMA_ROWS).astype(jnp.int32)
    return (nq, pad_start, pad_pieces, dst.reshape(n_tiles, 1, LOCAL_PIECES), be,
            live.astype(jnp.int32), n_used.reshape(1), n_blocks)


def _layer(x, norm1_g, w_in, gate_up, gate_bias, gla_norm_g, q_norm_g, k_norm_g, rel_bias, w_out,
           norm2_g, router_w, router_b, moe_w_in, moe_b_in, moe_w_out, moe_b_out):
    batch, seq, d = x.shape
    n = batch * seq
    assert seq % SEQ_TILE == 0 and n % MID_TILE == 0 and ROW_TILE == SEQ_TILE, (batch, seq)
    x2 = x.reshape(n, d)

    pieces = jnp.split(w_in, np.cumsum(IN_SIZES)[:-1].tolist(), axis=-1)
    pieces[3] = jnp.pad(pieces[3], ((0, 0), (0, LANE - GLA_GATE_RANK)))
    widths = [p.shape[-1] for p in pieces]
    w_all = jnp.concatenate(pieces, axis=-1).astype(bf16)
    tile2 = lambda g: jnp.tile(g.reshape(1, -1), (1, LANE // ATT_DH))
    gq, gk, gv, glr, gog, qs, ak, av = _inproj(x2, norm1_g.reshape(1, d), w_all, widths,
                                               tile2(q_norm_g), tile2(k_norm_g))

    gup = jnp.pad(gate_up, ((0, LANE - GLA_GATE_RANK), (0, 0))).astype(bf16)
    o_gla = _gla(gq, gk, gv, glr, gog, gup, gate_bias.reshape(1, -1), gla_norm_g.reshape(1, -1),
                 batch, seq)
    o_att = _attention(qs, ak, av, _rel_bias_table(rel_bias), batch, seq)

    rw = jnp.pad(router_w, ((0, 0), (0, LANE - N_EXPERTS)))
    rw_hi = rw.astype(bf16)
    rw_lo = (rw - rw_hi.astype(f32)).astype(bf16)
    rb = jnp.pad(router_b, (0, LANE - N_EXPERTS)).reshape(1, LANE)
    xm, h2, meta, meta_t, gates, cnt = _mid(o_gla, o_att, x2, w_out.astype(bf16),
                                            norm2_g.reshape(1, d),
                                            jnp.concatenate([rw_hi, rw_lo], axis=1), rb)

    nq, pad_start, pad_pieces, dst, be, live, n_used, n_blocks = _routing_tables(cnt, n)
    xb = _dispatch(nq, pad_start, pad_pieces, n_used, dst, meta_t, h2, n_blocks * MOE_BLOCK)
    y = _experts(be, live, n_used, xb, moe_w_in, moe_b_in, moe_w_out, moe_b_out)
    out = _combine(nq, dst, meta, gates, xm, y)
    return out.reshape(batch, seq, d)


def kernel(x, norm1_g, w_in, gla_gate_up, gla_gate_bias, gla_norm_g, q_norm_g, k_norm_g, rel_bias, w_out, norm2_g, router_w, router_b, moe_w_in, moe_b_in, moe_w_out, moe_b_out):
    for l in range(norm1_g.shape[0]):
        x = _layer(x, norm1_g[l], w_in[l], gla_gate_up[l], gla_gate_bias[l], gla_norm_g[l],
                   q_norm_g[l], k_norm_g[l], rel_bias[l], w_out[l], norm2_g[l], router_w[l],
                   router_b[l], moe_w_in[l], moe_b_in[l], moe_w_out[l], moe_b_out[l])
    return x
```

```python
import functools

import numpy as np
import jax
import jax.numpy as jnp
from jax import lax
from jax.experimental import pallas as pl
from jax.experimental.pallas import tpu as pltpu

CHUNK = 64
EPS = 1e-6
GLA_HEADS = 4
GLA_DK = 64
GLA_DV = 128
GLA_GATE_RANK = 16
GLA_GATE_TAU = 16.0
ATT_HEADS = 8
ATT_DH = 64
N_BACK_CHUNKS = 8
BAND_CHUNKS = N_BACK_CHUNKS + 1
REL_CLIP = 256
MASK_VALUE = -1e30
N_EXPERTS = 32
TOP_K = 4
SWIGLU_ALPHA = 1.702
SWIGLU_LIMIT = 7.0
MOE_BLOCK = 1024
EXPERT_SUB_BLOCK = 256

LANE = 128
SUBLANE = 8
GLA_QK_W = GLA_HEADS * GLA_DK
GLA_V_W = GLA_HEADS * GLA_DV
ATT_W = ATT_HEADS * ATT_DH
IN_SIZES = (GLA_QK_W, GLA_QK_W, GLA_V_W, GLA_GATE_RANK, GLA_V_W, ATT_W, ATT_W, ATT_W)
SEQ_TILE = N_BACK_CHUNKS * CHUNK
GLA_TILE = 4 * SEQ_TILE
ROW_TILE = 512
MID_TILE = 1024
MOE_TILE = 256
DMA_ROWS = SUBLANE
LOCAL_PIECES = 160
LOCAL_ROWS = LOCAL_PIECES * DMA_ROWS
MIN_PIECES = MOE_TILE * TOP_K // DMA_ROWS
VMEM_LIMIT = 48 * 1024 * 1024

f32 = jnp.float32
bf16 = jnp.bfloat16


def _rms(x, g):
    return x * lax.rsqrt(jnp.mean(x * x, axis=-1, keepdims=True) + EPS) * g


def _dot(a, b):
    return jnp.dot(a, b, preferred_element_type=f32)


def _dot_nt(a, b):
    return lax.dot_general(a, b, (((1,), (1,)), ((), ())), preferred_element_type=f32)


def _dot_tn(a, b):
    return lax.dot_general(a, b, (((0,), (0,)), ((), ())), preferred_element_type=f32)


def _split_bf16(x):
    hi = x.astype(bf16)
    lo = (x - hi.astype(f32)).astype(bf16)
    return hi, lo


def _head_norm(x, g):
    lo = lax.broadcasted_iota(jnp.int32, (1, LANE), 1) < ATT_DH
    sq = x * x
    s0 = jnp.sum(jnp.where(lo, sq, 0.0), axis=-1, keepdims=True)
    s1 = jnp.sum(jnp.where(lo, 0.0, sq), axis=-1, keepdims=True)
    r = jnp.where(lo, lax.rsqrt(s0 * (1.0 / ATT_DH) + EPS), lax.rsqrt(s1 * (1.0 / ATT_DH) + EPS))
    return x * r * g


def _inproj_kernel(x_ref, g_ref, w_ref, qg_ref, kg_ref, gq, gk, gv, glr, gog, qs_ref, ak, av):
    t = x_ref.shape[0]
    n_pairs = ATT_W // LANE
    h = _rms(x_ref[...], g_ref[...]).astype(bf16)
    offsets = np.cumsum((0,) + tuple(r.shape[-1] for r in (gq, gk, gv, glr, gog)) + (ATT_W, ATT_W))
    lo = lax.broadcasted_iota(jnp.int32, (1, LANE), 1) < ATT_DH

    q_all = _dot(h, w_ref[:, offsets[5]:offsets[5] + ATT_W])
    k_all = _dot(h, w_ref[:, offsets[6]:offsets[6] + ATT_W])
    for p in range(n_pairs):
        pair = slice(p * LANE, (p + 1) * LANE)
        qn = _head_norm(q_all[:, pair], qg_ref[...]) * (ATT_DH ** -0.5)
        q_lo = jnp.where(lo, qn, 0.0).astype(bf16)
        q_hi = jnp.where(lo, 0.0, qn).astype(bf16)
        for c in range(t // CHUNK):
            rows = slice(c * CHUNK, (c + 1) * CHUNK)
            qs_ref[c * n_pairs + p, 0:CHUNK, :] = q_lo[rows]
            qs_ref[c * n_pairs + p, CHUNK:2 * CHUNK, :] = q_hi[rows]
        ak[:, pair] = _head_norm(k_all[:, pair], kg_ref[...]).astype(ak.dtype)
    for o_ref, off in zip((gq, gk, gv, glr, gog, av), tuple(offsets[:5]) + (offsets[7],)):
        o_ref[...] = _dot(h, w_ref[:, off:off + o_ref.shape[-1]]).astype(o_ref.dtype)


def _inproj(x2, g, w, widths, qg, kg):
    n, d = x2.shape
    n_pairs = ATT_W // LANE
    blocks_per_tile = ROW_TILE // CHUNK * n_pairs
    rows = lambda wd: pl.BlockSpec((ROW_TILE, wd), lambda i: (i, 0))
    full = lambda a: pl.BlockSpec(a.shape, lambda i: (0,) * a.ndim)
    plain = lambda wd: jax.ShapeDtypeStruct((n, wd), bf16)
    return pl.pallas_call(
        _inproj_kernel,
        grid=(n // ROW_TILE,),
        in_specs=[rows(d), full(g), full(w), full(qg), full(kg)],
        out_specs=[rows(wd) for wd in widths[:5]]
        + [pl.BlockSpec((blocks_per_tile, 2 * CHUNK, LANE), lambda i: (i, 0, 0)), rows(ATT_W), rows(ATT_W)],
        out_shape=[plain(wd) for wd in widths[:5]]
        + [jax.ShapeDtypeStruct((n // CHUNK * n_pairs, 2 * CHUNK, LANE), bf16), plain(ATT_W), plain(ATT_W)],
        compiler_params=pltpu.CompilerParams(
            dimension_semantics=("arbitrary",), vmem_limit_bytes=VMEM_LIMIT),
        name="inproj",
    )(x2, g, w, qg, kg)


def _gla_kernel(q_ref, k_ref, v_ref, lr_ref, og_ref, gup_ref, gb_ref, ng_ref, o_ref, st_ref,
                u_ref, sb_ref):
    t = q_ref.shape[0]

    @pl.when(pl.program_id(1) == 0)
    def _():
        st_ref[...] = jnp.zeros_like(st_ref)

    row = lax.broadcasted_iota(jnp.int32, (SEQ_TILE, SEQ_TILE), 0)
    col = lax.broadcasted_iota(jnp.int32, (SEQ_TILE, SEQ_TILE), 1)
    tri = jnp.where((col <= row) & ((col // CHUNK) == (row // CHUNK)), 1.0, 0.0).astype(bf16)
    cum_parts = []
    for r0 in range(0, t, SEQ_TILE):
        z = _dot(lr_ref[r0:r0 + SEQ_TILE, :], gup_ref[...]) + gb_ref[...]
        log_a = (jnp.minimum(z, 0.0) - jnp.log1p(jnp.exp(-jnp.abs(z)))) * (1.0 / GLA_GATE_TAU)
        la_hi, la_lo = _split_bf16(log_a)
        cum_parts.append(_dot(tri, la_hi) + _dot(tri, la_lo))
    cum_all = jnp.concatenate(cum_parts, axis=0)
    lane = lax.broadcasted_iota(jnp.int32, (1, LANE), 1)
    half_mask = (lane < GLA_DK, lane >= GLA_DK)
    n_chunks = t // CHUNK
    chunk_rows = [slice(c * CHUNK, (c + 1) * CHUNK) for c in range(n_chunks)]
    pair_of = lambda h: slice((h // 2) * LANE, (h // 2 + 1) * LANE)
    head_of = lambda h: slice(h * GLA_DV, (h + 1) * GLA_DV)

    decs = []
    for c, rows in enumerate(chunk_rows):
        cum = cum_all[rows]
        tot = cum[CHUNK - 1:CHUNK]
        kdec = k_ref[rows, :].astype(f32) * jnp.exp(tot - cum)
        decs.append(jnp.exp(tot))
        for h in range(GLA_HEADS):
            kd = jnp.where(half_mask[h % 2], kdec[:, pair_of(h)], 0.0).astype(bf16)
            u_ref[c, h] = _dot_tn(v_ref[rows, head_of(h)], kd)

    for h in range(GLA_HEADS):
        st = st_ref[h]
        for c in range(n_chunks):
            st = st * decs[c][:, pair_of(h)] + u_ref[c, h]
            sb_ref[c, h] = st.astype(bf16)
        st_ref[h] = st

    for c, rows in enumerate(chunk_rows):
        for h in range(GLA_HEADS):
            o = _dot_nt(q_ref[rows, pair_of(h)], sb_ref[c, h]) * (GLA_DK ** -0.5)
            o = _rms(o, ng_ref[...])
            g = og_ref[rows, head_of(h)].astype(f32)
            o_ref[rows, head_of(h)] = (o * (g / (1.0 + jnp.exp(-g)))).astype(o_ref.dtype)


def _gla(gq, gk, gv, glr, gog, gup, gb, ng, batch, seq):
    nt = seq // GLA_TILE
    tile = lambda w: pl.BlockSpec((GLA_TILE, w), lambda b, i: (b * nt + i, 0))
    full = lambda a: pl.BlockSpec(a.shape, lambda b, i: (0,) * a.ndim)
    return pl.pallas_call(
        _gla_kernel,
        grid=(batch, nt),
        in_specs=[tile(GLA_QK_W), tile(GLA_QK_W), tile(GLA_V_W), tile(LANE), tile(GLA_V_W),
                  full(gup), full(gb), full(ng)],
        out_specs=tile(GLA_V_W),
        out_shape=jax.ShapeDtypeStruct((batch * seq, GLA_V_W), bf16),
        scratch_shapes=[pltpu.VMEM((GLA_HEADS, GLA_DV, LANE), f32),
                        pltpu.VMEM((GLA_TILE // CHUNK, GLA_HEADS, GLA_DV, LANE), f32),
                        pltpu.VMEM((GLA_TILE // CHUNK, GLA_HEADS, GLA_DV, LANE), bf16)],
        compiler_params=pltpu.CompilerParams(
            dimension_semantics=("arbitrary", "arbitrary"), vmem_limit_bytes=VMEM_LIMIT),
        name="gla",
    )(gq, gk, gv, glr, gog, gup, gb, ng)


def _att_kernel(qs_ref, k_ref, v_ref, bias_ref, o_ref, kb_ref, vb_ref):
    t = k_ref.shape[0]
    n_pairs = ATT_W // LANE
    band = BAND_CHUNKS * CHUNK
    first = pl.program_id(1) == 0

    @pl.when(first)
    def _():
        kb_ref[0:t, :] = jnp.zeros((t, ATT_W), bf16)
        vb_ref[0:t, :] = jnp.zeros((t, 2 * ATT_W), bf16)

    @pl.when(jnp.logical_not(first))
    def _():
        kb_ref[0:t, :] = kb_ref[t:2 * t, :]
        vb_ref[0:t, :] = vb_ref[t:2 * t, :]

    lane = lax.broadcasted_iota(jnp.int32, (1, LANE), 1)
    lo = lane < ATT_DH

    kb_ref[t:2 * t, :] = k_ref[...]
    for p in range(n_pairs):
        vb_ref[t:2 * t, 2 * p * LANE:(2 * p + 1) * LANE] = v_ref[:, p * LANE:(p + 1) * LANE]
        vb_ref[t:2 * t, (2 * p + 1) * LANE:(2 * p + 2) * LANE] = jnp.ones((t, LANE), bf16)

    colk = lax.broadcasted_iota(jnp.int32, (1, band), 1)

    def chunk_loop(masked):
        n_chunks = t // CHUNK

        def scores(c, p):
            k2 = kb_ref[c * CHUNK:c * CHUNK + band, p * LANE:(p + 1) * LANE]
            return _dot_nt(qs_ref[c * n_pairs + p], k2)

        def weighted_values(e, c, p):
            v2 = vb_ref[c * CHUNK:c * CHUNK + band, 2 * p * LANE:(2 * p + 2) * LANE]
            pvl = _dot(e, v2)
            pv = pvl[:, 0:LANE] / pvl[:, LANE:2 * LANE]
            o_ref[c * CHUNK:(c + 1) * CHUNK, p * LANE:(p + 1) * LANE] = jnp.where(
                lo, pv[0:CHUNK], pv[CHUNK:2 * CHUNK]).astype(o_ref.dtype)

        blocks = [(c, p) for c in range(n_chunks) for p in range(n_pairs)]
        s_next = scores(0, 0)
        e_prev = None
        for i, (c, p) in enumerate(blocks):
            s = s_next + bias_ref[p]
            if i + 1 < len(blocks):
                s_next = scores(*blocks[i + 1])
            if masked:
                s = jnp.where(colk >= t - c * CHUNK, s, MASK_VALUE)
            e = jnp.exp(s - jnp.max(s, axis=-1, keepdims=True)).astype(bf16)
            if i > 0:
                weighted_values(e_prev, *blocks[i - 1])
            e_prev = e
        weighted_values(e_prev, *blocks[-1])

    @pl.when(first)
    def _():
        chunk_loop(True)

    @pl.when(jnp.logical_not(first))
    def _():
        chunk_loop(False)


def _attention(qs, ak, av, bias, batch, seq):
    nt = seq // SEQ_TILE
    n_pairs = ATT_W // LANE
    tile = pl.BlockSpec((SEQ_TILE, ATT_W), lambda b, i: (b * nt + i, 0))
    q_tile = pl.BlockSpec((SEQ_TILE // CHUNK * n_pairs, 2 * CHUNK, LANE), lambda b, i: (b * nt + i, 0, 0))
    full = lambda a: pl.BlockSpec(a.shape, lambda b, i: (0,) * a.ndim)
    bias2 = bias.reshape(n_pairs, 2 * CHUNK, BAND_CHUNKS * CHUNK)
    return pl.pallas_call(
        _att_kernel,
        grid=(batch, nt),
        in_specs=[q_tile, tile, tile, full(bias2)],
        out_specs=tile,
        out_shape=jax.ShapeDtypeStruct((batch * seq, ATT_W), bf16),
        scratch_shapes=[pltpu.VMEM((2 * SEQ_TILE, ATT_W), bf16),
                        pltpu.VMEM((2 * SEQ_TILE, 2 * ATT_W), bf16)],
        compiler_params=pltpu.CompilerParams(
            dimension_semantics=("arbitrary", "arbitrary"), vmem_limit_bytes=VMEM_LIMIT),
        name="attention",
    )(qs, ak, av, bias2)


def _mid_kernel(og_ref, oa_ref, x_ref, wo_ref, g2_ref, rw_ref, rb_ref,
                xm_ref, h_ref, meta_ref, metat_ref, gate_ref, cnt_ref):
    t, d = x_ref.shape
    mt = MOE_TILE
    tiles = [slice(s * mt, (s + 1) * mt) for s in range(t // mt)]
    lane = lax.broadcasted_iota(jnp.int32, (mt, LANE), 1)
    lane_f = lane.astype(f32)
    row = lax.broadcasted_iota(jnp.int32, (mt, mt), 0)
    col = lax.broadcasted_iota(jnp.int32, (mt, mt), 1)
    earlier = jnp.where(col < row, 1.0, 0.0).astype(bf16)
    e_r = lax.broadcasted_iota(jnp.int32, (LANE, LANE), 0)
    e_c = lax.broadcasted_iota(jnp.int32, (LANE, LANE), 1)
    before = jnp.where(e_r < e_c, 1.0, 0.0).astype(bf16)

    def project(rows):
        xm = (x_ref[rows, :] + _dot(og_ref[rows, :], wo_ref[0:GLA_V_W, :])
              + _dot(oa_ref[rows, :], wo_ref[GLA_V_W:, :]))
        xm_ref[rows, :] = xm
        h_hi, h_lo = _split_bf16(_rms(xm, g2_ref[...]))
        h_ref[rows, :] = h_hi
        return h_hi, h_lo

    def route(h_hi, h_lo):
        both = _dot(h_hi, rw_ref[...])
        return both[:, 0:LANE] + both[:, LANE:2 * LANE] + _dot(h_lo, rw_ref[:, 0:LANE]) + rb_ref[...]

    def top_k(logits):
        l = jnp.where(lane < N_EXPERTS, logits, -jnp.inf)
        vals, onehots = [], []
        for _ in range(TOP_K):
            m = jnp.max(l, axis=-1, keepdims=True)
            ik = jnp.min(jnp.where(l == m, lane_f, float(LANE)), axis=-1, keepdims=True)
            vals.append(m)
            onehots.append(lane_f == ik)
            l = jnp.where(onehots[-1], -jnp.inf, l)
        es = [jnp.exp(v - vals[0]) for v in vals]
        den = es[0] + es[1] + es[2] + es[3]
        sel = jnp.zeros((mt, LANE), f32)
        for oh in onehots:
            sel = sel + jnp.where(oh, 1.0, 0.0)
        return onehots, [e / den for e in es], sel

    def slots(s, rows, onehots, gate_vals, sel):
        prefix = _dot(earlier, sel.astype(bf16))
        c_row = jnp.sum(sel, axis=0, keepdims=True)
        cnt_ref[0, s:s + 1, :] = c_row
        padded = jnp.ceil(c_row * (1.0 / DMA_ROWS)) * DMA_ROWS
        start_row = _dot(jnp.broadcast_to(padded, (SUBLANE, LANE)).astype(bf16), before)[0:1]
        slot_base = prefix + start_row
        meta = jnp.zeros((mt, LANE), f32)
        gates = jnp.zeros((mt, LANE), f32)
        for k in range(TOP_K):
            slot_k = jnp.sum(jnp.where(onehots[k], slot_base, 0.0), axis=-1, keepdims=True)
            meta = jnp.where(lane == k, slot_k, meta)
            gates = jnp.where(lane == k, gate_vals[k], gates)
        meta_ref[rows, :] = meta.astype(jnp.int32)
        metat_ref[:, rows] = meta.T[0:SUBLANE, :].astype(jnp.int32)
        gate_ref[rows, :] = gates

    hs = [project(rows) for rows in tiles]
    logits = [route(*h) for h in hs]
    routed = [top_k(lg) for lg in logits]
    for s, rows in enumerate(tiles):
        slots(s, rows, *routed[s])


def _mid(o_gla, o_att, x2, wo, g2, rw, rb):
    n, d = x2.shape
    sub = MID_TILE // MOE_TILE
    tile = lambda w: pl.BlockSpec((MID_TILE, w), lambda i: (i, 0))
    full = lambda a: pl.BlockSpec(a.shape, lambda i: (0,) * a.ndim)
    return pl.pallas_call(
        _mid_kernel,
        grid=(n // MID_TILE,),
        in_specs=[tile(GLA_V_W), tile(ATT_W), tile(d), full(wo), full(g2), full(rw), full(rb)],
        out_specs=[tile(d), tile(d), tile(LANE),
                   pl.BlockSpec((SUBLANE, MID_TILE), lambda i: (0, i)), tile(LANE),
                   pl.BlockSpec((1, sub, LANE), lambda i: (i, 0, 0))],
        out_shape=[jax.ShapeDtypeStruct((n, d), f32),
                   jax.ShapeDtypeStruct((n, d), bf16),
                   jax.ShapeDtypeStruct((n, LANE), jnp.int32),
                   jax.ShapeDtypeStruct((SUBLANE, n), jnp.int32),
                   jax.ShapeDtypeStruct((n, LANE), f32),
                   jax.ShapeDtypeStruct((n // MID_TILE, sub, LANE), f32)],
        compiler_params=pltpu.CompilerParams(
            dimension_semantics=("arbitrary",), vmem_limit_bytes=VMEM_LIMIT),
        name="mid",
    )(o_gla, o_att, x2, wo, g2, rw, rb)


def _slot_matrix(slots, values, slot_axis):
    n_tokens = slots[0].shape[1 - slot_axis]
    shape = (LOCAL_ROWS, n_tokens) if slot_axis == 0 else (n_tokens, LOCAL_ROWS)
    iota_shape = (LOCAL_ROWS, 1) if slot_axis == 0 else (1, LOCAL_ROWS)
    pos = lax.broadcasted_iota(jnp.int32, iota_shape, slot_axis)
    out = jnp.zeros(shape, f32)
    for slot_k, value_k in zip(slots, values):
        out = jnp.where(pos == slot_k, value_k, out)
    return out


def _pack_bf16_pairs(a, b):
    return (pltpu.bitcast(a, jnp.uint32) >> 16) | (pltpu.bitcast(b, jnp.uint32) & jnp.uint32(0xFFFF0000))


def _unpack_bf16_pairs(w):
    a = pltpu.bitcast(w << 16, f32).astype(bf16)
    b = pltpu.bitcast(w & jnp.uint32(0xFFFF0000), f32).astype(bf16)
    return a, b


def _piece_copy(src_ref, dst_ref, src_row, dst_row, sem):
    return pltpu.make_async_copy(src_ref.at[pl.ds(pl.multiple_of(src_row, DMA_ROWS), DMA_ROWS)],
                                 dst_ref.at[pl.ds(pl.multiple_of(dst_row, DMA_ROWS), DMA_ROWS)], sem)


def _start_guaranteed_pieces(piece):
    for q in range(MIN_PIECES):
        piece(q).start()


def _start_remaining_pieces(n, piece):
    lax.fori_loop(MIN_PIECES, n, lambda q, c: (piece(q).start(), c)[1], 0)


def _wait_pieces(n, piece, bulk):
    bulk.wait()
    lax.fori_loop(MIN_PIECES, n, lambda q, c: (piece(q).wait(), c)[1], 0)


def _zero_fill_padding(pad_start_ref, pad_pieces_ref, n_used_ref, xb_ref, z_ref, sem):
    z_ref[...] = jnp.zeros_like(z_ref)
    n_blocks = xb_ref.shape[0] // MOE_BLOCK

    def tail_piece(e, q):
        return _piece_copy(z_ref, xb_ref, 0, pad_start_ref[e] + q * DMA_ROWS, sem)

    def block_copy(b):
        row = pl.multiple_of(b * MOE_BLOCK, MOE_BLOCK)
        return pltpu.make_async_copy(z_ref, xb_ref.at[pl.ds(row, MOE_BLOCK)], sem)

    def each_tail_piece(fn):
        def per_expert(e, c):
            lax.fori_loop(0, pad_pieces_ref[e], lambda q, cc: (fn(tail_piece(e, q)), cc)[1], 0)
            return c
        lax.fori_loop(0, N_EXPERTS, per_expert, 0)

    def each_block(fn):
        lax.fori_loop(n_used_ref[0], n_blocks, lambda b, c: (fn(block_copy(b)), c)[1], 0)

    each_tail_piece(lambda cp: cp.start())
    each_block(lambda cp: cp.start())
    each_tail_piece(lambda cp: cp.wait())
    each_block(lambda cp: cp.wait())


def _dispatch_kernel(nq_ref, pad_start_ref, pad_pieces_ref, n_used_ref, dst_ref, dstp_ref, metat_ref,
                     h_ref, xb_ref, l_ref, z_ref, sem):
    i = pl.program_id(0)
    last = pl.num_programs(0) - 1
    hw = l_ref.shape[2]
    cur = i % 2
    prv = 1 - cur
    n_prev = nq_ref[jnp.maximum(i - 1, 0)]
    min_rows = MIN_PIECES * DMA_ROWS

    def local_sort():
        slots = [metat_ref[k:k + 1, :] for k in range(TOP_K)]
        perm = _slot_matrix(slots, [1.0] * TOP_K, slot_axis=0).astype(bf16)
        return _pack_bf16_pairs(_dot(perm, h_ref[:, 0:hw]), _dot(perm, h_ref[:, hw:]))

    def piece(table_ref, buf):
        return lambda q: _piece_copy(l_ref.at[buf], xb_ref, q * DMA_ROWS, table_ref[0, 0, q],
                                     sem.at[buf])

    def wait(n, table_ref, buf):
        bulk = pltpu.make_async_copy(l_ref.at[buf, 0:min_rows], xb_ref.at[0:min_rows], sem.at[buf])
        _wait_pieces(n, piece(table_ref, buf), bulk)

    @pl.when(i == 0)
    def _():
        l_ref[cur] = local_sort()

    @pl.when(i > 0)
    def _():
        _start_guaranteed_pieces(piece(dstp_ref, prv))
        packed = local_sort()

        @pl.when(i > 1)
        def _():
            wait(nq_ref[jnp.maximum(i - 2, 0)], dstp_ref, cur)

        l_ref[cur] = packed
        _start_remaining_pieces(n_prev, piece(dstp_ref, prv))

    @pl.when(i == last)
    def _():
        _start_guaranteed_pieces(piece(dst_ref, cur))
        _start_remaining_pieces(nq_ref[i], piece(dst_ref, cur))
        _zero_fill_padding(pad_start_ref, pad_pieces_ref, n_used_ref, xb_ref, z_ref, sem.at[2])

        @pl.when(i > 0)
        def _():
            wait(n_prev, dstp_ref, prv)

        wait(nq_ref[i], dst_ref, cur)


def _dispatch(nq, pad_start, pad_pieces, n_used, dst, meta_t, h, n_rows):
    n, d = h.shape
    hw = d // 2
    t = MOE_TILE
    grid_spec = pltpu.PrefetchScalarGridSpec(
        num_scalar_prefetch=4,
        grid=(n // t,),
        in_specs=[pl.BlockSpec((1, 1, LOCAL_PIECES), lambda i, *_: (i, 0, 0), memory_space=pltpu.SMEM),
                  pl.BlockSpec((1, 1, LOCAL_PIECES), lambda i, *_: (jnp.maximum(i - 1, 0), 0, 0),
                               memory_space=pltpu.SMEM),
                  pl.BlockSpec((SUBLANE, t), lambda i, *_: (0, i)),
                  pl.BlockSpec((t, d), lambda i, *_: (i, 0))],
        out_specs=pl.BlockSpec(memory_space=pl.ANY),
        scratch_shapes=[pltpu.VMEM((2, LOCAL_ROWS, hw), jnp.uint32),
                        pltpu.VMEM((MOE_BLOCK, hw), jnp.uint32),
                        pltpu.SemaphoreType.DMA((3,))],
    )
    return pl.pallas_call(
        _dispatch_kernel,
        grid_spec=grid_spec,
        out_shape=jax.ShapeDtypeStruct((n_rows, hw), jnp.uint32),
        compiler_params=pltpu.CompilerParams(
            dimension_semantics=("arbitrary",), vmem_limit_bytes=VMEM_LIMIT),
        name="dispatch",
    )(nq, pad_start, pad_pieces, n_used, dst, dst, meta_t, h)


def _expert_kernel(be_ref, ns_ref, nu_ref, x_ref, wi_ref, bi_ref, wo_ref, bo_ref, y_ref,
                   wib_ref, wob_ref):
    b = pl.program_id(0)
    blk, hw = x_ref.shape
    dff = wo_ref.shape[1]
    subs = [slice(s * EXPERT_SUB_BLOCK, (s + 1) * EXPERT_SUB_BLOCK)
            for s in range(blk // EXPERT_SUB_BLOCK)]
    n_live = ns_ref[b]

    @pl.when(jnp.logical_and(n_live > 0,
                             jnp.logical_or(b == 0, be_ref[b] != be_ref[jnp.maximum(b - 1, 0)])))
    def _():
        for j in range(dff // LANE):
            wib_ref[:, 2 * j * LANE:(2 * j + 1) * LANE] = wi_ref[0, :, j * LANE:(j + 1) * LANE].astype(bf16)
            wib_ref[:, (2 * j + 1) * LANE:(2 * j + 2) * LANE] = (
                wi_ref[0, :, dff + j * LANE:dff + (j + 1) * LANE].astype(bf16))
        wob_ref[...] = wo_ref[0].astype(bf16)

    def up(rows):
        xa, xb = _unpack_bf16_pairs(x_ref[rows, :])
        return _dot(xa, wib_ref[0:hw, :]) + _dot(xb, wib_ref[hw:, :]) + bi_ref[0]

    def down(rows, hc):
        acts = []
        for j in range(dff // LANE):
            glu = jnp.minimum(hc[:, 2 * j * LANE:(2 * j + 1) * LANE], SWIGLU_LIMIT)
            lin = jnp.clip(hc[:, (2 * j + 1) * LANE:(2 * j + 2) * LANE], -SWIGLU_LIMIT, SWIGLU_LIMIT)
            acts.append((glu * (1.0 / (1.0 + jnp.exp(-SWIGLU_ALPHA * glu))) * (lin + 1.0)).astype(bf16))
        act = jnp.concatenate(acts, axis=1)
        y = (_dot(act, wob_ref[...]) + bo_ref[0]).astype(bf16).astype(f32)
        y_ref[rows, :] = _pack_bf16_pairs(y[:, 0:hw], y[:, hw:])

    def run(n):
        if n > 0:
            hc_next = up(subs[0])
        for s in range(n):
            hc = hc_next
            if s + 1 < n:
                hc_next = up(subs[s + 1])
            down(subs[s], hc)
        for rows in subs[n:]:
            y_ref[rows, :] = jnp.zeros((EXPERT_SUB_BLOCK, hw), y_ref.dtype)

    for n in range(len(subs) + 1):
        pl.when(n_live == n)(functools.partial(run, n))


def _experts(block_expert, block_live, n_used, xb, w_in, b_in, w_out, b_out):
    n_rows, hw = xb.shape
    e, d, dff2 = w_in.shape
    dff = w_out.shape[1]
    n_blocks = n_rows // MOE_BLOCK
    b_in_interleaved = b_in.reshape(e, 2, dff // LANE, LANE).transpose(0, 2, 1, 3)
    grid_spec = pltpu.PrefetchScalarGridSpec(
        num_scalar_prefetch=3,
        grid=(n_blocks,),
        in_specs=[
            pl.BlockSpec((MOE_BLOCK, hw), lambda b, be, ns, nu: (jnp.minimum(b, nu[0] - 1), 0)),
            pl.BlockSpec((1, d, dff2), lambda b, be, ns, nu: (be[b], 0, 0)),
            pl.BlockSpec((1, 1, dff2), lambda b, be, ns, nu: (be[b], 0, 0)),
            pl.BlockSpec((1, dff, d), lambda b, be, ns, nu: (be[b], 0, 0)),
            pl.BlockSpec((1, 1, d), lambda b, be, ns, nu: (be[b], 0, 0)),
        ],
        out_specs=pl.BlockSpec((MOE_BLOCK, hw), lambda b, be, ns, nu: (b, 0)),
        scratch_shapes=[pltpu.VMEM((d, dff2), bf16), pltpu.VMEM((dff, d), bf16)],
    )
    return pl.pallas_call(
        _expert_kernel,
        grid_spec=grid_spec,
        out_shape=jax.ShapeDtypeStruct((n_rows, hw), jnp.uint32),
        compiler_params=pltpu.CompilerParams(
            dimension_semantics=("arbitrary",), vmem_limit_bytes=VMEM_LIMIT),
        name="experts",
    )(block_expert, block_live, n_used, xb, w_in, b_in_interleaved.reshape(e, 1, dff2), w_out,
      b_out.reshape(e, 1, d))


def _combine_kernel(nq_ref, dst_ref, dstn_ref, meta_ref, gate_ref, xm_ref, y_ref, o_ref,
                    ly_ref, sem):
    i = pl.program_id(0)
    last = pl.num_programs(0) - 1
    hw = ly_ref.shape[2]
    cur = i % 2
    nxt = 1 - cur
    next_tile = jnp.minimum(i + 1, last)
    min_rows = MIN_PIECES * DMA_ROWS

    def piece(table_ref, buf):
        return lambda q: _piece_copy(y_ref, ly_ref.at[buf], table_ref[0, 0, q], q * DMA_ROWS,
                                     sem.at[buf])

    def zero_tail(buf):
        ly_ref[buf, MOE_TILE * TOP_K:, :] = jnp.zeros((LOCAL_ROWS - MOE_TILE * TOP_K, hw), jnp.uint32)

    def wait(n, buf):
        bulk = pltpu.make_async_copy(y_ref.at[0:min_rows], ly_ref.at[buf, 0:min_rows], sem.at[buf])
        _wait_pieces(n, piece(dst_ref, buf), bulk)

    @pl.when(i == 0)
    def _():
        zero_tail(cur)
        _start_guaranteed_pieces(piece(dst_ref, cur))
        _start_remaining_pieces(nq_ref[i], piece(dst_ref, cur))

    zero_tail(nxt)
    _start_remaining_pieces(nq_ref[next_tile], piece(dstn_ref, nxt))
    _start_guaranteed_pieces(piece(dstn_ref, nxt))
    gates = gate_ref[...]
    meta = meta_ref[...]
    g = _slot_matrix([meta[:, k:k + 1] for k in range(TOP_K)],
                     [gates[:, k:k + 1] for k in range(TOP_K)], slot_axis=1).astype(bf16)
    wait(nq_ref[i], cur)
    ya, yb = _unpack_bf16_pairs(ly_ref[cur])
    o_ref[:, 0:hw] = xm_ref[:, 0:hw] + _dot(g, ya)
    o_ref[:, hw:] = xm_ref[:, hw:] + _dot(g, yb)

    @pl.when(i == last)
    def _():
        wait(nq_ref[next_tile], nxt)


def _combine(nq, dst, meta, gates, xm, y):
    n, d = xm.shape
    hw = d // 2
    t = MOE_TILE
    n_tiles = n // t
    grid_spec = pltpu.PrefetchScalarGridSpec(
        num_scalar_prefetch=1,
        grid=(n_tiles,),
        in_specs=[pl.BlockSpec((1, 1, LOCAL_PIECES), lambda i, nq: (i, 0, 0), memory_space=pltpu.SMEM),
                  pl.BlockSpec((1, 1, LOCAL_PIECES), lambda i, nq: (jnp.minimum(i + 1, n_tiles - 1), 0, 0),
                               memory_space=pltpu.SMEM),
                  pl.BlockSpec((t, LANE), lambda i, nq: (i, 0)),
                  pl.BlockSpec((t, LANE), lambda i, nq: (i, 0)),
                  pl.BlockSpec((t, d), lambda i, nq: (i, 0)),
                  pl.BlockSpec(memory_space=pl.ANY)],
        out_specs=pl.BlockSpec((t, d), lambda i, nq: (i, 0)),
        scratch_shapes=[pltpu.VMEM((2, LOCAL_ROWS, hw), jnp.uint32), pltpu.SemaphoreType.DMA((2,))],
    )
    return pl.pallas_call(
        _combine_kernel,
        grid_spec=grid_spec,
        out_shape=jax.ShapeDtypeStruct((n, d), f32),
        compiler_params=pltpu.CompilerParams(
            dimension_semantics=("arbitrary",), vmem_limit_bytes=VMEM_LIMIT),
        name="combine",
    )(nq, dst, dst, meta, gates, xm, y)


def _rel_bias_table(rel_bias):
    band = BAND_CHUNKS * CHUNK
    width = band + CHUNK
    dist = (np.arange(width) - CHUNK)[::-1]
    ext = rel_bias[:, np.clip(dist, -REL_CLIP, REL_CLIP) + REL_CLIP].astype(f32)
    heads = ext.shape[0]
    tiled = jnp.broadcast_to(ext[:, None, :], (heads, CHUNK, width)).reshape(heads, CHUNK * width)
    skewed = tiled[:, :CHUNK * (width - 1)].reshape(heads, CHUNK, width - 1)
    return skewed[:, :, CHUNK - 1:CHUNK - 1 + band]


def _round_up(x, m):
    return (x + m - 1) // m * m


def _routing_tables(cnt, n_tokens):
    experts = jnp.arange(N_EXPERTS, dtype=jnp.int32)
    c = cnt.reshape(-1, LANE)[:, :N_EXPERTS].astype(jnp.int32)
    n_tiles = c.shape[0]
    cp = _round_up(c, DMA_ROWS)
    lend = jnp.cumsum(cp, axis=1)
    lstart = lend - cp
    nq = (lend[:, -1] // DMA_ROWS).astype(jnp.int32)
    region = jnp.sum(cp, axis=0)
    padded = _round_up(region, MOE_BLOCK)
    pend = jnp.cumsum(padded)
    pstart = pend - padded
    base = pstart[None, :] + jnp.cumsum(cp, axis=0) - cp
    q0 = jnp.arange(LOCAL_PIECES, dtype=jnp.int32) * DMA_ROWS
    e_q = jnp.minimum(jnp.sum(lend[:, None, :] <= q0[None, :, None], axis=-1), N_EXPERTS - 1)
    shift = jnp.sum(jnp.where(e_q[:, :, None] == experts, (base - lstart)[:, None, :], 0), axis=-1)
    dst = jnp.where(q0[None, :] < lend[:, -1:], shift + q0[None, :], 0).astype(jnp.int32)

    n_blocks = -(-(n_tokens * TOP_K + n_tiles * N_EXPERTS * (DMA_ROWS - 1)) // MOE_BLOCK) + N_EXPERTS
    blk0 = jnp.arange(n_blocks, dtype=jnp.int32) * MOE_BLOCK
    n_used = (pend[-1] // MOE_BLOCK).astype(jnp.int32)
    be = jnp.minimum(jnp.sum(pend[None, :] <= blk0[:, None], axis=1), N_EXPERTS - 1).astype(jnp.int32)
    last = jnp.sum(jnp.where(jnp.arange(n_blocks) == n_used - 1, be, 0))
    be = jnp.where(jnp.arange(n_blocks) < n_used, be, last)
    onehot_be = be[:, None] == experts
    region_end = jnp.sum(jnp.where(onehot_be, pstart + region, 0), axis=1)
    live_rows = jnp.clip(region_end - blk0, 0, MOE_BLOCK)
    live = jnp.where(jnp.arange(n_blocks) < n_used, -(-live_rows // EXPERT_SUB_BLOCK), 0)
    pad_start = (pstart + region).astype(jnp.int32)
    pad_pieces = ((padded - region) // DMA_ROWS).astype(jnp.int32)
    return (nq, pad_start, pad_pieces, dst.reshape(n_tiles, 1, LOCAL_PIECES), be,
            live.astype(jnp.int32), n_used.reshape(1), n_blocks)


def _layer(x, norm1_g, w_in, gate_up, gate_bias, gla_norm_g, q_norm_g, k_norm_g, rel_bias, w_out,
           norm2_g, router_w, router_b, moe_w_in, moe_b_in, moe_w_out, moe_b_out):
    batch, seq, d = x.shape
    n = batch * seq
    assert seq % GLA_TILE == 0 and n % MID_TILE == 0 and ROW_TILE == SEQ_TILE, (batch, seq)
    x2 = x.reshape(n, d)

    pieces = jnp.split(w_in, np.cumsum(IN_SIZES)[:-1].tolist(), axis=-1)
    pieces[3] = jnp.pad(pieces[3], ((0, 0), (0, LANE - GLA_GATE_RANK)))
    widths = [p.shape[-1] for p in pieces]
    w_all = jnp.concatenate(pieces, axis=-1).astype(bf16)
    tile2 = lambda g: jnp.tile(g.reshape(1, -1), (1, LANE // ATT_DH))
    gq, gk, gv, glr, gog, qs, ak, av = _inproj(x2, norm1_g.reshape(1, d), w_all, widths,
                                               tile2(q_norm_g), tile2(k_norm_g))

    gup = jnp.pad(gate_up, ((0, LANE - GLA_GATE_RANK), (0, 0))).astype(bf16)
    o_gla = _gla(gq, gk, gv, glr, gog, gup, gate_bias.reshape(1, -1), gla_norm_g.reshape(1, -1),
                 batch, seq)
    o_att = _attention(qs, ak, av, _rel_bias_table(rel_bias), batch, seq)

    rw = jnp.pad(router_w, ((0, 0), (0, LANE - N_EXPERTS)))
    rw_hi = rw.astype(bf16)
    rw_lo = (rw - rw_hi.astype(f32)).astype(bf16)
    rb = jnp.pad(router_b, (0, LANE - N_EXPERTS)).reshape(1, LANE)
    xm, h2, meta, meta_t, gates, cnt = _mid(o_gla, o_att, x2, w_out.astype(bf16),
                                            norm2_g.reshape(1, d),
                                            jnp.concatenate([rw_hi, rw_lo], axis=1), rb)

    nq, pad_start, pad_pieces, dst, be, live, n_used, n_blocks = _routing_tables(cnt, n)
    xb = _dispatch(nq, pad_start, pad_pieces, n_used, dst, meta_t, h2, n_blocks * MOE_BLOCK)
    y = _experts(be, live, n_used, xb, moe_w_in, moe_b_in, moe_w_out, moe_b_out)
    out = _combine(nq, dst, meta, gates, xm, y)
    return out.reshape(batch, seq, d)


def kernel(x, norm1_g, w_in, gla_gate_up, gla_gate_bias, gla_norm_g, q_norm_g, k_norm_g, rel_bias, w_out, norm2_g, router_w, router_b, moe_w_in, moe_b_in, moe_w_out, moe_b_out):
    for l in range(norm1_g.shape[0]):
        x = _layer(x, norm1_g[l], w_in[l], gla_gate_up[l], gla_gate_bias[l], gla_norm_g[l],
                   q_norm_g[l], k_norm_g[l], rel_bias[l], w_out[l], norm2_g[l], router_w[l],
                   router_b[l], moe_w_in[l], moe_b_in[l], moe_w_out[l], moe_b_out[l])
    return x
```

```python
import functools

import numpy as np
import jax
import jax.numpy as jnp
from jax import lax
from jax.experimental import pallas as pl
from jax.experimental.pallas import tpu as pltpu

CHUNK = 64
EPS = 1e-6
GLA_HEADS = 4
GLA_DK = 64
GLA_DV = 128
GLA_GATE_RANK = 16
GLA_GATE_TAU = 16.0
ATT_HEADS = 8
ATT_DH = 64
N_BACK_CHUNKS = 8
BAND_CHUNKS = N_BACK_CHUNKS + 1
REL_CLIP = 256
MASK_VALUE = -1e30
N_EXPERTS = 32
TOP_K = 4
SWIGLU_ALPHA = 1.702
SWIGLU_LIMIT = 7.0
MOE_BLOCK = 1024
EXPERT_SUB_BLOCK = 256

LANE = 128
SUBLANE = 8
GLA_QK_W = GLA_HEADS * GLA_DK
GLA_V_W = GLA_HEADS * GLA_DV
ATT_W = ATT_HEADS * ATT_DH
IN_SIZES = (GLA_QK_W, GLA_QK_W, GLA_V_W, GLA_GATE_RANK, GLA_V_W, ATT_W, ATT_W, ATT_W)
SEQ_TILE = N_BACK_CHUNKS * CHUNK
GLA_TILE = 4 * SEQ_TILE
ROW_TILE = 1024
MID_TILE = 1024
MOE_TILE = 256
DMA_ROWS = SUBLANE
LOCAL_ROWS = -(-(MOE_TILE * TOP_K + N_EXPERTS * (DMA_ROWS - 1)) // LANE) * LANE
LOCAL_PIECES = LOCAL_ROWS // DMA_ROWS
MIN_PIECES = MOE_TILE * TOP_K // DMA_ROWS
VMEM_LIMIT = 48 * 1024 * 1024

f32 = jnp.float32
bf16 = jnp.bfloat16


def _rms(x, g):
    return x * lax.rsqrt(jnp.mean(x * x, axis=-1, keepdims=True) + EPS) * g


def _dot(a, b):
    return jnp.dot(a, b, preferred_element_type=f32)


def _dot_nt(a, b):
    return lax.dot_general(a, b, (((1,), (1,)), ((), ())), preferred_element_type=f32)


def _dot_tn(a, b):
    return lax.dot_general(a, b, (((0,), (0,)), ((), ())), preferred_element_type=f32)


def _split_bf16(x):
    hi = x.astype(bf16)
    lo = (x - hi.astype(f32)).astype(bf16)
    return hi, lo


def _head_norm(x, g):
    lo = lax.broadcasted_iota(jnp.int32, (1, LANE), 1) < ATT_DH
    sq = x * x
    s0 = jnp.sum(jnp.where(lo, sq, 0.0), axis=-1, keepdims=True)
    s1 = jnp.sum(jnp.where(lo, 0.0, sq), axis=-1, keepdims=True)
    r = jnp.where(lo, lax.rsqrt(s0 * (1.0 / ATT_DH) + EPS), lax.rsqrt(s1 * (1.0 / ATT_DH) + EPS))
    return x * r * g


def _inproj_kernel(x_ref, g_ref, w_ref, qg_ref, kg_ref, gq, gk, gv, glr, gog, qs_ref, ak, av):
    t = x_ref.shape[0]
    n_pairs = ATT_W // LANE
    h = _rms(x_ref[...], g_ref[...]).astype(bf16)
    offsets = np.cumsum((0,) + tuple(r.shape[-1] for r in (gq, gk, gv, glr, gog)) + (ATT_W, ATT_W))
    lo = lax.broadcasted_iota(jnp.int32, (1, LANE), 1) < ATT_DH

    q_all = _dot(h, w_ref[:, offsets[5]:offsets[5] + ATT_W])
    k_all = _dot(h, w_ref[:, offsets[6]:offsets[6] + ATT_W])
    for p in range(n_pairs):
        pair = slice(p * LANE, (p + 1) * LANE)
        qn = _head_norm(q_all[:, pair], qg_ref[...]) * (ATT_DH ** -0.5)
        q_lo = jnp.where(lo, qn, 0.0).astype(bf16)
        q_hi = jnp.where(lo, 0.0, qn).astype(bf16)
        for c in range(t // CHUNK):
            rows = slice(c * CHUNK, (c + 1) * CHUNK)
            qs_ref[c * n_pairs + p, 0:CHUNK, :] = q_lo[rows]
            qs_ref[c * n_pairs + p, CHUNK:2 * CHUNK, :] = q_hi[rows]
        ak[:, pair] = _head_norm(k_all[:, pair], kg_ref[...]).astype(ak.dtype)
    for o_ref, off in zip((gq, gk, gv, glr, gog, av), tuple(offsets[:5]) + (offsets[7],)):
        o_ref[...] = _dot(h, w_ref[:, off:off + o_ref.shape[-1]]).astype(o_ref.dtype)


def _inproj(x2, g, w, widths, qg, kg):
    n, d = x2.shape
    n_pairs = ATT_W // LANE
    blocks_per_tile = ROW_TILE // CHUNK * n_pairs
    rows = lambda wd: pl.BlockSpec((ROW_TILE, wd), lambda i: (i, 0))
    full = lambda a: pl.BlockSpec(a.shape, lambda i: (0,) * a.ndim)
    plain = lambda wd: jax.ShapeDtypeStruct((n, wd), bf16)
    return pl.pallas_call(
        _inproj_kernel,
        grid=(n // ROW_TILE,),
        in_specs=[rows(d), full(g), full(w), full(qg), full(kg)],
        out_specs=[rows(wd) for wd in widths[:5]]
        + [pl.BlockSpec((blocks_per_tile, 2 * CHUNK, LANE), lambda i: (i, 0, 0)), rows(ATT_W), rows(ATT_W)],
        out_shape=[plain(wd) for wd in widths[:5]]
        + [jax.ShapeDtypeStruct((n // CHUNK * n_pairs, 2 * CHUNK, LANE), bf16), plain(ATT_W), plain(ATT_W)],
        compiler_params=pltpu.CompilerParams(
            dimension_semantics=("arbitrary",), vmem_limit_bytes=VMEM_LIMIT),
        name="inproj",
    )(x2, g, w, qg, kg)


def _gla_kernel(q_ref, k_ref, v_ref, lr_ref, og_ref, gup_ref, gb_ref, ng_ref, o_ref, st_ref,
                u_ref, sb_ref):
    t = q_ref.shape[0]

    @pl.when(pl.program_id(1) == 0)
    def _():
        st_ref[...] = jnp.zeros_like(st_ref)

    row = lax.broadcasted_iota(jnp.int32, (SEQ_TILE, SEQ_TILE), 0)
    col = lax.broadcasted_iota(jnp.int32, (SEQ_TILE, SEQ_TILE), 1)
    tri = jnp.where((col <= row) & ((col // CHUNK) == (row // CHUNK)), 1.0, 0.0).astype(bf16)
    cum_parts = []
    for r0 in range(0, t, SEQ_TILE):
        z = _dot(lr_ref[r0:r0 + SEQ_TILE, :], gup_ref[...]) + gb_ref[...]
        log_a = (jnp.minimum(z, 0.0) - jnp.log1p(jnp.exp(-jnp.abs(z)))) * (1.0 / GLA_GATE_TAU)
        la_hi, la_lo = _split_bf16(log_a)
        cum_parts.append(_dot(tri, la_hi) + _dot(tri, la_lo))
    cum_all = jnp.concatenate(cum_parts, axis=0)
    lane = lax.broadcasted_iota(jnp.int32, (1, LANE), 1)
    half_mask = (lane < GLA_DK, lane >= GLA_DK)
    n_chunks = t // CHUNK
    chunk_rows = [slice(c * CHUNK, (c + 1) * CHUNK) for c in range(n_chunks)]
    pair_of = lambda h: slice((h // 2) * LANE, (h // 2 + 1) * LANE)
    head_of = lambda h: slice(h * GLA_DV, (h + 1) * GLA_DV)

    decs = []
    for c, rows in enumerate(chunk_rows):
        cum = cum_all[rows]
        tot = cum[CHUNK - 1:CHUNK]
        kdec = k_ref[rows, :].astype(f32) * jnp.exp(tot - cum)
        decs.append(jnp.exp(tot))
        for h in range(GLA_HEADS):
            kd = jnp.where(half_mask[h % 2], kdec[:, pair_of(h)], 0.0).astype(bf16)
            u_ref[c, h] = _dot_tn(v_ref[rows, head_of(h)], kd)

    for h in range(GLA_HEADS):
        st = st_ref[h]
        for c in range(n_chunks):
            st = st * decs[c][:, pair_of(h)] + u_ref[c, h]
            sb_ref[c, h] = st.astype(bf16)
        st_ref[h] = st

    for c, rows in enumerate(chunk_rows):
        for h in range(GLA_HEADS):
            o = _dot_nt(q_ref[rows, pair_of(h)], sb_ref[c, h]) * (GLA_DK ** -0.5)
            o = _rms(o, ng_ref[...])
            g = og_ref[rows, head_of(h)].astype(f32)
            o_ref[rows, head_of(h)] = (o * (g / (1.0 + jnp.exp(-g)))).astype(o_ref.dtype)


def _gla(gq, gk, gv, glr, gog, gup, gb, ng, batch, seq):
    nt = seq // GLA_TILE
    tile = lambda w: pl.BlockSpec((GLA_TILE, w), lambda b, i: (b * nt + i, 0))
    full = lambda a: pl.BlockSpec(a.shape, lambda b, i: (0,) * a.ndim)
    return pl.pallas_call(
        _gla_kernel,
        grid=(batch, nt),
        in_specs=[tile(GLA_QK_W), tile(GLA_QK_W), tile(GLA_V_W), tile(LANE), tile(GLA_V_W),
                  full(gup), full(gb), full(ng)],
        out_specs=tile(GLA_V_W),
        out_shape=jax.ShapeDtypeStruct((batch * seq, GLA_V_W), bf16),
        scratch_shapes=[pltpu.VMEM((GLA_HEADS, GLA_DV, LANE), f32),
                        pltpu.VMEM((GLA_TILE // CHUNK, GLA_HEADS, GLA_DV, LANE), f32),
                        pltpu.VMEM((GLA_TILE // CHUNK, GLA_HEADS, GLA_DV, LANE), bf16)],
        compiler_params=pltpu.CompilerParams(
            dimension_semantics=("arbitrary", "arbitrary"), vmem_limit_bytes=VMEM_LIMIT),
        name="gla",
    )(gq, gk, gv, glr, gog, gup, gb, ng)


def _att_kernel(qs_ref, k_ref, v_ref, bias_ref, o_ref, kb_ref, vb_ref):
    t = k_ref.shape[0]
    n_pairs = ATT_W // LANE
    band = BAND_CHUNKS * CHUNK
    first = pl.program_id(1) == 0

    @pl.when(first)
    def _():
        kb_ref[0:t, :] = jnp.zeros((t, ATT_W), bf16)
        vb_ref[0:t, :] = jnp.zeros((t, 2 * ATT_W), bf16)

    @pl.when(jnp.logical_not(first))
    def _():
        kb_ref[0:t, :] = kb_ref[t:2 * t, :]
        vb_ref[0:t, :] = vb_ref[t:2 * t, :]

    lane = lax.broadcasted_iota(jnp.int32, (1, LANE), 1)
    lo = lane < ATT_DH

    kb_ref[t:2 * t, :] = k_ref[...]
    for p in range(n_pairs):
        vb_ref[t:2 * t, 2 * p * LANE:(2 * p + 1) * LANE] = v_ref[:, p * LANE:(p + 1) * LANE]
        vb_ref[t:2 * t, (2 * p + 1) * LANE:(2 * p + 2) * LANE] = jnp.ones((t, LANE), bf16)

    colk = lax.broadcasted_iota(jnp.int32, (1, band), 1)

    def chunk_loop(masked):
        n_chunks = t // CHUNK

        def scores(c, p):
            k2 = kb_ref[c * CHUNK:c * CHUNK + band, p * LANE:(p + 1) * LANE]
            return _dot_nt(qs_ref[c * n_pairs + p], k2)

        def weighted_values(e, c, p):
            v2 = vb_ref[c * CHUNK:c * CHUNK + band, 2 * p * LANE:(2 * p + 2) * LANE]
            pvl = _dot(e, v2)
            pv = pvl[:, 0:LANE] / pvl[:, LANE:2 * LANE]
            o_ref[c * CHUNK:(c + 1) * CHUNK, p * LANE:(p + 1) * LANE] = jnp.where(
                lo, pv[0:CHUNK], pv[CHUNK:2 * CHUNK]).astype(o_ref.dtype)

        blocks = [(c, p) for c in range(n_chunks) for p in range(n_pairs)]
        s_next = scores(0, 0)
        e_prev = None
        for i, (c, p) in enumerate(blocks):
            s = s_next + bias_ref[p]
            if i + 1 < len(blocks):
                s_next = scores(*blocks[i + 1])
            if masked:
                s = jnp.where(colk >= t - c * CHUNK, s, MASK_VALUE)
            e = jnp.exp(s - jnp.max(s, axis=-1, keepdims=True)).astype(bf16)
            if i > 0:
                weighted_values(e_prev, *blocks[i - 1])
            e_prev = e
        weighted_values(e_prev, *blocks[-1])

    @pl.when(first)
    def _():
        chunk_loop(True)

    @pl.when(jnp.logical_not(first))
    def _():
        chunk_loop(False)


def _attention(qs, ak, av, bias, batch, seq):
    nt = seq // SEQ_TILE
    n_pairs = ATT_W // LANE
    tile = pl.BlockSpec((SEQ_TILE, ATT_W), lambda b, i: (b * nt + i, 0))
    q_tile = pl.BlockSpec((SEQ_TILE // CHUNK * n_pairs, 2 * CHUNK, LANE), lambda b, i: (b * nt + i, 0, 0))
    full = lambda a: pl.BlockSpec(a.shape, lambda b, i: (0,) * a.ndim)
    bias2 = bias.reshape(n_pairs, 2 * CHUNK, BAND_CHUNKS * CHUNK)
    return pl.pallas_call(
        _att_kernel,
        grid=(batch, nt),
        in_specs=[q_tile, tile, tile, full(bias2)],
        out_specs=tile,
        out_shape=jax.ShapeDtypeStruct((batch * seq, ATT_W), bf16),
        scratch_shapes=[pltpu.VMEM((2 * SEQ_TILE, ATT_W), bf16),
                        pltpu.VMEM((2 * SEQ_TILE, 2 * ATT_W), bf16)],
        compiler_params=pltpu.CompilerParams(
            dimension_semantics=("arbitrary", "arbitrary"), vmem_limit_bytes=VMEM_LIMIT),
        name="attention",
    )(qs, ak, av, bias2)


def _mid_kernel(og_ref, oa_ref, x_ref, wo_ref, g2_ref, rw_ref, rb_ref,
                xm_ref, h_ref, meta_ref, metat_ref, gate_ref, cnt_ref):
    t, d = x_ref.shape
    mt = MOE_TILE
    tiles = [slice(s * mt, (s + 1) * mt) for s in range(t // mt)]
    lane = lax.broadcasted_iota(jnp.int32, (mt, LANE), 1)
    lane_f = lane.astype(f32)
    row = lax.broadcasted_iota(jnp.int32, (mt, mt), 0)
    col = lax.broadcasted_iota(jnp.int32, (mt, mt), 1)
    earlier = jnp.where(col < row, 1.0, 0.0).astype(bf16)
    e_r = lax.broadcasted_iota(jnp.int32, (LANE, LANE), 0)
    e_c = lax.broadcasted_iota(jnp.int32, (LANE, LANE), 1)
    before = jnp.where(e_r < e_c, 1.0, 0.0).astype(bf16)

    def project(rows):
        xm = (x_ref[rows, :] + _dot(og_ref[rows, :], wo_ref[0:GLA_V_W, :])
              + _dot(oa_ref[rows, :], wo_ref[GLA_V_W:, :]))
        xm_ref[rows, :] = xm
        h_hi, h_lo = _split_bf16(_rms(xm, g2_ref[...]))
        h_ref[rows, :] = h_hi
        return h_hi, h_lo

    def route(h_hi, h_lo):
        both = _dot(h_hi, rw_ref[...])
        return both[:, 0:LANE] + both[:, LANE:2 * LANE] + _dot(h_lo, rw_ref[:, 0:LANE]) + rb_ref[...]

    def top_k(logits):
        l = jnp.where(lane < N_EXPERTS, logits, -jnp.inf)
        vals, onehots = [], []
        for _ in range(TOP_K):
            m = jnp.max(l, axis=-1, keepdims=True)
            ik = jnp.min(jnp.where(l == m, lane_f, float(LANE)), axis=-1, keepdims=True)
            vals.append(m)
            onehots.append(lane_f == ik)
            l = jnp.where(onehots[-1], -jnp.inf, l)
        es = [jnp.exp(v - vals[0]) for v in vals]
        den = es[0] + es[1] + es[2] + es[3]
        sel = jnp.zeros((mt, LANE), f32)
        for oh in onehots:
            sel = sel + jnp.where(oh, 1.0, 0.0)
        return onehots, [e / den for e in es], sel

    def slots(s, rows, onehots, gate_vals, sel):
        prefix = _dot(earlier, sel.astype(bf16))
        c_row = jnp.sum(sel, axis=0, keepdims=True)
        cnt_ref[0, s:s + 1, :] = c_row
        padded = jnp.ceil(c_row * (1.0 / DMA_ROWS)) * DMA_ROWS
        start_row = _dot(jnp.broadcast_to(padded, (SUBLANE, LANE)).astype(bf16), before)[0:1]
        slot_base = prefix + start_row
        meta = jnp.zeros((mt, LANE), f32)
        gates = jnp.zeros((mt, LANE), f32)
        for k in range(TOP_K):
            slot_k = jnp.sum(jnp.where(onehots[k], slot_base, 0.0), axis=-1, keepdims=True)
            meta = jnp.where(lane == k, slot_k, meta)
            gates = jnp.where(lane == k, gate_vals[k], gates)
        meta_ref[rows, :] = meta.astype(jnp.int32)
        metat_ref[:, rows] = meta.T[0:SUBLANE, :].astype(jnp.int32)
        gate_ref[rows, :] = gates

    hs, logits = [project(tiles[0])], []
    for rows in tiles[1:]:
        hs.append(project(rows))
        logits.append(route(*hs[len(logits)]))
    logits.append(route(*hs[-1]))
    routed = [top_k(lg) for lg in logits]
    for s, rows in enumerate(tiles):
        slots(s, rows, *routed[s])


def _mid(o_gla, o_att, x2, wo, g2, rw, rb):
    n, d = x2.shape
    sub = MID_TILE // MOE_TILE
    tile = lambda w: pl.BlockSpec((MID_TILE, w), lambda i: (i, 0))
    full = lambda a: pl.BlockSpec(a.shape, lambda i: (0,) * a.ndim)
    return pl.pallas_call(
        _mid_kernel,
        grid=(n // MID_TILE,),
        in_specs=[tile(GLA_V_W), tile(ATT_W), tile(d), full(wo), full(g2), full(rw), full(rb)],
        out_specs=[tile(d), tile(d), tile(LANE),
                   pl.BlockSpec((SUBLANE, MID_TILE), lambda i: (0, i)), tile(LANE),
                   pl.BlockSpec((1, sub, LANE), lambda i: (i, 0, 0))],
        out_shape=[jax.ShapeDtypeStruct((n, d), f32),
                   jax.ShapeDtypeStruct((n, d), bf16),
                   jax.ShapeDtypeStruct((n, LANE), jnp.int32),
                   jax.ShapeDtypeStruct((SUBLANE, n), jnp.int32),
                   jax.ShapeDtypeStruct((n, LANE), f32),
                   jax.ShapeDtypeStruct((n // MID_TILE, sub, LANE), f32)],
        compiler_params=pltpu.CompilerParams(
            dimension_semantics=("arbitrary",), vmem_limit_bytes=VMEM_LIMIT),
        name="mid",
    )(o_gla, o_att, x2, wo, g2, rw, rb)


def _slot_matrix(slots, values, slot_axis):
    n_tokens = slots[0].shape[1 - slot_axis]
    shape = (LOCAL_ROWS, n_tokens) if slot_axis == 0 else (n_tokens, LOCAL_ROWS)
    iota_shape = (LOCAL_ROWS, 1) if slot_axis == 0 else (1, LOCAL_ROWS)
    pos = lax.broadcasted_iota(jnp.int32, iota_shape, slot_axis)
    out = jnp.zeros(shape, f32)
    for slot_k, value_k in zip(slots, values):
        out = jnp.where(pos == slot_k, value_k, out)
    return out


def _pack_bf16_pairs(a, b):
    return (pltpu.bitcast(a, jnp.uint32) >> 16) | (pltpu.bitcast(b, jnp.uint32) & jnp.uint32(0xFFFF0000))


def _unpack_bf16_pairs(w):
    a = pltpu.bitcast(w << 16, f32).astype(bf16)
    b = pltpu.bitcast(w & jnp.uint32(0xFFFF0000), f32).astype(bf16)
    return a, b


def _piece_copy(src_ref, dst_ref, src_row, dst_row, sem):
    return pltpu.make_async_copy(src_ref.at[pl.ds(pl.multiple_of(src_row, DMA_ROWS), DMA_ROWS)],
                                 dst_ref.at[pl.ds(pl.multiple_of(dst_row, DMA_ROWS), DMA_ROWS)], sem)


def _start_guaranteed_pieces(piece):
    for q in range(MIN_PIECES):
        piece(q).start()


def _start_remaining_pieces(n, piece):
    lax.fori_loop(MIN_PIECES, n, lambda q, c: (piece(q).start(), c)[1], 0)


def _wait_pieces(n, piece, bulk):
    bulk.wait()
    lax.fori_loop(MIN_PIECES, n, lambda q, c: (piece(q).wait(), c)[1], 0)


def _zero_fill_padding(pad_start_ref, pad_pieces_ref, n_used_ref, xb_ref, z_ref, sem):
    z_ref[...] = jnp.zeros_like(z_ref)
    n_blocks = xb_ref.shape[0] // MOE_BLOCK

    def tail_piece(e, q):
        return _piece_copy(z_ref, xb_ref, 0, pad_start_ref[e] + q * DMA_ROWS, sem)

    def block_copy(b):
        row = pl.multiple_of(b * MOE_BLOCK, MOE_BLOCK)
        return pltpu.make_async_copy(z_ref, xb_ref.at[pl.ds(row, MOE_BLOCK)], sem)

    def each_tail_piece(fn):
        def per_expert(e, c):
            lax.fori_loop(0, pad_pieces_ref[e], lambda q, cc: (fn(tail_piece(e, q)), cc)[1], 0)
            return c
        lax.fori_loop(0, N_EXPERTS, per_expert, 0)

    def each_block(fn):
        lax.fori_loop(n_used_ref[0], n_blocks, lambda b, c: (fn(block_copy(b)), c)[1], 0)

    each_tail_piece(lambda cp: cp.start())
    each_block(lambda cp: cp.start())
    each_tail_piece(lambda cp: cp.wait())
    each_block(lambda cp: cp.wait())


def _dispatch_kernel(nq_ref, pad_start_ref, pad_pieces_ref, n_used_ref, dst_ref, dstp_ref, metat_ref,
                     h_ref, xb_ref, l_ref, z_ref, sem):
    i = pl.program_id(0)
    last = pl.num_programs(0) - 1
    hw = l_ref.shape[2]
    cur = i % 2
    prv = 1 - cur
    n_prev = nq_ref[jnp.maximum(i - 1, 0)]
    min_rows = MIN_PIECES * DMA_ROWS

    def local_sort():
        slots = [metat_ref[k:k + 1, :] for k in range(TOP_K)]
        perm = _slot_matrix(slots, [1.0] * TOP_K, slot_axis=0).astype(bf16)
        return _pack_bf16_pairs(_dot(perm, h_ref[:, 0:hw]), _dot(perm, h_ref[:, hw:]))

    def piece(table_ref, buf):
        return lambda q: _piece_copy(l_ref.at[buf], xb_ref, q * DMA_ROWS, table_ref[0, 0, q],
                                     sem.at[buf])

    def wait(n, table_ref, buf):
        bulk = pltpu.make_async_copy(l_ref.at[buf, 0:min_rows], xb_ref.at[0:min_rows], sem.at[buf])
        _wait_pieces(n, piece(table_ref, buf), bulk)

    @pl.when(i == 0)
    def _():
        l_ref[cur] = local_sort()

    @pl.when(i > 0)
    def _():
        _start_guaranteed_pieces(piece(dstp_ref, prv))
        packed = local_sort()

        @pl.when(i > 1)
        def _():
            wait(nq_ref[jnp.maximum(i - 2, 0)], dstp_ref, cur)

        l_ref[cur] = packed
        _start_remaining_pieces(n_prev, piece(dstp_ref, prv))

    @pl.when(i == last)
    def _():
        _start_guaranteed_pieces(piece(dst_ref, cur))
        _start_remaining_pieces(nq_ref[i], piece(dst_ref, cur))
        _zero_fill_padding(pad_start_ref, pad_pieces_ref, n_used_ref, xb_ref, z_ref, sem.at[2])

        @pl.when(i > 0)
        def _():
            wait(n_prev, dstp_ref, prv)

        wait(nq_ref[i], dst_ref, cur)


def _dispatch(nq, pad_start, pad_pieces, n_used, dst, meta_t, h, n_rows):
    n, d = h.shape
    hw = d // 2
    t = MOE_TILE
    grid_spec = pltpu.PrefetchScalarGridSpec(
        num_scalar_prefetch=4,
        grid=(n // t,),
        in_specs=[pl.BlockSpec((1, 1, LOCAL_PIECES), lambda i, *_: (i, 0, 0), memory_space=pltpu.SMEM),
                  pl.BlockSpec((1, 1, LOCAL_PIECES), lambda i, *_: (jnp.maximum(i - 1, 0), 0, 0),
                               memory_space=pltpu.SMEM),
                  pl.BlockSpec((SUBLANE, t), lambda i, *_: (0, i)),
                  pl.BlockSpec((t, d), lambda i, *_: (i, 0))],
        out_specs=pl.BlockSpec(memory_space=pl.ANY),
        scratch_shapes=[pltpu.VMEM((2, LOCAL_ROWS, hw), jnp.uint32),
                        pltpu.VMEM((MOE_BLOCK, hw), jnp.uint32),
                        pltpu.SemaphoreType.DMA((3,))],
    )
    return pl.pallas_call(
        _dispatch_kernel,
        grid_spec=grid_spec,
        out_shape=jax.ShapeDtypeStruct((n_rows, hw), jnp.uint32),
        compiler_params=pltpu.CompilerParams(
            dimension_semantics=("arbitrary",), vmem_limit_bytes=VMEM_LIMIT),
        name="dispatch",
    )(nq, pad_start, pad_pieces, n_used, dst, dst, meta_t, h)


def _expert_kernel(be_ref, ns_ref, nu_ref, x_ref, wi_ref, bi_ref, wo_ref, bo_ref, y_ref,
                   wib_ref, wob_ref):
    b = pl.program_id(0)
    blk, hw = x_ref.shape
    dff = wo_ref.shape[1]
    subs = [slice(s * EXPERT_SUB_BLOCK, (s + 1) * EXPERT_SUB_BLOCK)
            for s in range(blk // EXPERT_SUB_BLOCK)]
    n_live = ns_ref[b]

    @pl.when(jnp.logical_and(n_live > 0,
                             jnp.logical_or(b == 0, be_ref[b] != be_ref[jnp.maximum(b - 1, 0)])))
    def _():
        for j in range(dff // LANE):
            wib_ref[:, 2 * j * LANE:(2 * j + 1) * LANE] = wi_ref[0, :, j * LANE:(j + 1) * LANE].astype(bf16)
            wib_ref[:, (2 * j + 1) * LANE:(2 * j + 2) * LANE] = (
                wi_ref[0, :, dff + j * LANE:dff + (j + 1) * LANE].astype(bf16))
        wob_ref[...] = wo_ref[0].astype(bf16)

    def up(rows):
        xa, xb = _unpack_bf16_pairs(x_ref[rows, :])
        return _dot(xa, wib_ref[0:hw, :]) + _dot(xb, wib_ref[hw:, :]) + bi_ref[0]

    def down(rows, hc):
        acts = []
        for j in range(dff // LANE):
            glu = jnp.minimum(hc[:, 2 * j * LANE:(2 * j + 1) * LANE], SWIGLU_LIMIT)
            lin = jnp.clip(hc[:, (2 * j + 1) * LANE:(2 * j + 2) * LANE], -SWIGLU_LIMIT, SWIGLU_LIMIT)
            acts.append((glu * (1.0 / (1.0 + jnp.exp(-SWIGLU_ALPHA * glu))) * (lin + 1.0)).astype(bf16))
        act = jnp.concatenate(acts, axis=1)
        y = (_dot(act, wob_ref[...]) + bo_ref[0]).astype(bf16).astype(f32)
        y_ref[rows, :] = _pack_bf16_pairs(y[:, 0:hw], y[:, hw:])

    def run(n):
        if n > 0:
            hc_next = up(subs[0])
        for s in range(n):
            hc = hc_next
            if s + 1 < n:
                hc_next = up(subs[s + 1])
            down(subs[s], hc)
        for rows in subs[n:]:
            y_ref[rows, :] = jnp.zeros((EXPERT_SUB_BLOCK, hw), y_ref.dtype)

    for n in range(len(subs) + 1):
        pl.when(n_live == n)(functools.partial(run, n))


def _experts(block_expert, block_live, n_used, xb, w_in, b_in, w_out, b_out):
    n_rows, hw = xb.shape
    e, d, dff2 = w_in.shape
    dff = w_out.shape[1]
    n_blocks = n_rows // MOE_BLOCK
    b_in_interleaved = b_in.reshape(e, 2, dff // LANE, LANE).transpose(0, 2, 1, 3)
    grid_spec = pltpu.PrefetchScalarGridSpec(
        num_scalar_prefetch=3,
        grid=(n_blocks,),
        in_specs=[
            pl.BlockSpec((MOE_BLOCK, hw), lambda b, be, ns, nu: (jnp.minimum(b, nu[0] - 1), 0)),
            pl.BlockSpec((1, d, dff2), lambda b, be, ns, nu: (be[b], 0, 0)),
            pl.BlockSpec((1, 1, dff2), lambda b, be, ns, nu: (be[b], 0, 0)),
            pl.BlockSpec((1, dff, d), lambda b, be, ns, nu: (be[b], 0, 0)),
            pl.BlockSpec((1, 1, d), lambda b, be, ns, nu: (be[b], 0, 0)),
        ],
        out_specs=pl.BlockSpec((MOE_BLOCK, hw), lambda b, be, ns, nu: (b, 0)),
        scratch_shapes=[pltpu.VMEM((d, dff2), bf16), pltpu.VMEM((dff, d), bf16)],
    )
    return pl.pallas_call(
        _expert_kernel,
        grid_spec=grid_spec,
        out_shape=jax.ShapeDtypeStruct((n_rows, hw), jnp.uint32),
        compiler_params=pltpu.CompilerParams(
            dimension_semantics=("arbitrary",), vmem_limit_bytes=VMEM_LIMIT),
        name="experts",
    )(block_expert, block_live, n_used, xb, w_in, b_in_interleaved.reshape(e, 1, dff2), w_out,
      b_out.reshape(e, 1, d))


def _combine_kernel(nq_ref, dst_ref, dstn_ref, meta_ref, gate_ref, xm_ref, y_ref, o_ref,
                    ly_ref, sem):
    i = pl.program_id(0)
    last = pl.num_programs(0) - 1
    hw = ly_ref.shape[2]
    cur = i % 2
    nxt = 1 - cur
    next_tile = jnp.minimum(i + 1, last)
    min_rows = MIN_PIECES * DMA_ROWS

    def piece(table_ref, buf):
        return lambda q: _piece_copy(y_ref, ly_ref.at[buf], table_ref[0, 0, q], q * DMA_ROWS,
                                     sem.at[buf])

    def zero_tail(buf):
        ly_ref[buf, MOE_TILE * TOP_K:, :] = jnp.zeros((LOCAL_ROWS - MOE_TILE * TOP_K, hw), jnp.uint32)

    def wait(n, buf):
        bulk = pltpu.make_async_copy(y_ref.at[0:min_rows], ly_ref.at[buf, 0:min_rows], sem.at[buf])
        _wait_pieces(n, piece(dst_ref, buf), bulk)

    @pl.when(i == 0)
    def _():
        zero_tail(cur)
        _start_guaranteed_pieces(piece(dst_ref, cur))
        _start_remaining_pieces(nq_ref[i], piece(dst_ref, cur))

    zero_tail(nxt)
    _start_remaining_pieces(nq_ref[next_tile], piece(dstn_ref, nxt))
    _start_guaranteed_pieces(piece(dstn_ref, nxt))
    gates = gate_ref[...]
    meta = meta_ref[...]
    g = _slot_matrix([meta[:, k:k + 1] for k in range(TOP_K)],
                     [gates[:, k:k + 1] for k in range(TOP_K)], slot_axis=1).astype(bf16)
    wait(nq_ref[i], cur)
    ya, yb = _unpack_bf16_pairs(ly_ref[cur])
    o_ref[:, 0:hw] = xm_ref[:, 0:hw] + _dot(g, ya)
    o_ref[:, hw:] = xm_ref[:, hw:] + _dot(g, yb)

    @pl.when(i == last)
    def _():
        wait(nq_ref[next_tile], nxt)


def _combine(nq, dst, meta, gates, xm, y):
    n, d = xm.shape
    hw = d // 2
    t = MOE_TILE
    n_tiles = n // t
    grid_spec = pltpu.PrefetchScalarGridSpec(
        num_scalar_prefetch=1,
        grid=(n_tiles,),
        in_specs=[pl.BlockSpec((1, 1, LOCAL_PIECES), lambda i, nq: (i, 0, 0), memory_space=pltpu.SMEM),
                  pl.BlockSpec((1, 1, LOCAL_PIECES), lambda i, nq: (jnp.minimum(i + 1, n_tiles - 1), 0, 0),
                               memory_space=pltpu.SMEM),
                  pl.BlockSpec((t, LANE), lambda i, nq: (i, 0)),
                  pl.BlockSpec((t, LANE), lambda i, nq: (i, 0)),
                  pl.BlockSpec((t, d), lambda i, nq: (i, 0)),
                  pl.BlockSpec(memory_space=pl.ANY)],
        out_specs=pl.BlockSpec((t, d), lambda i, nq: (i, 0)),
        scratch_shapes=[pltpu.VMEM((2, LOCAL_ROWS, hw), jnp.uint32), pltpu.SemaphoreType.DMA((2,))],
    )
    return pl.pallas_call(
        _combine_kernel,
        grid_spec=grid_spec,
        out_shape=jax.ShapeDtypeStruct((n, d), f32),
        compiler_params=pltpu.CompilerParams(
            dimension_semantics=("arbitrary",), vmem_limit_bytes=VMEM_LIMIT),
        name="combine",
    )(nq, dst, dst, meta, gates, xm, y)


def _rel_bias_table(rel_bias):
    band = BAND_CHUNKS * CHUNK
    width = band + CHUNK
    dist = (np.arange(width) - CHUNK)[::-1]
    ext = rel_bias[:, np.clip(dist, -REL_CLIP, REL_CLIP) + REL_CLIP].astype(f32)
    heads = ext.shape[0]
    tiled = jnp.broadcast_to(ext[:, None, :], (heads, CHUNK, width)).reshape(heads, CHUNK * width)
    skewed = tiled[:, :CHUNK * (width - 1)].reshape(heads, CHUNK, width - 1)
    return skewed[:, :, CHUNK - 1:CHUNK - 1 + band]


def _round_up(x, m):
    return (x + m - 1) // m * m


def _routing_tables(cnt, n_tokens):
    experts = jnp.arange(N_EXPERTS, dtype=jnp.int32)
    c = cnt.reshape(-1, LANE)[:, :N_EXPERTS].astype(jnp.int32)
    n_tiles = c.shape[0]
    cp = _round_up(c, DMA_ROWS)
    lend = jnp.cumsum(cp, axis=1)
    lstart = lend - cp
    nq = (lend[:, -1] // DMA_ROWS).astype(jnp.int32)
    region = jnp.sum(cp, axis=0)
    padded = _round_up(region, MOE_BLOCK)
    pend = jnp.cumsum(padded)
    pstart = pend - padded
    base = pstart[None, :] + jnp.cumsum(cp, axis=0) - cp
    q0 = jnp.arange(LOCAL_PIECES, dtype=jnp.int32) * DMA_ROWS
    e_q = jnp.minimum(jnp.sum(lend[:, None, :] <= q0[None, :, None], axis=-1), N_EXPERTS - 1)
    shift = jnp.sum(jnp.where(e_q[:, :, None] == experts, (base - lstart)[:, None, :], 0), axis=-1)
    dst = jnp.where(q0[None, :] < lend[:, -1:], shift + q0[None, :], 0).astype(jnp.int32)

    n_blocks = -(-(n_tokens * TOP_K + n_tiles * N_EXPERTS * (DMA_ROWS - 1)) // MOE_BLOCK) + N_EXPERTS
    blk0 = jnp.arange(n_blocks, dtype=jnp.int32) * MOE_BLOCK
    n_used = (pend[-1] // MOE_BLOCK).astype(jnp.int32)
    be = jnp.minimum(jnp.sum(pend[None, :] <= blk0[:, None], axis=1), N_EXPERTS - 1).astype(jnp.int32)
    last = jnp.sum(jnp.where(jnp.arange(n_blocks) == n_used - 1, be, 0))
    be = jnp.where(jnp.arange(n_blocks) < n_used, be, last)
    onehot_be = be[:, None] == experts
    region_end = jnp.sum(jnp.where(onehot_be, pstart + region, 0), axis=1)
    live_rows = jnp.clip(region_end - blk0, 0, MOE_BLOCK)
    live = jnp.where(jnp.arange(n_blocks) < n_used, -(-live_rows // EXPERT_SUB_BLOCK), 0)
    pad_start = (pstart + region).astype(jnp.int32)
    pad_pieces = ((padded - region) // DMA_ROWS).astype(jnp.int32)
    return (nq, pad_start, pad_pieces, dst.reshape(n_tiles, 1, LOCAL_PIECES), be,
            live.astype(jnp.int32), n_used.reshape(1), n_blocks)


def _layer(x, norm1_g, w_in, gate_up, gate_bias, gla_norm_g, q_norm_g, k_norm_g, rel_bias, w_out,
           norm2_g, router_w, router_b, moe_w_in, moe_b_in, moe_w_out, moe_b_out):
    batch, seq, d = x.shape
    n = batch * seq
    assert seq % GLA_TILE == 0 and n % MID_TILE == 0 and n % ROW_TILE == 0, (batch, seq)
    x2 = x.reshape(n, d)

    pieces = jnp.split(w_in, np.cumsum(IN_SIZES)[:-1].tolist(), axis=-1)
    pieces[3] = jnp.pad(pieces[3], ((0, 0), (0, LANE - GLA_GATE_RANK)))
    widths = [p.shape[-1] for p in pieces]
    w_all = jnp.concatenate(pieces, axis=-1).astype(bf16)
    tile2 = lambda g: jnp.tile(g.reshape(1, -1), (1, LANE // ATT_DH))
    gq, gk, gv, glr, gog, qs, ak, av = _inproj(x2, norm1_g.reshape(1, d), w_all, widths,
                                               tile2(q_norm_g), tile2(k_norm_g))

    gup = jnp.pad(gate_up, ((0, LANE - GLA_GATE_RANK), (0, 0))).astype(bf16)
    o_gla = _gla(gq, gk, gv, glr, gog, gup, gate_bias.reshape(1, -1), gla_norm_g.reshape(1, -1),
                 batch, seq)
    o_att = _attention(qs, ak, av, _rel_bias_table(rel_bias), batch, seq)

    rw = jnp.pad(router_w, ((0, 0), (0, LANE - N_EXPERTS)))
    rw_hi = rw.astype(bf16)
    rw_lo = (rw - rw_hi.astype(f32)).astype(bf16)
    rb = jnp.pad(router_b, (0, LANE - N_EXPERTS)).reshape(1, LANE)
    xm, h2, meta, meta_t, gates, cnt = _mid(o_gla, o_att, x2, w_out.astype(bf16),
                                            norm2_g.reshape(1, d),
                                            jnp.concatenate([rw_hi, rw_lo], axis=1), rb)

    nq, pad_start, pad_pieces, dst, be, live, n_used, n_blocks = _routing_tables(cnt, n)
    xb = _dispatch(nq, pad_start, pad_pieces, n_used, dst, meta_t, h2, n_blocks * MOE_BLOCK)
    y = _experts(be, live, n_used, xb, moe_w_in, moe_b_in, moe_w_out, moe_b_out)
    out = _combine(nq, dst, meta, gates, xm, y)
    return out.reshape(batch, seq, d)


def kernel(x, norm1_g, w_in, gla_gate_up, gla_gate_bias, gla_norm_g, q_norm_g, k_norm_g, rel_bias, w_out, norm2_g, router_w, router_b, moe_w_in, moe_b_in, moe_w_out, moe_b_out):
    for l in range(norm1_g.shape[0]):
        x = _layer(x, norm1_g[l], w_in[l], gla_gate_up[l], gla_gate_bias[l], gla_norm_g[l],
                   q_norm_g[l], k_norm_g[l], rel_bias[l], w_out[l], norm2_g[l], router_w[l],
                   router_b[l], moe_w_in[l], moe_b_in[l], moe_w_out[l], moe_b_out[l])
    return x
```

```python
import functools

import numpy as np
import jax
import jax.numpy as jnp
from jax import lax
from jax.experimental import pallas as pl
from jax.experimental.pallas import tpu as pltpu

CHUNK = 64
EPS = 1e-6
GLA_HEADS = 4
GLA_DK = 64
GLA_DV = 128
GLA_GATE_RANK = 16
GLA_GATE_TAU = 16.0
ATT_HEADS = 8
ATT_DH = 64
N_BACK_CHUNKS = 8
BAND_CHUNKS = N_BACK_CHUNKS + 1
REL_CLIP = 256
MASK_VALUE = -1e30
N_EXPERTS = 32
TOP_K = 4
SWIGLU_ALPHA = 1.702
SWIGLU_LIMIT = 7.0
MOE_BLOCK = 1024
EXPERT_SUB_BLOCK = 256

LANE = 128
SUBLANE = 8
GLA_QK_W = GLA_HEADS * GLA_DK
GLA_V_W = GLA_HEADS * GLA_DV
ATT_W = ATT_HEADS * ATT_DH
IN_SIZES = (GLA_QK_W, GLA_QK_W, GLA_V_W, GLA_GATE_RANK, GLA_V_W, ATT_W, ATT_W, ATT_W)
SEQ_TILE = N_BACK_CHUNKS * CHUNK
GLA_TILE = 4 * SEQ_TILE
ROW_TILE = 1024
MID_TILE = 1024
MOE_TILE = 256
DMA_ROWS = SUBLANE
LOCAL_ROWS = -(-(MOE_TILE * TOP_K + N_EXPERTS * (DMA_ROWS - 1)) // LANE) * LANE
LOCAL_PIECES = LOCAL_ROWS // DMA_ROWS
MIN_PIECES = MOE_TILE * TOP_K // DMA_ROWS
DMA_PRIORITIES = 2
VMEM_LIMIT = 48 * 1024 * 1024

f32 = jnp.float32
bf16 = jnp.bfloat16


def _rms(x, g):
    return x * lax.rsqrt(jnp.mean(x * x, axis=-1, keepdims=True) + EPS) * g


def _dot(a, b):
    return jnp.dot(a, b, preferred_element_type=f32)


def _dot_nt(a, b):
    return lax.dot_general(a, b, (((1,), (1,)), ((), ())), preferred_element_type=f32)


def _dot_tn(a, b):
    return lax.dot_general(a, b, (((0,), (0,)), ((), ())), preferred_element_type=f32)


def _split_bf16(x):
    hi = x.astype(bf16)
    lo = (x - hi.astype(f32)).astype(bf16)
    return hi, lo


def _head_norm(x, g):
    lo = lax.broadcasted_iota(jnp.int32, (1, LANE), 1) < ATT_DH
    sq = x * x
    s0 = jnp.sum(jnp.where(lo, sq, 0.0), axis=-1, keepdims=True)
    s1 = jnp.sum(jnp.where(lo, 0.0, sq), axis=-1, keepdims=True)
    r = jnp.where(lo, lax.rsqrt(s0 * (1.0 / ATT_DH) + EPS), lax.rsqrt(s1 * (1.0 / ATT_DH) + EPS))
    return x * r * g


def _inproj_kernel(x_ref, g_ref, w_ref, qg_ref, kg_ref, gq, gk, gv, glr, gog, qs_ref, ak, av):
    t = x_ref.shape[0]
    n_pairs = ATT_W // LANE
    h = _rms(x_ref[...], g_ref[...]).astype(bf16)
    offsets = np.cumsum((0,) + tuple(r.shape[-1] for r in (gq, gk, gv, glr, gog)) + (ATT_W, ATT_W))
    lo = lax.broadcasted_iota(jnp.int32, (1, LANE), 1) < ATT_DH

    q_all = _dot(h, w_ref[:, offsets[5]:offsets[5] + ATT_W])
    k_all = _dot(h, w_ref[:, offsets[6]:offsets[6] + ATT_W])
    for p in range(n_pairs):
        pair = slice(p * LANE, (p + 1) * LANE)
        qn = _head_norm(q_all[:, pair], qg_ref[...]) * (ATT_DH ** -0.5)
        q_lo = jnp.where(lo, qn, 0.0).astype(bf16)
        q_hi = jnp.where(lo, 0.0, qn).astype(bf16)
        for c in range(t // CHUNK):
            rows = slice(c * CHUNK, (c + 1) * CHUNK)
            qs_ref[c * n_pairs + p, 0:CHUNK, :] = q_lo[rows]
            qs_ref[c * n_pairs + p, CHUNK:2 * CHUNK, :] = q_hi[rows]
        ak[:, pair] = _head_norm(k_all[:, pair], kg_ref[...]).astype(ak.dtype)
    for o_ref, off in zip((gq, gk, gv, glr, gog, av), tuple(offsets[:5]) + (offsets[7],)):
        o_ref[...] = _dot(h, w_ref[:, off:off + o_ref.shape[-1]]).astype(o_ref.dtype)


def _inproj(x2, g, w, widths, qg, kg):
    n, d = x2.shape
    n_pairs = ATT_W // LANE
    blocks_per_tile = ROW_TILE // CHUNK * n_pairs
    rows = lambda wd: pl.BlockSpec((ROW_TILE, wd), lambda i: (i, 0))
    full = lambda a: pl.BlockSpec(a.shape, lambda i: (0,) * a.ndim)
    plain = lambda wd: jax.ShapeDtypeStruct((n, wd), bf16)
    return pl.pallas_call(
        _inproj_kernel,
        grid=(n // ROW_TILE,),
        in_specs=[rows(d), full(g), full(w), full(qg), full(kg)],
        out_specs=[rows(wd) for wd in widths[:5]]
        + [pl.BlockSpec((blocks_per_tile, 2 * CHUNK, LANE), lambda i: (i, 0, 0)), rows(ATT_W), rows(ATT_W)],
        out_shape=[plain(wd) for wd in widths[:5]]
        + [jax.ShapeDtypeStruct((n // CHUNK * n_pairs, 2 * CHUNK, LANE), bf16), plain(ATT_W), plain(ATT_W)],
        compiler_params=pltpu.CompilerParams(
            dimension_semantics=("arbitrary",), vmem_limit_bytes=VMEM_LIMIT),
        name="inproj",
    )(x2, g, w, qg, kg)


def _gla_kernel(q_ref, k_ref, v_ref, lr_ref, og_ref, gup_ref, gb_ref, ng_ref, o_ref, st_ref,
                u_ref, sb_ref):
    t = q_ref.shape[0]

    @pl.when(pl.program_id(1) == 0)
    def _():
        st_ref[...] = jnp.zeros_like(st_ref)

    row = lax.broadcasted_iota(jnp.int32, (SEQ_TILE, SEQ_TILE), 0)
    col = lax.broadcasted_iota(jnp.int32, (SEQ_TILE, SEQ_TILE), 1)
    tri = jnp.where((col <= row) & ((col // CHUNK) == (row // CHUNK)), 1.0, 0.0).astype(bf16)
    cum_parts = []
    for r0 in range(0, t, SEQ_TILE):
        z = _dot(lr_ref[r0:r0 + SEQ_TILE, :], gup_ref[...]) + gb_ref[...]
        log_a = (jnp.minimum(z, 0.0) - jnp.log1p(jnp.exp(-jnp.abs(z)))) * (1.0 / GLA_GATE_TAU)
        la_hi, la_lo = _split_bf16(log_a)
        cum_parts.append(_dot(tri, la_hi) + _dot(tri, la_lo))
    cum_all = jnp.concatenate(cum_parts, axis=0)
    lane = lax.broadcasted_iota(jnp.int32, (1, LANE), 1)
    half_mask = (lane < GLA_DK, lane >= GLA_DK)
    n_chunks = t // CHUNK
    chunk_rows = [slice(c * CHUNK, (c + 1) * CHUNK) for c in range(n_chunks)]
    pair_of = lambda h: slice((h // 2) * LANE, (h // 2 + 1) * LANE)
    head_of = lambda h: slice(h * GLA_DV, (h + 1) * GLA_DV)

    decs = []
    for c, rows in enumerate(chunk_rows):
        cum = cum_all[rows]
        tot = cum[CHUNK - 1:CHUNK]
        kdec = k_ref[rows, :].astype(f32) * jnp.exp(tot - cum)
        decs.append(jnp.exp(tot))
        for h in range(GLA_HEADS):
            kd = jnp.where(half_mask[h % 2], kdec[:, pair_of(h)], 0.0).astype(bf16)
            u_ref[c, h] = _dot_tn(v_ref[rows, head_of(h)], kd)

    for h in range(GLA_HEADS):
        st = st_ref[h]
        for c in range(n_chunks):
            st = st * decs[c][:, pair_of(h)] + u_ref[c, h]
            sb_ref[c, h] = st.astype(bf16)
        st_ref[h] = st

    for c, rows in enumerate(chunk_rows):
        for h in range(GLA_HEADS):
            o = _dot_nt(q_ref[rows, pair_of(h)], sb_ref[c, h]) * (GLA_DK ** -0.5)
            o = _rms(o, ng_ref[...])
            g = og_ref[rows, head_of(h)].astype(f32)
            o_ref[rows, head_of(h)] = (o * (g / (1.0 + jnp.exp(-g)))).astype(o_ref.dtype)


def _gla(gq, gk, gv, glr, gog, gup, gb, ng, batch, seq):
    nt = seq // GLA_TILE
    tile = lambda w: pl.BlockSpec((GLA_TILE, w), lambda b, i: (b * nt + i, 0))
    full = lambda a: pl.BlockSpec(a.shape, lambda b, i: (0,) * a.ndim)
    return pl.pallas_call(
        _gla_kernel,
        grid=(batch, nt),
        in_specs=[tile(GLA_QK_W), tile(GLA_QK_W), tile(GLA_V_W), tile(LANE), tile(GLA_V_W),
                  full(gup), full(gb), full(ng)],
        out_specs=tile(GLA_V_W),
        out_shape=jax.ShapeDtypeStruct((batch * seq, GLA_V_W), bf16),
        scratch_shapes=[pltpu.VMEM((GLA_HEADS, GLA_DV, LANE), f32),
                        pltpu.VMEM((GLA_TILE // CHUNK, GLA_HEADS, GLA_DV, LANE), f32),
                        pltpu.VMEM((GLA_TILE // CHUNK, GLA_HEADS, GLA_DV, LANE), bf16)],
        compiler_params=pltpu.CompilerParams(
            dimension_semantics=("arbitrary", "arbitrary"), vmem_limit_bytes=VMEM_LIMIT),
        name="gla",
    )(gq, gk, gv, glr, gog, gup, gb, ng)


def _att_kernel(qs_ref, k_ref, v_ref, bias_ref, o_ref, kb_ref, vb_ref):
    t = k_ref.shape[0]
    n_pairs = ATT_W // LANE
    band = BAND_CHUNKS * CHUNK
    first = pl.program_id(1) == 0

    @pl.when(first)
    def _():
        kb_ref[0:t, :] = jnp.zeros((t, ATT_W), bf16)
        vb_ref[0:t, :] = jnp.zeros((t, 2 * ATT_W), bf16)

    @pl.when(jnp.logical_not(first))
    def _():
        kb_ref[0:t, :] = kb_ref[t:2 * t, :]
        vb_ref[0:t, :] = vb_ref[t:2 * t, :]

    lane = lax.broadcasted_iota(jnp.int32, (1, LANE), 1)
    lo = lane < ATT_DH

    kb_ref[t:2 * t, :] = k_ref[...]
    for p in range(n_pairs):
        vb_ref[t:2 * t, 2 * p * LANE:(2 * p + 1) * LANE] = v_ref[:, p * LANE:(p + 1) * LANE]
        vb_ref[t:2 * t, (2 * p + 1) * LANE:(2 * p + 2) * LANE] = jnp.ones((t, LANE), bf16)

    colk = lax.broadcasted_iota(jnp.int32, (1, band), 1)

    def chunk_loop(masked):
        n_chunks = t // CHUNK

        def scores(c, p):
            k2 = kb_ref[c * CHUNK:c * CHUNK + band, p * LANE:(p + 1) * LANE]
            return _dot_nt(qs_ref[c * n_pairs + p], k2)

        def weighted_values(e, c, p):
            v2 = vb_ref[c * CHUNK:c * CHUNK + band, 2 * p * LANE:(2 * p + 2) * LANE]
            pvl = _dot(e, v2)
            pv = pvl[:, 0:LANE] / pvl[:, LANE:2 * LANE]
            o_ref[c * CHUNK:(c + 1) * CHUNK, p * LANE:(p + 1) * LANE] = jnp.where(
                lo, pv[0:CHUNK], pv[CHUNK:2 * CHUNK]).astype(o_ref.dtype)

        blocks = [(c, p) for c in range(n_chunks) for p in range(n_pairs)]
        s_next = scores(0, 0)
        e_prev = None
        for i, (c, p) in enumerate(blocks):
            s = s_next + bias_ref[p]
            if i + 1 < len(blocks):
                s_next = scores(*blocks[i + 1])
            if masked:
                s = jnp.where(colk >= t - c * CHUNK, s, MASK_VALUE)
            e = jnp.exp(s - jnp.max(s, axis=-1, keepdims=True)).astype(bf16)
            if i > 0:
                weighted_values(e_prev, *blocks[i - 1])
            e_prev = e
        weighted_values(e_prev, *blocks[-1])

    @pl.when(first)
    def _():
        chunk_loop(True)

    @pl.when(jnp.logical_not(first))
    def _():
        chunk_loop(False)


def _attention(qs, ak, av, bias, batch, seq):
    nt = seq // SEQ_TILE
    n_pairs = ATT_W // LANE
    tile = pl.BlockSpec((SEQ_TILE, ATT_W), lambda b, i: (b * nt + i, 0))
    q_tile = pl.BlockSpec((SEQ_TILE // CHUNK * n_pairs, 2 * CHUNK, LANE), lambda b, i: (b * nt + i, 0, 0))
    full = lambda a: pl.BlockSpec(a.shape, lambda b, i: (0,) * a.ndim)
    bias2 = bias.reshape(n_pairs, 2 * CHUNK, BAND_CHUNKS * CHUNK)
    return pl.pallas_call(
        _att_kernel,
        grid=(batch, nt),
        in_specs=[q_tile, tile, tile, full(bias2)],
        out_specs=tile,
        out_shape=jax.ShapeDtypeStruct((batch * seq, ATT_W), bf16),
        scratch_shapes=[pltpu.VMEM((2 * SEQ_TILE, ATT_W), bf16),
                        pltpu.VMEM((2 * SEQ_TILE, 2 * ATT_W), bf16)],
        compiler_params=pltpu.CompilerParams(
            dimension_semantics=("arbitrary", "arbitrary"), vmem_limit_bytes=VMEM_LIMIT),
        name="attention",
    )(qs, ak, av, bias2)


def _mid_kernel(og_ref, oa_ref, x_ref, wo_ref, g2_ref, rw_ref, rb_ref,
                xm_ref, h_ref, meta_ref, metat_ref, gate_ref, cnt_ref):
    t, d = x_ref.shape
    mt = MOE_TILE
    tiles = [slice(s * mt, (s + 1) * mt) for s in range(t // mt)]
    lane = lax.broadcasted_iota(jnp.int32, (mt, LANE), 1)
    lane_f = lane.astype(f32)
    row = lax.broadcasted_iota(jnp.int32, (mt, mt), 0)
    col = lax.broadcasted_iota(jnp.int32, (mt, mt), 1)
    earlier = jnp.where(col < row, 1.0, 0.0).astype(bf16)
    e_r = lax.broadcasted_iota(jnp.int32, (LANE, LANE), 0)
    e_c = lax.broadcasted_iota(jnp.int32, (LANE, LANE), 1)
    before = jnp.where(e_r < e_c, 1.0, 0.0).astype(bf16)

    def project(rows):
        xm = (x_ref[rows, :] + _dot(og_ref[rows, :], wo_ref[0:GLA_V_W, :])
              + _dot(oa_ref[rows, :], wo_ref[GLA_V_W:, :]))
        xm_ref[rows, :] = xm
        h_hi, h_lo = _split_bf16(_rms(xm, g2_ref[...]))
        h_ref[rows, :] = h_hi
        return h_hi, h_lo

    def route(h_hi, h_lo):
        both = _dot(h_hi, rw_ref[...])
        return both[:, 0:LANE] + both[:, LANE:2 * LANE] + _dot(h_lo, rw_ref[:, 0:LANE]) + rb_ref[...]

    def top_k(logits):
        l = jnp.where(lane < N_EXPERTS, logits, -jnp.inf)
        vals, onehots = [], []
        for _ in range(TOP_K):
            m = jnp.max(l, axis=-1, keepdims=True)
            ik = jnp.min(jnp.where(l == m, lane_f, float(LANE)), axis=-1, keepdims=True)
            vals.append(m)
            onehots.append(lane_f == ik)
            l = jnp.where(onehots[-1], -jnp.inf, l)
        es = [jnp.exp(v - vals[0]) for v in vals]
        den = es[0] + es[1] + es[2] + es[3]
        sel = jnp.zeros((mt, LANE), f32)
        for oh in onehots:
            sel = sel + jnp.where(oh, 1.0, 0.0)
        return onehots, [e / den for e in es], sel

    def slots(s, rows, onehots, gate_vals, sel):
        prefix = _dot(earlier, sel.astype(bf16))
        c_row = jnp.sum(sel, axis=0, keepdims=True)
        cnt_ref[0, s:s + 1, :] = c_row
        padded = jnp.ceil(c_row * (1.0 / DMA_ROWS)) * DMA_ROWS
        start_row = _dot(jnp.broadcast_to(padded, (SUBLANE, LANE)).astype(bf16), before)[0:1]
        slot_base = prefix + start_row
        meta = jnp.zeros((mt, LANE), f32)
        gates = jnp.zeros((mt, LANE), f32)
        for k in range(TOP_K):
            slot_k = jnp.sum(jnp.where(onehots[k], slot_base, 0.0), axis=-1, keepdims=True)
            meta = jnp.where(lane == k, slot_k, meta)
            gates = jnp.where(lane == k, gate_vals[k], gates)
        meta_ref[rows, :] = meta.astype(jnp.int32)
        metat_ref[:, rows] = meta.T[0:SUBLANE, :].astype(jnp.int32)
        gate_ref[rows, :] = gates

    hs, logits = [project(tiles[0])], []
    for rows in tiles[1:]:
        hs.append(project(rows))
        logits.append(route(*hs[len(logits)]))
    logits.append(route(*hs[-1]))
    routed = [top_k(lg) for lg in logits]
    for s, rows in enumerate(tiles):
        slots(s, rows, *routed[s])


def _mid(o_gla, o_att, x2, wo, g2, rw, rb):
    n, d = x2.shape
    sub = MID_TILE // MOE_TILE
    tile = lambda w: pl.BlockSpec((MID_TILE, w), lambda i: (i, 0))
    full = lambda a: pl.BlockSpec(a.shape, lambda i: (0,) * a.ndim)
    return pl.pallas_call(
        _mid_kernel,
        grid=(n // MID_TILE,),
        in_specs=[tile(GLA_V_W), tile(ATT_W), tile(d), full(wo), full(g2), full(rw), full(rb)],
        out_specs=[tile(d), tile(d), tile(LANE),
                   pl.BlockSpec((SUBLANE, MID_TILE), lambda i: (0, i)), tile(LANE),
                   pl.BlockSpec((1, sub, LANE), lambda i: (i, 0, 0))],
        out_shape=[jax.ShapeDtypeStruct((n, d), f32),
                   jax.ShapeDtypeStruct((n, d), bf16),
                   jax.ShapeDtypeStruct((n, LANE), jnp.int32),
                   jax.ShapeDtypeStruct((SUBLANE, n), jnp.int32),
                   jax.ShapeDtypeStruct((n, LANE), f32),
                   jax.ShapeDtypeStruct((n // MID_TILE, sub, LANE), f32)],
        compiler_params=pltpu.CompilerParams(
            dimension_semantics=("arbitrary",), vmem_limit_bytes=VMEM_LIMIT),
        name="mid",
    )(o_gla, o_att, x2, wo, g2, rw, rb)


def _slot_matrix(slots, values, slot_axis):
    n_tokens = slots[0].shape[1 - slot_axis]
    shape = (LOCAL_ROWS, n_tokens) if slot_axis == 0 else (n_tokens, LOCAL_ROWS)
    iota_shape = (LOCAL_ROWS, 1) if slot_axis == 0 else (1, LOCAL_ROWS)
    pos = lax.broadcasted_iota(jnp.int32, iota_shape, slot_axis)
    out = jnp.zeros(shape, f32)
    for slot_k, value_k in zip(slots, values):
        out = jnp.where(pos == slot_k, value_k, out)
    return out


def _pack_bf16_pairs(a, b):
    return (pltpu.bitcast(a, jnp.uint32) >> 16) | (pltpu.bitcast(b, jnp.uint32) & jnp.uint32(0xFFFF0000))


def _unpack_bf16_pairs(w):
    a = pltpu.bitcast(w << 16, f32).astype(bf16)
    b = pltpu.bitcast(w & jnp.uint32(0xFFFF0000), f32).astype(bf16)
    return a, b


def _piece_copy(src_ref, dst_ref, src_row, dst_row, sem):
    return pltpu.make_async_copy(src_ref.at[pl.ds(pl.multiple_of(src_row, DMA_ROWS), DMA_ROWS)],
                                 dst_ref.at[pl.ds(pl.multiple_of(dst_row, DMA_ROWS), DMA_ROWS)], sem)


def _start_guaranteed_pieces(piece):
    for q in range(MIN_PIECES):
        piece(q).start(priority=q % DMA_PRIORITIES)


def _start_remaining_pieces(n, piece):
    lax.fori_loop(MIN_PIECES, n, lambda q, c: (piece(q).start(), c)[1], 0)


def _wait_pieces(n, piece, bulk):
    bulk.wait()
    lax.fori_loop(MIN_PIECES, n, lambda q, c: (piece(q).wait(), c)[1], 0)


def _zero_fill_padding(pad_start_ref, pad_pieces_ref, n_used_ref, xb_ref, z_ref, sem):
    z_ref[...] = jnp.zeros_like(z_ref)
    n_blocks = xb_ref.shape[0] // MOE_BLOCK

    def tail_piece(e, q):
        return _piece_copy(z_ref, xb_ref, 0, pad_start_ref[e] + q * DMA_ROWS, sem)

    def block_copy(b):
        row = pl.multiple_of(b * MOE_BLOCK, MOE_BLOCK)
        return pltpu.make_async_copy(z_ref, xb_ref.at[pl.ds(row, MOE_BLOCK)], sem)

    def each_tail_piece(fn):
        def per_expert(e, c):
            lax.fori_loop(0, pad_pieces_ref[e], lambda q, cc: (fn(tail_piece(e, q)), cc)[1], 0)
            return c
        lax.fori_loop(0, N_EXPERTS, per_expert, 0)

    def each_block(fn):
        lax.fori_loop(n_used_ref[0], n_blocks, lambda b, c: (fn(block_copy(b)), c)[1], 0)

    each_tail_piece(lambda cp: cp.start())
    each_block(lambda cp: cp.start())
    each_tail_piece(lambda cp: cp.wait())
    each_block(lambda cp: cp.wait())


def _dispatch_kernel(nq_ref, pad_start_ref, pad_pieces_ref, n_used_ref, dst_ref, dstp_ref, metat_ref,
                     h_ref, xb_ref, l_ref, z_ref, sem):
    i = pl.program_id(0)
    last = pl.num_programs(0) - 1
    hw = l_ref.shape[2]
    cur = i % 2
    prv = 1 - cur
    n_prev = nq_ref[jnp.maximum(i - 1, 0)]
    min_rows = MIN_PIECES * DMA_ROWS

    def local_sort():
        slots = [metat_ref[k:k + 1, :] for k in range(TOP_K)]
        perm = _slot_matrix(slots, [1.0] * TOP_K, slot_axis=0).astype(bf16)
        return _pack_bf16_pairs(_dot(perm, h_ref[:, 0:hw]), _dot(perm, h_ref[:, hw:]))

    def piece(table_ref, buf):
        return lambda q: _piece_copy(l_ref.at[buf], xb_ref, q * DMA_ROWS, table_ref[0, 0, q],
                                     sem.at[buf])

    def wait(n, table_ref, buf):
        bulk = pltpu.make_async_copy(l_ref.at[buf, 0:min_rows], xb_ref.at[0:min_rows], sem.at[buf])
        _wait_pieces(n, piece(table_ref, buf), bulk)

    @pl.when(i == 0)
    def _():
        l_ref[cur] = local_sort()

    @pl.when(i > 0)
    def _():
        _start_guaranteed_pieces(piece(dstp_ref, prv))
        packed = local_sort()

        @pl.when(i > 1)
        def _():
            wait(nq_ref[jnp.maximum(i - 2, 0)], dstp_ref, cur)

        l_ref[cur] = packed
        _start_remaining_pieces(n_prev, piece(dstp_ref, prv))

    @pl.when(i == last)
    def _():
        _start_guaranteed_pieces(piece(dst_ref, cur))
        _start_remaining_pieces(nq_ref[i], piece(dst_ref, cur))
        _zero_fill_padding(pad_start_ref, pad_pieces_ref, n_used_ref, xb_ref, z_ref, sem.at[2])

        @pl.when(i > 0)
        def _():
            wait(n_prev, dstp_ref, prv)

        wait(nq_ref[i], dst_ref, cur)


def _dispatch(nq, pad_start, pad_pieces, n_used, dst, meta_t, h, n_rows):
    n, d = h.shape
    hw = d // 2
    t = MOE_TILE
    grid_spec = pltpu.PrefetchScalarGridSpec(
        num_scalar_prefetch=4,
        grid=(n // t,),
        in_specs=[pl.BlockSpec((1, 1, LOCAL_PIECES), lambda i, *_: (i, 0, 0), memory_space=pltpu.SMEM),
                  pl.BlockSpec((1, 1, LOCAL_PIECES), lambda i, *_: (jnp.maximum(i - 1, 0), 0, 0),
                               memory_space=pltpu.SMEM),
                  pl.BlockSpec((SUBLANE, t), lambda i, *_: (0, i)),
                  pl.BlockSpec((t, d), lambda i, *_: (i, 0))],
        out_specs=pl.BlockSpec(memory_space=pl.ANY),
        scratch_shapes=[pltpu.VMEM((2, LOCAL_ROWS, hw), jnp.uint32),
                        pltpu.VMEM((MOE_BLOCK, hw), jnp.uint32),
                        pltpu.SemaphoreType.DMA((3,))],
    )
    return pl.pallas_call(
        _dispatch_kernel,
        grid_spec=grid_spec,
        out_shape=jax.ShapeDtypeStruct((n_rows, hw), jnp.uint32),
        compiler_params=pltpu.CompilerParams(
            dimension_semantics=("arbitrary",), vmem_limit_bytes=VMEM_LIMIT),
        name="dispatch",
    )(nq, pad_start, pad_pieces, n_used, dst, dst, meta_t, h)


def _expert_kernel(be_ref, ns_ref, nu_ref, x_ref, wi_ref, bi_ref, wo_ref, bo_ref, y_ref,
                   wib_ref, wob_ref):
    b = pl.program_id(0)
    blk, hw = x_ref.shape
    dff = wo_ref.shape[1]
    subs = [slice(s * EXPERT_SUB_BLOCK, (s + 1) * EXPERT_SUB_BLOCK)
            for s in range(blk // EXPERT_SUB_BLOCK)]
    n_live = ns_ref[b]

    @pl.when(jnp.logical_and(n_live > 0,
                             jnp.logical_or(b == 0, be_ref[b] != be_ref[jnp.maximum(b - 1, 0)])))
    def _():
        for j in range(dff // LANE):
            wib_ref[:, 2 * j * LANE:(2 * j + 1) * LANE] = wi_ref[0, :, j * LANE:(j + 1) * LANE].astype(bf16)
            wib_ref[:, (2 * j + 1) * LANE:(2 * j + 2) * LANE] = (
                wi_ref[0, :, dff + j * LANE:dff + (j + 1) * LANE].astype(bf16))
        wob_ref[...] = wo_ref[0].astype(bf16)

    def up(rows):
        xa, xb = _unpack_bf16_pairs(x_ref[rows, :])
        return _dot(xa, wib_ref[0:hw, :]) + _dot(xb, wib_ref[hw:, :]) + bi_ref[0]

    def down(rows, hc):
        acts = []
        for j in range(dff // LANE):
            glu = jnp.minimum(hc[:, 2 * j * LANE:(2 * j + 1) * LANE], SWIGLU_LIMIT)
            lin = jnp.clip(hc[:, (2 * j + 1) * LANE:(2 * j + 2) * LANE], -SWIGLU_LIMIT, SWIGLU_LIMIT)
            acts.append((glu * (1.0 / (1.0 + jnp.exp(-SWIGLU_ALPHA * glu))) * (lin + 1.0)).astype(bf16))
        act = jnp.concatenate(acts, axis=1)
        y = (_dot(act, wob_ref[...]) + bo_ref[0]).astype(bf16).astype(f32)
        y_ref[rows, :] = _pack_bf16_pairs(y[:, 0:hw], y[:, hw:])

    def run(n):
        if n > 0:
            hc_next = up(subs[0])
        for s in range(n):
            hc = hc_next
            if s + 1 < n:
                hc_next = up(subs[s + 1])
            down(subs[s], hc)
        for rows in subs[n:]:
            y_ref[rows, :] = jnp.zeros((EXPERT_SUB_BLOCK, hw), y_ref.dtype)

    for n in range(len(subs) + 1):
        pl.when(n_live == n)(functools.partial(run, n))


def _experts(block_expert, block_live, n_used, xb, w_in, b_in, w_out, b_out):
    n_rows, hw = xb.shape
    e, d, dff2 = w_in.shape
    dff = w_out.shape[1]
    n_blocks = n_rows // MOE_BLOCK
    b_in_interleaved = b_in.reshape(e, 2, dff // LANE, LANE).transpose(0, 2, 1, 3)
    grid_spec = pltpu.PrefetchScalarGridSpec(
        num_scalar_prefetch=3,
        grid=(n_blocks,),
        in_specs=[
            pl.BlockSpec((MOE_BLOCK, hw), lambda b, be, ns, nu: (jnp.minimum(b, nu[0] - 1), 0)),
            pl.BlockSpec((1, d, dff2), lambda b, be, ns, nu: (be[b], 0, 0)),
            pl.BlockSpec((1, 1, dff2), lambda b, be, ns, nu: (be[b], 0, 0)),
            pl.BlockSpec((1, dff, d), lambda b, be, ns, nu: (be[b], 0, 0)),
            pl.BlockSpec((1, 1, d), lambda b, be, ns, nu: (be[b], 0, 0)),
        ],
        out_specs=pl.BlockSpec((MOE_BLOCK, hw), lambda b, be, ns, nu: (b, 0)),
        scratch_shapes=[pltpu.VMEM((d, dff2), bf16), pltpu.VMEM((dff, d), bf16)],
    )
    return pl.pallas_call(
        _expert_kernel,
        grid_spec=grid_spec,
        out_shape=jax.ShapeDtypeStruct((n_rows, hw), jnp.uint32),
        compiler_params=pltpu.CompilerParams(
            dimension_semantics=("arbitrary",), vmem_limit_bytes=VMEM_LIMIT),
        name="experts",
    )(block_expert, block_live, n_used, xb, w_in, b_in_interleaved.reshape(e, 1, dff2), w_out,
      b_out.reshape(e, 1, d))


def _combine_kernel(nq_ref, dst_ref, dstn_ref, meta_ref, gate_ref, xm_ref, y_ref, o_ref,
                    ly_ref, sem):
    i = pl.program_id(0)
    last = pl.num_programs(0) - 1
    hw = ly_ref.shape[2]
    cur = i % 2
    nxt = 1 - cur
    next_tile = jnp.minimum(i + 1, last)
    min_rows = MIN_PIECES * DMA_ROWS

    def piece(table_ref, buf):
        return lambda q: _piece_copy(y_ref, ly_ref.at[buf], table_ref[0, 0, q], q * DMA_ROWS,
                                     sem.at[buf])

    def zero_tail(buf):
        ly_ref[buf, MOE_TILE * TOP_K:, :] = jnp.zeros((LOCAL_ROWS - MOE_TILE * TOP_K, hw), jnp.uint32)

    def wait(n, buf):
        bulk = pltpu.make_async_copy(y_ref.at[0:min_rows], ly_ref.at[buf, 0:min_rows], sem.at[buf])
        _wait_pieces(n, piece(dst_ref, buf), bulk)

    @pl.when(i == 0)
    def _():
        zero_tail(cur)
        _start_guaranteed_pieces(piece(dst_ref, cur))
        _start_remaining_pieces(nq_ref[i], piece(dst_ref, cur))

    zero_tail(nxt)
    _start_remaining_pieces(nq_ref[next_tile], piece(dstn_ref, nxt))
    _start_guaranteed_pieces(piece(dstn_ref, nxt))
    gates = gate_ref[...]
    meta = meta_ref[...]
    g = _slot_matrix([meta[:, k:k + 1] for k in range(TOP_K)],
                     [gates[:, k:k + 1] for k in range(TOP_K)], slot_axis=1).astype(bf16)
    wait(nq_ref[i], cur)
    ya, yb = _unpack_bf16_pairs(ly_ref[cur])
    o_ref[:, 0:hw] = xm_ref[:, 0:hw] + _dot(g, ya)
    o_ref[:, hw:] = xm_ref[:, hw:] + _dot(g, yb)

    @pl.when(i == last)
    def _():
        wait(nq_ref[next_tile], nxt)


def _combine(nq, dst, meta, gates, xm, y):
    n, d = xm.shape
    hw = d // 2
    t = MOE_TILE
    n_tiles = n // t
    grid_spec = pltpu.PrefetchScalarGridSpec(
        num_scalar_prefetch=1,
        grid=(n_tiles,),
        in_specs=[pl.BlockSpec((1, 1, LOCAL_PIECES), lambda i, nq: (i, 0, 0), memory_space=pltpu.SMEM),
                  pl.BlockSpec((1, 1, LOCAL_PIECES), lambda i, nq: (jnp.minimum(i + 1, n_tiles - 1), 0, 0),
                               memory_space=pltpu.SMEM),
                  pl.BlockSpec((t, LANE), lambda i, nq: (i, 0)),
                  pl.BlockSpec((t, LANE), lambda i, nq: (i, 0)),
                  pl.BlockSpec((t, d), lambda i, nq: (i, 0)),
                  pl.BlockSpec(memory_space=pl.ANY)],
        out_specs=pl.BlockSpec((t, d), lambda i, nq: (i, 0)),
        scratch_shapes=[pltpu.VMEM((2, LOCAL_ROWS, hw), jnp.uint32), pltpu.SemaphoreType.DMA((2,))],
    )
    return pl.pallas_call(
        _combine_kernel,
        grid_spec=grid_spec,
        out_shape=jax.ShapeDtypeStruct((n, d), f32),
        compiler_params=pltpu.CompilerParams(
            dimension_semantics=("arbitrary",), vmem_limit_bytes=VMEM_LIMIT),
        name="combine",
    )(nq, dst, dst, meta, gates, xm, y)


def _rel_bias_table(rel_bias):
    band = BAND_CHUNKS * CHUNK
    width = band + CHUNK
    dist = (np.arange(width) - CHUNK)[::-1]
    ext = rel_bias[:, np.clip(dist, -REL_CLIP, REL_CLIP) + REL_CLIP].astype(f32)
    heads = ext.shape[0]
    tiled = jnp.broadcast_to(ext[:, None, :], (heads, CHUNK, width)).reshape(heads, CHUNK * width)
    skewed = tiled[:, :CHUNK * (width - 1)].reshape(heads, CHUNK, width - 1)
    return skewed[:, :, CHUNK - 1:CHUNK - 1 + band]


def _round_up(x, m):
    return (x + m - 1) // m * m


def _routing_tables(cnt, n_tokens):
    experts = jnp.arange(N_EXPERTS, dtype=jnp.int32)
    c = cnt.reshape(-1, LANE)[:, :N_EXPERTS].astype(jnp.int32)
    n_tiles = c.shape[0]
    cp = _round_up(c, DMA_ROWS)
    lend = jnp.cumsum(cp, axis=1)
    lstart = lend - cp
    nq = (lend[:, -1] // DMA_ROWS).astype(jnp.int32)
    region = jnp.sum(cp, axis=0)
    padded = _round_up(region, MOE_BLOCK)
    pend = jnp.cumsum(padded)
    pstart = pend - padded
    base = pstart[None, :] + jnp.cumsum(cp, axis=0) - cp
    q0 = jnp.arange(LOCAL_PIECES, dtype=jnp.int32) * DMA_ROWS
    e_q = jnp.minimum(jnp.sum(lend[:, None, :] <= q0[None, :, None], axis=-1), N_EXPERTS - 1)
    shift = jnp.sum(jnp.where(e_q[:, :, None] == experts, (base - lstart)[:, None, :], 0), axis=-1)
    dst = jnp.where(q0[None, :] < lend[:, -1:], shift + q0[None, :], 0).astype(jnp.int32)

    n_blocks = -(-(n_tokens * TOP_K + n_tiles * N_EXPERTS * (DMA_ROWS - 1)) // MOE_BLOCK) + N_EXPERTS
    blk0 = jnp.arange(n_blocks, dtype=jnp.int32) * MOE_BLOCK
    n_used = (pend[-1] // MOE_BLOCK).astype(jnp.int32)
    be = jnp.minimum(jnp.sum(pend[None, :] <= blk0[:, None], axis=1), N_EXPERTS - 1).astype(jnp.int32)
    last = jnp.sum(jnp.where(jnp.arange(n_blocks) == n_used - 1, be, 0))
    be = jnp.where(jnp.arange(n_blocks) < n_used, be, last)
    onehot_be = be[:, None] == experts
    region_end = jnp.sum(jnp.where(onehot_be, pstart + region, 0), axis=1)
    live_rows = jnp.clip(region_end - blk0, 0, MOE_BLOCK)
    live = jnp.where(jnp.arange(n_blocks) < n_used, -(-live_rows // EXPERT_SUB_BLOCK), 0)
    pad_start = (pstart + region).astype(jnp.int32)
    pad_pieces = ((padded - region) // DMA_ROWS).astype(jnp.int32)
    return (nq, pad_start, pad_pieces, dst.reshape(n_tiles, 1, LOCAL_PIECES), be,
            live.astype(jnp.int32), n_used.reshape(1), n_blocks)


def _layer(x, norm1_g, w_in, gate_up, gate_bias, gla_norm_g, q_norm_g, k_norm_g, rel_bias, w_out,
           norm2_g, router_w, router_b, moe_w_in, moe_b_in, moe_w_out, moe_b_out):
    batch, seq, d = x.shape
    n = batch * seq
    assert seq % GLA_TILE == 0 and n % MID_TILE == 0 and n % ROW_TILE == 0, (batch, seq)
    x2 = x.reshape(n, d)

    pieces = jnp.split(w_in, np.cumsum(IN_SIZES)[:-1].tolist(), axis=-1)
    pieces[3] = jnp.pad(pieces[3], ((0, 0), (0, LANE - GLA_GATE_RANK)))
    widths = [p.shape[-1] for p in pieces]
    w_all = jnp.concatenate(pieces, axis=-1).astype(bf16)
    tile2 = lambda g: jnp.tile(g.reshape(1, -1), (1, LANE // ATT_DH))
    gq, gk, gv, glr, gog, qs, ak, av = _inproj(x2, norm1_g.reshape(1, d), w_all, widths,
                                               tile2(q_norm_g), tile2(k_norm_g))

    gup = jnp.pad(gate_up, ((0, LANE - GLA_GATE_RANK), (0, 0))).astype(bf16)
    o_gla = _gla(gq, gk, gv, glr, gog, gup, gate_bias.reshape(1, -1), gla_norm_g.reshape(1, -1),
                 batch, seq)
    o_att = _attention(qs, ak, av, _rel_bias_table(rel_bias), batch, seq)

    rw = jnp.pad(router_w, ((0, 0), (0, LANE - N_EXPERTS)))
    rw_hi = rw.astype(bf16)
    rw_lo = (rw - rw_hi.astype(f32)).astype(bf16)
    rb = jnp.pad(router_b, (0, LANE - N_EXPERTS)).reshape(1, LANE)
    xm, h2, meta, meta_t, gates, cnt = _mid(o_gla, o_att, x2, w_out.astype(bf16),
                                            norm2_g.reshape(1, d),
                                            jnp.concatenate([rw_hi, rw_lo], axis=1), rb)

    nq, pad_start, pad_pieces, dst, be, live, n_used, n_blocks = _routing_tables(cnt, n)
    xb = _dispatch(nq, pad_start, pad_pieces, n_used, dst, meta_t, h2, n_blocks * MOE_BLOCK)
    y = _experts(be, live, n_used, xb, moe_w_in, moe_b_in, moe_w_out, moe_b_out)
    out = _combine(nq, dst, meta, gates, xm, y)
    return out.reshape(batch, seq, d)


def kernel(x, norm1_g, w_in, gla_gate_up, gla_gate_bias, gla_norm_g, q_norm_g, k_norm_g, rel_bias, w_out, norm2_g, router_w, router_b, moe_w_in, moe_b_in, moe_w_out, moe_b_out):
    for l in range(norm1_g.shape[0]):
        x = _layer(x, norm1_g[l], w_in[l], gla_gate_up[l], gla_gate_bias[l], gla_norm_g[l],
                   q_norm_g[l], k_norm_g[l], rel_bias[l], w_out[l], norm2_g[l], router_w[l],
                   router_b[l], moe_w_in[l], moe_b_in[l], moe_w_out[l], moe_b_out[l])
    return x
```
